```python
import math
import jax, jax.numpy as jnp
from jax import lax
import numpy as np

D_MODEL = 1024
BATCH = 8
SEQ = 4096
DEPTH = 1

N_META = 16
CHUNK = 128
ROPE_BASE = 10000.0
EPS = 1e-6
MLA_HEADS = 8
MLA_Q_LORA = 384
MLA_KV_LORA = 256
MLA_NOPE = 64
MLA_ROPE = 32
MLA_V = 64
MLA_QK = MLA_NOPE + MLA_ROPE
RET_HEADS = 8
RET_DK = 64
RET_DV = 128
PEER_HEADS = 8
PEER_NKEYS = 128
PEER_EXPERTS = PEER_NKEYS * PEER_NKEYS
PEER_DQ = 128
PEER_TOPK = 16
PEER_BLOCK = 256
IN_SIZES = (MLA_Q_LORA, MLA_KV_LORA, MLA_ROPE, RET_HEADS * RET_DK, RET_HEADS * RET_DK,
            RET_HEADS * RET_DV, RET_HEADS * RET_DV, D_MODEL, D_MODEL)
IN_WIDTH = sum(IN_SIZES)

kernel_name = "hybrid_mla_retention_peer_meta"


def rmsnorm(x, g):
    xf = x.astype(jnp.float32)
    y = xf * lax.rsqrt(jnp.mean(xf * xf, axis=-1, keepdims=True) + EPS)
    return (y * g.astype(jnp.float32)).astype(x.dtype)


def rope(x, pos):
    half = x.shape[-1] // 2
    inv = ROPE_BASE ** (-jnp.arange(half, dtype=jnp.float32) / half)
    ang = pos.astype(jnp.float32)[:, None] * inv[None, :]
    cos, sin = jnp.cos(ang), jnp.sin(ang)
    xf = x.astype(jnp.float32)
    x1, x2 = xf[..., :half], xf[..., half:]
    return jnp.concatenate([x1 * cos - x2 * sin, x1 * sin + x2 * cos], -1).astype(x.dtype)


def causal_block_attention(q, k, v):
    B, H, L, dqk = q.shape
    dv = v.shape[-1]
    nblk = -(-L // CHUNK)
    Lp = nblk * CHUNK
    qp = jnp.pad(q, ((0, 0), (0, 0), (0, Lp - L), (0, 0)))
    kpos = jnp.arange(L)
    scale = 1.0 / math.sqrt(dqk)

    def one_block(i):
        qb = lax.dynamic_slice_in_dim(qp, i * CHUNK, CHUNK, axis=2)
        s = jnp.einsum('bhqd,bhkd->bhqk', qb, k).astype(jnp.float32) * scale
        qpos = i * CHUNK + jnp.arange(CHUNK)
        s = jnp.where(kpos[None, :] <= qpos[:, None], s, -jnp.inf)
        p = jax.nn.softmax(s, axis=-1)
        return jnp.einsum('bhqk,bhkd->bhqd', p.astype(v.dtype), v)

    out = lax.map(one_block, jnp.arange(nblk))
    out = out.transpose(1, 2, 0, 3, 4).reshape(B, H, Lp, dv)
    return out[:, :, :L]


def mla_branch(c_q, c_kv, k_pe, pos, g_q_lora, w_uq, g_kv_lora, w_ukv, g_qk_q, g_qk_k):
    B, L, _ = c_q.shape
    q = (rmsnorm(c_q, g_q_lora) @ w_uq).reshape(B, L, MLA_HEADS, MLA_QK)
    kv = (rmsnorm(c_kv, g_kv_lora) @ w_ukv).reshape(B, L, MLA_HEADS, MLA_NOPE + MLA_V)
    k_nope, v = kv[..., :MLA_NOPE], kv[..., MLA_NOPE:]
    k = jnp.concatenate([k_nope, jnp.broadcast_to(k_pe[:, :, None, :], (B, L, MLA_HEADS, MLA_ROPE))], -1)
    q = rmsnorm(q, g_qk_q).transpose(0, 2, 1, 3)
    k = rmsnorm(k, g_qk_k).transpose(0, 2, 1, 3)
    q = jnp.concatenate([q[..., :MLA_NOPE], rope(q[..., MLA_NOPE:], pos)], -1)
    k = jnp.concatenate([k[..., :MLA_NOPE], rope(k[..., MLA_NOPE:], pos)], -1)
    o = causal_block_attention(q, k, v.transpose(0, 2, 1, 3))
    return o.transpose(0, 2, 1, 3).reshape(B, L, MLA_HEADS * MLA_V)


def retention_branch(q, k, v, gate, pos, g_gn):
    B, L, _ = q.shape
    H, C = RET_HEADS, CHUNK
    q = rope(q.reshape(B, L, H, RET_DK).transpose(0, 2, 1, 3), pos)
    k = rope(k.reshape(B, L, H, RET_DK).transpose(0, 2, 1, 3), pos) * (RET_DK ** -0.5)
    v = v.reshape(B, L, H, RET_DV).transpose(0, 2, 1, 3)
    P = (-N_META) % C
    Lp = L + P
    NC = Lp // C
    padf = lambda t: jnp.pad(t, ((0, 0), (0, 0), (P, 0), (0, 0))).reshape(B, H, NC, C, t.shape[-1])
    qc, kc, vc = padf(q), padf(k), padf(v)

    log_gamma = jnp.log(1.0 - 2.0 ** (-5.0 - jnp.arange(H, dtype=jnp.float32)))
    idx = jnp.arange(C, dtype=jnp.float32)
    diff = idx[:, None] - idx[None, :]
    decay = jnp.where(diff[None] >= 0, jnp.exp(jnp.maximum(diff, 0.0)[None] * log_gamma[:, None, None]), 0.0)
    zeta = jnp.exp((C - 1 - idx)[None, :] * log_gamma[:, None])
    xi = jnp.exp((idx + 1)[None, :] * log_gamma[:, None])
    gamma_c = jnp.exp(C * log_gamma)

    scores = jnp.einsum('bhncd,bhnmd->bhncm', qc, kc) * decay[None, :, None]
    inner = jnp.einsum('bhncm,bhnme->bhnce', scores, vc)
    chunk_kv = jnp.einsum('bhnmd,bhnme->bhnde', kc * zeta[None, :, None, :, None], vc)

    def step(state, kv_n):
        return gamma_c[None, :, None, None] * state + kv_n, state

    init = jnp.zeros((B, H, RET_DK, RET_DV), chunk_kv.dtype)
    _, prev_states = lax.scan(step, init, jnp.moveaxis(chunk_kv, 2, 0))
    prev_states = jnp.moveaxis(prev_states, 0, 2)
    cross = jnp.einsum('bhncd,bhnde->bhnce', qc, prev_states) * xi[None, :, None, :, None]
    y = (inner + cross).reshape(B, H, Lp, RET_DV)[:, :, P:]

    yf = y.astype(jnp.float32)
    mu = jnp.mean(yf, -1, keepdims=True)
    var = jnp.mean((yf - mu) ** 2, -1, keepdims=True)
    yn = ((yf - mu) * lax.rsqrt(var + EPS)).transpose(0, 2, 1, 3).reshape(B, L, H * RET_DV)
    yn = (yn * g_gn.astype(jnp.float32)).astype(gate.dtype)
    return jax.nn.silu(gate) * yn


def mixer_sublayer(h, pos, g_mix, w_in, g_q_lora, w_uq, g_kv_lora, w_ukv, g_qk_q, g_qk_k,
                   w_mla_out, g_ret_gn, w_ret_out, w_mix_out):
    hn = rmsnorm(h, g_mix)
    z = hn @ w_in
    c_q, c_kv, k_pe, r_q, r_k, r_v, r_g, za, zr = jnp.split(z, np.cumsum(IN_SIZES)[:-1].tolist(), axis=-1)
    y_a = mla_branch(c_q, c_kv, k_pe, pos, g_q_lora, w_uq, g_kv_lora, w_ukv, g_qk_q, g_qk_k) @ w_mla_out
    y_r = retention_branch(r_q, r_k, r_v, r_g, pos, g_ret_gn) @ w_ret_out
    merged = jax.nn.sigmoid(za) * y_a + jax.nn.sigmoid(zr) * y_r
    return merged @ w_mix_out


def peer_sublayer(h, g_ffn, w_peer_q, keys_1, keys_2, peer_u, peer_v):
    shape = h.shape
    x = rmsnorm(h, g_ffn).reshape(-1, shape[-1])
    T = x.shape[0]
    q = (x @ w_peer_q).reshape(T, PEER_HEADS, PEER_DQ)
    half = PEER_DQ // 2
    s1 = jnp.einsum('thd,kd->thk', q[..., :half], keys_1).astype(jnp.float32)
    s2 = jnp.einsum('thd,kd->thk', q[..., half:], keys_2).astype(jnp.float32)
    v1, i1 = lax.top_k(s1, PEER_TOPK)
    v2, i2 = lax.top_k(s2, PEER_TOPK)
    cand = (v1[..., :, None] + v2[..., None, :]).reshape(T, PEER_HEADS, PEER_TOPK * PEER_TOPK)
    sc, ci = lax.top_k(cand, PEER_TOPK)
    e1 = jnp.take_along_axis(i1, ci // PEER_TOPK, axis=-1)
    e2 = jnp.take_along_axis(i2, ci % PEER_TOPK, axis=-1)
    eidx = (e1 * PEER_NKEYS + e2).reshape(T, PEER_HEADS * PEER_TOPK)
    gates = jax.nn.softmax(sc, axis=-1).reshape(T, PEER_HEADS * PEER_TOPK).astype(x.dtype)

    pad = (-T) % PEER_BLOCK
    nb = (T + pad) // PEER_BLOCK
    xb = jnp.pad(x, ((0, pad), (0, 0))).reshape(nb, PEER_BLOCK, -1)
    ib = jnp.pad(eidx, ((0, pad), (0, 0))).reshape(nb, PEER_BLOCK, -1)
    gb = jnp.pad(gates, ((0, pad), (0, 0))).reshape(nb, PEER_BLOCK, -1)

    def block(args):
        xt, it, gt = args
        act = jax.nn.gelu(jnp.einsum('td,tkd->tk', xt, peer_u[it]), approximate=False)
        return jnp.einsum('tk,tkd->td', gt * act, peer_v[it])

    y = lax.map(block, (xb, ib, gb)).reshape(nb * PEER_BLOCK, -1)[:T]
    return y.reshape(shape)


def setup_inputs(seed: int = 0) -> dict:
    key = jax.random.key(seed)
    ks = jax.random.split(key, 20)
    f32 = jnp.float32
    nrm = lambda k, s, sc: jax.random.normal(k, s, f32) * sc
    gain = lambda k, s: 1.0 + 0.02 * jax.random.normal(k, s, f32)
    return {
        "x": nrm(ks[0], (BATCH, SEQ, D_MODEL), 1.0),
        "meta_tokens": nrm(ks[1], (N_META, D_MODEL), 1.0),
        "g_mix": gain(ks[2], (DEPTH, D_MODEL)),
        "w_in": nrm(ks[3], (DEPTH, D_MODEL, IN_WIDTH), D_MODEL ** -0.5),
        "g_q_lora": gain(ks[4], (DEPTH, MLA_Q_LORA)),
        "w_uq": nrm(ks[5], (DEPTH, MLA_Q_LORA, MLA_HEADS * MLA_QK), MLA_Q_LORA ** -0.5),
        "g_kv_lora": gain(ks[6], (DEPTH, MLA_KV_LORA)),
        "w_ukv": nrm(ks[7], (DEPTH, MLA_KV_LORA, MLA_HEADS * (MLA_NOPE + MLA_V)), MLA_KV_LORA ** -0.5),
        "g_qk_q": gain(ks[8], (DEPTH, MLA_QK)),
        "g_qk_k": gain(ks[9], (DEPTH, MLA_QK)),
        "w_mla_out": nrm(ks[10], (DEPTH, MLA_HEADS * MLA_V, D_MODEL), (MLA_HEADS * MLA_V) ** -0.5),
        "g_ret_gn": gain(ks[11], (DEPTH, RET_HEADS * RET_DV)),
        "w_ret_out": nrm(ks[12], (DEPTH, RET_HEADS * RET_DV, D_MODEL), (RET_HEADS * RET_DV) ** -0.5),
        "w_mix_out": nrm(ks[13], (DEPTH, D_MODEL, D_MODEL), D_MODEL ** -0.5),
        "g_ffn": gain(ks[14], (DEPTH, D_MODEL)),
        "w_peer_q": nrm(ks[15], (DEPTH, D_MODEL, PEER_HEADS * PEER_DQ), D_MODEL ** -0.5),
        "peer_keys_1": nrm(ks[16], (DEPTH, PEER_NKEYS, PEER_DQ // 2), (PEER_DQ // 2) ** -0.5),
        "peer_keys_2": nrm(ks[17], (DEPTH, PEER_NKEYS, PEER_DQ // 2), (PEER_DQ // 2) ** -0.5),
        "peer_u": nrm(ks[18], (DEPTH, PEER_EXPERTS, D_MODEL), D_MODEL ** -0.5),
        "peer_v": nrm(ks[19], (DEPTH, PEER_EXPERTS, D_MODEL), PEER_HEADS ** -0.5),
    }


def reference(x, meta_tokens, g_mix, w_in, g_q_lora, w_uq, g_kv_lora, w_ukv, g_qk_q, g_qk_k,
              w_mla_out, g_ret_gn, w_ret_out, w_mix_out, g_ffn, w_peer_q, peer_keys_1,
              peer_keys_2, peer_u, peer_v):
    B = x.shape[0]
    meta = jnp.broadcast_to(meta_tokens.astype(x.dtype)[None], (B, N_META, x.shape[-1]))
    h = jnp.concatenate([meta, x], axis=1)
    pos = jnp.arange(h.shape[1])
    for layer in range(DEPTH):
        h = h + mixer_sublayer(h, pos, g_mix[layer], w_in[layer], g_q_lora[layer], w_uq[layer],
                               g_kv_lora[layer], w_ukv[layer], g_qk_q[layer], g_qk_k[layer],
                               w_mla_out[layer], g_ret_gn[layer], w_ret_out[layer], w_mix_out[layer])
        if layer == DEPTH - 1:
            h = h[:, N_META:]
        h = h + peer_sublayer(h, g_ffn[layer], w_peer_q[layer], peer_keys_1[layer], peer_keys_2[layer],
                              peer_u[layer], peer_v[layer])
    return h
```

```python
import functools
import math

import jax
import jax.numpy as jnp
import numpy as np
from jax import lax
from jax.experimental import pallas as pl
from jax.experimental.pallas import tpu as pltpu

D_MODEL = 1024
N_META = 16
CHUNK = 128
ROPE_BASE = 10000.0
EPS = 1e-6
MLA_HEADS = 8
MLA_Q_LORA = 384
MLA_KV_LORA = 256
MLA_NOPE = 64
MLA_ROPE = 32
MLA_V = 64
MLA_QK = MLA_NOPE + MLA_ROPE
RET_HEADS = 8
RET_DK = 64
RET_DV = 128
PEER_HEADS = 8
PEER_NKEYS = 128
PEER_EXPERTS = PEER_NKEYS * PEER_NKEYS
PEER_DQ = 128
PEER_TOPK = 16
PEER_PAIRS = PEER_HEADS * PEER_TOPK
IN_SIZES = (MLA_Q_LORA, MLA_KV_LORA, MLA_ROPE, RET_HEADS * RET_DK, RET_HEADS * RET_DK,
            RET_HEADS * RET_DV, RET_HEADS * RET_DV, D_MODEL, D_MODEL)

LANES = 128
SUBLANES = 8
HALF_D = D_MODEL // 2
HALF_ROWS = HALF_D // LANES
PEER_TOKEN_BLOCK = 64
PEER_VMEM_LIMIT = 56 * 1024 * 1024
HI_MASK = -65536


def _rmsnorm(x, g):
    xf = x.astype(jnp.float32)
    y = xf * lax.rsqrt(jnp.mean(xf * xf, axis=-1, keepdims=True) + EPS)
    return (y * g.astype(jnp.float32)).astype(x.dtype)


def _rope(x, pos):
    half = x.shape[-1] // 2
    inv = ROPE_BASE ** (-jnp.arange(half, dtype=jnp.float32) / half)
    ang = pos.astype(jnp.float32)[:, None] * inv[None, :]
    cos, sin = jnp.cos(ang), jnp.sin(ang)
    xf = x.astype(jnp.float32)
    x1, x2 = xf[..., :half], xf[..., half:]
    return jnp.concatenate([x1 * cos - x2 * sin, x1 * sin + x2 * cos], -1).astype(x.dtype)


def _causal_block_attention(q, k, v):
    B, H, L, dqk = q.shape
    nblk = -(-L // CHUNK)
    Lp = nblk * CHUNK
    qp = jnp.pad(q, ((0, 0), (0, 0), (0, Lp - L), (0, 0)))
    kpos = jnp.arange(L)
    scale = 1.0 / math.sqrt(dqk)

    def one_block(i):
        qb = lax.dynamic_slice_in_dim(qp, i * CHUNK, CHUNK, axis=2)
        s = jnp.einsum('bhqd,bhkd->bhqk', qb, k).astype(jnp.float32) * scale
        qpos = i * CHUNK + jnp.arange(CHUNK)
        s = jnp.where(kpos[None, :] <= qpos[:, None], s, -jnp.inf)
        p = jax.nn.softmax(s, axis=-1)
        return jnp.einsum('bhqk,bhkd->bhqd', p.astype(v.dtype), v)

    out = lax.map(one_block, jnp.arange(nblk))
    out = out.transpose(1, 2, 0, 3, 4).reshape(B, H, Lp, v.shape[-1])
    return out[:, :, :L]


def _mla_branch(c_q, c_kv, k_pe, pos, g_q_lora, w_uq, g_kv_lora, w_ukv, g_qk_q, g_qk_k):
    B, L, _ = c_q.shape
    q = (_rmsnorm(c_q, g_q_lora) @ w_uq).reshape(B, L, MLA_HEADS, MLA_QK)
    kv = (_rmsnorm(c_kv, g_kv_lora) @ w_ukv).reshape(B, L, MLA_HEADS, MLA_NOPE + MLA_V)
    k_nope, v = kv[..., :MLA_NOPE], kv[..., MLA_NOPE:]
    k = jnp.concatenate([k_nope, jnp.broadcast_to(k_pe[:, :, None, :], (B, L, MLA_HEADS, MLA_ROPE))], -1)
    q = _rmsnorm(q, g_qk_q).transpose(0, 2, 1, 3)
    k = _rmsnorm(k, g_qk_k).transpose(0, 2, 1, 3)
    q = jnp.concatenate([q[..., :MLA_NOPE], _rope(q[..., MLA_NOPE:], pos)], -1)
    k = jnp.concatenate([k[..., :MLA_NOPE], _rope(k[..., MLA_NOPE:], pos)], -1)
    o = _causal_block_attention(q, k, v.transpose(0, 2, 1, 3))
    return o.transpose(0, 2, 1, 3).reshape(B, L, MLA_HEADS * MLA_V)


def _retention_branch(q, k, v, gate, pos, g_gn):
    B, L, _ = q.shape
    H, C = RET_HEADS, CHUNK
    q = _rope(q.reshape(B, L, H, RET_DK).transpose(0, 2, 1, 3), pos)
    k = _rope(k.reshape(B, L, H, RET_DK).transpose(0, 2, 1, 3), pos) * (RET_DK ** -0.5)
    v = v.reshape(B, L, H, RET_DV).transpose(0, 2, 1, 3)
    P = (-N_META) % C
    Lp = L + P
    NC = Lp // C
    padf = lambda t: jnp.pad(t, ((0, 0), (0, 0), (P, 0), (0, 0))).reshape(B, H, NC, C, t.shape[-1])
    qc, kc, vc = padf(q), padf(k), padf(v)
    log_gamma = jnp.log(1.0 - 2.0 ** (-5.0 - jnp.arange(H, dtype=jnp.float32)))
    idx = jnp.arange(C, dtype=jnp.float32)
    diff = idx[:, None] - idx[None, :]
    decay = jnp.where(diff[None] >= 0, jnp.exp(jnp.maximum(diff, 0.0)[None] * log_gamma[:, None, None]), 0.0)
    zeta = jnp.exp((C - 1 - idx)[None, :] * log_gamma[:, None])
    xi = jnp.exp((idx + 1)[None, :] * log_gamma[:, None])
    gamma_c = jnp.exp(C * log_gamma)
    scores = jnp.einsum('bhncd,bhnmd->bhncm', qc, kc) * decay[None, :, None]
    inner = jnp.einsum('bhncm,bhnme->bhnce', scores, vc)
    chunk_kv = jnp.einsum('bhnmd,bhnme->bhnde', kc * zeta[None, :, None, :, None], vc)

    def step(state, kv_n):
        return gamma_c[None, :, None, None] * state + kv_n, state

    init = jnp.zeros((B, H, RET_DK, RET_DV), chunk_kv.dtype)
    _, prev_states = lax.scan(step, init, jnp.moveaxis(chunk_kv, 2, 0))
    prev_states = jnp.moveaxis(prev_states, 0, 2)
    cross = jnp.einsum('bhncd,bhnde->bhnce', qc, prev_states) * xi[None, :, None, :, None]
    y = (inner + cross).reshape(B, H, Lp, RET_DV)[:, :, P:]
    yf = y.astype(jnp.float32)
    mu = jnp.mean(yf, -1, keepdims=True)
    var = jnp.mean((yf - mu) ** 2, -1, keepdims=True)
    yn = ((yf - mu) * lax.rsqrt(var + EPS)).transpose(0, 2, 1, 3).reshape(B, L, H * RET_DV)
    yn = (yn * g_gn.astype(jnp.float32)).astype(gate.dtype)
    return jax.nn.silu(gate) * yn


def _mixer_sublayer(h, pos, g_mix, w_in, g_q_lora, w_uq, g_kv_lora, w_ukv, g_qk_q, g_qk_k,
                    w_mla_out, g_ret_gn, w_ret_out, w_mix_out):
    hn = _rmsnorm(h, g_mix)
    z = hn @ w_in
    c_q, c_kv, k_pe, r_q, r_k, r_v, r_g, za, zr = jnp.split(z, np.cumsum(IN_SIZES)[:-1].tolist(), axis=-1)
    y_a = _mla_branch(c_q, c_kv, k_pe, pos, g_q_lora, w_uq, g_kv_lora, w_ukv, g_qk_q, g_qk_k) @ w_mla_out
    y_r = _retention_branch(r_q, r_k, r_v, r_g, pos, g_ret_gn) @ w_ret_out
    merged = jax.nn.sigmoid(za) * y_a + jax.nn.sigmoid(zr) * y_r
    return merged @ w_mix_out


def _peer_routing(x, w_peer_q, keys_1, keys_2):
    T = x.shape[0]
    q = (x @ w_peer_q).reshape(T, PEER_HEADS, PEER_DQ)
    half = PEER_DQ // 2
    s1 = jnp.einsum('thd,kd->thk', q[..., :half], keys_1).astype(jnp.float32)
    s2 = jnp.einsum('thd,kd->thk', q[..., half:], keys_2).astype(jnp.float32)
    v1, i1 = lax.top_k(s1, PEER_TOPK)
    v2, i2 = lax.top_k(s2, PEER_TOPK)
    cand = (v1[..., :, None] + v2[..., None, :]).reshape(T, PEER_HEADS, PEER_TOPK * PEER_TOPK)
    sc, ci = lax.top_k(cand, PEER_TOPK)
    e1 = jnp.take_along_axis(i1, ci // PEER_TOPK, axis=-1)
    e2 = jnp.take_along_axis(i2, ci % PEER_TOPK, axis=-1)
    eidx = (e1 * PEER_NKEYS + e2).reshape(T, PEER_PAIRS)
    gates = jax.nn.softmax(sc, axis=-1).reshape(T, PEER_PAIRS)
    return eidx.astype(jnp.int32), gates


def _pack_expert_table(tab):
    bits = lax.bitcast_convert_type(tab.astype(jnp.bfloat16), jnp.uint16).astype(jnp.uint32)
    words = bits[:, :HALF_D] | (bits[:, HALF_D:] << 16)
    return lax.bitcast_convert_type(words, jnp.int32).reshape(tab.shape[0] * HALF_ROWS, LANES)


def _unpack_words(words):
    lo = lax.bitcast_convert_type(words << 16, jnp.float32)
    hi = lax.bitcast_convert_type(words & HI_MASK, jnp.float32)
    return lo, hi


def _peer_hidden_kernel(idx_ref, x_ref, gate_ref, tab_ref, sel_ref, w_ref, prod_ref, lane_sum_ref):
    ones = jnp.ones((SUBLANES, LANES), jnp.bfloat16)

    def token(t, carry):
        xt = x_ref[t]
        x_lo = xt[:HALF_ROWS]
        x_hi = xt[HALF_ROWS:]
        for k in range(PEER_PAIRS):
            row = pl.multiple_of(idx_ref[t, k], HALF_ROWS)
            lo, hi = _unpack_words(tab_ref[pl.ds(row, HALF_ROWS), :])
            prod_ref[k * HALF_ROWS:(k + 1) * HALF_ROWS, :] = lo * x_lo + hi * x_hi
        prods = prod_ref[...].astype(jnp.bfloat16)
        sums = lax.dot_general(ones, prods, (((1,), (1,)), ((), ())),
                               preferred_element_type=jnp.float32)
        lane_sum_ref[pl.ds(t, 1), :] = sums[0:1, :]
        return carry

    lax.fori_loop(0, x_ref.shape[0], token, 0)
    sums = lane_sum_ref[...]
    sums_hi = sums.astype(jnp.bfloat16)
    sums_lo = (sums - sums_hi.astype(jnp.float32)).astype(jnp.bfloat16)
    sel = sel_ref[...]
    s = (jnp.dot(sums_hi, sel, preferred_element_type=jnp.float32)
         + jnp.dot(sums_lo, sel, preferred_element_type=jnp.float32))
    act = 0.5 * s * (1.0 + lax.erf(s * (1.0 / math.sqrt(2.0))))
    w_ref[...] = gate_ref[...] * act


def _peer_output_kernel(idx_ref, w_ref, h_ref, tab_ref, out_ref):
    n_acc = 4

    def token(t, carry):
        acc_lo = [jnp.zeros((HALF_ROWS, LANES), jnp.float32) for _ in range(n_acc)]
        acc_hi = [jnp.zeros((HALF_ROWS, LANES), jnp.float32) for _ in range(n_acc)]
        for k in range(PEER_PAIRS):
            row = pl.multiple_of(idx_ref[t, k], HALF_ROWS)
            w = w_ref[t, k]
            lo, hi = _unpack_words(tab_ref[pl.ds(row, HALF_ROWS), :])
            acc_lo[k % n_acc] = acc_lo[k % n_acc] + w * lo
            acc_hi[k % n_acc] = acc_hi[k % n_acc] + w * hi
        y_lo = (acc_lo[0] + acc_lo[1]) + (acc_lo[2] + acc_lo[3])
        y_hi = (acc_hi[0] + acc_hi[1]) + (acc_hi[2] + acc_hi[3])
        out_ref[t] = h_ref[t] + jnp.concatenate([y_lo, y_hi], axis=0)
        return carry

    lax.fori_loop(0, h_ref.shape[0], token, 0)


def _peer_experts(h, xn, eidx, gates, peer_u, peer_v):
    T = h.shape[0]
    tb = PEER_TOKEN_BLOCK
    rows = eidx * HALF_ROWS
    u_tab = _pack_expert_table(peer_u)
    v_tab = _pack_expert_table(peer_v)
    n_sum = PEER_PAIRS * HALF_ROWS
    sel = (jnp.arange(n_sum)[:, None] // HALF_ROWS == jnp.arange(PEER_PAIRS)[None, :]).astype(jnp.bfloat16)
    params = pltpu.CompilerParams(dimension_semantics=("arbitrary",), vmem_limit_bytes=PEER_VMEM_LIMIT)
    smem_block = pl.BlockSpec((tb, PEER_PAIRS), lambda i: (i, 0), memory_space=pltpu.SMEM)
    vmem_rows = pl.BlockSpec((tb, SUBLANES, LANES), lambda i: (i, 0, 0))
    whole_vmem = pl.BlockSpec(memory_space=pltpu.VMEM)

    w = pl.pallas_call(
        _peer_hidden_kernel,
        out_shape=jax.ShapeDtypeStruct((T, PEER_PAIRS), jnp.float32),
        grid=(T // tb,),
        in_specs=[smem_block, vmem_rows, pl.BlockSpec((tb, PEER_PAIRS), lambda i: (i, 0)),
                  whole_vmem, whole_vmem],
        out_specs=pl.BlockSpec((tb, PEER_PAIRS), lambda i: (i, 0)),
        scratch_shapes=[pltpu.VMEM((n_sum, LANES), jnp.float32), pltpu.VMEM((tb, n_sum), jnp.float32)],
        compiler_params=params,
        name="peer_hidden",
    )(rows, xn.reshape(T, SUBLANES, LANES), gates, u_tab, sel)

    out = pl.pallas_call(
        _peer_output_kernel,
        out_shape=jax.ShapeDtypeStruct((T, SUBLANES, LANES), jnp.float32),
        grid=(T // tb,),
        in_specs=[smem_block, smem_block, vmem_rows, whole_vmem],
        out_specs=vmem_rows,
        compiler_params=params,
        name="peer_output",
    )(rows, w, h.reshape(T, SUBLANES, LANES), v_tab)
    return out.reshape(T, D_MODEL)


def kernel(x, meta_tokens, g_mix, w_in, g_q_lora, w_uq, g_kv_lora, w_ukv, g_qk_q, g_qk_k, w_mla_out, g_ret_gn, w_ret_out, w_mix_out, g_ffn, w_peer_q, peer_keys_1, peer_keys_2, peer_u, peer_v):
    B, S, D = x.shape
    meta = jnp.broadcast_to(meta_tokens.astype(x.dtype)[None], (B, N_META, D))
    h = jnp.concatenate([meta, x], axis=1)
    pos = jnp.arange(h.shape[1])
    h = h + _mixer_sublayer(h, pos, g_mix[0], w_in[0], g_q_lora[0], w_uq[0], g_kv_lora[0], w_ukv[0],
                            g_qk_q[0], g_qk_k[0], w_mla_out[0], g_ret_gn[0], w_ret_out[0], w_mix_out[0])
    h = h[:, N_META:].reshape(B * S, D)
    xn = _rmsnorm(h, g_ffn[0])
    eidx, gates = _peer_routing(xn, w_peer_q[0], peer_keys_1[0], peer_keys_2[0])
    out = _peer_experts(h, xn, eidx, gates, peer_u[0], peer_v[0])
    return out.reshape(B, S, D)
```

```python
import math

import jax
import jax.numpy as jnp
import numpy as np
from jax import lax
from jax.experimental import pallas as pl
from jax.experimental.pallas import tpu as pltpu

D_MODEL = 1024
N_META = 16
CHUNK = 128
ROPE_BASE = 10000.0
EPS = 1e-6
MLA_HEADS = 8
MLA_Q_LORA = 384
MLA_KV_LORA = 256
MLA_NOPE = 64
MLA_ROPE = 32
MLA_V = 64
MLA_QK = MLA_NOPE + MLA_ROPE
RET_HEADS = 8
RET_DK = 64
RET_DV = 128
PEER_HEADS = 8
PEER_NKEYS = 128
PEER_EXPERTS = PEER_NKEYS * PEER_NKEYS
PEER_DQ = 128
PEER_TOPK = 16
PEER_PAIRS = PEER_HEADS * PEER_TOPK
IN_SIZES = (MLA_Q_LORA, MLA_KV_LORA, MLA_ROPE, RET_HEADS * RET_DK, RET_HEADS * RET_DK,
            RET_HEADS * RET_DV, RET_HEADS * RET_DV, D_MODEL, D_MODEL)

LANES = 128
SUBLANES = 8
WORD_ROWS = D_MODEL // (2 * LANES)
GATHER_WORD_ROWS = PEER_PAIRS * WORD_ROWS
GATHER_ROWS = 2 * GATHER_WORD_ROWS
PEER_TOKEN_BLOCK = 128
ROUTE_TOKEN_BLOCK = 128
N_SLOTS = 2
PEER_VMEM_LIMIT = 56 * 1024 * 1024
NT_DIMS = (((1,), (1,)), ((), ()))


def _rmsnorm(x, g):
    xf = x.astype(jnp.float32)
    y = xf * lax.rsqrt(jnp.mean(xf * xf, axis=-1, keepdims=True) + EPS)
    return (y * g.astype(jnp.float32)).astype(x.dtype)


def _rope(x, pos):
    half = x.shape[-1] // 2
    inv = ROPE_BASE ** (-jnp.arange(half, dtype=jnp.float32) / half)
    ang = pos.astype(jnp.float32)[:, None] * inv[None, :]
    cos, sin = jnp.cos(ang), jnp.sin(ang)
    xf = x.astype(jnp.float32)
    x1, x2 = xf[..., :half], xf[..., half:]
    return jnp.concatenate([x1 * cos - x2 * sin, x1 * sin + x2 * cos], -1).astype(x.dtype)


def _causal_block_attention(q, k, v):
    B, H, L, dqk = q.shape
    nblk = -(-L // CHUNK)
    Lp = nblk * CHUNK
    qp = jnp.pad(q, ((0, 0), (0, 0), (0, Lp - L), (0, 0)))
    kpos = jnp.arange(L)
    scale = 1.0 / math.sqrt(dqk)

    def one_block(i):
        qb = lax.dynamic_slice_in_dim(qp, i * CHUNK, CHUNK, axis=2)
        s = jnp.einsum('bhqd,bhkd->bhqk', qb, k).astype(jnp.float32) * scale
        qpos = i * CHUNK + jnp.arange(CHUNK)
        s = jnp.where(kpos[None, :] <= qpos[:, None], s, -jnp.inf)
        p = jax.nn.softmax(s, axis=-1)
        return jnp.einsum('bhqk,bhkd->bhqd', p.astype(v.dtype), v)

    out = lax.map(one_block, jnp.arange(nblk))
    out = out.transpose(1, 2, 0, 3, 4).reshape(B, H, Lp, v.shape[-1])
    return out[:, :, :L]


def _mla_branch(c_q, c_kv, k_pe, pos, g_q_lora, w_uq, g_kv_lora, w_ukv, g_qk_q, g_qk_k):
    B, L, _ = c_q.shape
    q = (_rmsnorm(c_q, g_q_lora) @ w_uq).reshape(B, L, MLA_HEADS, MLA_QK)
    kv = (_rmsnorm(c_kv, g_kv_lora) @ w_ukv).reshape(B, L, MLA_HEADS, MLA_NOPE + MLA_V)
    k_nope, v = kv[..., :MLA_NOPE], kv[..., MLA_NOPE:]
    k = jnp.concatenate([k_nope, jnp.broadcast_to(k_pe[:, :, None, :], (B, L, MLA_HEADS, MLA_ROPE))], -1)
    q = _rmsnorm(q, g_qk_q).transpose(0, 2, 1, 3)
    k = _rmsnorm(k, g_qk_k).transpose(0, 2, 1, 3)
    q = jnp.concatenate([q[..., :MLA_NOPE], _rope(q[..., MLA_NOPE:], pos)], -1)
    k = jnp.concatenate([k[..., :MLA_NOPE], _rope(k[..., MLA_NOPE:], pos)], -1)
    o = _causal_block_attention(q, k, v.transpose(0, 2, 1, 3))
    return o.transpose(0, 2, 1, 3).reshape(B, L, MLA_HEADS * MLA_V)


def _retention_branch(q, k, v, gate, pos, g_gn):
    B, L, _ = q.shape
    H, C = RET_HEADS, CHUNK
    q = _rope(q.reshape(B, L, H, RET_DK).transpose(0, 2, 1, 3), pos)
    k = _rope(k.reshape(B, L, H, RET_DK).transpose(0, 2, 1, 3), pos) * (RET_DK ** -0.5)
    v = v.reshape(B, L, H, RET_DV).transpose(0, 2, 1, 3)
    P = (-N_META) % C
    Lp = L + P
    NC = Lp // C
    padf = lambda t: jnp.pad(t, ((0, 0), (0, 0), (P, 0), (0, 0))).reshape(B, H, NC, C, t.shape[-1])
    qc, kc, vc = padf(q), padf(k), padf(v)
    log_gamma = jnp.log(1.0 - 2.0 ** (-5.0 - jnp.arange(H, dtype=jnp.float32)))
    idx = jnp.arange(C, dtype=jnp.float32)
    diff = idx[:, None] - idx[None, :]
    decay = jnp.where(diff[None] >= 0, jnp.exp(jnp.maximum(diff, 0.0)[None] * log_gamma[:, None, None]), 0.0)
    zeta = jnp.exp((C - 1 - idx)[None, :] * log_gamma[:, None])
    xi = jnp.exp((idx + 1)[None, :] * log_gamma[:, None])
    gamma_c = jnp.exp(C * log_gamma)
    scores = jnp.einsum('bhncd,bhnmd->bhncm', qc, kc) * decay[None, :, None]
    inner = jnp.einsum('bhncm,bhnme->bhnce', scores, vc)
    chunk_kv = jnp.einsum('bhnmd,bhnme->bhnde', kc * zeta[None, :, None, :, None], vc)

    def step(state, kv_n):
        return gamma_c[None, :, None, None] * state + kv_n, state

    init = jnp.zeros((B, H, RET_DK, RET_DV), chunk_kv.dtype)
    _, prev_states = lax.scan(step, init, jnp.moveaxis(chunk_kv, 2, 0))
    prev_states = jnp.moveaxis(prev_states, 0, 2)
    cross = jnp.einsum('bhncd,bhnde->bhnce', qc, prev_states) * xi[None, :, None, :, None]
    y = (inner + cross).reshape(B, H, Lp, RET_DV)[:, :, P:]
    yf = y.astype(jnp.float32)
    mu = jnp.mean(yf, -1, keepdims=True)
    var = jnp.mean((yf - mu) ** 2, -1, keepdims=True)
    yn = ((yf - mu) * lax.rsqrt(var + EPS)).transpose(0, 2, 1, 3).reshape(B, L, H * RET_DV)
    yn = (yn * g_gn.astype(jnp.float32)).astype(gate.dtype)
    return jax.nn.silu(gate) * yn


def _mixer_sublayer(h, pos, g_mix, w_in, g_q_lora, w_uq, g_kv_lora, w_ukv, g_qk_q, g_qk_k,
                    w_mla_out, g_ret_gn, w_ret_out, w_mix_out):
    hn = _rmsnorm(h, g_mix)
    z = hn @ w_in
    c_q, c_kv, k_pe, r_q, r_k, r_v, r_g, za, zr = jnp.split(z, np.cumsum(IN_SIZES)[:-1].tolist(), axis=-1)
    y_a = _mla_branch(c_q, c_kv, k_pe, pos, g_q_lora, w_uq, g_kv_lora, w_ukv, g_qk_q, g_qk_k) @ w_mla_out
    y_r = _retention_branch(r_q, r_k, r_v, r_g, pos, g_ret_gn) @ w_ret_out
    merged = jax.nn.sigmoid(za) * y_a + jax.nn.sigmoid(zr) * y_r
    return merged @ w_mix_out


def _top_rows(s, count):
    n = s.shape[0]
    iota = lax.broadcasted_iota(jnp.int32, s.shape, 0)
    vals, idxs = [], []
    for _ in range(count):
        m = jnp.max(s, axis=0, keepdims=True)
        idx = jnp.min(jnp.where(s == m, iota, n), axis=0, keepdims=True)
        vals.append(m)
        idxs.append(idx)
        s = jnp.where(iota == idx, -jnp.inf, s)
    return vals, idxs


def _peer_route_kernel(h_ref, g_ref, wq_ref, k1_ref, k2_ref, xn_ref, rows_ref, gates_ref,
                       xb_ref, rows_scr, gates_scr):
    hd = pl.program_id(1)

    @pl.when(hd == 0)
    def _():
        x = h_ref[...]
        xn = x * lax.rsqrt(jnp.mean(x * x, axis=-1, keepdims=True) + EPS) * g_ref[...]
        xn_ref[...] = xn
        xb_ref[...] = xn.astype(jnp.bfloat16)

    q = jnp.dot(xb_ref[...], wq_ref[...], preferred_element_type=jnp.float32).astype(jnp.bfloat16)
    s1 = lax.dot_general(k1_ref[...], q, NT_DIMS, preferred_element_type=jnp.float32)
    s2 = lax.dot_general(k2_ref[...], q, NT_DIMS, preferred_element_type=jnp.float32)
    v1, i1 = _top_rows(s1, PEER_TOPK)
    v2, i2 = _top_rows(s2, PEER_TOPK)
    v2_all = jnp.concatenate(v2, axis=0)
    i2_all = jnp.concatenate(i2, axis=0)
    n_mid = SUBLANES
    cand = [v1[0] + v2_all] + [v1[a] + v2_all[:n_mid] for a in range(1, n_mid)]
    cand.append(jnp.concatenate(v1[n_mid:], axis=0) + v2[0])
    cand_rows = [i1[0] * PEER_NKEYS + i2_all] + [i1[a] * PEER_NKEYS + i2_all[:n_mid] for a in range(1, n_mid)]
    cand_rows.append(jnp.concatenate(i1[n_mid:], axis=0) * PEER_NKEYS + i2[0])
    cand = jnp.concatenate(cand, axis=0)
    cand_rows = jnp.concatenate(cand_rows, axis=0) * WORD_ROWS
    sc, ci = _top_rows(cand, PEER_TOPK)
    iota = lax.broadcasted_iota(jnp.int32, cand.shape, 0)
    rows = [jnp.max(jnp.where(iota == c, cand_rows, -1), axis=0, keepdims=True) for c in ci]
    ex = [jnp.exp(v - sc[0]) for v in sc]
    denom = ex[0]
    for e in ex[1:]:
        denom = denom + e
    base = pl.multiple_of(hd * PEER_TOPK, PEER_TOPK)
    rows_scr[pl.ds(base, PEER_TOPK), :] = jnp.concatenate(rows, axis=0)
    gates_scr[pl.ds(base, PEER_TOPK), :] = jnp.concatenate(ex, axis=0) / denom

    @pl.when(hd == PEER_HEADS - 1)
    def _():
        rows_ref[...] = rows_scr[...].T
        gates_ref[...] = gates_scr[...].T


def _peer_route(h, g_ffn, w_peer_q, keys_1, keys_2):
    T = h.shape[0]
    tb = ROUTE_TOKEN_BLOCK
    half = PEER_DQ // 2
    zeros = jnp.zeros((PEER_NKEYS, half), jnp.bfloat16)
    k1 = jnp.concatenate([keys_1.astype(jnp.bfloat16), zeros], axis=1)
    k2 = jnp.concatenate([zeros, keys_2.astype(jnp.bfloat16)], axis=1)
    keys_spec = pl.BlockSpec((PEER_NKEYS, PEER_DQ), lambda i, hd: (0, 0))
    pair_spec = pl.BlockSpec((tb, PEER_PAIRS), lambda i, hd: (i, 0))
    return pl.pallas_call(
        _peer_route_kernel,
        out_shape=(jax.ShapeDtypeStruct((T, D_MODEL), jnp.float32),
                   jax.ShapeDtypeStruct((T, PEER_PAIRS), jnp.int32),
                   jax.ShapeDtypeStruct((T, PEER_PAIRS), jnp.float32)),
        grid=(T // tb, PEER_HEADS),
        in_specs=[pl.BlockSpec((tb, D_MODEL), lambda i, hd: (i, 0)),
                  pl.BlockSpec((1, D_MODEL), lambda i, hd: (0, 0)),
                  pl.BlockSpec((D_MODEL, PEER_DQ), lambda i, hd: (0, hd)),
                  keys_spec, keys_spec],
        out_specs=(pl.BlockSpec((tb, D_MODEL), lambda i, hd: (i, 0)), pair_spec, pair_spec),
        scratch_shapes=[pltpu.VMEM((tb, D_MODEL), jnp.bfloat16),
                        pltpu.VMEM((PEER_PAIRS, tb), jnp.int32),
                        pltpu.VMEM((PEER_PAIRS, tb), jnp.float32)],
        compiler_params=pltpu.CompilerParams(dimension_semantics=("arbitrary", "arbitrary")),
        name="peer_route",
    )(h, g_ffn.reshape(1, D_MODEL), w_peer_q.astype(jnp.bfloat16), k1, k2)


def _pack_expert_table(tab):
    bits = lax.bitcast_convert_type(tab.astype(jnp.bfloat16), jnp.uint16).astype(jnp.uint32)
    bits = bits.reshape(tab.shape[0], WORD_ROWS, 2, LANES)
    words = bits[:, :, 0, :] | (bits[:, :, 1, :] << 16)
    return lax.bitcast_convert_type(words, jnp.int32).reshape(tab.shape[0] * WORD_ROWS, LANES)


def _gather_token_rows(idx_ref, tab_ref, dst_ref, t):
    idx_t = idx_ref.at[t]
    for k in range(PEER_PAIRS):
        row = pl.multiple_of(idx_t[k], WORD_ROWS)
        dst_ref[k * WORD_ROWS:(k + 1) * WORD_ROWS, :] = tab_ref[pl.ds(row, WORD_ROWS), :]


def _pipelined_tokens(n_tokens, gather, consume):
    assert n_tokens % N_SLOTS == 0
    for s in range(N_SLOTS):
        gather(s, s)

    def step(i, carry):
        t = N_SLOTS * i
        for s in range(N_SLOTS):
            consume(t + s, s)
        for s in range(N_SLOTS):
            gather(t + N_SLOTS + s, s)
        return carry

    lax.fori_loop(0, n_tokens // N_SLOTS - 1, step, 0)
    for s in range(N_SLOTS):
        consume(n_tokens - N_SLOTS + s, s)


def _diag_mask():
    lane = lax.broadcasted_iota(jnp.int32, (SUBLANES, GATHER_ROWS), 1)
    sub = lax.broadcasted_iota(jnp.int32, (SUBLANES, GATHER_ROWS), 0)
    return (lane & (SUBLANES - 1)) == sub


def _split_bf16(x):
    hi = x.astype(jnp.bfloat16)
    lo = (x - hi.astype(jnp.float32)).astype(jnp.bfloat16)
    return hi, lo


def _peer_hidden_kernel(idx_ref, x_ref, gate_ref, tab_ref, sel_ref, w_ref, diag_ref, *rows_refs):
    mask = _diag_mask()

    def gather(t, slot):
        _gather_token_rows(idx_ref, tab_ref, rows_refs[slot], t)

    def consume(t, slot):
        rows = pltpu.bitcast(rows_refs[slot][...], jnp.bfloat16)
        xb = x_ref[t].astype(jnp.bfloat16)
        d = lax.dot_general(xb, rows, NT_DIMS, preferred_element_type=jnp.float32)
        diag_ref[pl.ds(t, 1), :] = jnp.sum(jnp.where(mask, d, 0.0), axis=0, keepdims=True)

    _pipelined_tokens(x_ref.shape[0], gather, consume)
    hi, lo = _split_bf16(diag_ref[...])
    sel = sel_ref[...]
    s = (jnp.dot(hi, sel, preferred_element_type=jnp.float32)
         + jnp.dot(lo, sel, preferred_element_type=jnp.float32))
    act = 0.5 * s * (1.0 + lax.erf(s * (1.0 / math.sqrt(2.0))))
    w_ref[...] = gate_ref[...] * act


def _peer_output_kernel(idx_ref, w_ref, h_ref, tab_ref, expand_ref, out_ref, whi_ref, wlo_ref, *rows_refs):
    mask = _diag_mask()
    hi, lo = _split_bf16(w_ref[...])
    expand = expand_ref[...]
    whi_ref[...] = jnp.dot(hi, expand, preferred_element_type=jnp.float32)
    wlo_ref[...] = jnp.dot(lo, expand, preferred_element_type=jnp.float32)

    def gather(t, slot):
        _gather_token_rows(idx_ref, tab_ref, rows_refs[slot], t)

    def consume(t, slot):
        rows = pltpu.bitcast(rows_refs[slot][...], jnp.bfloat16)
        parts = []
        for w_part in (whi_ref, wlo_ref):
            w_row = jnp.broadcast_to(w_part[pl.ds(t, 1), :], (SUBLANES, GATHER_ROWS))
            parts.append(jnp.where(mask, w_row, 0.0))
        lhs = jnp.concatenate(parts, axis=0).astype(jnp.bfloat16)
        y = jnp.dot(lhs, rows, preferred_element_type=jnp.float32)
        out_ref[t] = h_ref[t] + (y[:SUBLANES] + y[SUBLANES:])

    _pipelined_tokens(h_ref.shape[0], gather, consume)


def _peer_experts(h, xn, rows, gates, peer_u, peer_v):
    T = h.shape[0]
    tb = PEER_TOKEN_BLOCK
    u_tab = _pack_expert_table(peer_u)
    v_tab = _pack_expert_table(peer_v)
    expand = (jnp.arange(PEER_PAIRS)[:, None] == jnp.arange(GATHER_ROWS)[None, :] // SUBLANES
              ).astype(jnp.bfloat16)
    params = pltpu.CompilerParams(dimension_semantics=("arbitrary",), vmem_limit_bytes=PEER_VMEM_LIMIT)
    smem_block = pl.BlockSpec((tb, PEER_PAIRS), lambda i: (i, 0), memory_space=pltpu.SMEM)
    pair_block = pl.BlockSpec((tb, PEER_PAIRS), lambda i: (i, 0))
    vmem_rows = pl.BlockSpec((tb, SUBLANES, LANES), lambda i: (i, 0, 0))
    whole_vmem = pl.BlockSpec(memory_space=pltpu.VMEM)
    rows_scratch = pltpu.VMEM((GATHER_WORD_ROWS, LANES), jnp.int32)
    wide_scratch = pltpu.VMEM((tb, GATHER_ROWS), jnp.float32)

    w = pl.pallas_call(
        _peer_hidden_kernel,
        out_shape=jax.ShapeDtypeStruct((T, PEER_PAIRS), jnp.float32),
        grid=(T // tb,),
        in_specs=[smem_block, vmem_rows, pair_block, whole_vmem, whole_vmem],
        out_specs=pair_block,
        scratch_shapes=[wide_scratch] + [rows_scratch] * N_SLOTS,
        compiler_params=params,
        name="peer_hidden",
    )(rows, xn.reshape(T, SUBLANES, LANES), gates, u_tab, expand.T)

    out = pl.pallas_call(
        _peer_output_kernel,
        out_shape=jax.ShapeDtypeStruct((T, SUBLANES, LANES), jnp.float32),
        grid=(T // tb,),
        in_specs=[smem_block, pair_block, vmem_rows, whole_vmem, whole_vmem],
        out_specs=vmem_rows,
        scratch_shapes=[wide_scratch, wide_scratch] + [rows_scratch] * N_SLOTS,
        compiler_params=params,
        name="peer_output",
    )(rows, w, h.reshape(T, SUBLANES, LANES), v_tab, expand)
    return out.reshape(T, D_MODEL)


def kernel(x, meta_tokens, g_mix, w_in, g_q_lora, w_uq, g_kv_lora, w_ukv, g_qk_q, g_qk_k, w_mla_out, g_ret_gn, w_ret_out, w_mix_out, g_ffn, w_peer_q, peer_keys_1, peer_keys_2, peer_u, peer_v):
    B, S, D = x.shape
    meta = jnp.broadcast_to(meta_tokens.astype(x.dtype)[None], (B, N_META, D))
    h = jnp.concatenate([meta, x], axis=1)
    pos = jnp.arange(h.shape[1])
    h = h + _mixer_sublayer(h, pos, g_mix[0], w_in[0], g_q_lora[0], w_uq[0], g_kv_lora[0], w_ukv[0],
                            g_qk_q[0], g_qk_k[0], w_mla_out[0], g_ret_gn[0], w_ret_out[0], w_mix_out[0])
    h = h[:, N_META:].reshape(B * S, D)
    xn, rows, gates = _peer_route(h, g_ffn[0], w_peer_q[0], peer_keys_1[0], peer_keys_2[0])
    out = _peer_experts(h, xn, rows, gates, peer_u[0], peer_v[0])
    return out.reshape(B, S, D)
```

```python
import math

import jax
import jax.numpy as jnp
import numpy as np
from jax import lax
from jax.experimental import pallas as pl
from jax.experimental.pallas import tpu as pltpu

D_MODEL = 1024
N_META = 16
CHUNK = 128
ROPE_BASE = 10000.0
EPS = 1e-6
MLA_HEADS = 8
MLA_Q_LORA = 384
MLA_KV_LORA = 256
MLA_NOPE = 64
MLA_ROPE = 32
MLA_V = 64
MLA_QK = MLA_NOPE + MLA_ROPE
RET_HEADS = 8
RET_DK = 64
RET_DV = 128
PEER_HEADS = 8
PEER_NKEYS = 128
PEER_EXPERTS = PEER_NKEYS * PEER_NKEYS
PEER_DQ = 128
PEER_TOPK = 16
PEER_PAIRS = PEER_HEADS * PEER_TOPK
IN_SIZES = (MLA_Q_LORA, MLA_KV_LORA, MLA_ROPE, RET_HEADS * RET_DK, RET_HEADS * RET_DK,
            RET_HEADS * RET_DV, RET_HEADS * RET_DV, D_MODEL, D_MODEL)

LANES = 128
SUBLANES = 8
WORD_ROWS = D_MODEL // (2 * LANES)
GATHER_WORD_ROWS = PEER_PAIRS * WORD_ROWS
GATHER_ROWS = 2 * GATHER_WORD_ROWS
PEER_TOKEN_BLOCK = 128
ROUTE_TOKEN_BLOCK = 128
N_SLOTS = 2
PEER_VMEM_LIMIT = 56 * 1024 * 1024
NT_DIMS = (((1,), (1,)), ((), ()))


def _rmsnorm(x, g):
    xf = x.astype(jnp.float32)
    y = xf * lax.rsqrt(jnp.mean(xf * xf, axis=-1, keepdims=True) + EPS)
    return (y * g.astype(jnp.float32)).astype(x.dtype)


def _rope(x, pos):
    half = x.shape[-1] // 2
    inv = ROPE_BASE ** (-jnp.arange(half, dtype=jnp.float32) / half)
    ang = pos.astype(jnp.float32)[:, None] * inv[None, :]
    cos, sin = jnp.cos(ang), jnp.sin(ang)
    xf = x.astype(jnp.float32)
    x1, x2 = xf[..., :half], xf[..., half:]
    return jnp.concatenate([x1 * cos - x2 * sin, x1 * sin + x2 * cos], -1).astype(x.dtype)


def _causal_block_attention(q, k, v):
    B, H, L, dqk = q.shape
    nblk = -(-L // CHUNK)
    Lp = nblk * CHUNK
    qp = jnp.pad(q, ((0, 0), (0, 0), (0, Lp - L), (0, 0)))
    kpos = jnp.arange(L)
    scale = 1.0 / math.sqrt(dqk)

    def one_block(i):
        qb = lax.dynamic_slice_in_dim(qp, i * CHUNK, CHUNK, axis=2)
        s = jnp.einsum('bhqd,bhkd->bhqk', qb, k).astype(jnp.float32) * scale
        qpos = i * CHUNK + jnp.arange(CHUNK)
        s = jnp.where(kpos[None, :] <= qpos[:, None], s, -jnp.inf)
        p = jax.nn.softmax(s, axis=-1)
        return jnp.einsum('bhqk,bhkd->bhqd', p.astype(v.dtype), v)

    out = lax.map(one_block, jnp.arange(nblk))
    out = out.transpose(1, 2, 0, 3, 4).reshape(B, H, Lp, v.shape[-1])
    return out[:, :, :L]


def _mla_branch(c_q, c_kv, k_pe, pos, g_q_lora, w_uq, g_kv_lora, w_ukv, g_qk_q, g_qk_k):
    B, L, _ = c_q.shape
    q = (_rmsnorm(c_q, g_q_lora) @ w_uq).reshape(B, L, MLA_HEADS, MLA_QK)
    kv = (_rmsnorm(c_kv, g_kv_lora) @ w_ukv).reshape(B, L, MLA_HEADS, MLA_NOPE + MLA_V)
    k_nope, v = kv[..., :MLA_NOPE], kv[..., MLA_NOPE:]
    k = jnp.concatenate([k_nope, jnp.broadcast_to(k_pe[:, :, None, :], (B, L, MLA_HEADS, MLA_ROPE))], -1)
    q = _rmsnorm(q, g_qk_q).transpose(0, 2, 1, 3)
    k = _rmsnorm(k, g_qk_k).transpose(0, 2, 1, 3)
    q = jnp.concatenate([q[..., :MLA_NOPE], _rope(q[..., MLA_NOPE:], pos)], -1)
    k = jnp.concatenate([k[..., :MLA_NOPE], _rope(k[..., MLA_NOPE:], pos)], -1)
    o = _causal_block_attention(q, k, v.transpose(0, 2, 1, 3))
    return o.transpose(0, 2, 1, 3).reshape(B, L, MLA_HEADS * MLA_V)


def _retention_branch(q, k, v, gate, pos, g_gn):
    B, L, _ = q.shape
    H, C = RET_HEADS, CHUNK
    q = _rope(q.reshape(B, L, H, RET_DK).transpose(0, 2, 1, 3), pos)
    k = _rope(k.reshape(B, L, H, RET_DK).transpose(0, 2, 1, 3), pos) * (RET_DK ** -0.5)
    v = v.reshape(B, L, H, RET_DV).transpose(0, 2, 1, 3)
    P = (-N_META) % C
    Lp = L + P
    NC = Lp // C
    padf = lambda t: jnp.pad(t, ((0, 0), (0, 0), (P, 0), (0, 0))).reshape(B, H, NC, C, t.shape[-1])
    qc, kc, vc = padf(q), padf(k), padf(v)
    log_gamma = jnp.log(1.0 - 2.0 ** (-5.0 - jnp.arange(H, dtype=jnp.float32)))
    idx = jnp.arange(C, dtype=jnp.float32)
    diff = idx[:, None] - idx[None, :]
    decay = jnp.where(diff[None] >= 0, jnp.exp(jnp.maximum(diff, 0.0)[None] * log_gamma[:, None, None]), 0.0)
    zeta = jnp.exp((C - 1 - idx)[None, :] * log_gamma[:, None])
    xi = jnp.exp((idx + 1)[None, :] * log_gamma[:, None])
    gamma_c = jnp.exp(C * log_gamma)
    scores = jnp.einsum('bhncd,bhnmd->bhncm', qc, kc) * decay[None, :, None]
    inner = jnp.einsum('bhncm,bhnme->bhnce', scores, vc)
    chunk_kv = jnp.einsum('bhnmd,bhnme->bhnde', kc * zeta[None, :, None, :, None], vc)

    def step(state, kv_n):
        return gamma_c[None, :, None, None] * state + kv_n, state

    init = jnp.zeros((B, H, RET_DK, RET_DV), chunk_kv.dtype)
    _, prev_states = lax.scan(step, init, jnp.moveaxis(chunk_kv, 2, 0))
    prev_states = jnp.moveaxis(prev_states, 0, 2)
    cross = jnp.einsum('bhncd,bhnde->bhnce', qc, prev_states) * xi[None, :, None, :, None]
    y = (inner + cross).reshape(B, H, Lp, RET_DV)[:, :, P:]
    yf = y.astype(jnp.float32)
    mu = jnp.mean(yf, -1, keepdims=True)
    var = jnp.mean((yf - mu) ** 2, -1, keepdims=True)
    yn = ((yf - mu) * lax.rsqrt(var + EPS)).transpose(0, 2, 1, 3).reshape(B, L, H * RET_DV)
    yn = (yn * g_gn.astype(jnp.float32)).astype(gate.dtype)
    return jax.nn.silu(gate) * yn


def _mixer_sublayer(h, pos, g_mix, w_in, g_q_lora, w_uq, g_kv_lora, w_ukv, g_qk_q, g_qk_k,
                    w_mla_out, g_ret_gn, w_ret_out, w_mix_out):
    hn = _rmsnorm(h, g_mix)
    z = hn @ w_in
    c_q, c_kv, k_pe, r_q, r_k, r_v, r_g, za, zr = jnp.split(z, np.cumsum(IN_SIZES)[:-1].tolist(), axis=-1)
    y_a = _mla_branch(c_q, c_kv, k_pe, pos, g_q_lora, w_uq, g_kv_lora, w_ukv, g_qk_q, g_qk_k) @ w_mla_out
    y_r = _retention_branch(r_q, r_k, r_v, r_g, pos, g_ret_gn) @ w_ret_out
    merged = jax.nn.sigmoid(za) * y_a + jax.nn.sigmoid(zr) * y_r
    return merged @ w_mix_out


PAD_ROWS = (-N_META) % CHUNK
HEAD_LANES = 128
MLA_WIDE = MLA_HEADS * HEAD_LANES
RET_QK_WIDE = RET_HEADS * RET_DK
RET_V_WIDE = RET_HEADS * RET_DV
MASK_VALUE = -1e30
MIXER_VMEM_LIMIT = 48 * 1024 * 1024
ATTN_HEAD_GROUP = 4
ATTN_TILE = 3 * CHUNK
_SEG_SIZES = (MLA_Q_LORA, MLA_KV_LORA, HEAD_LANES, RET_QK_WIDE, RET_QK_WIDE, RET_V_WIDE, RET_V_WIDE,
              D_MODEL, D_MODEL)
_SEG_STARTS = tuple(int(v) for v in np.cumsum((0,) + _SEG_SIZES))


def _rope_tables(n_rows, dim, block_offset, repeat):
    half = dim // 2
    pos = jnp.arange(n_rows, dtype=jnp.float32) - PAD_ROWS
    inv = ROPE_BASE ** (-jnp.arange(half, dtype=jnp.float32) / half)
    lane = np.arange(LANES)
    rel = (lane - block_offset) % repeat
    in_rope = (lane >= block_offset) & (rel < dim)
    ang = pos[:, None] * inv[rel % half][None, :]
    cos = jnp.where(in_rope[None, :], jnp.cos(ang), 1.0)
    sin = jnp.sin(ang)
    s1 = jnp.where((in_rope & (rel < half))[None, :], -sin, 0.0)
    s2 = jnp.where((in_rope & (rel >= half))[None, :], sin, 0.0)
    return cos, s1, s2


def _apply_rope(x, cos, s1, s2, half):
    return x * cos + pltpu.roll(x, LANES - half, 1) * s1 + pltpu.roll(x, half, 1) * s2


def _mixer_in_kernel(h_ref, gmix_ref, win_ref, gq_ref, wuq_ref, gkv_ref, wk_ref, wv_ref, gqkq_ref, gqkk_ref,
                     mc_ref, ms1_ref, ms2_ref, rc_ref, rs1_ref, rs2_ref,
                     q_ref, k_ref, v_ref, rq_ref, rk_ref, rv_ref, sg_ref, ga_ref, gr_ref):
    x = h_ref[...]
    hn = (x * lax.rsqrt(jnp.mean(x * x, axis=-1, keepdims=True) + EPS) * gmix_ref[...]).astype(jnp.bfloat16)

    def seg(i):
        return jnp.dot(hn, win_ref[:, _SEG_STARTS[i]:_SEG_STARTS[i + 1]], preferred_element_type=jnp.float32)

    def lora_norm(c, g_ref):
        return (c * lax.rsqrt(jnp.mean(c * c, axis=-1, keepdims=True) + EPS) * g_ref[...]).astype(jnp.bfloat16)

    cq = lora_norm(seg(0), gq_ref)
    ckv = lora_norm(seg(1), gkv_ref)
    kpe = seg(2)
    q = jnp.dot(cq, wuq_ref[...], preferred_element_type=jnp.float32)
    kn = jnp.dot(ckv, wk_ref[...], preferred_element_type=jnp.float32)
    v_ref[...] = jnp.dot(ckv, wv_ref[...], preferred_element_type=jnp.float32).astype(jnp.bfloat16)
    mc, ms1, ms2 = mc_ref[...], ms1_ref[...], ms2_ref[...]
    scale = 1.0 / math.sqrt(MLA_QK)

    def qk_norm_rope(t, g_ref):
        t = t * lax.rsqrt(jnp.sum(t * t, axis=-1, keepdims=True) * (1.0 / MLA_QK) + EPS) * g_ref[...]
        return _apply_rope(t, mc, ms1, ms2, MLA_ROPE // 2)

    for hd in range(MLA_HEADS):
        blk = slice(hd * HEAD_LANES, (hd + 1) * HEAD_LANES)
        q_ref[:, blk] = (qk_norm_rope(q[:, blk], gqkq_ref) * scale).astype(jnp.bfloat16)
        k_ref[:, blk] = qk_norm_rope(kn[:, blk] + kpe, gqkk_ref).astype(jnp.bfloat16)

    rc, rs1, rs2 = rc_ref[...], rs1_ref[...], rs2_ref[...]
    rq = seg(3)
    rk = seg(4)
    for j in range(RET_QK_WIDE // LANES):
        blk = slice(j * LANES, (j + 1) * LANES)
        rq_ref[:, blk] = _apply_rope(rq[:, blk], rc, rs1, rs2, RET_DK // 2).astype(jnp.bfloat16)
        rk_ref[:, blk] = (_apply_rope(rk[:, blk], rc, rs1, rs2, RET_DK // 2) * (RET_DK ** -0.5)).astype(jnp.bfloat16)
    rv_ref[...] = seg(5).astype(jnp.bfloat16)
    gate = seg(6)
    sg_ref[...] = (gate * jax.nn.sigmoid(gate)).astype(jnp.bfloat16)
    ga_ref[...] = jax.nn.sigmoid(seg(7)).astype(jnp.bfloat16)
    gr_ref[...] = jax.nn.sigmoid(seg(8)).astype(jnp.bfloat16)


def _attention_kernel(q_ref, k_ref, v_ref, o_ref, m_ref, l_ref, acc_ref):
    qi = pl.program_id(2)
    t = ATTN_TILE
    n_heads = q_ref.shape[1] // HEAD_LANES
    m_ref[...] = jnp.full(m_ref.shape, MASK_VALUE, jnp.float32)
    l_ref[...] = jnp.zeros(l_ref.shape, jnp.float32)
    acc_ref[...] = jnp.zeros(acc_ref.shape, jnp.float32)

    def step(j, masked):
        start = pl.multiple_of(j * t, t)
        if masked:
            r = lax.broadcasted_iota(jnp.int32, (t, t), 0) + qi * t
            c = lax.broadcasted_iota(jnp.int32, (t, t), 1) + j * t
            allowed = (c <= r) & ((c >= PAD_ROWS) | (r < PAD_ROWS))
        for hd in range(n_heads):
            blk = slice(hd * HEAD_LANES, (hd + 1) * HEAD_LANES)
            kj = k_ref[pl.ds(start, t), blk]
            vj = v_ref[pl.ds(start, t), blk]
            s = lax.dot_general(q_ref[:, blk], kj, NT_DIMS, preferred_element_type=jnp.float32)
            if masked:
                s = jnp.where(allowed, s, MASK_VALUE)
            m = m_ref[hd]
            m_new = jnp.maximum(m, jnp.max(s, axis=-1, keepdims=True))
            alpha = jnp.exp(m - m_new)
            p = jnp.exp(s - m_new)
            m_ref[hd] = m_new
            l_ref[hd] = alpha * l_ref[hd] + jnp.sum(p, axis=-1, keepdims=True)
            acc_ref[hd] = alpha * acc_ref[hd] + jnp.dot(p.astype(jnp.bfloat16), vj,
                                                        preferred_element_type=jnp.float32)

    def body(j, carry):
        edge = (j == 0) | (j == qi)

        @pl.when(edge)
        def _():
            step(j, True)

        @pl.when(jnp.logical_not(edge))
        def _():
            step(j, False)

        return carry

    lax.fori_loop(0, qi + 1, body, 0)
    for hd in range(n_heads):
        blk = slice(hd * HEAD_LANES, (hd + 1) * HEAD_LANES)
        o_ref[:, blk] = (acc_ref[hd] / l_ref[hd]).astype(jnp.bfloat16)


def _retention_kernel(q_ref, k_ref, v_ref, sg_ref, gn_ref, decay_ref, zeta_ref, xi_ref, gamma_ref, o_ref):
    n_chunks = q_ref.shape[0] // CHUNK
    lane = lax.broadcasted_iota(jnp.int32, (CHUNK, LANES), 1)
    row = lax.broadcasted_iota(jnp.int32, (LANES, RET_DV), 0)
    is_a = lane < RET_DK
    gn = gn_ref[...]
    contract_rows = (((0,), (0,)), ((), ()))

    def group_norm(y):
        mu = jnp.mean(y, axis=-1, keepdims=True)
        d = y - mu
        return d * lax.rsqrt(jnp.mean(d * d, axis=-1, keepdims=True) + EPS)

    def chunk(n, state):
        rows = pl.ds(pl.multiple_of(n * CHUNK, CHUNK), CHUNK)
        qb = q_ref[rows, :]
        kb = k_ref[rows, :]
        state_bf = state.astype(jnp.bfloat16)
        new_parts = []
        for hd in range(2):
            qh = jnp.where(is_a if hd == 0 else ~is_a, qb, jnp.zeros_like(qb))
            vh = v_ref[rows, hd * RET_DV:(hd + 1) * RET_DV]
            scores = lax.dot_general(qh, kb, NT_DIMS, preferred_element_type=jnp.float32) * decay_ref[hd]
            inner = jnp.dot(scores.astype(jnp.bfloat16), vh, preferred_element_type=jnp.float32)
            cross = jnp.dot(qh, state_bf, preferred_element_type=jnp.float32) * xi_ref[hd]
            yn = group_norm(inner + cross) * gn[:, hd * RET_DV:(hd + 1) * RET_DV]
            sg = sg_ref[rows, hd * RET_DV:(hd + 1) * RET_DV].astype(jnp.float32)
            o_ref[rows, hd * RET_DV:(hd + 1) * RET_DV] = (sg * yn).astype(jnp.bfloat16)
            kz = (kb.astype(jnp.float32) * zeta_ref[hd]).astype(jnp.bfloat16)
            new_parts.append(lax.dot_general(kz, vh, contract_rows, preferred_element_type=jnp.float32))
        return gamma_ref[...] * state + jnp.where(row < RET_DK, new_parts[0], new_parts[1])

    lax.fori_loop(0, n_chunks, chunk, jnp.zeros((LANES, RET_DV), jnp.float32))


def _mixer_out_kernel(h_ref, o_ref, yr_ref, ga_ref, gr_ref, wa_ref, wr_ref, wm_ref, out_ref):
    y_a = jnp.dot(o_ref[...], wa_ref[...], preferred_element_type=jnp.float32)
    y_r = jnp.dot(yr_ref[...], wr_ref[...], preferred_element_type=jnp.float32)
    merged = ga_ref[...].astype(jnp.float32) * y_a + gr_ref[...].astype(jnp.float32) * y_r
    out_ref[...] = h_ref[...] + jnp.dot(merged.astype(jnp.bfloat16), wm_ref[...],
                                        preferred_element_type=jnp.float32)


def _pad_heads(w, heads, width):
    lead = w.shape[:-1]
    w = w.reshape(lead + (heads, width))
    w = jnp.pad(w, [(0, 0)] * len(lead) + [(0, 0), (0, HEAD_LANES - width)])
    return w.reshape(lead + (heads * HEAD_LANES,))


def _mixer(x, meta_tokens, g_mix, w_in, g_q_lora, w_uq, g_kv_lora, w_ukv, g_qk_q, g_qk_k,
           w_mla_out, g_ret_gn, w_ret_out, w_mix_out):
    B, S, D = x.shape
    bf = jnp.bfloat16
    rows = PAD_ROWS + N_META + S
    nt = rows // CHUNK
    hp = jnp.concatenate([jnp.zeros((B, PAD_ROWS, D), x.dtype),
                          jnp.broadcast_to(meta_tokens.astype(x.dtype)[None], (B, N_META, D)), x], axis=1)
    hp = hp.reshape(B * rows, D)

    segs = jnp.split(w_in, np.cumsum(IN_SIZES)[:-1].tolist(), axis=1)
    kpe_cols = jnp.pad(segs[2], ((0, 0), (MLA_NOPE, HEAD_LANES - MLA_NOPE - MLA_ROPE)))
    win = jnp.concatenate(segs[:2] + [kpe_cols] + segs[3:], axis=1).astype(bf)
    wuq = _pad_heads(w_uq, MLA_HEADS, MLA_QK).astype(bf)
    wkv = w_ukv.reshape(MLA_KV_LORA, MLA_HEADS, MLA_NOPE + MLA_V)
    wk = _pad_heads(wkv[:, :, :MLA_NOPE].reshape(MLA_KV_LORA, -1), MLA_HEADS, MLA_NOPE).astype(bf)
    wv = _pad_heads(wkv[:, :, MLA_NOPE:].reshape(MLA_KV_LORA, -1), MLA_HEADS, MLA_V).astype(bf)
    gqkq = jnp.pad(g_qk_q, (0, HEAD_LANES - MLA_QK)).reshape(1, HEAD_LANES)
    gqkk = jnp.pad(g_qk_k, (0, HEAD_LANES - MLA_QK)).reshape(1, HEAD_LANES)
    wa = _pad_heads(w_mla_out.T, MLA_HEADS, MLA_V).T.astype(bf)
    mla_tabs = _rope_tables(rows, MLA_ROPE, MLA_NOPE, HEAD_LANES)
    ret_tabs = _rope_tables(rows, RET_DK, 0, RET_DK)

    row_tile = lambda width: pl.BlockSpec((CHUNK, width), lambda b, i: (b * nt + i, 0))
    const = lambda shape: pl.BlockSpec(shape, lambda b, i: (0,) * len(shape))
    tab = pl.BlockSpec((CHUNK, LANES), lambda b, i: (i, 0))
    n_rows = B * rows
    widths = (MLA_WIDE, MLA_WIDE, MLA_WIDE, RET_QK_WIDE, RET_QK_WIDE, RET_V_WIDE, RET_V_WIDE, D, D)
    q, k, v, rq, rk, rv, sg, ga, gr = pl.pallas_call(
        _mixer_in_kernel,
        out_shape=tuple(jax.ShapeDtypeStruct((n_rows, w), bf) for w in widths),
        grid=(B, nt),
        in_specs=[row_tile(D), const((1, D)), const(win.shape), const((1, MLA_Q_LORA)), const(wuq.shape),
                  const((1, MLA_KV_LORA)), const(wk.shape), const(wv.shape), const((1, HEAD_LANES)),
                  const((1, HEAD_LANES))] + [tab] * 6,
        out_specs=tuple(row_tile(w) for w in widths),
        compiler_params=pltpu.CompilerParams(dimension_semantics=("arbitrary", "arbitrary"),
                                             vmem_limit_bytes=MIXER_VMEM_LIMIT),
        name="mixer_in",
    )(hp, g_mix.reshape(1, D), win, g_q_lora.reshape(1, -1), wuq, g_kv_lora.reshape(1, -1), wk, wv,
      gqkq, gqkk, *mla_tabs, *ret_tabs)

    group = ATTN_HEAD_GROUP * HEAD_LANES
    seq_spec = pl.BlockSpec((rows, group), lambda b, g, i: (b, g))
    n_attn = rows // ATTN_TILE
    tile_spec = pl.BlockSpec((ATTN_TILE, group), lambda b, g, i: (b * n_attn + i, g))
    o = pl.pallas_call(
        _attention_kernel,
        out_shape=jax.ShapeDtypeStruct((n_rows, MLA_WIDE), bf),
        grid=(B, MLA_HEADS // ATTN_HEAD_GROUP, n_attn),
        in_specs=[tile_spec, seq_spec, seq_spec],
        out_specs=tile_spec,
        scratch_shapes=[pltpu.VMEM((ATTN_HEAD_GROUP, ATTN_TILE, 1), jnp.float32),
                        pltpu.VMEM((ATTN_HEAD_GROUP, ATTN_TILE, 1), jnp.float32),
                        pltpu.VMEM((ATTN_HEAD_GROUP, ATTN_TILE, HEAD_LANES), jnp.float32)],
        compiler_params=pltpu.CompilerParams(dimension_semantics=("arbitrary",) * 3,
                                             vmem_limit_bytes=MIXER_VMEM_LIMIT),
        name="mla_attention",
    )(q, k, v)

    log_gamma = jnp.log(1.0 - 2.0 ** (-5.0 - jnp.arange(RET_HEADS, dtype=jnp.float32)))
    idx = jnp.arange(CHUNK, dtype=jnp.float32)
    diff = idx[:, None] - idx[None, :]
    decay = jnp.where(diff[None] >= 0, jnp.exp(jnp.maximum(diff, 0.0)[None] * log_gamma[:, None, None]), 0.0)
    zeta = jnp.exp((CHUNK - 1 - idx)[None, :] * log_gamma[:, None])
    xi = jnp.exp((idx + 1)[None, :] * log_gamma[:, None])
    gamma_c = jnp.exp(CHUNK * log_gamma)
    zeta_b = jnp.broadcast_to(zeta[:, :, None], (RET_HEADS, CHUNK, LANES))
    xi_b = jnp.broadcast_to(xi[:, :, None], (RET_HEADS, CHUNK, RET_DV))
    gamma_rows = jnp.broadcast_to(jnp.repeat(gamma_c, RET_DK).reshape(RET_HEADS // 2, LANES, 1),
                                  (RET_HEADS // 2, LANES, RET_DV))
    pair_seq = lambda width: pl.BlockSpec((rows, width), lambda b, p: (b, p))
    pair_tab = lambda d1, d2: pl.BlockSpec((2, d1, d2), lambda b, p: (p, 0, 0))
    yr = pl.pallas_call(
        _retention_kernel,
        out_shape=jax.ShapeDtypeStruct((n_rows, RET_V_WIDE), bf),
        grid=(B, RET_HEADS // 2),
        in_specs=[pair_seq(LANES), pair_seq(LANES), pair_seq(2 * RET_DV), pair_seq(2 * RET_DV),
                  pl.BlockSpec((1, 2 * RET_DV), lambda b, p: (0, p)),
                  pair_tab(CHUNK, CHUNK), pair_tab(CHUNK, LANES), pair_tab(CHUNK, RET_DV),
                  pl.BlockSpec((None, LANES, RET_DV), lambda b, p: (p, 0, 0))],
        out_specs=pair_seq(2 * RET_DV),
        compiler_params=pltpu.CompilerParams(dimension_semantics=("arbitrary", "arbitrary"),
                                             vmem_limit_bytes=MIXER_VMEM_LIMIT),
        name="retention",
    )(rq, rk, rv, sg, g_ret_gn.reshape(1, -1), decay, zeta_b, xi_b, gamma_rows)

    seq_tiles = S // CHUNK
    first = (PAD_ROWS + N_META) // CHUNK
    in_tile = lambda width: pl.BlockSpec((CHUNK, width), lambda b, i: (b * nt + first + i, 0))
    return pl.pallas_call(
        _mixer_out_kernel,
        out_shape=jax.ShapeDtypeStruct((B * S, D), jnp.float32),
        grid=(B, seq_tiles),
        in_specs=[in_tile(D), in_tile(MLA_WIDE), in_tile(RET_V_WIDE), in_tile(D), in_tile(D),
                  const(wa.shape), const(w_ret_out.shape), const(w_mix_out.shape)],
        out_specs=pl.BlockSpec((CHUNK, D), lambda b, i: (b * seq_tiles + i, 0)),
        compiler_params=pltpu.CompilerParams(dimension_semantics=("arbitrary", "arbitrary"),
                                             vmem_limit_bytes=MIXER_VMEM_LIMIT),
        name="mixer_out",
    )(hp, o, yr, ga, gr, wa, w_ret_out.astype(bf), w_mix_out.astype(bf))


def _top_rows(s, count):
    n = s.shape[0]
    iota = lax.broadcasted_iota(jnp.int32, s.shape, 0)
    vals, idxs = [], []
    for _ in range(count):
        m = jnp.max(s, axis=0, keepdims=True)
        idx = jnp.min(jnp.where(s == m, iota, n), axis=0, keepdims=True)
        vals.append(m)
        idxs.append(idx)
        s = jnp.where(iota == idx, -jnp.inf, s)
    return vals, idxs


def _peer_route_kernel(h_ref, g_ref, wq_ref, k1_ref, k2_ref, xn_ref, rows_ref, gates_ref,
                       xb_ref, rows_scr, gates_scr):
    hd = pl.program_id(1)

    @pl.when(hd == 0)
    def _():
        x = h_ref[...]
        xn = x * lax.rsqrt(jnp.mean(x * x, axis=-1, keepdims=True) + EPS) * g_ref[...]
        xn_ref[...] = xn
        xb_ref[...] = xn.astype(jnp.bfloat16)

    q = jnp.dot(xb_ref[...], wq_ref[...], preferred_element_type=jnp.float32).astype(jnp.bfloat16)
    s1 = lax.dot_general(k1_ref[...], q, NT_DIMS, preferred_element_type=jnp.float32)
    s2 = lax.dot_general(k2_ref[...], q, NT_DIMS, preferred_element_type=jnp.float32)
    v1, i1 = _top_rows(s1, PEER_TOPK)
    v2, i2 = _top_rows(s2, PEER_TOPK)
    v2_all = jnp.concatenate(v2, axis=0)
    i2_all = jnp.concatenate(i2, axis=0)
    n_mid = SUBLANES
    cand = [v1[0] + v2_all] + [v1[a] + v2_all[:n_mid] for a in range(1, n_mid)]
    cand.append(jnp.concatenate(v1[n_mid:], axis=0) + v2[0])
    cand_rows = [i1[0] * PEER_NKEYS + i2_all] + [i1[a] * PEER_NKEYS + i2_all[:n_mid] for a in range(1, n_mid)]
    cand_rows.append(jnp.concatenate(i1[n_mid:], axis=0) * PEER_NKEYS + i2[0])
    cand = jnp.concatenate(cand, axis=0)
    cand_rows = jnp.concatenate(cand_rows, axis=0) * WORD_ROWS
    sc, ci = _top_rows(cand, PEER_TOPK)
    iota = lax.broadcasted_iota(jnp.int32, cand.shape, 0)
    rows = [jnp.max(jnp.where(iota == c, cand_rows, -1), axis=0, keepdims=True) for c in ci]
    ex = [jnp.exp(v - sc[0]) for v in sc]
    denom = ex[0]
    for e in ex[1:]:
        denom = denom + e
    base = pl.multiple_of(hd * PEER_TOPK, PEER_TOPK)
    rows_scr[pl.ds(base, PEER_TOPK), :] = jnp.concatenate(rows, axis=0)
    gates_scr[pl.ds(base, PEER_TOPK), :] = jnp.concatenate(ex, axis=0) / denom

    @pl.when(hd == PEER_HEADS - 1)
    def _():
        rows_ref[...] = rows_scr[...].T
        gates_ref[...] = gates_scr[...].T


def _peer_route(h, g_ffn, w_peer_q, keys_1, keys_2):
    T = h.shape[0]
    tb = ROUTE_TOKEN_BLOCK
    half = PEER_DQ // 2
    zeros = jnp.zeros((PEER_NKEYS, half), jnp.bfloat16)
    k1 = jnp.concatenate([keys_1.astype(jnp.bfloat16), zeros], axis=1)
    k2 = jnp.concatenate([zeros, keys_2.astype(jnp.bfloat16)], axis=1)
    keys_spec = pl.BlockSpec((PEER_NKEYS, PEER_DQ), lambda i, hd: (0, 0))
    pair_spec = pl.BlockSpec((tb, PEER_PAIRS), lambda i, hd: (i, 0))
    return pl.pallas_call(
        _peer_route_kernel,
        out_shape=(jax.ShapeDtypeStruct((T, D_MODEL), jnp.float32),
                   jax.ShapeDtypeStruct((T, PEER_PAIRS), jnp.int32),
                   jax.ShapeDtypeStruct((T, PEER_PAIRS), jnp.float32)),
        grid=(T // tb, PEER_HEADS),
        in_specs=[pl.BlockSpec((tb, D_MODEL), lambda i, hd: (i, 0)),
                  pl.BlockSpec((1, D_MODEL), lambda i, hd: (0, 0)),
                  pl.BlockSpec((D_MODEL, PEER_DQ), lambda i, hd: (0, hd)),
                  keys_spec, keys_spec],
        out_specs=(pl.BlockSpec((tb, D_MODEL), lambda i, hd: (i, 0)), pair_spec, pair_spec),
        scratch_shapes=[pltpu.VMEM((tb, D_MODEL), jnp.bfloat16),
                        pltpu.VMEM((PEER_PAIRS, tb), jnp.int32),
                        pltpu.VMEM((PEER_PAIRS, tb), jnp.float32)],
        compiler_params=pltpu.CompilerParams(dimension_semantics=("arbitrary", "arbitrary")),
        name="peer_route",
    )(h, g_ffn.reshape(1, D_MODEL), w_peer_q.astype(jnp.bfloat16), k1, k2)


def _pack_expert_table(tab):
    bits = lax.bitcast_convert_type(tab.astype(jnp.bfloat16), jnp.uint16).astype(jnp.uint32)
    bits = bits.reshape(tab.shape[0], WORD_ROWS, 2, LANES)
    words = bits[:, :, 0, :] | (bits[:, :, 1, :] << 16)
    return lax.bitcast_convert_type(words, jnp.int32).reshape(tab.shape[0] * WORD_ROWS, LANES)


def _gather_token_rows(idx_ref, tab_ref, dst_ref, t):
    idx_t = idx_ref.at[t]
    for k in range(PEER_PAIRS):
        row = pl.multiple_of(idx_t[k], WORD_ROWS)
        dst_ref[k * WORD_ROWS:(k + 1) * WORD_ROWS, :] = tab_ref[pl.ds(row, WORD_ROWS), :]


def _pipelined_tokens(n_tokens, gather, consume):
    assert n_tokens % N_SLOTS == 0
    for s in range(N_SLOTS):
        gather(s, s)

    def step(i, carry):
        t = N_SLOTS * i
        for s in range(N_SLOTS):
            consume(t + s, s)
        for s in range(N_SLOTS):
            gather(t + N_SLOTS + s, s)
        return carry

    lax.fori_loop(0, n_tokens // N_SLOTS - 1, step, 0)
    for s in range(N_SLOTS):
        consume(n_tokens - N_SLOTS + s, s)


def _diag_mask():
    lane = lax.broadcasted_iota(jnp.int32, (SUBLANES, GATHER_ROWS), 1)
    sub = lax.broadcasted_iota(jnp.int32, (SUBLANES, GATHER_ROWS), 0)
    return (lane & (SUBLANES - 1)) == sub


def _split_bf16(x):
    hi = x.astype(jnp.bfloat16)
    lo = (x - hi.astype(jnp.float32)).astype(jnp.bfloat16)
    return hi, lo


def _peer_hidden_kernel(idx_ref, x_ref, gate_ref, tab_ref, sel_ref, w_ref, diag_ref, *rows_refs):
    mask = _diag_mask()

    def gather(t, slot):
        _gather_token_rows(idx_ref, tab_ref, rows_refs[slot], t)

    def consume(t, slot):
        rows = pltpu.bitcast(rows_refs[slot][...], jnp.bfloat16)
        xb = x_ref[t].astype(jnp.bfloat16)
        d = lax.dot_general(xb, rows, NT_DIMS, preferred_element_type=jnp.float32)
        diag_ref[pl.ds(t, 1), :] = jnp.sum(jnp.where(mask, d, 0.0), axis=0, keepdims=True)

    _pipelined_tokens(x_ref.shape[0], gather, consume)
    hi, lo = _split_bf16(diag_ref[...])
    sel = sel_ref[...]
    s = (jnp.dot(hi, sel, preferred_element_type=jnp.float32)
         + jnp.dot(lo, sel, preferred_element_type=jnp.float32))
    act = 0.5 * s * (1.0 + lax.erf(s * (1.0 / math.sqrt(2.0))))
    w_ref[...] = gate_ref[...] * act


def _peer_output_kernel(idx_ref, w_ref, h_ref, tab_ref, expand_ref, out_ref, whi_ref, wlo_ref, *rows_refs):
    mask = _diag_mask()
    hi, lo = _split_bf16(w_ref[...])
    expand = expand_ref[...]
    whi_ref[...] = jnp.dot(hi, expand, preferred_element_type=jnp.float32)
    wlo_ref[...] = jnp.dot(lo, expand, preferred_element_type=jnp.float32)

    def gather(t, slot):
        _gather_token_rows(idx_ref, tab_ref, rows_refs[slot], t)

    def consume(t, slot):
        rows = pltpu.bitcast(rows_refs[slot][...], jnp.bfloat16)
        parts = []
        for w_part in (whi_ref, wlo_ref):
            w_row = jnp.broadcast_to(w_part[pl.ds(t, 1), :], (SUBLANES, GATHER_ROWS))
            parts.append(jnp.where(mask, w_row, 0.0))
        lhs = jnp.concatenate(parts, axis=0).astype(jnp.bfloat16)
        y = jnp.dot(lhs, rows, preferred_element_type=jnp.float32)
        out_ref[t] = h_ref[t] + (y[:SUBLANES] + y[SUBLANES:])

    _pipelined_tokens(h_ref.shape[0], gather, consume)


def _peer_experts(h, xn, rows, gates, peer_u, peer_v):
    T = h.shape[0]
    tb = PEER_TOKEN_BLOCK
    u_tab = _pack_expert_table(peer_u)
    v_tab = _pack_expert_table(peer_v)
    expand = (jnp.arange(PEER_PAIRS)[:, None] == jnp.arange(GATHER_ROWS)[None, :] // SUBLANES
              ).astype(jnp.bfloat16)
    params = pltpu.CompilerParams(dimension_semantics=("arbitrary",), vmem_limit_bytes=PEER_VMEM_LIMIT)
    smem_block = pl.BlockSpec((tb, PEER_PAIRS), lambda i: (i, 0), memory_space=pltpu.SMEM)
    pair_block = pl.BlockSpec((tb, PEER_PAIRS), lambda i: (i, 0))
    vmem_rows = pl.BlockSpec((tb, SUBLANES, LANES), lambda i: (i, 0, 0))
    whole_vmem = pl.BlockSpec(memory_space=pltpu.VMEM)
    rows_scratch = pltpu.VMEM((GATHER_WORD_ROWS, LANES), jnp.int32)
    wide_scratch = pltpu.VMEM((tb, GATHER_ROWS), jnp.float32)

    w = pl.pallas_call(
        _peer_hidden_kernel,
        out_shape=jax.ShapeDtypeStruct((T, PEER_PAIRS), jnp.float32),
        grid=(T // tb,),
        in_specs=[smem_block, vmem_rows, pair_block, whole_vmem, whole_vmem],
        out_specs=pair_block,
        scratch_shapes=[wide_scratch] + [rows_scratch] * N_SLOTS,
        compiler_params=params,
        name="peer_hidden",
    )(rows, xn.reshape(T, SUBLANES, LANES), gates, u_tab, expand.T)

    out = pl.pallas_call(
        _peer_output_kernel,
        out_shape=jax.ShapeDtypeStruct((T, SUBLANES, LANES), jnp.float32),
        grid=(T // tb,),
        in_specs=[smem_block, pair_block, vmem_rows, whole_vmem, whole_vmem],
        out_specs=vmem_rows,
        scratch_shapes=[wide_scratch, wide_scratch] + [rows_scratch] * N_SLOTS,
        compiler_params=params,
        name="peer_output",
    )(rows, w, h.reshape(T, SUBLANES, LANES), v_tab, expand)
    return out.reshape(T, D_MODEL)


def kernel(x, meta_tokens, g_mix, w_in, g_q_lora, w_uq, g_kv_lora, w_ukv, g_qk_q, g_qk_k, w_mla_out, g_ret_gn, w_ret_out, w_mix_out, g_ffn, w_peer_q, peer_keys_1, peer_keys_2, peer_u, peer_v):
    B, S, D = x.shape
    h = _mixer(x, meta_tokens, g_mix[0], w_in[0], g_q_lora[0], w_uq[0], g_kv_lora[0], w_ukv[0],
               g_qk_q[0], g_qk_k[0], w_mla_out[0], g_ret_gn[0], w_ret_out[0], w_mix_out[0])
    xn, rows, gates = _peer_route(h, g_ffn[0], w_peer_q[0], peer_keys_1[0], peer_keys_2[0])
    out = _peer_experts(h, xn, rows, gates, peer_u[0], peer_v[0])
    return out.reshape(B, S, D)
```

```python
import math

import jax
import jax.numpy as jnp
import numpy as np
from jax import lax
from jax.experimental import pallas as pl
from jax.experimental.pallas import tpu as pltpu

D_MODEL = 1024
N_META = 16
CHUNK = 128
ROPE_BASE = 10000.0
EPS = 1e-6
MLA_HEADS = 8
MLA_Q_LORA = 384
MLA_KV_LORA = 256
MLA_NOPE = 64
MLA_ROPE = 32
MLA_V = 64
MLA_QK = MLA_NOPE + MLA_ROPE
RET_HEADS = 8
RET_DK = 64
RET_DV = 128
PEER_HEADS = 8
PEER_NKEYS = 128
PEER_EXPERTS = PEER_NKEYS * PEER_NKEYS
PEER_DQ = 128
PEER_TOPK = 16
PEER_PAIRS = PEER_HEADS * PEER_TOPK
IN_SIZES = (MLA_Q_LORA, MLA_KV_LORA, MLA_ROPE, RET_HEADS * RET_DK, RET_HEADS * RET_DK,
            RET_HEADS * RET_DV, RET_HEADS * RET_DV, D_MODEL, D_MODEL)

LANES = 128
SUBLANES = 8
WORD_ROWS = D_MODEL // (2 * LANES)
GATHER_WORD_ROWS = PEER_PAIRS * WORD_ROWS
GATHER_ROWS = 2 * GATHER_WORD_ROWS
PEER_TOKEN_BLOCK = 256
ROUTE_TOKEN_BLOCK = 256
HIDDEN_SLOTS = 4
OUTPUT_SLOTS = 2
PEER_VMEM_LIMIT = 56 * 1024 * 1024
NT_DIMS = (((1,), (1,)), ((), ()))


def _rmsnorm(x, g):
    xf = x.astype(jnp.float32)
    y = xf * lax.rsqrt(jnp.mean(xf * xf, axis=-1, keepdims=True) + EPS)
    return (y * g.astype(jnp.float32)).astype(x.dtype)


def _rope(x, pos):
    half = x.shape[-1] // 2
    inv = ROPE_BASE ** (-jnp.arange(half, dtype=jnp.float32) / half)
    ang = pos.astype(jnp.float32)[:, None] * inv[None, :]
    cos, sin = jnp.cos(ang), jnp.sin(ang)
    xf = x.astype(jnp.float32)
    x1, x2 = xf[..., :half], xf[..., half:]
    return jnp.concatenate([x1 * cos - x2 * sin, x1 * sin + x2 * cos], -1).astype(x.dtype)


def _causal_block_attention(q, k, v):
    B, H, L, dqk = q.shape
    nblk = -(-L // CHUNK)
    Lp = nblk * CHUNK
    qp = jnp.pad(q, ((0, 0), (0, 0), (0, Lp - L), (0, 0)))
    kpos = jnp.arange(L)
    scale = 1.0 / math.sqrt(dqk)

    def one_block(i):
        qb = lax.dynamic_slice_in_dim(qp, i * CHUNK, CHUNK, axis=2)
        s = jnp.einsum('bhqd,bhkd->bhqk', qb, k).astype(jnp.float32) * scale
        qpos = i * CHUNK + jnp.arange(CHUNK)
        s = jnp.where(kpos[None, :] <= qpos[:, None], s, -jnp.inf)
        p = jax.nn.softmax(s, axis=-1)
        return jnp.einsum('bhqk,bhkd->bhqd', p.astype(v.dtype), v)

    out = lax.map(one_block, jnp.arange(nblk))
    out = out.transpose(1, 2, 0, 3, 4).reshape(B, H, Lp, v.shape[-1])
    return out[:, :, :L]


def _mla_branch(c_q, c_kv, k_pe, pos, g_q_lora, w_uq, g_kv_lora, w_ukv, g_qk_q, g_qk_k):
    B, L, _ = c_q.shape
    q = (_rmsnorm(c_q, g_q_lora) @ w_uq).reshape(B, L, MLA_HEADS, MLA_QK)
    kv = (_rmsnorm(c_kv, g_kv_lora) @ w_ukv).reshape(B, L, MLA_HEADS, MLA_NOPE + MLA_V)
    k_nope, v = kv[..., :MLA_NOPE], kv[..., MLA_NOPE:]
    k = jnp.concatenate([k_nope, jnp.broadcast_to(k_pe[:, :, None, :], (B, L, MLA_HEADS, MLA_ROPE))], -1)
    q = _rmsnorm(q, g_qk_q).transpose(0, 2, 1, 3)
    k = _rmsnorm(k, g_qk_k).transpose(0, 2, 1, 3)
    q = jnp.concatenate([q[..., :MLA_NOPE], _rope(q[..., MLA_NOPE:], pos)], -1)
    k = jnp.concatenate([k[..., :MLA_NOPE], _rope(k[..., MLA_NOPE:], pos)], -1)
    o = _causal_block_attention(q, k, v.transpose(0, 2, 1, 3))
    return o.transpose(0, 2, 1, 3).reshape(B, L, MLA_HEADS * MLA_V)


def _retention_branch(q, k, v, gate, pos, g_gn):
    B, L, _ = q.shape
    H, C = RET_HEADS, CHUNK
    q = _rope(q.reshape(B, L, H, RET_DK).transpose(0, 2, 1, 3), pos)
    k = _rope(k.reshape(B, L, H, RET_DK).transpose(0, 2, 1, 3), pos) * (RET_DK ** -0.5)
    v = v.reshape(B, L, H, RET_DV).transpose(0, 2, 1, 3)
    P = (-N_META) % C
    Lp = L + P
    NC = Lp // C
    padf = lambda t: jnp.pad(t, ((0, 0), (0, 0), (P, 0), (0, 0))).reshape(B, H, NC, C, t.shape[-1])
    qc, kc, vc = padf(q), padf(k), padf(v)
    log_gamma = jnp.log(1.0 - 2.0 ** (-5.0 - jnp.arange(H, dtype=jnp.float32)))
    idx = jnp.arange(C, dtype=jnp.float32)
    diff = idx[:, None] - idx[None, :]
    decay = jnp.where(diff[None] >= 0, jnp.exp(jnp.maximum(diff, 0.0)[None] * log_gamma[:, None, None]), 0.0)
    zeta = jnp.exp((C - 1 - idx)[None, :] * log_gamma[:, None])
    xi = jnp.exp((idx + 1)[None, :] * log_gamma[:, None])
    gamma_c = jnp.exp(C * log_gamma)
    scores = jnp.einsum('bhncd,bhnmd->bhncm', qc, kc) * decay[None, :, None]
    inner = jnp.einsum('bhncm,bhnme->bhnce', scores, vc)
    chunk_kv = jnp.einsum('bhnmd,bhnme->bhnde', kc * zeta[None, :, None, :, None], vc)

    def step(state, kv_n):
        return gamma_c[None, :, None, None] * state + kv_n, state

    init = jnp.zeros((B, H, RET_DK, RET_DV), chunk_kv.dtype)
    _, prev_states = lax.scan(step, init, jnp.moveaxis(chunk_kv, 2, 0))
    prev_states = jnp.moveaxis(prev_states, 0, 2)
    cross = jnp.einsum('bhncd,bhnde->bhnce', qc, prev_states) * xi[None, :, None, :, None]
    y = (inner + cross).reshape(B, H, Lp, RET_DV)[:, :, P:]
    yf = y.astype(jnp.float32)
    mu = jnp.mean(yf, -1, keepdims=True)
    var = jnp.mean((yf - mu) ** 2, -1, keepdims=True)
    yn = ((yf - mu) * lax.rsqrt(var + EPS)).transpose(0, 2, 1, 3).reshape(B, L, H * RET_DV)
    yn = (yn * g_gn.astype(jnp.float32)).astype(gate.dtype)
    return jax.nn.silu(gate) * yn


def _mixer_sublayer(h, pos, g_mix, w_in, g_q_lora, w_uq, g_kv_lora, w_ukv, g_qk_q, g_qk_k,
                    w_mla_out, g_ret_gn, w_ret_out, w_mix_out):
    hn = _rmsnorm(h, g_mix)
    z = hn @ w_in
    c_q, c_kv, k_pe, r_q, r_k, r_v, r_g, za, zr = jnp.split(z, np.cumsum(IN_SIZES)[:-1].tolist(), axis=-1)
    y_a = _mla_branch(c_q, c_kv, k_pe, pos, g_q_lora, w_uq, g_kv_lora, w_ukv, g_qk_q, g_qk_k) @ w_mla_out
    y_r = _retention_branch(r_q, r_k, r_v, r_g, pos, g_ret_gn) @ w_ret_out
    merged = jax.nn.sigmoid(za) * y_a + jax.nn.sigmoid(zr) * y_r
    return merged @ w_mix_out


PAD_ROWS = (-N_META) % CHUNK
HEAD_LANES = 128
MLA_WIDE = MLA_HEADS * HEAD_LANES
RET_QK_WIDE = RET_HEADS * RET_DK
RET_V_WIDE = RET_HEADS * RET_DV
MASK_VALUE = -1e30
MIXER_VMEM_LIMIT = 48 * 1024 * 1024
ATTN_HEAD_GROUP = 4
ATTN_TILE = 3 * CHUNK
_SEG_SIZES = (MLA_Q_LORA, MLA_KV_LORA, HEAD_LANES, RET_QK_WIDE, RET_QK_WIDE, RET_V_WIDE, RET_V_WIDE,
              D_MODEL, D_MODEL)
_SEG_STARTS = tuple(int(v) for v in np.cumsum((0,) + _SEG_SIZES))


def _rope_tables(n_rows, dim, block_offset, repeat):
    half = dim // 2
    pos = jnp.arange(n_rows, dtype=jnp.float32) - PAD_ROWS
    inv = ROPE_BASE ** (-jnp.arange(half, dtype=jnp.float32) / half)
    lane = np.arange(LANES)
    rel = (lane - block_offset) % repeat
    in_rope = (lane >= block_offset) & (rel < dim)
    ang = pos[:, None] * inv[rel % half][None, :]
    cos = jnp.where(in_rope[None, :], jnp.cos(ang), 1.0)
    sin = jnp.sin(ang)
    s1 = jnp.where((in_rope & (rel < half))[None, :], -sin, 0.0)
    s2 = jnp.where((in_rope & (rel >= half))[None, :], sin, 0.0)
    return cos, s1, s2


def _apply_rope(x, cos, s1, s2, half):
    return x * cos + pltpu.roll(x, LANES - half, 1) * s1 + pltpu.roll(x, half, 1) * s2


def _mixer_in_kernel(h_ref, gmix_ref, win_ref, gq_ref, wuq_ref, gkv_ref, wk_ref, wv_ref, gqkq_ref, gqkk_ref,
                     mc_ref, ms1_ref, ms2_ref, rc_ref, rs1_ref, rs2_ref,
                     q_ref, k_ref, v_ref, rq_ref, rk_ref, rv_ref, sg_ref, ga_ref, gr_ref):
    x = h_ref[...]
    hn = (x * lax.rsqrt(jnp.mean(x * x, axis=-1, keepdims=True) + EPS) * gmix_ref[...]).astype(jnp.bfloat16)

    def seg(i):
        return jnp.dot(hn, win_ref[:, _SEG_STARTS[i]:_SEG_STARTS[i + 1]], preferred_element_type=jnp.float32)

    def lora_norm(c, g_ref):
        return (c * lax.rsqrt(jnp.mean(c * c, axis=-1, keepdims=True) + EPS) * g_ref[...]).astype(jnp.bfloat16)

    cq = lora_norm(seg(0), gq_ref)
    ckv = lora_norm(seg(1), gkv_ref)
    kpe = seg(2)
    q = jnp.dot(cq, wuq_ref[...], preferred_element_type=jnp.float32)
    kn = jnp.dot(ckv, wk_ref[...], preferred_element_type=jnp.float32)
    v_ref[...] = jnp.dot(ckv, wv_ref[...], preferred_element_type=jnp.float32).astype(jnp.bfloat16)
    mc, ms1, ms2 = mc_ref[...], ms1_ref[...], ms2_ref[...]
    scale = math.log2(math.e) / math.sqrt(MLA_QK)

    def qk_norm_rope(t, g_ref):
        t = t * lax.rsqrt(jnp.sum(t * t, axis=-1, keepdims=True) * (1.0 / MLA_QK) + EPS) * g_ref[...]
        return _apply_rope(t, mc, ms1, ms2, MLA_ROPE // 2)

    for hd in range(MLA_HEADS):
        blk = slice(hd * HEAD_LANES, (hd + 1) * HEAD_LANES)
        q_ref[:, blk] = (qk_norm_rope(q[:, blk], gqkq_ref) * scale).astype(jnp.bfloat16)
        k_ref[:, blk] = qk_norm_rope(kn[:, blk] + kpe, gqkk_ref).astype(jnp.bfloat16)

    rc, rs1, rs2 = rc_ref[...], rs1_ref[...], rs2_ref[...]
    rq = seg(3)
    rk = seg(4)
    for j in range(RET_QK_WIDE // LANES):
        blk = slice(j * LANES, (j + 1) * LANES)
        rq_ref[:, blk] = _apply_rope(rq[:, blk], rc, rs1, rs2, RET_DK // 2).astype(jnp.bfloat16)
        rk_ref[:, blk] = (_apply_rope(rk[:, blk], rc, rs1, rs2, RET_DK // 2) * (RET_DK ** -0.5)).astype(jnp.bfloat16)
    rv_ref[...] = seg(5).astype(jnp.bfloat16)
    gate = seg(6)
    sg_ref[...] = (gate * jax.nn.sigmoid(gate)).astype(jnp.bfloat16)
    ga_ref[...] = jax.nn.sigmoid(seg(7)).astype(jnp.bfloat16)
    gr_ref[...] = jax.nn.sigmoid(seg(8)).astype(jnp.bfloat16)


def _attention_kernel(q_ref, k_ref, v_ref, o_ref, m_ref, l_ref, acc_ref):
    qi = pl.program_id(2)
    t = ATTN_TILE
    n_heads = q_ref.shape[1] // HEAD_LANES
    n_lane_tiles = t // LANES
    m_ref[...] = jnp.full(m_ref.shape, MASK_VALUE, jnp.float32)
    l_ref[...] = jnp.zeros(l_ref.shape, jnp.float32)
    acc_ref[...] = jnp.zeros(acc_ref.shape, jnp.float32)

    def lane_tiles(x):
        return [x[:, i * LANES:(i + 1) * LANES] for i in range(n_lane_tiles)]

    def scores(j, hd, masked):
        blk = slice(hd * HEAD_LANES, (hd + 1) * HEAD_LANES)
        start = pl.multiple_of(j * t, t)
        s = lax.dot_general(q_ref[:, blk], k_ref[pl.ds(start, t), blk], NT_DIMS,
                            preferred_element_type=jnp.float32)
        if masked:
            r = lax.broadcasted_iota(jnp.int32, (t, t), 0) + qi * t
            c = lax.broadcasted_iota(jnp.int32, (t, t), 1) + j * t
            s = jnp.where((c <= r) & ((c >= PAD_ROWS) | (r < PAD_ROWS)), s, MASK_VALUE)
        return s

    def max_step(j, masked):
        for hd in range(n_heads):
            m = m_ref[hd]
            for part in lane_tiles(scores(j, hd, masked)):
                m = jnp.maximum(m, part)
            m_ref[hd] = m

    def sum_step(j, masked):
        start = pl.multiple_of(j * t, t)
        for hd in range(n_heads):
            blk = slice(hd * HEAD_LANES, (hd + 1) * HEAD_LANES)
            p = jnp.exp2(scores(j, hd, masked) - jnp.tile(m_ref[hd], (1, n_lane_tiles)))
            l = l_ref[hd]
            for part in lane_tiles(p):
                l = l + part
            l_ref[hd] = l
            acc_ref[hd] += jnp.dot(p.astype(jnp.bfloat16), v_ref[pl.ds(start, t), blk],
                                   preferred_element_type=jnp.float32)

    def sweep(step):
        def body(j, carry):
            edge = (j == 0) | (j == qi)

            @pl.when(edge)
            def _():
                step(j, True)

            @pl.when(jnp.logical_not(edge))
            def _():
                step(j, False)

            return carry

        lax.fori_loop(0, qi + 1, body, 0)

    sweep(max_step)
    for hd in range(n_heads):
        m_ref[hd] = jnp.broadcast_to(jnp.max(m_ref[hd], axis=-1, keepdims=True), (t, LANES))
    sweep(sum_step)
    for hd in range(n_heads):
        blk = slice(hd * HEAD_LANES, (hd + 1) * HEAD_LANES)
        o_ref[:, blk] = (acc_ref[hd] / jnp.sum(l_ref[hd], axis=-1, keepdims=True)).astype(jnp.bfloat16)


def _retention_kernel(q_ref, k_ref, v_ref, sg_ref, gn_ref, decay_ref, zeta_ref, xi_ref, gamma_ref, o_ref):
    n_chunks = q_ref.shape[0] // CHUNK
    lane = lax.broadcasted_iota(jnp.int32, (CHUNK, LANES), 1)
    row = lax.broadcasted_iota(jnp.int32, (LANES, RET_DV), 0)
    is_a = lane < RET_DK
    gn = gn_ref[...]
    contract_rows = (((0,), (0,)), ((), ()))

    def group_norm(y):
        mu = jnp.mean(y, axis=-1, keepdims=True)
        d = y - mu
        return d * lax.rsqrt(jnp.mean(d * d, axis=-1, keepdims=True) + EPS)

    def chunk(n, state):
        rows = pl.ds(pl.multiple_of(n * CHUNK, CHUNK), CHUNK)
        qb = q_ref[rows, :]
        kb = k_ref[rows, :]
        state_bf = state.astype(jnp.bfloat16)
        new_parts = []
        for hd in range(2):
            qh = jnp.where(is_a if hd == 0 else ~is_a, qb, jnp.zeros_like(qb))
            vh = v_ref[rows, hd * RET_DV:(hd + 1) * RET_DV]
            scores = lax.dot_general(qh, kb, NT_DIMS, preferred_element_type=jnp.float32) * decay_ref[hd]
            inner = jnp.dot(scores.astype(jnp.bfloat16), vh, preferred_element_type=jnp.float32)
            cross = jnp.dot(qh, state_bf, preferred_element_type=jnp.float32) * xi_ref[hd]
            yn = group_norm(inner + cross) * gn[:, hd * RET_DV:(hd + 1) * RET_DV]
            sg = sg_ref[rows, hd * RET_DV:(hd + 1) * RET_DV].astype(jnp.float32)
            o_ref[rows, hd * RET_DV:(hd + 1) * RET_DV] = (sg * yn).astype(jnp.bfloat16)
            kz = (kb.astype(jnp.float32) * zeta_ref[hd]).astype(jnp.bfloat16)
            new_parts.append(lax.dot_general(kz, vh, contract_rows, preferred_element_type=jnp.float32))
        return gamma_ref[...] * state + jnp.where(row < RET_DK, new_parts[0], new_parts[1])

    lax.fori_loop(0, n_chunks, chunk, jnp.zeros((LANES, RET_DV), jnp.float32))


def _mixer_out_kernel(h_ref, o_ref, yr_ref, ga_ref, gr_ref, wa_ref, wr_ref, wm_ref, out_ref):
    y_a = jnp.dot(o_ref[...], wa_ref[...], preferred_element_type=jnp.float32)
    y_r = jnp.dot(yr_ref[...], wr_ref[...], preferred_element_type=jnp.float32)
    merged = ga_ref[...].astype(jnp.float32) * y_a + gr_ref[...].astype(jnp.float32) * y_r
    out_ref[...] = h_ref[...] + jnp.dot(merged.astype(jnp.bfloat16), wm_ref[...],
                                        preferred_element_type=jnp.float32)


def _pad_heads(w, heads, width):
    lead = w.shape[:-1]
    w = w.reshape(lead + (heads, width))
    w = jnp.pad(w, [(0, 0)] * len(lead) + [(0, 0), (0, HEAD_LANES - width)])
    return w.reshape(lead + (heads * HEAD_LANES,))


def _mixer(x, meta_tokens, g_mix, w_in, g_q_lora, w_uq, g_kv_lora, w_ukv, g_qk_q, g_qk_k,
           w_mla_out, g_ret_gn, w_ret_out, w_mix_out):
    B, S, D = x.shape
    bf = jnp.bfloat16
    rows = PAD_ROWS + N_META + S
    nt = rows // CHUNK
    hp = jnp.concatenate([jnp.zeros((B, PAD_ROWS, D), x.dtype),
                          jnp.broadcast_to(meta_tokens.astype(x.dtype)[None], (B, N_META, D)), x], axis=1)
    hp = hp.reshape(B * rows, D)

    segs = jnp.split(w_in, np.cumsum(IN_SIZES)[:-1].tolist(), axis=1)
    kpe_cols = jnp.pad(segs[2], ((0, 0), (MLA_NOPE, HEAD_LANES - MLA_NOPE - MLA_ROPE)))
    win = jnp.concatenate(segs[:2] + [kpe_cols] + segs[3:], axis=1).astype(bf)
    wuq = _pad_heads(w_uq, MLA_HEADS, MLA_QK).astype(bf)
    wkv = w_ukv.reshape(MLA_KV_LORA, MLA_HEADS, MLA_NOPE + MLA_V)
    wk = _pad_heads(wkv[:, :, :MLA_NOPE].reshape(MLA_KV_LORA, -1), MLA_HEADS, MLA_NOPE).astype(bf)
    wv = _pad_heads(wkv[:, :, MLA_NOPE:].reshape(MLA_KV_LORA, -1), MLA_HEADS, MLA_V).astype(bf)
    gqkq = jnp.pad(g_qk_q, (0, HEAD_LANES - MLA_QK)).reshape(1, HEAD_LANES)
    gqkk = jnp.pad(g_qk_k, (0, HEAD_LANES - MLA_QK)).reshape(1, HEAD_LANES)
    wa = _pad_heads(w_mla_out.T, MLA_HEADS, MLA_V).T.astype(bf)
    mla_tabs = _rope_tables(rows, MLA_ROPE, MLA_NOPE, HEAD_LANES)
    ret_tabs = _rope_tables(rows, RET_DK, 0, RET_DK)

    row_tile = lambda width: pl.BlockSpec((CHUNK, width), lambda b, i: (b * nt + i, 0))
    const = lambda shape: pl.BlockSpec(shape, lambda b, i: (0,) * len(shape))
    tab = pl.BlockSpec((CHUNK, LANES), lambda b, i: (i, 0))
    n_rows = B * rows
    widths = (MLA_WIDE, MLA_WIDE, MLA_WIDE, RET_QK_WIDE, RET_QK_WIDE, RET_V_WIDE, RET_V_WIDE, D, D)
    q, k, v, rq, rk, rv, sg, ga, gr = pl.pallas_call(
        _mixer_in_kernel,
        out_shape=tuple(jax.ShapeDtypeStruct((n_rows, w), bf) for w in widths),
        grid=(B, nt),
        in_specs=[row_tile(D), const((1, D)), const(win.shape), const((1, MLA_Q_LORA)), const(wuq.shape),
                  const((1, MLA_KV_LORA)), const(wk.shape), const(wv.shape), const((1, HEAD_LANES)),
                  const((1, HEAD_LANES))] + [tab] * 6,
        out_specs=tuple(row_tile(w) for w in widths),
        compiler_params=pltpu.CompilerParams(dimension_semantics=("arbitrary", "arbitrary"),
                                             vmem_limit_bytes=MIXER_VMEM_LIMIT),
        name="mixer_in",
    )(hp, g_mix.reshape(1, D), win, g_q_lora.reshape(1, -1), wuq, g_kv_lora.reshape(1, -1), wk, wv,
      gqkq, gqkk, *mla_tabs, *ret_tabs)

    group = ATTN_HEAD_GROUP * HEAD_LANES
    seq_spec = pl.BlockSpec((rows, group), lambda b, g, i: (b, g))
    n_attn = rows // ATTN_TILE
    tile_spec = pl.BlockSpec((ATTN_TILE, group), lambda b, g, i: (b * n_attn + i, g))
    o = pl.pallas_call(
        _attention_kernel,
        out_shape=jax.ShapeDtypeStruct((n_rows, MLA_WIDE), bf),
        grid=(B, MLA_HEADS // ATTN_HEAD_GROUP, n_attn),
        in_specs=[tile_spec, seq_spec, seq_spec],
        out_specs=tile_spec,
        scratch_shapes=[pltpu.VMEM((ATTN_HEAD_GROUP, ATTN_TILE, HEAD_LANES), jnp.float32)] * 3,
        compiler_params=pltpu.CompilerParams(dimension_semantics=("arbitrary",) * 3,
                                             vmem_limit_bytes=MIXER_VMEM_LIMIT),
        name="mla_attention",
    )(q, k, v)

    log_gamma = jnp.log(1.0 - 2.0 ** (-5.0 - jnp.arange(RET_HEADS, dtype=jnp.float32)))
    idx = jnp.arange(CHUNK, dtype=jnp.float32)
    diff = idx[:, None] - idx[None, :]
    decay = jnp.where(diff[None] >= 0, jnp.exp(jnp.maximum(diff, 0.0)[None] * log_gamma[:, None, None]), 0.0)
    zeta = jnp.exp((CHUNK - 1 - idx)[None, :] * log_gamma[:, None])
    xi = jnp.exp((idx + 1)[None, :] * log_gamma[:, None])
    gamma_c = jnp.exp(CHUNK * log_gamma)
    zeta_b = jnp.broadcast_to(zeta[:, :, None], (RET_HEADS, CHUNK, LANES))
    xi_b = jnp.broadcast_to(xi[:, :, None], (RET_HEADS, CHUNK, RET_DV))
    gamma_rows = jnp.broadcast_to(jnp.repeat(gamma_c, RET_DK).reshape(RET_HEADS // 2, LANES, 1),
                                  (RET_HEADS // 2, LANES, RET_DV))
    pair_seq = lambda width: pl.BlockSpec((rows, width), lambda b, p: (b, p))
    pair_tab = lambda d1, d2: pl.BlockSpec((2, d1, d2), lambda b, p: (p, 0, 0))
    yr = pl.pallas_call(
        _retention_kernel,
        out_shape=jax.ShapeDtypeStruct((n_rows, RET_V_WIDE), bf),
        grid=(B, RET_HEADS // 2),
        in_specs=[pair_seq(LANES), pair_seq(LANES), pair_seq(2 * RET_DV), pair_seq(2 * RET_DV),
                  pl.BlockSpec((1, 2 * RET_DV), lambda b, p: (0, p)),
                  pair_tab(CHUNK, CHUNK), pair_tab(CHUNK, LANES), pair_tab(CHUNK, RET_DV),
                  pl.BlockSpec((None, LANES, RET_DV), lambda b, p: (p, 0, 0))],
        out_specs=pair_seq(2 * RET_DV),
        compiler_params=pltpu.CompilerParams(dimension_semantics=("arbitrary", "arbitrary"),
                                             vmem_limit_bytes=MIXER_VMEM_LIMIT),
        name="retention",
    )(rq, rk, rv, sg, g_ret_gn.reshape(1, -1), decay, zeta_b, xi_b, gamma_rows)

    seq_tiles = S // CHUNK
    first = (PAD_ROWS + N_META) // CHUNK
    in_tile = lambda width: pl.BlockSpec((CHUNK, width), lambda b, i: (b * nt + first + i, 0))
    return pl.pallas_call(
        _mixer_out_kernel,
        out_shape=jax.ShapeDtypeStruct((B * S, D), jnp.float32),
        grid=(B, seq_tiles),
        in_specs=[in_tile(D), in_tile(MLA_WIDE), in_tile(RET_V_WIDE), in_tile(D), in_tile(D),
                  const(wa.shape), const(w_ret_out.shape), const(w_mix_out.shape)],
        out_specs=pl.BlockSpec((CHUNK, D), lambda b, i: (b * seq_tiles + i, 0)),
        compiler_params=pltpu.CompilerParams(dimension_semantics=("arbitrary", "arbitrary"),
                                             vmem_limit_bytes=MIXER_VMEM_LIMIT),
        name="mixer_out",
    )(hp, o, yr, ga, gr, wa, w_ret_out.astype(bf), w_mix_out.astype(bf))


def _top_rows(s, count):
    n = s.shape[0]
    iota = lax.broadcasted_iota(jnp.int32, s.shape, 0)
    vals, idxs = [], []
    for _ in range(count):
        m = jnp.max(s, axis=0, keepdims=True)
        idx = jnp.min(jnp.where(s == m, iota, n), axis=0, keepdims=True)
        vals.append(m)
        idxs.append(idx)
        s = jnp.where(iota == idx, -jnp.inf, s)
    return vals, idxs


def _peer_route_kernel(h_ref, g_ref, wq_ref, k1_ref, k2_ref, xn_ref, rows_ref, gates_ref,
                       xb_ref, rows_scr, gates_scr):
    hd = pl.program_id(1)

    @pl.when(hd == 0)
    def _():
        x = h_ref[...]
        xn = x * lax.rsqrt(jnp.mean(x * x, axis=-1, keepdims=True) + EPS) * g_ref[...]
        xn_ref[...] = xn
        xb_ref[...] = xn.astype(jnp.bfloat16)

    q = jnp.dot(xb_ref[...], wq_ref[...], preferred_element_type=jnp.float32).astype(jnp.bfloat16)
    s1 = lax.dot_general(k1_ref[...], q, NT_DIMS, preferred_element_type=jnp.float32)
    s2 = lax.dot_general(k2_ref[...], q, NT_DIMS, preferred_element_type=jnp.float32)
    v1, i1 = _top_rows(s1, PEER_TOPK)
    v2, i2 = _top_rows(s2, PEER_TOPK)
    v2_all = jnp.concatenate(v2, axis=0)
    i2_all = jnp.concatenate(i2, axis=0)
    n_mid = SUBLANES
    cand = [v1[0] + v2_all] + [v1[a] + v2_all[:n_mid] for a in range(1, n_mid)]
    cand.append(jnp.concatenate(v1[n_mid:], axis=0) + v2[0])
    cand_rows = [i1[0] * PEER_NKEYS + i2_all] + [i1[a] * PEER_NKEYS + i2_all[:n_mid] for a in range(1, n_mid)]
    cand_rows.append(jnp.concatenate(i1[n_mid:], axis=0) * PEER_NKEYS + i2[0])
    cand = jnp.concatenate(cand, axis=0)
    cand_rows = jnp.concatenate(cand_rows, axis=0) * WORD_ROWS
    sc, ci = _top_rows(cand, PEER_TOPK)
    iota = lax.broadcasted_iota(jnp.int32, cand.shape, 0)
    rows = [jnp.max(jnp.where(iota == c, cand_rows, -1), axis=0, keepdims=True) for c in ci]
    ex = [jnp.exp(v - sc[0]) for v in sc]
    denom = ex[0]
    for e in ex[1:]:
        denom = denom + e
    base = pl.multiple_of(hd * PEER_TOPK, PEER_TOPK)
    rows_scr[pl.ds(base, PEER_TOPK), :] = jnp.concatenate(rows, axis=0)
    gates_scr[pl.ds(base, PEER_TOPK), :] = jnp.concatenate(ex, axis=0) / denom

    @pl.when(hd == PEER_HEADS - 1)
    def _():
        rows_ref[...] = rows_scr[...].T
        gates_ref[...] = gates_scr[...].T


def _peer_route(h, g_ffn, w_peer_q, keys_1, keys_2):
    T = h.shape[0]
    tb = ROUTE_TOKEN_BLOCK
    half = PEER_DQ // 2
    zeros = jnp.zeros((PEER_NKEYS, half), jnp.bfloat16)
    k1 = jnp.concatenate([keys_1.astype(jnp.bfloat16), zeros], axis=1)
    k2 = jnp.concatenate([zeros, keys_2.astype(jnp.bfloat16)], axis=1)
    keys_spec = pl.BlockSpec((PEER_NKEYS, PEER_DQ), lambda i, hd: (0, 0))
    pair_spec = pl.BlockSpec((tb, PEER_PAIRS), lambda i, hd: (i, 0))
    return pl.pallas_call(
        _peer_route_kernel,
        out_shape=(jax.ShapeDtypeStruct((T, D_MODEL), jnp.float32),
                   jax.ShapeDtypeStruct((T, PEER_PAIRS), jnp.int32),
                   jax.ShapeDtypeStruct((T, PEER_PAIRS), jnp.float32)),
        grid=(T // tb, PEER_HEADS),
        in_specs=[pl.BlockSpec((tb, D_MODEL), lambda i, hd: (i, 0)),
                  pl.BlockSpec((1, D_MODEL), lambda i, hd: (0, 0)),
                  pl.BlockSpec((D_MODEL, PEER_DQ), lambda i, hd: (0, hd)),
                  keys_spec, keys_spec],
        out_specs=(pl.BlockSpec((tb, D_MODEL), lambda i, hd: (i, 0)), pair_spec, pair_spec),
        scratch_shapes=[pltpu.VMEM((tb, D_MODEL), jnp.bfloat16),
                        pltpu.VMEM((PEER_PAIRS, tb), jnp.int32),
                        pltpu.VMEM((PEER_PAIRS, tb), jnp.float32)],
        compiler_params=pltpu.CompilerParams(dimension_semantics=("arbitrary", "arbitrary")),
        name="peer_route",
    )(h, g_ffn.reshape(1, D_MODEL), w_peer_q.astype(jnp.bfloat16), k1, k2)


def _pack_expert_table(tab):
    bits = lax.bitcast_convert_type(tab.astype(jnp.bfloat16), jnp.uint16).astype(jnp.uint32)
    bits = bits.reshape(tab.shape[0], WORD_ROWS, 2, LANES)
    words = bits[:, :, 0, :] | (bits[:, :, 1, :] << 16)
    return lax.bitcast_convert_type(words, jnp.int32).reshape(tab.shape[0] * WORD_ROWS, LANES)


def _gather_token_rows(idx_ref, tab_ref, dst_ref, t):
    idx_t = idx_ref.at[t]
    for k in range(PEER_PAIRS):
        row = pl.multiple_of(idx_t[k], WORD_ROWS)
        dst_ref[k * WORD_ROWS:(k + 1) * WORD_ROWS, :] = tab_ref[pl.ds(row, WORD_ROWS), :]


def _pipelined_tokens(n_tokens, n_slots, gather, consume):
    assert n_tokens % n_slots == 0
    for s in range(n_slots):
        gather(s, s)

    def step(i, carry):
        t = n_slots * i
        for s in range(n_slots):
            consume(t + s, s)
        for s in range(n_slots):
            gather(t + n_slots + s, s)
        return carry

    lax.fori_loop(0, n_tokens // n_slots - 1, step, 0)
    for s in range(n_slots):
        consume(n_tokens - n_slots + s, s)


def _diag_mask():
    lane = lax.broadcasted_iota(jnp.int32, (SUBLANES, GATHER_ROWS), 1)
    sub = lax.broadcasted_iota(jnp.int32, (SUBLANES, GATHER_ROWS), 0)
    return (lane & (SUBLANES - 1)) == sub


def _split_bf16(x):
    hi = x.astype(jnp.bfloat16)
    lo = (x - hi.astype(jnp.float32)).astype(jnp.bfloat16)
    return hi, lo


def _peer_hidden_kernel(idx_ref, x_ref, gate_ref, tab_ref, sel_ref, w_ref, diag_ref, *rows_refs):
    mask = _diag_mask()

    def gather(t, slot):
        _gather_token_rows(idx_ref, tab_ref, rows_refs[slot], t)

    def consume(t, slot):
        rows = pltpu.bitcast(rows_refs[slot][...], jnp.bfloat16)
        xb = x_ref[t].astype(jnp.bfloat16)
        d = lax.dot_general(xb, rows, NT_DIMS, preferred_element_type=jnp.float32)
        diag_ref[pl.ds(t, 1), :] = jnp.sum(jnp.where(mask, d, 0.0), axis=0, keepdims=True)

    _pipelined_tokens(x_ref.shape[0], len(rows_refs), gather, consume)
    hi, lo = _split_bf16(diag_ref[...])
    sel = sel_ref[...]
    s = (jnp.dot(hi, sel, preferred_element_type=jnp.float32)
         + jnp.dot(lo, sel, preferred_element_type=jnp.float32))
    act = 0.5 * s * (1.0 + lax.erf(s * (1.0 / math.sqrt(2.0))))
    w_ref[...] = gate_ref[...] * act


def _peer_output_kernel(idx_ref, w_ref, h_ref, tab_ref, expand_ref, out_ref, whi_ref, wlo_ref, *rows_refs):
    mask = _diag_mask()
    hi, lo = _split_bf16(w_ref[...])
    expand = expand_ref[...]
    whi_ref[...] = jnp.dot(hi, expand, preferred_element_type=jnp.float32)
    wlo_ref[...] = jnp.dot(lo, expand, preferred_element_type=jnp.float32)

    def gather(t, slot):
        _gather_token_rows(idx_ref, tab_ref, rows_refs[slot], t)

    def consume(t, slot):
        rows = pltpu.bitcast(rows_refs[slot][...], jnp.bfloat16)
        parts = []
        for w_part in (whi_ref, wlo_ref):
            w_row = jnp.broadcast_to(w_part[pl.ds(t, 1), :], (SUBLANES, GATHER_ROWS))
            parts.append(jnp.where(mask, w_row, 0.0))
        lhs = jnp.concatenate(parts, axis=0).astype(jnp.bfloat16)
        y = jnp.dot(lhs, rows, preferred_element_type=jnp.float32)
        out_ref[t] = h_ref[t] + (y[:SUBLANES] + y[SUBLANES:])

    _pipelined_tokens(h_ref.shape[0], len(rows_refs), gather, consume)


def _peer_experts(h, xn, rows, gates, peer_u, peer_v):
    T = h.shape[0]
    tb = PEER_TOKEN_BLOCK
    u_tab = _pack_expert_table(peer_u)
    v_tab = _pack_expert_table(peer_v)
    expand = (jnp.arange(PEER_PAIRS)[:, None] == jnp.arange(GATHER_ROWS)[None, :] // SUBLANES
              ).astype(jnp.bfloat16)
    params = pltpu.CompilerParams(dimension_semantics=("arbitrary",), vmem_limit_bytes=PEER_VMEM_LIMIT)
    smem_block = pl.BlockSpec((tb, PEER_PAIRS), lambda i: (i, 0), memory_space=pltpu.SMEM)
    pair_block = pl.BlockSpec((tb, PEER_PAIRS), lambda i: (i, 0))
    vmem_rows = pl.BlockSpec((tb, SUBLANES, LANES), lambda i: (i, 0, 0))
    whole_vmem = pl.BlockSpec(memory_space=pltpu.VMEM)
    rows_scratch = pltpu.VMEM((GATHER_WORD_ROWS, LANES), jnp.int32)
    wide_scratch = pltpu.VMEM((tb, GATHER_ROWS), jnp.float32)

    w = pl.pallas_call(
        _peer_hidden_kernel,
        out_shape=jax.ShapeDtypeStruct((T, PEER_PAIRS), jnp.float32),
        grid=(T // tb,),
        in_specs=[smem_block, vmem_rows, pair_block, whole_vmem, whole_vmem],
        out_specs=pair_block,
        scratch_shapes=[wide_scratch] + [rows_scratch] * HIDDEN_SLOTS,
        compiler_params=params,
        name="peer_hidden",
    )(rows, xn.reshape(T, SUBLANES, LANES), gates, u_tab, expand.T)

    out = pl.pallas_call(
        _peer_output_kernel,
        out_shape=jax.ShapeDtypeStruct((T, SUBLANES, LANES), jnp.float32),
        grid=(T // tb,),
        in_specs=[smem_block, pair_block, vmem_rows, whole_vmem, whole_vmem],
        out_specs=vmem_rows,
        scratch_shapes=[wide_scratch, wide_scratch] + [rows_scratch] * OUTPUT_SLOTS,
        compiler_params=params,
        name="peer_output",
    )(rows, w, h.reshape(T, SUBLANES, LANES), v_tab, expand)
    return out.reshape(T, D_MODEL)


def kernel(x, meta_tokens, g_mix, w_in, g_q_lora, w_uq, g_kv_lora, w_ukv, g_qk_q, g_qk_k, w_mla_out, g_ret_gn, w_ret_out, w_mix_out, g_ffn, w_peer_q, peer_keys_1, peer_keys_2, peer_u, peer_v):
    B, S, D = x.shape
    h = _mixer(x, meta_tokens, g_mix[0], w_in[0], g_q_lora[0], w_uq[0], g_kv_lora[0], w_ukv[0],
               g_qk_q[0], g_qk_k[0], w_mla_out[0], g_ret_gn[0], w_ret_out[0], w_mix_out[0])
    xn, rows, gates = _peer_route(h, g_ffn[0], w_peer_q[0], peer_keys_1[0], peer_keys_2[0])
    out = _peer_experts(h, xn, rows, gates, peer_u[0], peer_v[0])
    return out.reshape(B, S, D)
```

```python
import math

import jax
import jax.numpy as jnp
import numpy as np
from jax import lax
from jax.experimental import pallas as pl
from jax.experimental.pallas import tpu as pltpu

D_MODEL = 1024
N_META = 16
CHUNK = 128
ROPE_BASE = 10000.0
EPS = 1e-6
MLA_HEADS = 8
MLA_Q_LORA = 384
MLA_KV_LORA = 256
MLA_NOPE = 64
MLA_ROPE = 32
MLA_V = 64
MLA_QK = MLA_NOPE + MLA_ROPE
RET_HEADS = 8
RET_DK = 64
RET_DV = 128
PEER_HEADS = 8
PEER_NKEYS = 128
PEER_EXPERTS = PEER_NKEYS * PEER_NKEYS
PEER_DQ = 128
PEER_TOPK = 16
PEER_PAIRS = PEER_HEADS * PEER_TOPK
IN_SIZES = (MLA_Q_LORA, MLA_KV_LORA, MLA_ROPE, RET_HEADS * RET_DK, RET_HEADS * RET_DK,
            RET_HEADS * RET_DV, RET_HEADS * RET_DV, D_MODEL, D_MODEL)

LANES = 128
SUBLANES = 8
WORD_ROWS = D_MODEL // (2 * LANES)
GATHER_WORD_ROWS = PEER_PAIRS * WORD_ROWS
GATHER_ROWS = 2 * GATHER_WORD_ROWS
PEER_TOKEN_BLOCK = 256
ROUTE_TOKEN_BLOCK = 256
ROUTE_HEAD_GROUP = 4
HIDDEN_SLOTS = 4
OUTPUT_SLOTS = 2
PEER_VMEM_LIMIT = 56 * 1024 * 1024
NT_DIMS = (((1,), (1,)), ((), ()))


def _rmsnorm(x, g):
    xf = x.astype(jnp.float32)
    y = xf * lax.rsqrt(jnp.mean(xf * xf, axis=-1, keepdims=True) + EPS)
    return (y * g.astype(jnp.float32)).astype(x.dtype)


def _rope(x, pos):
    half = x.shape[-1] // 2
    inv = ROPE_BASE ** (-jnp.arange(half, dtype=jnp.float32) / half)
    ang = pos.astype(jnp.float32)[:, None] * inv[None, :]
    cos, sin = jnp.cos(ang), jnp.sin(ang)
    xf = x.astype(jnp.float32)
    x1, x2 = xf[..., :half], xf[..., half:]
    return jnp.concatenate([x1 * cos - x2 * sin, x1 * sin + x2 * cos], -1).astype(x.dtype)


def _causal_block_attention(q, k, v):
    B, H, L, dqk = q.shape
    nblk = -(-L // CHUNK)
    Lp = nblk * CHUNK
    qp = jnp.pad(q, ((0, 0), (0, 0), (0, Lp - L), (0, 0)))
    kpos = jnp.arange(L)
    scale = 1.0 / math.sqrt(dqk)

    def one_block(i):
        qb = lax.dynamic_slice_in_dim(qp, i * CHUNK, CHUNK, axis=2)
        s = jnp.einsum('bhqd,bhkd->bhqk', qb, k).astype(jnp.float32) * scale
        qpos = i * CHUNK + jnp.arange(CHUNK)
        s = jnp.where(kpos[None, :] <= qpos[:, None], s, -jnp.inf)
        p = jax.nn.softmax(s, axis=-1)
        return jnp.einsum('bhqk,bhkd->bhqd', p.astype(v.dtype), v)

    out = lax.map(one_block, jnp.arange(nblk))
    out = out.transpose(1, 2, 0, 3, 4).reshape(B, H, Lp, v.shape[-1])
    return out[:, :, :L]


def _mla_branch(c_q, c_kv, k_pe, pos, g_q_lora, w_uq, g_kv_lora, w_ukv, g_qk_q, g_qk_k):
    B, L, _ = c_q.shape
    q = (_rmsnorm(c_q, g_q_lora) @ w_uq).reshape(B, L, MLA_HEADS, MLA_QK)
    kv = (_rmsnorm(c_kv, g_kv_lora) @ w_ukv).reshape(B, L, MLA_HEADS, MLA_NOPE + MLA_V)
    k_nope, v = kv[..., :MLA_NOPE], kv[..., MLA_NOPE:]
    k = jnp.concatenate([k_nope, jnp.broadcast_to(k_pe[:, :, None, :], (B, L, MLA_HEADS, MLA_ROPE))], -1)
    q = _rmsnorm(q, g_qk_q).transpose(0, 2, 1, 3)
    k = _rmsnorm(k, g_qk_k).transpose(0, 2, 1, 3)
    q = jnp.concatenate([q[..., :MLA_NOPE], _rope(q[..., MLA_NOPE:], pos)], -1)
    k = jnp.concatenate([k[..., :MLA_NOPE], _rope(k[..., MLA_NOPE:], pos)], -1)
    o = _causal_block_attention(q, k, v.transpose(0, 2, 1, 3))
    return o.transpose(0, 2, 1, 3).reshape(B, L, MLA_HEADS * MLA_V)


def _retention_branch(q, k, v, gate, pos, g_gn):
    B, L, _ = q.shape
    H, C = RET_HEADS, CHUNK
    q = _rope(q.reshape(B, L, H, RET_DK).transpose(0, 2, 1, 3), pos)
    k = _rope(k.reshape(B, L, H, RET_DK).transpose(0, 2, 1, 3), pos) * (RET_DK ** -0.5)
    v = v.reshape(B, L, H, RET_DV).transpose(0, 2, 1, 3)
    P = (-N_META) % C
    Lp = L + P
    NC = Lp // C
    padf = lambda t: jnp.pad(t, ((0, 0), (0, 0), (P, 0), (0, 0))).reshape(B, H, NC, C, t.shape[-1])
    qc, kc, vc = padf(q), padf(k), padf(v)
    log_gamma = jnp.log(1.0 - 2.0 ** (-5.0 - jnp.arange(H, dtype=jnp.float32)))
    idx = jnp.arange(C, dtype=jnp.float32)
    diff = idx[:, None] - idx[None, :]
    decay = jnp.where(diff[None] >= 0, jnp.exp(jnp.maximum(diff, 0.0)[None] * log_gamma[:, None, None]), 0.0)
    zeta = jnp.exp((C - 1 - idx)[None, :] * log_gamma[:, None])
    xi = jnp.exp((idx + 1)[None, :] * log_gamma[:, None])
    gamma_c = jnp.exp(C * log_gamma)
    scores = jnp.einsum('bhncd,bhnmd->bhncm', qc, kc) * decay[None, :, None]
    inner = jnp.einsum('bhncm,bhnme->bhnce', scores, vc)
    chunk_kv = jnp.einsum('bhnmd,bhnme->bhnde', kc * zeta[None, :, None, :, None], vc)

    def step(state, kv_n):
        return gamma_c[None, :, None, None] * state + kv_n, state

    init = jnp.zeros((B, H, RET_DK, RET_DV), chunk_kv.dtype)
    _, prev_states = lax.scan(step, init, jnp.moveaxis(chunk_kv, 2, 0))
    prev_states = jnp.moveaxis(prev_states, 0, 2)
    cross = jnp.einsum('bhncd,bhnde->bhnce', qc, prev_states) * xi[None, :, None, :, None]
    y = (inner + cross).reshape(B, H, Lp, RET_DV)[:, :, P:]
    yf = y.astype(jnp.float32)
    mu = jnp.mean(yf, -1, keepdims=True)
    var = jnp.mean((yf - mu) ** 2, -1, keepdims=True)
    yn = ((yf - mu) * lax.rsqrt(var + EPS)).transpose(0, 2, 1, 3).reshape(B, L, H * RET_DV)
    yn = (yn * g_gn.astype(jnp.float32)).astype(gate.dtype)
    return jax.nn.silu(gate) * yn


def _mixer_sublayer(h, pos, g_mix, w_in, g_q_lora, w_uq, g_kv_lora, w_ukv, g_qk_q, g_qk_k,
                    w_mla_out, g_ret_gn, w_ret_out, w_mix_out):
    hn = _rmsnorm(h, g_mix)
    z = hn @ w_in
    c_q, c_kv, k_pe, r_q, r_k, r_v, r_g, za, zr = jnp.split(z, np.cumsum(IN_SIZES)[:-1].tolist(), axis=-1)
    y_a = _mla_branch(c_q, c_kv, k_pe, pos, g_q_lora, w_uq, g_kv_lora, w_ukv, g_qk_q, g_qk_k) @ w_mla_out
    y_r = _retention_branch(r_q, r_k, r_v, r_g, pos, g_ret_gn) @ w_ret_out
    merged = jax.nn.sigmoid(za) * y_a + jax.nn.sigmoid(zr) * y_r
    return merged @ w_mix_out


PAD_ROWS = (-N_META) % CHUNK
HEAD_LANES = 128
MLA_WIDE = MLA_HEADS * HEAD_LANES
RET_QK_WIDE = RET_HEADS * RET_DK
RET_V_WIDE = RET_HEADS * RET_DV
MASK_VALUE = -1e30
MIXER_VMEM_LIMIT = 48 * 1024 * 1024
ATTN_HEAD_GROUP = 4
ATTN_TILE = 3 * CHUNK
_SEG_SIZES = (MLA_Q_LORA, MLA_KV_LORA, HEAD_LANES, RET_QK_WIDE, RET_QK_WIDE, RET_V_WIDE, RET_V_WIDE,
              D_MODEL, D_MODEL)
_SEG_STARTS = tuple(int(v) for v in np.cumsum((0,) + _SEG_SIZES))


def _rope_tables(n_rows, dim, block_offset, repeat):
    half = dim // 2
    pos = jnp.arange(n_rows, dtype=jnp.float32) - PAD_ROWS
    inv = ROPE_BASE ** (-jnp.arange(half, dtype=jnp.float32) / half)
    lane = np.arange(LANES)
    rel = (lane - block_offset) % repeat
    in_rope = (lane >= block_offset) & (rel < dim)
    ang = pos[:, None] * inv[rel % half][None, :]
    cos = jnp.where(in_rope[None, :], jnp.cos(ang), 1.0)
    sin = jnp.sin(ang)
    s1 = jnp.where((in_rope & (rel < half))[None, :], -sin, 0.0)
    s2 = jnp.where((in_rope & (rel >= half))[None, :], sin, 0.0)
    return cos, s1, s2


def _apply_rope(x, cos, s1, s2, half):
    return x * cos + pltpu.roll(x, LANES - half, 1) * s1 + pltpu.roll(x, half, 1) * s2


def _mixer_in_kernel(h_ref, gmix_ref, win_ref, gq_ref, wuq_ref, gkv_ref, wk_ref, wv_ref, gqkq_ref, gqkk_ref,
                     mc_ref, ms1_ref, ms2_ref, rc_ref, rs1_ref, rs2_ref,
                     q_ref, k_ref, v_ref, rq_ref, rk_ref, rv_ref, sg_ref, ga_ref, gr_ref):
    x = h_ref[...]
    hn = (x * lax.rsqrt(jnp.mean(x * x, axis=-1, keepdims=True) + EPS) * gmix_ref[...]).astype(jnp.bfloat16)

    def seg(i):
        return jnp.dot(hn, win_ref[:, _SEG_STARTS[i]:_SEG_STARTS[i + 1]], preferred_element_type=jnp.float32)

    def lora_norm(c, g_ref):
        return (c * lax.rsqrt(jnp.mean(c * c, axis=-1, keepdims=True) + EPS) * g_ref[...]).astype(jnp.bfloat16)

    cq = lora_norm(seg(0), gq_ref)
    ckv = lora_norm(seg(1), gkv_ref)
    kpe = seg(2)
    q = jnp.dot(cq, wuq_ref[...], preferred_element_type=jnp.float32)
    kn = jnp.dot(ckv, wk_ref[...], preferred_element_type=jnp.float32)
    v_ref[...] = jnp.dot(ckv, wv_ref[...], preferred_element_type=jnp.float32).astype(jnp.bfloat16)
    mc, ms1, ms2 = mc_ref[...], ms1_ref[...], ms2_ref[...]
    scale = math.log2(math.e) / math.sqrt(MLA_QK)

    def qk_norm_rope(t, g_ref):
        t = t * lax.rsqrt(jnp.sum(t * t, axis=-1, keepdims=True) * (1.0 / MLA_QK) + EPS) * g_ref[...]
        return _apply_rope(t, mc, ms1, ms2, MLA_ROPE // 2)

    for hd in range(MLA_HEADS):
        blk = slice(hd * HEAD_LANES, (hd + 1) * HEAD_LANES)
        q_ref[:, blk] = (qk_norm_rope(q[:, blk], gqkq_ref) * scale).astype(jnp.bfloat16)
        k_ref[:, blk] = qk_norm_rope(kn[:, blk] + kpe, gqkk_ref).astype(jnp.bfloat16)

    rc, rs1, rs2 = rc_ref[...], rs1_ref[...], rs2_ref[...]
    rq = seg(3)
    rk = seg(4)
    for j in range(RET_QK_WIDE // LANES):
        blk = slice(j * LANES, (j + 1) * LANES)
        rq_ref[:, blk] = _apply_rope(rq[:, blk], rc, rs1, rs2, RET_DK // 2).astype(jnp.bfloat16)
        rk_ref[:, blk] = (_apply_rope(rk[:, blk], rc, rs1, rs2, RET_DK // 2) * (RET_DK ** -0.5)).astype(jnp.bfloat16)
    rv_ref[...] = seg(5).astype(jnp.bfloat16)
    gate = seg(6)
    sg_ref[...] = (gate * jax.nn.sigmoid(gate)).astype(jnp.bfloat16)
    ga_ref[...] = jax.nn.sigmoid(seg(7)).astype(jnp.bfloat16)
    gr_ref[...] = jax.nn.sigmoid(seg(8)).astype(jnp.bfloat16)


def _attention_kernel(q_ref, k_ref, v_ref, o_ref, m_ref, l_ref, acc_ref):
    qi = pl.program_id(2)
    t = ATTN_TILE
    n_heads = q_ref.shape[1] // HEAD_LANES
    n_lane_tiles = t // LANES
    m_ref[...] = jnp.full(m_ref.shape, MASK_VALUE, jnp.float32)
    l_ref[...] = jnp.zeros(l_ref.shape, jnp.float32)
    acc_ref[...] = jnp.zeros(acc_ref.shape, jnp.float32)

    def lane_tiles(x):
        return [x[:, i * LANES:(i + 1) * LANES] for i in range(n_lane_tiles)]

    def scores(j, hd, masked):
        blk = slice(hd * HEAD_LANES, (hd + 1) * HEAD_LANES)
        start = pl.multiple_of(j * t, t)
        s = lax.dot_general(q_ref[:, blk], k_ref[pl.ds(start, t), blk], NT_DIMS,
                            preferred_element_type=jnp.float32)
        if masked:
            r = lax.broadcasted_iota(jnp.int32, (t, t), 0) + qi * t
            c = lax.broadcasted_iota(jnp.int32, (t, t), 1) + j * t
            s = jnp.where((c <= r) & ((c >= PAD_ROWS) | (r < PAD_ROWS)), s, MASK_VALUE)
        return s

    def max_step(j, masked):
        for hd in range(n_heads):
            m = m_ref[hd]
            for part in lane_tiles(scores(j, hd, masked)):
                m = jnp.maximum(m, part)
            m_ref[hd] = m

    def sum_step(j, masked):
        start = pl.multiple_of(j * t, t)
        for hd in range(n_heads):
            blk = slice(hd * HEAD_LANES, (hd + 1) * HEAD_LANES)
            p = jnp.exp2(scores(j, hd, masked) - jnp.tile(m_ref[hd], (1, n_lane_tiles)))
            l = l_ref[hd]
            for part in lane_tiles(p):
                l = l + part
            l_ref[hd] = l
            acc_ref[hd] += jnp.dot(p.astype(jnp.bfloat16), v_ref[pl.ds(start, t), blk],
                                   preferred_element_type=jnp.float32)

    def sweep(step):
        def body(j, carry):
            edge = (j == 0) | (j == qi)

            @pl.when(edge)
            def _():
                step(j, True)

            @pl.when(jnp.logical_not(edge))
            def _():
                step(j, False)

            return carry

        lax.fori_loop(0, qi + 1, body, 0)

    sweep(max_step)
    for hd in range(n_heads):
        m_ref[hd] = jnp.broadcast_to(jnp.max(m_ref[hd], axis=-1, keepdims=True), (t, LANES))
    sweep(sum_step)
    for hd in range(n_heads):
        blk = slice(hd * HEAD_LANES, (hd + 1) * HEAD_LANES)
        o_ref[:, blk] = (acc_ref[hd] / jnp.sum(l_ref[hd], axis=-1, keepdims=True)).astype(jnp.bfloat16)


def _retention_kernel(q_ref, k_ref, v_ref, sg_ref, gn_ref, decay_ref, zeta_ref, xi_ref, gamma_ref, o_ref):
    n_chunks = q_ref.shape[0] // CHUNK
    lane = lax.broadcasted_iota(jnp.int32, (CHUNK, LANES), 1)
    row = lax.broadcasted_iota(jnp.int32, (LANES, RET_DV), 0)
    is_a = lane < RET_DK
    gn = gn_ref[...]
    contract_rows = (((0,), (0,)), ((), ()))

    def group_norm(y):
        mu = jnp.mean(y, axis=-1, keepdims=True)
        d = y - mu
        return d * lax.rsqrt(jnp.mean(d * d, axis=-1, keepdims=True) + EPS)

    def chunk(n, state):
        rows = pl.ds(pl.multiple_of(n * CHUNK, CHUNK), CHUNK)
        qb = q_ref[rows, :]
        kb = k_ref[rows, :]
        state_bf = state.astype(jnp.bfloat16)
        new_parts = []
        for hd in range(2):
            qh = jnp.where(is_a if hd == 0 else ~is_a, qb, jnp.zeros_like(qb))
            vh = v_ref[rows, hd * RET_DV:(hd + 1) * RET_DV]
            scores = lax.dot_general(qh, kb, NT_DIMS, preferred_element_type=jnp.float32) * decay_ref[hd]
            inner = jnp.dot(scores.astype(jnp.bfloat16), vh, preferred_element_type=jnp.float32)
            cross = jnp.dot(qh, state_bf, preferred_element_type=jnp.float32) * xi_ref[hd]
            yn = group_norm(inner + cross) * gn[:, hd * RET_DV:(hd + 1) * RET_DV]
            sg = sg_ref[rows, hd * RET_DV:(hd + 1) * RET_DV].astype(jnp.float32)
            o_ref[rows, hd * RET_DV:(hd + 1) * RET_DV] = (sg * yn).astype(jnp.bfloat16)
            kz = (kb.astype(jnp.float32) * zeta_ref[hd]).astype(jnp.bfloat16)
            new_parts.append(lax.dot_general(kz, vh, contract_rows, preferred_element_type=jnp.float32))
        return gamma_ref[...] * state + jnp.where(row < RET_DK, new_parts[0], new_parts[1])

    lax.fori_loop(0, n_chunks, chunk, jnp.zeros((LANES, RET_DV), jnp.float32))


def _mixer_out_kernel(h_ref, o_ref, yr_ref, ga_ref, gr_ref, wa_ref, wr_ref, wm_ref, out_ref):
    y_a = jnp.dot(o_ref[...], wa_ref[...], preferred_element_type=jnp.float32)
    y_r = jnp.dot(yr_ref[...], wr_ref[...], preferred_element_type=jnp.float32)
    merged = ga_ref[...].astype(jnp.float32) * y_a + gr_ref[...].astype(jnp.float32) * y_r
    out_ref[...] = h_ref[...] + jnp.dot(merged.astype(jnp.bfloat16), wm_ref[...],
                                        preferred_element_type=jnp.float32)


def _pad_heads(w, heads, width):
    lead = w.shape[:-1]
    w = w.reshape(lead + (heads, width))
    w = jnp.pad(w, [(0, 0)] * len(lead) + [(0, 0), (0, HEAD_LANES - width)])
    return w.reshape(lead + (heads * HEAD_LANES,))


def _mixer(x, meta_tokens, g_mix, w_in, g_q_lora, w_uq, g_kv_lora, w_ukv, g_qk_q, g_qk_k,
           w_mla_out, g_ret_gn, w_ret_out, w_mix_out):
    B, S, D = x.shape
    bf = jnp.bfloat16
    rows = PAD_ROWS + N_META + S
    nt = rows // CHUNK
    hp = jnp.concatenate([jnp.zeros((B, PAD_ROWS, D), x.dtype),
                          jnp.broadcast_to(meta_tokens.astype(x.dtype)[None], (B, N_META, D)), x], axis=1)
    hp = hp.reshape(B * rows, D)

    segs = jnp.split(w_in, np.cumsum(IN_SIZES)[:-1].tolist(), axis=1)
    kpe_cols = jnp.pad(segs[2], ((0, 0), (MLA_NOPE, HEAD_LANES - MLA_NOPE - MLA_ROPE)))
    win = jnp.concatenate(segs[:2] + [kpe_cols] + segs[3:], axis=1).astype(bf)
    wuq = _pad_heads(w_uq, MLA_HEADS, MLA_QK).astype(bf)
    wkv = w_ukv.reshape(MLA_KV_LORA, MLA_HEADS, MLA_NOPE + MLA_V)
    wk = _pad_heads(wkv[:, :, :MLA_NOPE].reshape(MLA_KV_LORA, -1), MLA_HEADS, MLA_NOPE).astype(bf)
    wv = _pad_heads(wkv[:, :, MLA_NOPE:].reshape(MLA_KV_LORA, -1), MLA_HEADS, MLA_V).astype(bf)
    gqkq = jnp.pad(g_qk_q, (0, HEAD_LANES - MLA_QK)).reshape(1, HEAD_LANES)
    gqkk = jnp.pad(g_qk_k, (0, HEAD_LANES - MLA_QK)).reshape(1, HEAD_LANES)
    wa = _pad_heads(w_mla_out.T, MLA_HEADS, MLA_V).T.astype(bf)
    mla_tabs = _rope_tables(rows, MLA_ROPE, MLA_NOPE, HEAD_LANES)
    ret_tabs = _rope_tables(rows, RET_DK, 0, RET_DK)

    row_tile = lambda width: pl.BlockSpec((CHUNK, width), lambda b, i: (b * nt + i, 0))
    const = lambda shape: pl.BlockSpec(shape, lambda b, i: (0,) * len(shape))
    tab = pl.BlockSpec((CHUNK, LANES), lambda b, i: (i, 0))
    n_rows = B * rows
    widths = (MLA_WIDE, MLA_WIDE, MLA_WIDE, RET_QK_WIDE, RET_QK_WIDE, RET_V_WIDE, RET_V_WIDE, D, D)
    q, k, v, rq, rk, rv, sg, ga, gr = pl.pallas_call(
        _mixer_in_kernel,
        out_shape=tuple(jax.ShapeDtypeStruct((n_rows, w), bf) for w in widths),
        grid=(B, nt),
        in_specs=[row_tile(D), const((1, D)), const(win.shape), const((1, MLA_Q_LORA)), const(wuq.shape),
                  const((1, MLA_KV_LORA)), const(wk.shape), const(wv.shape), const((1, HEAD_LANES)),
                  const((1, HEAD_LANES))] + [tab] * 6,
        out_specs=tuple(row_tile(w) for w in widths),
        compiler_params=pltpu.CompilerParams(dimension_semantics=("arbitrary", "arbitrary"),
                                             vmem_limit_bytes=MIXER_VMEM_LIMIT),
        name="mixer_in",
    )(hp, g_mix.reshape(1, D), win, g_q_lora.reshape(1, -1), wuq, g_kv_lora.reshape(1, -1), wk, wv,
      gqkq, gqkk, *mla_tabs, *ret_tabs)

    group = ATTN_HEAD_GROUP * HEAD_LANES
    seq_spec = pl.BlockSpec((rows, group), lambda b, g, i: (b, g))
    n_attn = rows // ATTN_TILE
    tile_spec = pl.BlockSpec((ATTN_TILE, group), lambda b, g, i: (b * n_attn + i, g))
    o = pl.pallas_call(
        _attention_kernel,
        out_shape=jax.ShapeDtypeStruct((n_rows, MLA_WIDE), bf),
        grid=(B, MLA_HEADS // ATTN_HEAD_GROUP, n_attn),
        in_specs=[tile_spec, seq_spec, seq_spec],
        out_specs=tile_spec,
        scratch_shapes=[pltpu.VMEM((ATTN_HEAD_GROUP, ATTN_TILE, HEAD_LANES), jnp.float32)] * 3,
        compiler_params=pltpu.CompilerParams(dimension_semantics=("arbitrary",) * 3,
                                             vmem_limit_bytes=MIXER_VMEM_LIMIT),
        name="mla_attention",
    )(q, k, v)

    log_gamma = jnp.log(1.0 - 2.0 ** (-5.0 - jnp.arange(RET_HEADS, dtype=jnp.float32)))
    idx = jnp.arange(CHUNK, dtype=jnp.float32)
    diff = idx[:, None] - idx[None, :]
    decay = jnp.where(diff[None] >= 0, jnp.exp(jnp.maximum(diff, 0.0)[None] * log_gamma[:, None, None]), 0.0)
    zeta = jnp.exp((CHUNK - 1 - idx)[None, :] * log_gamma[:, None])
    xi = jnp.exp((idx + 1)[None, :] * log_gamma[:, None])
    gamma_c = jnp.exp(CHUNK * log_gamma)
    zeta_b = jnp.broadcast_to(zeta[:, :, None], (RET_HEADS, CHUNK, LANES))
    xi_b = jnp.broadcast_to(xi[:, :, None], (RET_HEADS, CHUNK, RET_DV))
    gamma_rows = jnp.broadcast_to(jnp.repeat(gamma_c, RET_DK).reshape(RET_HEADS // 2, LANES, 1),
                                  (RET_HEADS // 2, LANES, RET_DV))
    pair_seq = lambda width: pl.BlockSpec((rows, width), lambda b, p: (b, p))
    pair_tab = lambda d1, d2: pl.BlockSpec((2, d1, d2), lambda b, p: (p, 0, 0))
    yr = pl.pallas_call(
        _retention_kernel,
        out_shape=jax.ShapeDtypeStruct((n_rows, RET_V_WIDE), bf),
        grid=(B, RET_HEADS // 2),
        in_specs=[pair_seq(LANES), pair_seq(LANES), pair_seq(2 * RET_DV), pair_seq(2 * RET_DV),
                  pl.BlockSpec((1, 2 * RET_DV), lambda b, p: (0, p)),
                  pair_tab(CHUNK, CHUNK), pair_tab(CHUNK, LANES), pair_tab(CHUNK, RET_DV),
                  pl.BlockSpec((None, LANES, RET_DV), lambda b, p: (p, 0, 0))],
        out_specs=pair_seq(2 * RET_DV),
        compiler_params=pltpu.CompilerParams(dimension_semantics=("arbitrary", "arbitrary"),
                                             vmem_limit_bytes=MIXER_VMEM_LIMIT),
        name="retention",
    )(rq, rk, rv, sg, g_ret_gn.reshape(1, -1), decay, zeta_b, xi_b, gamma_rows)

    seq_tiles = S // CHUNK
    first = (PAD_ROWS + N_META) // CHUNK
    in_tile = lambda width: pl.BlockSpec((CHUNK, width), lambda b, i: (b * nt + first + i, 0))
    return pl.pallas_call(
        _mixer_out_kernel,
        out_shape=jax.ShapeDtypeStruct((B * S, D), jnp.float32),
        grid=(B, seq_tiles),
        in_specs=[in_tile(D), in_tile(MLA_WIDE), in_tile(RET_V_WIDE), in_tile(D), in_tile(D),
                  const(wa.shape), const(w_ret_out.shape), const(w_mix_out.shape)],
        out_specs=pl.BlockSpec((CHUNK, D), lambda b, i: (b * seq_tiles + i, 0)),
        compiler_params=pltpu.CompilerParams(dimension_semantics=("arbitrary", "arbitrary"),
                                             vmem_limit_bytes=MIXER_VMEM_LIMIT),
        name="mixer_out",
    )(hp, o, yr, ga, gr, wa, w_ret_out.astype(bf), w_mix_out.astype(bf))


def _top_rows(s, count):
    n = s.shape[0]
    iota = lax.broadcasted_iota(jnp.int32, s.shape, 0)
    vals, idxs = [], []
    for _ in range(count):
        m = jnp.max(s, axis=0, keepdims=True)
        idx = jnp.min(jnp.where(s == m, iota, n), axis=0, keepdims=True)
        vals.append(m)
        idxs.append(idx)
        s = jnp.where(iota == idx, -jnp.inf, s)
    return vals, idxs


def _route_head(q, k1, k2):
    s1 = lax.dot_general(k1, q, NT_DIMS, preferred_element_type=jnp.float32)
    s2 = lax.dot_general(k2, q, NT_DIMS, preferred_element_type=jnp.float32)
    v1, i1 = _top_rows(s1, PEER_TOPK)
    v2, i2 = _top_rows(s2, PEER_TOPK)
    v2_all = jnp.concatenate(v2, axis=0)
    i2_all = jnp.concatenate(i2, axis=0)
    n_mid = SUBLANES
    cand = [v1[0] + v2_all] + [v1[a] + v2_all[:n_mid] for a in range(1, n_mid)]
    cand.append(jnp.concatenate(v1[n_mid:], axis=0) + v2[0])
    cand_rows = [i1[0] * PEER_NKEYS + i2_all] + [i1[a] * PEER_NKEYS + i2_all[:n_mid] for a in range(1, n_mid)]
    cand_rows.append(jnp.concatenate(i1[n_mid:], axis=0) * PEER_NKEYS + i2[0])
    cand = jnp.concatenate(cand, axis=0)
    cand_rows = jnp.concatenate(cand_rows, axis=0) * WORD_ROWS
    sc, ci = _top_rows(cand, PEER_TOPK)
    iota = lax.broadcasted_iota(jnp.int32, cand.shape, 0)
    rows = [jnp.max(jnp.where(iota == c, cand_rows, -1), axis=0, keepdims=True) for c in ci]
    ex = [jnp.exp(v - sc[0]) for v in sc]
    denom = ex[0]
    for e in ex[1:]:
        denom = denom + e
    return jnp.concatenate(rows, axis=0), jnp.concatenate(ex, axis=0) / denom


def _peer_route_kernel(h_ref, g_ref, wq_ref, k1_ref, k2_ref, xn_ref, rows_ref, gates_ref,
                       xb_ref, rows_scr, gates_scr):
    grp = pl.program_id(1)

    @pl.when(grp == 0)
    def _():
        x = h_ref[...]
        xn = x * lax.rsqrt(jnp.mean(x * x, axis=-1, keepdims=True) + EPS) * g_ref[...]
        xn_ref[...] = xn
        xb_ref[...] = xn.astype(jnp.bfloat16)

    q = jnp.dot(xb_ref[...], wq_ref[...], preferred_element_type=jnp.float32).astype(jnp.bfloat16)
    k1, k2 = k1_ref[...], k2_ref[...]
    for j in range(ROUTE_HEAD_GROUP):
        rows, gates = _route_head(q[:, j * PEER_DQ:(j + 1) * PEER_DQ], k1, k2)
        base = pl.multiple_of((grp * ROUTE_HEAD_GROUP + j) * PEER_TOPK, PEER_TOPK)
        rows_scr[pl.ds(base, PEER_TOPK), :] = rows
        gates_scr[pl.ds(base, PEER_TOPK), :] = gates

    @pl.when(grp == PEER_HEADS // ROUTE_HEAD_GROUP - 1)
    def _():
        rows_ref[...] = rows_scr[...].T
        gates_ref[...] = gates_scr[...].T


def _peer_route(h, g_ffn, w_peer_q, keys_1, keys_2):
    T = h.shape[0]
    tb = ROUTE_TOKEN_BLOCK
    half = PEER_DQ // 2
    zeros = jnp.zeros((PEER_NKEYS, half), jnp.bfloat16)
    k1 = jnp.concatenate([keys_1.astype(jnp.bfloat16), zeros], axis=1)
    k2 = jnp.concatenate([zeros, keys_2.astype(jnp.bfloat16)], axis=1)
    keys_spec = pl.BlockSpec((PEER_NKEYS, PEER_DQ), lambda i, g: (0, 0))
    pair_spec = pl.BlockSpec((tb, PEER_PAIRS), lambda i, g: (i, 0))
    return pl.pallas_call(
        _peer_route_kernel,
        out_shape=(jax.ShapeDtypeStruct((T, D_MODEL), jnp.float32),
                   jax.ShapeDtypeStruct((T, PEER_PAIRS), jnp.int32),
                   jax.ShapeDtypeStruct((T, PEER_PAIRS), jnp.float32)),
        grid=(T // tb, PEER_HEADS // ROUTE_HEAD_GROUP),
        in_specs=[pl.BlockSpec((tb, D_MODEL), lambda i, g: (i, 0)),
                  pl.BlockSpec((1, D_MODEL), lambda i, g: (0, 0)),
                  pl.BlockSpec((D_MODEL, ROUTE_HEAD_GROUP * PEER_DQ), lambda i, g: (0, g)),
                  keys_spec, keys_spec],
        out_specs=(pl.BlockSpec((tb, D_MODEL), lambda i, g: (i, 0)), pair_spec, pair_spec),
        scratch_shapes=[pltpu.VMEM((tb, D_MODEL), jnp.bfloat16),
                        pltpu.VMEM((PEER_PAIRS, tb), jnp.int32),
                        pltpu.VMEM((PEER_PAIRS, tb), jnp.float32)],
        compiler_params=pltpu.CompilerParams(dimension_semantics=("arbitrary", "arbitrary")),
        name="peer_route",
    )(h, g_ffn.reshape(1, D_MODEL), w_peer_q.astype(jnp.bfloat16), k1, k2)


def _pack_expert_table(tab):
    bits = lax.bitcast_convert_type(tab.astype(jnp.bfloat16), jnp.uint16).astype(jnp.uint32)
    bits = bits.reshape(tab.shape[0], WORD_ROWS, 2, LANES)
    words = bits[:, :, 0, :] | (bits[:, :, 1, :] << 16)
    return lax.bitcast_convert_type(words, jnp.int32).reshape(tab.shape[0] * WORD_ROWS, LANES)


def _gather_rows(idx_ref, tab_ref, dst_refs, t0):
    idx = [idx_ref.at[t0 + s] for s in range(len(dst_refs))]
    for k in range(PEER_PAIRS):
        for idx_t, dst_ref in zip(idx, dst_refs):
            row = pl.multiple_of(idx_t[k], WORD_ROWS)
            dst_ref[k * WORD_ROWS:(k + 1) * WORD_ROWS, :] = tab_ref[pl.ds(row, WORD_ROWS), :]


def _pipelined_tokens(n_tokens, n_slots, gather, consume):
    assert n_tokens % n_slots == 0
    gather(0)

    def step(i, carry):
        t = n_slots * i
        for s in range(n_slots):
            consume(t + s, s)
        gather(t + n_slots)
        return carry

    lax.fori_loop(0, n_tokens // n_slots - 1, step, 0)
    for s in range(n_slots):
        consume(n_tokens - n_slots + s, s)


def _diag_mask():
    lane = lax.broadcasted_iota(jnp.int32, (SUBLANES, GATHER_ROWS), 1)
    sub = lax.broadcasted_iota(jnp.int32, (SUBLANES, GATHER_ROWS), 0)
    return (lane & (SUBLANES - 1)) == sub


def _split_bf16(x):
    hi = x.astype(jnp.bfloat16)
    lo = (x - hi.astype(jnp.float32)).astype(jnp.bfloat16)
    return hi, lo


def _peer_hidden_kernel(idx_ref, x_ref, gate_ref, tab_ref, sel_ref, w_ref, diag_ref, *rows_refs):
    mask = _diag_mask()

    def gather(t0):
        _gather_rows(idx_ref, tab_ref, rows_refs, t0)

    def consume(t, slot):
        rows = pltpu.bitcast(rows_refs[slot][...], jnp.bfloat16)
        xb = x_ref[t].astype(jnp.bfloat16)
        d = lax.dot_general(xb, rows, NT_DIMS, preferred_element_type=jnp.float32)
        diag_ref[pl.ds(t, 1), :] = jnp.sum(jnp.where(mask, d, 0.0), axis=0, keepdims=True)

    _pipelined_tokens(x_ref.shape[0], len(rows_refs), gather, consume)
    hi, lo = _split_bf16(diag_ref[...])
    sel = sel_ref[...]
    s = (jnp.dot(hi, sel, preferred_element_type=jnp.float32)
         + jnp.dot(lo, sel, preferred_element_type=jnp.float32))
    act = 0.5 * s * (1.0 + lax.erf(s * (1.0 / math.sqrt(2.0))))
    w_ref[...] = gate_ref[...] * act


def _peer_output_kernel(idx_ref, w_ref, h_ref, tab_ref, expand_ref, out_ref, whi_ref, wlo_ref, *rows_refs):
    mask = _diag_mask()
    hi, lo = _split_bf16(w_ref[...])
    expand = expand_ref[...]
    whi_ref[...] = jnp.dot(hi, expand, preferred_element_type=jnp.float32)
    wlo_ref[...] = jnp.dot(lo, expand, preferred_element_type=jnp.float32)

    def gather(t0):
        _gather_rows(idx_ref, tab_ref, rows_refs, t0)

    def consume(t, slot):
        rows = pltpu.bitcast(rows_refs[slot][...], jnp.bfloat16)
        parts = []
        for w_part in (whi_ref, wlo_ref):
            w_row = jnp.broadcast_to(w_part[pl.ds(t, 1), :], (SUBLANES, GATHER_ROWS))
            parts.append(jnp.where(mask, w_row, 0.0))
        lhs = jnp.concatenate(parts, axis=0).astype(jnp.bfloat16)
        y = jnp.dot(lhs, rows, preferred_element_type=jnp.float32)
        y = y[:SUBLANES] + y[SUBLANES:]
        y_row = jnp.concatenate([y[r:r + 1, :] for r in range(SUBLANES)], axis=1)
        out_ref[pl.ds(t, 1), :] = h_ref[pl.ds(t, 1), :] + y_row

    _pipelined_tokens(h_ref.shape[0], len(rows_refs), gather, consume)


def _peer_experts(h, xn, rows, gates, peer_u, peer_v):
    T = h.shape[0]
    tb = PEER_TOKEN_BLOCK
    u_tab = _pack_expert_table(peer_u)
    v_tab = _pack_expert_table(peer_v)
    expand = (jnp.arange(PEER_PAIRS)[:, None] == jnp.arange(GATHER_ROWS)[None, :] // SUBLANES
              ).astype(jnp.bfloat16)
    params = pltpu.CompilerParams(dimension_semantics=("arbitrary",), vmem_limit_bytes=PEER_VMEM_LIMIT)
    smem_block = pl.BlockSpec((tb, PEER_PAIRS), lambda i: (i, 0), memory_space=pltpu.SMEM)
    pair_block = pl.BlockSpec((tb, PEER_PAIRS), lambda i: (i, 0))
    vmem_rows = pl.BlockSpec((tb, D_MODEL), lambda i: (i, 0))
    whole_vmem = pl.BlockSpec(memory_space=pltpu.VMEM)
    rows_scratch = pltpu.VMEM((GATHER_WORD_ROWS, LANES), jnp.int32)
    wide_scratch = pltpu.VMEM((tb, GATHER_ROWS), jnp.float32)

    w = pl.pallas_call(
        _peer_hidden_kernel,
        out_shape=jax.ShapeDtypeStruct((T, PEER_PAIRS), jnp.float32),
        grid=(T // tb,),
        in_specs=[smem_block, pl.BlockSpec((tb, SUBLANES, LANES), lambda i: (i, 0, 0)), pair_block,
                  whole_vmem, whole_vmem],
        out_specs=pair_block,
        scratch_shapes=[wide_scratch] + [rows_scratch] * HIDDEN_SLOTS,
        compiler_params=params,
        name="peer_hidden",
    )(rows, xn.reshape(T, SUBLANES, LANES), gates, u_tab, expand.T)

    return pl.pallas_call(
        _peer_output_kernel,
        out_shape=jax.ShapeDtypeStruct((T, D_MODEL), jnp.float32),
        grid=(T // tb,),
        in_specs=[smem_block, pair_block, vmem_rows, whole_vmem, whole_vmem],
        out_specs=vmem_rows,
        scratch_shapes=[wide_scratch, wide_scratch] + [rows_scratch] * OUTPUT_SLOTS,
        compiler_params=params,
        name="peer_output",
    )(rows, w, h, v_tab, expand)


def kernel(x, meta_tokens, g_mix, w_in, g_q_lora, w_uq, g_kv_lora, w_ukv, g_qk_q, g_qk_k, w_mla_out, g_ret_gn, w_ret_out, w_mix_out, g_ffn, w_peer_q, peer_keys_1, peer_keys_2, peer_u, peer_v):
    B, S, D = x.shape
    h = _mixer(x, meta_tokens, g_mix[0], w_in[0], g_q_lora[0], w_uq[0], g_kv_lora[0], w_ukv[0],
               g_qk_q[0], g_qk_k[0], w_mla_out[0], g_ret_gn[0], w_ret_out[0], w_mix_out[0])
    xn, rows, gates = _peer_route(h, g_ffn[0], w_peer_q[0], peer_keys_1[0], peer_keys_2[0])
    out = _peer_experts(h, xn, rows, gates, peer_u[0], peer_v[0])
    return out.reshape(B, S, D)
```

```python
import math

import jax
import jax.numpy as jnp
import numpy as np
from jax import lax
from jax.experimental import pallas as pl
from jax.experimental.pallas import tpu as pltpu

D_MODEL = 1024
N_META = 16
CHUNK = 128
ROPE_BASE = 10000.0
EPS = 1e-6
MLA_HEADS = 8
MLA_Q_LORA = 384
MLA_KV_LORA = 256
MLA_NOPE = 64
MLA_ROPE = 32
MLA_V = 64
MLA_QK = MLA_NOPE + MLA_ROPE
RET_HEADS = 8
RET_DK = 64
RET_DV = 128
PEER_HEADS = 8
PEER_NKEYS = 128
PEER_EXPERTS = PEER_NKEYS * PEER_NKEYS
PEER_DQ = 128
PEER_TOPK = 16
PEER_PAIRS = PEER_HEADS * PEER_TOPK
IN_SIZES = (MLA_Q_LORA, MLA_KV_LORA, MLA_ROPE, RET_HEADS * RET_DK, RET_HEADS * RET_DK,
            RET_HEADS * RET_DV, RET_HEADS * RET_DV, D_MODEL, D_MODEL)

LANES = 128
SUBLANES = 8
WORD_ROWS = D_MODEL // (2 * LANES)
GATHER_WORD_ROWS = PEER_PAIRS * WORD_ROWS
GATHER_ROWS = 2 * GATHER_WORD_ROWS
PEER_TOKEN_BLOCK = 128
ROUTE_TOKEN_BLOCK = 128
ROUTE_HEAD_GROUP = 4
HIDDEN_SLOTS = 16
OUTPUT_SLOTS = 16
PEER_VMEM_LIMIT = 56 * 1024 * 1024
NT_DIMS = (((1,), (1,)), ((), ()))


def _rmsnorm(x, g):
    xf = x.astype(jnp.float32)
    y = xf * lax.rsqrt(jnp.mean(xf * xf, axis=-1, keepdims=True) + EPS)
    return (y * g.astype(jnp.float32)).astype(x.dtype)


def _rope(x, pos):
    half = x.shape[-1] // 2
    inv = ROPE_BASE ** (-jnp.arange(half, dtype=jnp.float32) / half)
    ang = pos.astype(jnp.float32)[:, None] * inv[None, :]
    cos, sin = jnp.cos(ang), jnp.sin(ang)
    xf = x.astype(jnp.float32)
    x1, x2 = xf[..., :half], xf[..., half:]
    return jnp.concatenate([x1 * cos - x2 * sin, x1 * sin + x2 * cos], -1).astype(x.dtype)


def _causal_block_attention(q, k, v):
    B, H, L, dqk = q.shape
    nblk = -(-L // CHUNK)
    Lp = nblk * CHUNK
    qp = jnp.pad(q, ((0, 0), (0, 0), (0, Lp - L), (0, 0)))
    kpos = jnp.arange(L)
    scale = 1.0 / math.sqrt(dqk)

    def one_block(i):
        qb = lax.dynamic_slice_in_dim(qp, i * CHUNK, CHUNK, axis=2)
        s = jnp.einsum('bhqd,bhkd->bhqk', qb, k).astype(jnp.float32) * scale
        qpos = i * CHUNK + jnp.arange(CHUNK)
        s = jnp.where(kpos[None, :] <= qpos[:, None], s, -jnp.inf)
        p = jax.nn.softmax(s, axis=-1)
        return jnp.einsum('bhqk,bhkd->bhqd', p.astype(v.dtype), v)

    out = lax.map(one_block, jnp.arange(nblk))
    out = out.transpose(1, 2, 0, 3, 4).reshape(B, H, Lp, v.shape[-1])
    return out[:, :, :L]


def _mla_branch(c_q, c_kv, k_pe, pos, g_q_lora, w_uq, g_kv_lora, w_ukv, g_qk_q, g_qk_k):
    B, L, _ = c_q.shape
    q = (_rmsnorm(c_q, g_q_lora) @ w_uq).reshape(B, L, MLA_HEADS, MLA_QK)
    kv = (_rmsnorm(c_kv, g_kv_lora) @ w_ukv).reshape(B, L, MLA_HEADS, MLA_NOPE + MLA_V)
    k_nope, v = kv[..., :MLA_NOPE], kv[..., MLA_NOPE:]
    k = jnp.concatenate([k_nope, jnp.broadcast_to(k_pe[:, :, None, :], (B, L, MLA_HEADS, MLA_ROPE))], -1)
    q = _rmsnorm(q, g_qk_q).transpose(0, 2, 1, 3)
    k = _rmsnorm(k, g_qk_k).transpose(0, 2, 1, 3)
    q = jnp.concatenate([q[..., :MLA_NOPE], _rope(q[..., MLA_NOPE:], pos)], -1)
    k = jnp.concatenate([k[..., :MLA_NOPE], _rope(k[..., MLA_NOPE:], pos)], -1)
    o = _causal_block_attention(q, k, v.transpose(0, 2, 1, 3))
    return o.transpose(0, 2, 1, 3).reshape(B, L, MLA_HEADS * MLA_V)


def _retention_branch(q, k, v, gate, pos, g_gn):
    B, L, _ = q.shape
    H, C = RET_HEADS, CHUNK
    q = _rope(q.reshape(B, L, H, RET_DK).transpose(0, 2, 1, 3), pos)
    k = _rope(k.reshape(B, L, H, RET_DK).transpose(0, 2, 1, 3), pos) * (RET_DK ** -0.5)
    v = v.reshape(B, L, H, RET_DV).transpose(0, 2, 1, 3)
    P = (-N_META) % C
    Lp = L + P
    NC = Lp // C
    padf = lambda t: jnp.pad(t, ((0, 0), (0, 0), (P, 0), (0, 0))).reshape(B, H, NC, C, t.shape[-1])
    qc, kc, vc = padf(q), padf(k), padf(v)
    log_gamma = jnp.log(1.0 - 2.0 ** (-5.0 - jnp.arange(H, dtype=jnp.float32)))
    idx = jnp.arange(C, dtype=jnp.float32)
    diff = idx[:, None] - idx[None, :]
    decay = jnp.where(diff[None] >= 0, jnp.exp(jnp.maximum(diff, 0.0)[None] * log_gamma[:, None, None]), 0.0)
    zeta = jnp.exp((C - 1 - idx)[None, :] * log_gamma[:, None])
    xi = jnp.exp((idx + 1)[None, :] * log_gamma[:, None])
    gamma_c = jnp.exp(C * log_gamma)
    scores = jnp.einsum('bhncd,bhnmd->bhncm', qc, kc) * decay[None, :, None]
    inner = jnp.einsum('bhncm,bhnme->bhnce', scores, vc)
    chunk_kv = jnp.einsum('bhnmd,bhnme->bhnde', kc * zeta[None, :, None, :, None], vc)

    def step(state, kv_n):
        return gamma_c[None, :, None, None] * state + kv_n, state

    init = jnp.zeros((B, H, RET_DK, RET_DV), chunk_kv.dtype)
    _, prev_states = lax.scan(step, init, jnp.moveaxis(chunk_kv, 2, 0))
    prev_states = jnp.moveaxis(prev_states, 0, 2)
    cross = jnp.einsum('bhncd,bhnde->bhnce', qc, prev_states) * xi[None, :, None, :, None]
    y = (inner + cross).reshape(B, H, Lp, RET_DV)[:, :, P:]
    yf = y.astype(jnp.float32)
    mu = jnp.mean(yf, -1, keepdims=True)
    var = jnp.mean((yf - mu) ** 2, -1, keepdims=True)
    yn = ((yf - mu) * lax.rsqrt(var + EPS)).transpose(0, 2, 1, 3).reshape(B, L, H * RET_DV)
    yn = (yn * g_gn.astype(jnp.float32)).astype(gate.dtype)
    return jax.nn.silu(gate) * yn


def _mixer_sublayer(h, pos, g_mix, w_in, g_q_lora, w_uq, g_kv_lora, w_ukv, g_qk_q, g_qk_k,
                    w_mla_out, g_ret_gn, w_ret_out, w_mix_out):
    hn = _rmsnorm(h, g_mix)
    z = hn @ w_in
    c_q, c_kv, k_pe, r_q, r_k, r_v, r_g, za, zr = jnp.split(z, np.cumsum(IN_SIZES)[:-1].tolist(), axis=-1)
    y_a = _mla_branch(c_q, c_kv, k_pe, pos, g_q_lora, w_uq, g_kv_lora, w_ukv, g_qk_q, g_qk_k) @ w_mla_out
    y_r = _retention_branch(r_q, r_k, r_v, r_g, pos, g_ret_gn) @ w_ret_out
    merged = jax.nn.sigmoid(za) * y_a + jax.nn.sigmoid(zr) * y_r
    return merged @ w_mix_out


PAD_ROWS = (-N_META) % CHUNK
HEAD_LANES = 128
MLA_WIDE = MLA_HEADS * HEAD_LANES
RET_QK_WIDE = RET_HEADS * RET_DK
RET_V_WIDE = RET_HEADS * RET_DV
MASK_VALUE = -1e30
MIXER_VMEM_LIMIT = 48 * 1024 * 1024
ATTN_HEAD_GROUP = 4
ATTN_TILE = 3 * CHUNK
_SEG_SIZES = (MLA_Q_LORA, MLA_KV_LORA, HEAD_LANES, RET_QK_WIDE, RET_QK_WIDE, RET_V_WIDE, RET_V_WIDE,
              D_MODEL, D_MODEL)
_SEG_STARTS = tuple(int(v) for v in np.cumsum((0,) + _SEG_SIZES))


def _rope_tables(n_rows, dim, block_offset, repeat):
    half = dim // 2
    pos = jnp.arange(n_rows, dtype=jnp.float32) - PAD_ROWS
    inv = ROPE_BASE ** (-jnp.arange(half, dtype=jnp.float32) / half)
    lane = np.arange(LANES)
    rel = (lane - block_offset) % repeat
    in_rope = (lane >= block_offset) & (rel < dim)
    ang = pos[:, None] * inv[rel % half][None, :]
    cos = jnp.where(in_rope[None, :], jnp.cos(ang), 1.0)
    sin = jnp.sin(ang)
    s1 = jnp.where((in_rope & (rel < half))[None, :], -sin, 0.0)
    s2 = jnp.where((in_rope & (rel >= half))[None, :], sin, 0.0)
    return cos, s1, s2


def _apply_rope(x, cos, s1, s2, half):
    return x * cos + pltpu.roll(x, LANES - half, 1) * s1 + pltpu.roll(x, half, 1) * s2


def _mixer_in_kernel(h_ref, gmix_ref, win_ref, gq_ref, wuq_ref, gkv_ref, wk_ref, wv_ref, gqkq_ref, gqkk_ref,
                     mc_ref, ms1_ref, ms2_ref, rc_ref, rs1_ref, rs2_ref,
                     q_ref, k_ref, v_ref, rq_ref, rk_ref, rv_ref, sg_ref, ga_ref, gr_ref):
    x = h_ref[...]
    hn = (x * lax.rsqrt(jnp.mean(x * x, axis=-1, keepdims=True) + EPS) * gmix_ref[...]).astype(jnp.bfloat16)

    def seg(i):
        return jnp.dot(hn, win_ref[:, _SEG_STARTS[i]:_SEG_STARTS[i + 1]], preferred_element_type=jnp.float32)

    def lora_norm(c, g_ref):
        return (c * lax.rsqrt(jnp.mean(c * c, axis=-1, keepdims=True) + EPS) * g_ref[...]).astype(jnp.bfloat16)

    cq = lora_norm(seg(0), gq_ref)
    ckv = lora_norm(seg(1), gkv_ref)
    kpe = seg(2)
    q = jnp.dot(cq, wuq_ref[...], preferred_element_type=jnp.float32)
    kn = jnp.dot(ckv, wk_ref[...], preferred_element_type=jnp.float32)
    v_ref[...] = jnp.dot(ckv, wv_ref[...], preferred_element_type=jnp.float32).astype(jnp.bfloat16)
    mc, ms1, ms2 = mc_ref[...], ms1_ref[...], ms2_ref[...]
    scale = math.log2(math.e) / math.sqrt(MLA_QK)

    def qk_norm_rope(t, g_ref):
        t = t * lax.rsqrt(jnp.sum(t * t, axis=-1, keepdims=True) * (1.0 / MLA_QK) + EPS) * g_ref[...]
        return _apply_rope(t, mc, ms1, ms2, MLA_ROPE // 2)

    for hd in range(MLA_HEADS):
        blk = slice(hd * HEAD_LANES, (hd + 1) * HEAD_LANES)
        q_ref[:, blk] = (qk_norm_rope(q[:, blk], gqkq_ref) * scale).astype(jnp.bfloat16)
        k_ref[:, blk] = qk_norm_rope(kn[:, blk] + kpe, gqkk_ref).astype(jnp.bfloat16)

    rc, rs1, rs2 = rc_ref[...], rs1_ref[...], rs2_ref[...]
    rq = seg(3)
    rk = seg(4)
    for j in range(RET_QK_WIDE // LANES):
        blk = slice(j * LANES, (j + 1) * LANES)
        rq_ref[:, blk] = _apply_rope(rq[:, blk], rc, rs1, rs2, RET_DK // 2).astype(jnp.bfloat16)
        rk_ref[:, blk] = (_apply_rope(rk[:, blk], rc, rs1, rs2, RET_DK // 2) * (RET_DK ** -0.5)).astype(jnp.bfloat16)
    rv_ref[...] = seg(5).astype(jnp.bfloat16)
    gate = seg(6)
    sg_ref[...] = (gate * jax.nn.sigmoid(gate)).astype(jnp.bfloat16)
    ga_ref[...] = jax.nn.sigmoid(seg(7)).astype(jnp.bfloat16)
    gr_ref[...] = jax.nn.sigmoid(seg(8)).astype(jnp.bfloat16)


def _attention_kernel(q_ref, k_ref, v_ref, o_ref, m_ref, l_ref, acc_ref):
    qi = pl.program_id(2)
    t = ATTN_TILE
    n_heads = q_ref.shape[1] // HEAD_LANES
    n_lane_tiles = t // LANES
    m_ref[...] = jnp.full(m_ref.shape, MASK_VALUE, jnp.float32)
    l_ref[...] = jnp.zeros(l_ref.shape, jnp.float32)
    acc_ref[...] = jnp.zeros(acc_ref.shape, jnp.float32)

    def lane_tiles(x):
        return [x[:, i * LANES:(i + 1) * LANES] for i in range(n_lane_tiles)]

    def scores(j, hd, masked):
        blk = slice(hd * HEAD_LANES, (hd + 1) * HEAD_LANES)
        start = pl.multiple_of(j * t, t)
        s = lax.dot_general(q_ref[:, blk], k_ref[pl.ds(start, t), blk], NT_DIMS,
                            preferred_element_type=jnp.float32)
        if masked:
            r = lax.broadcasted_iota(jnp.int32, (t, t), 0) + qi * t
            c = lax.broadcasted_iota(jnp.int32, (t, t), 1) + j * t
            s = jnp.where((c <= r) & ((c >= PAD_ROWS) | (r < PAD_ROWS)), s, MASK_VALUE)
        return s

    def max_step(j, masked):
        for hd in range(n_heads):
            m = m_ref[hd]
            for part in lane_tiles(scores(j, hd, masked)):
                m = jnp.maximum(m, part)
            m_ref[hd] = m

    def sum_step(j, masked):
        start = pl.multiple_of(j * t, t)
        for hd in range(n_heads):
            blk = slice(hd * HEAD_LANES, (hd + 1) * HEAD_LANES)
            p = jnp.exp2(scores(j, hd, masked) - jnp.tile(m_ref[hd], (1, n_lane_tiles)))
            l = l_ref[hd]
            for part in lane_tiles(p):
                l = l + part
            l_ref[hd] = l
            acc_ref[hd] += jnp.dot(p.astype(jnp.bfloat16), v_ref[pl.ds(start, t), blk],
                                   preferred_element_type=jnp.float32)

    def sweep(step):
        def body(j, carry):
            edge = (j == 0) | (j == qi)

            @pl.when(edge)
            def _():
                step(j, True)

            @pl.when(jnp.logical_not(edge))
            def _():
                step(j, False)

            return carry

        lax.fori_loop(0, qi + 1, body, 0)

    sweep(max_step)
    for hd in range(n_heads):
        m_ref[hd] = jnp.broadcast_to(jnp.max(m_ref[hd], axis=-1, keepdims=True), (t, LANES))
    sweep(sum_step)
    for hd in range(n_heads):
        blk = slice(hd * HEAD_LANES, (hd + 1) * HEAD_LANES)
        o_ref[:, blk] = (acc_ref[hd] / jnp.sum(l_ref[hd], axis=-1, keepdims=True)).astype(jnp.bfloat16)


def _retention_kernel(q_ref, k_ref, v_ref, sg_ref, gn_ref, decay_ref, zeta_ref, xi_ref, gamma_ref, o_ref):
    n_chunks = q_ref.shape[0] // CHUNK
    lane = lax.broadcasted_iota(jnp.int32, (CHUNK, LANES), 1)
    row = lax.broadcasted_iota(jnp.int32, (LANES, RET_DV), 0)
    is_a = lane < RET_DK
    gn = gn_ref[...]
    contract_rows = (((0,), (0,)), ((), ()))

    def group_norm(y):
        mu = jnp.mean(y, axis=-1, keepdims=True)
        d = y - mu
        return d * lax.rsqrt(jnp.mean(d * d, axis=-1, keepdims=True) + EPS)

    def chunk(n, state):
        rows = pl.ds(pl.multiple_of(n * CHUNK, CHUNK), CHUNK)
        qb = q_ref[rows, :]
        kb = k_ref[rows, :]
        state_bf = state.astype(jnp.bfloat16)
        new_parts = []
        for hd in range(2):
            qh = jnp.where(is_a if hd == 0 else ~is_a, qb, jnp.zeros_like(qb))
            vh = v_ref[rows, hd * RET_DV:(hd + 1) * RET_DV]
            scores = lax.dot_general(qh, kb, NT_DIMS, preferred_element_type=jnp.float32) * decay_ref[hd]
            inner = jnp.dot(scores.astype(jnp.bfloat16), vh, preferred_element_type=jnp.float32)
            cross = jnp.dot(qh, state_bf, preferred_element_type=jnp.float32) * xi_ref[hd]
            yn = group_norm(inner + cross) * gn[:, hd * RET_DV:(hd + 1) * RET_DV]
            sg = sg_ref[rows, hd * RET_DV:(hd + 1) * RET_DV].astype(jnp.float32)
            o_ref[rows, hd * RET_DV:(hd + 1) * RET_DV] = (sg * yn).astype(jnp.bfloat16)
            kz = (kb.astype(jnp.float32) * zeta_ref[hd]).astype(jnp.bfloat16)
            new_parts.append(lax.dot_general(kz, vh, contract_rows, preferred_element_type=jnp.float32))
        return gamma_ref[...] * state + jnp.where(row < RET_DK, new_parts[0], new_parts[1])

    lax.fori_loop(0, n_chunks, chunk, jnp.zeros((LANES, RET_DV), jnp.float32))


def _mixer_out_kernel(h_ref, o_ref, yr_ref, ga_ref, gr_ref, wa_ref, wr_ref, wm_ref, out_ref):
    y_a = jnp.dot(o_ref[...], wa_ref[...], preferred_element_type=jnp.float32)
    y_r = jnp.dot(yr_ref[...], wr_ref[...], preferred_element_type=jnp.float32)
    merged = ga_ref[...].astype(jnp.float32) * y_a + gr_ref[...].astype(jnp.float32) * y_r
    out_ref[...] = h_ref[...] + jnp.dot(merged.astype(jnp.bfloat16), wm_ref[...],
                                        preferred_element_type=jnp.float32)


def _pad_heads(w, heads, width):
    lead = w.shape[:-1]
    w = w.reshape(lead + (heads, width))
    w = jnp.pad(w, [(0, 0)] * len(lead) + [(0, 0), (0, HEAD_LANES - width)])
    return w.reshape(lead + (heads * HEAD_LANES,))


def _mixer(x, meta_tokens, g_mix, w_in, g_q_lora, w_uq, g_kv_lora, w_ukv, g_qk_q, g_qk_k,
           w_mla_out, g_ret_gn, w_ret_out, w_mix_out):
    B, S, D = x.shape
    bf = jnp.bfloat16
    rows = PAD_ROWS + N_META + S
    nt = rows // CHUNK
    hp = jnp.concatenate([jnp.zeros((B, PAD_ROWS, D), x.dtype),
                          jnp.broadcast_to(meta_tokens.astype(x.dtype)[None], (B, N_META, D)), x], axis=1)
    hp = hp.reshape(B * rows, D)

    segs = jnp.split(w_in, np.cumsum(IN_SIZES)[:-1].tolist(), axis=1)
    kpe_cols = jnp.pad(segs[2], ((0, 0), (MLA_NOPE, HEAD_LANES - MLA_NOPE - MLA_ROPE)))
    win = jnp.concatenate(segs[:2] + [kpe_cols] + segs[3:], axis=1).astype(bf)
    wuq = _pad_heads(w_uq, MLA_HEADS, MLA_QK).astype(bf)
    wkv = w_ukv.reshape(MLA_KV_LORA, MLA_HEADS, MLA_NOPE + MLA_V)
    wk = _pad_heads(wkv[:, :, :MLA_NOPE].reshape(MLA_KV_LORA, -1), MLA_HEADS, MLA_NOPE).astype(bf)
    wv = _pad_heads(wkv[:, :, MLA_NOPE:].reshape(MLA_KV_LORA, -1), MLA_HEADS, MLA_V).astype(bf)
    gqkq = jnp.pad(g_qk_q, (0, HEAD_LANES - MLA_QK)).reshape(1, HEAD_LANES)
    gqkk = jnp.pad(g_qk_k, (0, HEAD_LANES - MLA_QK)).reshape(1, HEAD_LANES)
    wa = _pad_heads(w_mla_out.T, MLA_HEADS, MLA_V).T.astype(bf)
    mla_tabs = _rope_tables(rows, MLA_ROPE, MLA_NOPE, HEAD_LANES)
    ret_tabs = _rope_tables(rows, RET_DK, 0, RET_DK)

    row_tile = lambda width: pl.BlockSpec((CHUNK, width), lambda b, i: (b * nt + i, 0))
    const = lambda shape: pl.BlockSpec(shape, lambda b, i: (0,) * len(shape))
    tab = pl.BlockSpec((CHUNK, LANES), lambda b, i: (i, 0))
    n_rows = B * rows
    widths = (MLA_WIDE, MLA_WIDE, MLA_WIDE, RET_QK_WIDE, RET_QK_WIDE, RET_V_WIDE, RET_V_WIDE, D, D)
    q, k, v, rq, rk, rv, sg, ga, gr = pl.pallas_call(
        _mixer_in_kernel,
        out_shape=tuple(jax.ShapeDtypeStruct((n_rows, w), bf) for w in widths),
        grid=(B, nt),
        in_specs=[row_tile(D), const((1, D)), const(win.shape), const((1, MLA_Q_LORA)), const(wuq.shape),
                  const((1, MLA_KV_LORA)), const(wk.shape), const(wv.shape), const((1, HEAD_LANES)),
                  const((1, HEAD_LANES))] + [tab] * 6,
        out_specs=tuple(row_tile(w) for w in widths),
        compiler_params=pltpu.CompilerParams(dimension_semantics=("arbitrary", "arbitrary"),
                                             vmem_limit_bytes=MIXER_VMEM_LIMIT),
        name="mixer_in",
    )(hp, g_mix.reshape(1, D), win, g_q_lora.reshape(1, -1), wuq, g_kv_lora.reshape(1, -1), wk, wv,
      gqkq, gqkk, *mla_tabs, *ret_tabs)

    group = ATTN_HEAD_GROUP * HEAD_LANES
    seq_spec = pl.BlockSpec((rows, group), lambda b, g, i: (b, g))
    n_attn = rows // ATTN_TILE
    tile_spec = pl.BlockSpec((ATTN_TILE, group), lambda b, g, i: (b * n_attn + i, g))
    o = pl.pallas_call(
        _attention_kernel,
        out_shape=jax.ShapeDtypeStruct((n_rows, MLA_WIDE), bf),
        grid=(B, MLA_HEADS // ATTN_HEAD_GROUP, n_attn),
        in_specs=[tile_spec, seq_spec, seq_spec],
        out_specs=tile_spec,
        scratch_shapes=[pltpu.VMEM((ATTN_HEAD_GROUP, ATTN_TILE, HEAD_LANES), jnp.float32)] * 3,
        compiler_params=pltpu.CompilerParams(dimension_semantics=("arbitrary",) * 3,
                                             vmem_limit_bytes=MIXER_VMEM_LIMIT),
        name="mla_attention",
    )(q, k, v)

    log_gamma = jnp.log(1.0 - 2.0 ** (-5.0 - jnp.arange(RET_HEADS, dtype=jnp.float32)))
    idx = jnp.arange(CHUNK, dtype=jnp.float32)
    diff = idx[:, None] - idx[None, :]
    decay = jnp.where(diff[None] >= 0, jnp.exp(jnp.maximum(diff, 0.0)[None] * log_gamma[:, None, None]), 0.0)
    zeta = jnp.exp((CHUNK - 1 - idx)[None, :] * log_gamma[:, None])
    xi = jnp.exp((idx + 1)[None, :] * log_gamma[:, None])
    gamma_c = jnp.exp(CHUNK * log_gamma)
    zeta_b = jnp.broadcast_to(zeta[:, :, None], (RET_HEADS, CHUNK, LANES))
    xi_b = jnp.broadcast_to(xi[:, :, None], (RET_HEADS, CHUNK, RET_DV))
    gamma_rows = jnp.broadcast_to(jnp.repeat(gamma_c, RET_DK).reshape(RET_HEADS // 2, LANES, 1),
                                  (RET_HEADS // 2, LANES, RET_DV))
    pair_seq = lambda width: pl.BlockSpec((rows, width), lambda b, p: (b, p))
    pair_tab = lambda d1, d2: pl.BlockSpec((2, d1, d2), lambda b, p: (p, 0, 0))
    yr = pl.pallas_call(
        _retention_kernel,
        out_shape=jax.ShapeDtypeStruct((n_rows, RET_V_WIDE), bf),
        grid=(B, RET_HEADS // 2),
        in_specs=[pair_seq(LANES), pair_seq(LANES), pair_seq(2 * RET_DV), pair_seq(2 * RET_DV),
                  pl.BlockSpec((1, 2 * RET_DV), lambda b, p: (0, p)),
                  pair_tab(CHUNK, CHUNK), pair_tab(CHUNK, LANES), pair_tab(CHUNK, RET_DV),
                  pl.BlockSpec((None, LANES, RET_DV), lambda b, p: (p, 0, 0))],
        out_specs=pair_seq(2 * RET_DV),
        compiler_params=pltpu.CompilerParams(dimension_semantics=("arbitrary", "arbitrary"),
                                             vmem_limit_bytes=MIXER_VMEM_LIMIT),
        name="retention",
    )(rq, rk, rv, sg, g_ret_gn.reshape(1, -1), decay, zeta_b, xi_b, gamma_rows)

    seq_tiles = S // CHUNK
    first = (PAD_ROWS + N_META) // CHUNK
    in_tile = lambda width: pl.BlockSpec((CHUNK, width), lambda b, i: (b * nt + first + i, 0))
    return pl.pallas_call(
        _mixer_out_kernel,
        out_shape=jax.ShapeDtypeStruct((B * S, D), jnp.float32),
        grid=(B, seq_tiles),
        in_specs=[in_tile(D), in_tile(MLA_WIDE), in_tile(RET_V_WIDE), in_tile(D), in_tile(D),
                  const(wa.shape), const(w_ret_out.shape), const(w_mix_out.shape)],
        out_specs=pl.BlockSpec((CHUNK, D), lambda b, i: (b * seq_tiles + i, 0)),
        compiler_params=pltpu.CompilerParams(dimension_semantics=("arbitrary", "arbitrary"),
                                             vmem_limit_bytes=MIXER_VMEM_LIMIT),
        name="mixer_out",
    )(hp, o, yr, ga, gr, wa, w_ret_out.astype(bf), w_mix_out.astype(bf))


def _top_rows(s, count):
    n = s.shape[0]
    iota = lax.broadcasted_iota(jnp.int32, s.shape, 0)
    vals, idxs = [], []
    for _ in range(count):
        m = jnp.max(s, axis=0, keepdims=True)
        idx = jnp.min(jnp.where(s == m, iota, n), axis=0, keepdims=True)
        vals.append(m)
        idxs.append(idx)
        s = jnp.where(iota == idx, -jnp.inf, s)
    return vals, idxs


def _route_head(q, k1, k2):
    s1 = lax.dot_general(k1, q, NT_DIMS, preferred_element_type=jnp.float32)
    s2 = lax.dot_general(k2, q, NT_DIMS, preferred_element_type=jnp.float32)
    v1, i1 = _top_rows(s1, PEER_TOPK)
    v2, i2 = _top_rows(s2, PEER_TOPK)
    v2_all = jnp.concatenate(v2, axis=0)
    i2_all = jnp.concatenate(i2, axis=0)
    n_mid = SUBLANES
    cand = [v1[0] + v2_all] + [v1[a] + v2_all[:n_mid] for a in range(1, n_mid)]
    cand.append(jnp.concatenate(v1[n_mid:], axis=0) + v2[0])
    cand_rows = [i1[0] * PEER_NKEYS + i2_all] + [i1[a] * PEER_NKEYS + i2_all[:n_mid] for a in range(1, n_mid)]
    cand_rows.append(jnp.concatenate(i1[n_mid:], axis=0) * PEER_NKEYS + i2[0])
    cand = jnp.concatenate(cand, axis=0)
    cand_rows = jnp.concatenate(cand_rows, axis=0) * WORD_ROWS
    sc, ci = _top_rows(cand, PEER_TOPK)
    iota = lax.broadcasted_iota(jnp.int32, cand.shape, 0)
    rows = [jnp.max(jnp.where(iota == c, cand_rows, -1), axis=0, keepdims=True) for c in ci]
    ex = [jnp.exp(v - sc[0]) for v in sc]
    denom = ex[0]
    for e in ex[1:]:
        denom = denom + e
    return jnp.concatenate(rows, axis=0), jnp.concatenate(ex, axis=0) / denom


def _peer_route_kernel(h_ref, g_ref, wq_ref, k1_ref, k2_ref, xn_ref, rows_ref, gates_ref,
                       xb_ref, rows_scr, gates_scr):
    grp = pl.program_id(1)

    @pl.when(grp == 0)
    def _():
        x = h_ref[...]
        xn = x * lax.rsqrt(jnp.mean(x * x, axis=-1, keepdims=True) + EPS) * g_ref[...]
        xn_ref[...] = xn
        xb_ref[...] = xn.astype(jnp.bfloat16)

    q = jnp.dot(xb_ref[...], wq_ref[...], preferred_element_type=jnp.float32).astype(jnp.bfloat16)
    k1, k2 = k1_ref[...], k2_ref[...]
    for j in range(ROUTE_HEAD_GROUP):
        rows, gates = _route_head(q[:, j * PEER_DQ:(j + 1) * PEER_DQ], k1, k2)
        base = pl.multiple_of((grp * ROUTE_HEAD_GROUP + j) * PEER_TOPK, PEER_TOPK)
        rows_scr[pl.ds(base, PEER_TOPK), :] = rows
        gates_scr[pl.ds(base, PEER_TOPK), :] = gates

    @pl.when(grp == PEER_HEADS // ROUTE_HEAD_GROUP - 1)
    def _():
        rows_ref[...] = rows_scr[...]
        gates_ref[...] = gates_scr[...].T


def _peer_route(h, g_ffn, w_peer_q, keys_1, keys_2):
    T = h.shape[0]
    tb = ROUTE_TOKEN_BLOCK
    per_block = PEER_TOKEN_BLOCK // tb
    half = PEER_DQ // 2
    zeros = jnp.zeros((PEER_NKEYS, half), jnp.bfloat16)
    k1 = jnp.concatenate([keys_1.astype(jnp.bfloat16), zeros], axis=1)
    k2 = jnp.concatenate([zeros, keys_2.astype(jnp.bfloat16)], axis=1)
    keys_spec = pl.BlockSpec((PEER_NKEYS, PEER_DQ), lambda i, g: (0, 0))
    pair_spec = pl.BlockSpec((tb, PEER_PAIRS), lambda i, g: (i, 0))
    return pl.pallas_call(
        _peer_route_kernel,
        out_shape=(jax.ShapeDtypeStruct((T, D_MODEL), jnp.float32),
                   jax.ShapeDtypeStruct((T // PEER_TOKEN_BLOCK, PEER_PAIRS, PEER_TOKEN_BLOCK), jnp.int32),
                   jax.ShapeDtypeStruct((T, PEER_PAIRS), jnp.float32)),
        grid=(T // tb, PEER_HEADS // ROUTE_HEAD_GROUP),
        in_specs=[pl.BlockSpec((tb, D_MODEL), lambda i, g: (i, 0)),
                  pl.BlockSpec((1, D_MODEL), lambda i, g: (0, 0)),
                  pl.BlockSpec((D_MODEL, ROUTE_HEAD_GROUP * PEER_DQ), lambda i, g: (0, g)),
                  keys_spec, keys_spec],
        out_specs=(pl.BlockSpec((tb, D_MODEL), lambda i, g: (i, 0)),
                   pl.BlockSpec((None, PEER_PAIRS, tb), lambda i, g: (i // per_block, 0, i % per_block)),
                   pair_spec),
        scratch_shapes=[pltpu.VMEM((tb, D_MODEL), jnp.bfloat16),
                        pltpu.VMEM((PEER_PAIRS, tb), jnp.int32),
                        pltpu.VMEM((PEER_PAIRS, tb), jnp.float32)],
        compiler_params=pltpu.CompilerParams(dimension_semantics=("arbitrary", "arbitrary")),
        name="peer_route",
    )(h, g_ffn.reshape(1, D_MODEL), w_peer_q.astype(jnp.bfloat16), k1, k2)


def _pack_expert_table(tab):
    bits = lax.bitcast_convert_type(tab.astype(jnp.bfloat16), jnp.uint16).astype(jnp.uint32)
    bits = bits.reshape(tab.shape[0], WORD_ROWS, 2, LANES)
    words = bits[:, :, 0, :] | (bits[:, :, 1, :] << 16)
    return lax.bitcast_convert_type(words, jnp.int32).reshape(tab.shape[0] * WORD_ROWS, LANES)


def _gather_rows(idx_ref, tab_ref, dst_refs, t0):
    for k in range(PEER_PAIRS):
        idx_k = idx_ref.at[k]
        for s, dst_ref in enumerate(dst_refs):
            row = pl.multiple_of(idx_k[t0 + s], WORD_ROWS)
            dst_ref[k * WORD_ROWS:(k + 1) * WORD_ROWS, :] = tab_ref[pl.ds(row, WORD_ROWS), :]


def _pipelined_tokens(n_tokens, n_slots, gather, consume):
    assert n_tokens % n_slots == 0
    gather(0)

    def step(i, carry):
        t = n_slots * i
        for s in range(n_slots):
            consume(t + s, s)
        gather(t + n_slots)
        return carry

    lax.fori_loop(0, n_tokens // n_slots - 1, step, 0)
    for s in range(n_slots):
        consume(n_tokens - n_slots + s, s)


def _with_index_block(idx_hbm, idx_bufs, sem, body):
    i = pl.program_id(0)
    n = pl.num_programs(0)

    def copy(block, slot):
        return pltpu.make_async_copy(idx_hbm.at[block], idx_bufs[slot], sem.at[slot])

    @pl.when(i == 0)
    def _():
        copy(0, 0).start()

    def run(slot):
        copy(i, slot).wait()

        @pl.when(i + 1 < n)
        def _():
            copy(i + 1, 1 - slot).start()

        body(idx_bufs[slot])

    for slot in range(2):
        pl.when(lax.rem(i, 2) == slot)(lambda slot=slot: run(slot))


def _diag_mask():
    lane = lax.broadcasted_iota(jnp.int32, (SUBLANES, GATHER_ROWS), 1)
    sub = lax.broadcasted_iota(jnp.int32, (SUBLANES, GATHER_ROWS), 0)
    return (lane & (SUBLANES - 1)) == sub


def _split_bf16(x):
    hi = x.astype(jnp.bfloat16)
    lo = (x - hi.astype(jnp.float32)).astype(jnp.bfloat16)
    return hi, lo


def _peer_hidden_kernel(idx_hbm, x_ref, gate_ref, tab_ref, sel_ref, w_ref, diag_ref, idx_a, idx_b, sem,
                        *rows_refs):
    mask = _diag_mask()

    def consume(t, slot):
        rows = pltpu.bitcast(rows_refs[slot][...], jnp.bfloat16)
        xb = x_ref[t].astype(jnp.bfloat16)
        d = lax.dot_general(xb, rows, NT_DIMS, preferred_element_type=jnp.float32)
        diag_ref[pl.ds(t, 1), :] = jnp.sum(jnp.where(mask, d, 0.0), axis=0, keepdims=True)

    def all_tokens(idx_ref):
        _pipelined_tokens(x_ref.shape[0], len(rows_refs),
                          lambda t0: _gather_rows(idx_ref, tab_ref, rows_refs, t0), consume)

    _with_index_block(idx_hbm, (idx_a, idx_b), sem, all_tokens)
    hi, lo = _split_bf16(diag_ref[...])
    sel = sel_ref[...]
    s = (jnp.dot(hi, sel, preferred_element_type=jnp.float32)
         + jnp.dot(lo, sel, preferred_element_type=jnp.float32))
    act = 0.5 * s * (1.0 + lax.erf(s * (1.0 / math.sqrt(2.0))))
    w_ref[...] = gate_ref[...] * act


def _peer_output_kernel(idx_hbm, w_ref, h_ref, tab_ref, expand_ref, out_ref, whi_ref, wlo_ref, idx_a, idx_b, sem,
                        *rows_refs):
    mask = _diag_mask()
    hi, lo = _split_bf16(w_ref[...])
    expand = expand_ref[...]
    whi_ref[...] = jnp.dot(hi, expand, preferred_element_type=jnp.float32)
    wlo_ref[...] = jnp.dot(lo, expand, preferred_element_type=jnp.float32)

    def consume(t, slot):
        rows = pltpu.bitcast(rows_refs[slot][...], jnp.bfloat16)
        parts = []
        for w_part in (whi_ref, wlo_ref):
            w_row = jnp.broadcast_to(w_part[pl.ds(t, 1), :], (SUBLANES, GATHER_ROWS))
            parts.append(jnp.where(mask, w_row, 0.0))
        lhs = jnp.concatenate(parts, axis=0).astype(jnp.bfloat16)
        y = jnp.dot(lhs, rows, preferred_element_type=jnp.float32)
        y = y[:SUBLANES] + y[SUBLANES:]
        y_row = jnp.concatenate([y[r:r + 1, :] for r in range(SUBLANES)], axis=1)
        out_ref[pl.ds(t, 1), :] = h_ref[pl.ds(t, 1), :] + y_row

    def all_tokens(idx_ref):
        _pipelined_tokens(h_ref.shape[0], len(rows_refs),
                          lambda t0: _gather_rows(idx_ref, tab_ref, rows_refs, t0), consume)

    _with_index_block(idx_hbm, (idx_a, idx_b), sem, all_tokens)


def _peer_experts(h, xn, rows, gates, peer_u, peer_v):
    T = h.shape[0]
    tb = PEER_TOKEN_BLOCK
    u_tab = _pack_expert_table(peer_u)
    v_tab = _pack_expert_table(peer_v)
    expand = (jnp.arange(PEER_PAIRS)[:, None] == jnp.arange(GATHER_ROWS)[None, :] // SUBLANES
              ).astype(jnp.bfloat16)
    params = pltpu.CompilerParams(dimension_semantics=("arbitrary",), vmem_limit_bytes=PEER_VMEM_LIMIT)
    hbm_ref = pl.BlockSpec(memory_space=pl.ANY)
    index_scratch = [pltpu.SMEM((PEER_PAIRS, tb), jnp.int32)] * 2 + [pltpu.SemaphoreType.DMA((2,))]
    pair_block = pl.BlockSpec((tb, PEER_PAIRS), lambda i: (i, 0))
    vmem_rows = pl.BlockSpec((tb, D_MODEL), lambda i: (i, 0))
    whole_vmem = pl.BlockSpec(memory_space=pltpu.VMEM)
    rows_scratch = pltpu.VMEM((GATHER_WORD_ROWS, LANES), jnp.int32)
    wide_scratch = pltpu.VMEM((tb, GATHER_ROWS), jnp.float32)

    w = pl.pallas_call(
        _peer_hidden_kernel,
        out_shape=jax.ShapeDtypeStruct((T, PEER_PAIRS), jnp.float32),
        grid=(T // tb,),
        in_specs=[hbm_ref, pl.BlockSpec((tb, SUBLANES, LANES), lambda i: (i, 0, 0)), pair_block,
                  whole_vmem, whole_vmem],
        out_specs=pair_block,
        scratch_shapes=[wide_scratch] + index_scratch + [rows_scratch] * HIDDEN_SLOTS,
        compiler_params=params,
        name="peer_hidden",
    )(rows, xn.reshape(T, SUBLANES, LANES), gates, u_tab, expand.T)

    return pl.pallas_call(
        _peer_output_kernel,
        out_shape=jax.ShapeDtypeStruct((T, D_MODEL), jnp.float32),
        grid=(T // tb,),
        in_specs=[hbm_ref, pair_block, vmem_rows, whole_vmem, whole_vmem],
        out_specs=vmem_rows,
        scratch_shapes=[wide_scratch, wide_scratch] + index_scratch + [rows_scratch] * OUTPUT_SLOTS,
        compiler_params=params,
        name="peer_output",
    )(rows, w, h, v_tab, expand)


def kernel(x, meta_tokens, g_mix, w_in, g_q_lora, w_uq, g_kv_lora, w_ukv, g_qk_q, g_qk_k, w_mla_out, g_ret_gn, w_ret_out, w_mix_out, g_ffn, w_peer_q, peer_keys_1, peer_keys_2, peer_u, peer_v):
    B, S, D = x.shape
    h = _mixer(x, meta_tokens, g_mix[0], w_in[0], g_q_lora[0], w_uq[0], g_kv_lora[0], w_ukv[0],
               g_qk_q[0], g_qk_k[0], w_mla_out[0], g_ret_gn[0], w_ret_out[0], w_mix_out[0])
    xn, rows, gates = _peer_route(h, g_ffn[0], w_peer_q[0], peer_keys_1[0], peer_keys_2[0])
    out = _peer_experts(h, xn, rows, gates, peer_u[0], peer_v[0])
    return out.reshape(B, S, D)
```

```python
import math

import jax
import jax.numpy as jnp
import numpy as np
from jax import lax
from jax.experimental import pallas as pl
from jax.experimental.pallas import tpu as pltpu

D_MODEL = 1024
N_META = 16
CHUNK = 128
ROPE_BASE = 10000.0
EPS = 1e-6
MLA_HEADS = 8
MLA_Q_LORA = 384
MLA_KV_LORA = 256
MLA_NOPE = 64
MLA_ROPE = 32
MLA_V = 64
MLA_QK = MLA_NOPE + MLA_ROPE
RET_HEADS = 8
RET_DK = 64
RET_DV = 128
PEER_HEADS = 8
PEER_NKEYS = 128
PEER_EXPERTS = PEER_NKEYS * PEER_NKEYS
PEER_DQ = 128
PEER_TOPK = 16
PEER_PAIRS = PEER_HEADS * PEER_TOPK
IN_SIZES = (MLA_Q_LORA, MLA_KV_LORA, MLA_ROPE, RET_HEADS * RET_DK, RET_HEADS * RET_DK,
            RET_HEADS * RET_DV, RET_HEADS * RET_DV, D_MODEL, D_MODEL)

LANES = 128
SUBLANES = 8
WORD_ROWS = D_MODEL // (2 * LANES)
GATHER_WORD_ROWS = PEER_PAIRS * WORD_ROWS
GATHER_ROWS = 2 * GATHER_WORD_ROWS
PEER_TOKEN_BLOCK = LANES
ROUTE_TOKEN_BLOCK = 256
ROUTE_HEAD_GROUP = 4
HIDDEN_SLOTS = 16
OUTPUT_SLOTS = 16
PEER_VMEM_LIMIT = 56 * 1024 * 1024
NT_DIMS = (((1,), (1,)), ((), ()))


PAD_ROWS = (-N_META) % CHUNK
HEAD_LANES = 128
MLA_WIDE = MLA_HEADS * HEAD_LANES
RET_QK_WIDE = RET_HEADS * RET_DK
RET_V_WIDE = RET_HEADS * RET_DV
MASK_VALUE = -1e30
MIXER_VMEM_LIMIT = 48 * 1024 * 1024
ATTN_HEAD_GROUP = 4
ATTN_TILE = 3 * CHUNK
_SEG_SIZES = (MLA_Q_LORA, MLA_KV_LORA, HEAD_LANES, RET_QK_WIDE, RET_QK_WIDE, RET_V_WIDE, RET_V_WIDE,
              D_MODEL, D_MODEL)
_SEG_STARTS = tuple(int(v) for v in np.cumsum((0,) + _SEG_SIZES))


def _rope_tables(n_rows, dim, block_offset, repeat):
    half = dim // 2
    pos = jnp.arange(n_rows, dtype=jnp.float32) - PAD_ROWS
    inv = ROPE_BASE ** (-jnp.arange(half, dtype=jnp.float32) / half)
    lane = np.arange(LANES)
    rel = (lane - block_offset) % repeat
    in_rope = (lane >= block_offset) & (rel < dim)
    ang = pos[:, None] * inv[rel % half][None, :]
    cos = jnp.where(in_rope[None, :], jnp.cos(ang), 1.0)
    sin = jnp.sin(ang)
    s1 = jnp.where((in_rope & (rel < half))[None, :], -sin, 0.0)
    s2 = jnp.where((in_rope & (rel >= half))[None, :], sin, 0.0)
    return cos, s1, s2


def _apply_rope(x, cos, s1, s2, half):
    return x * cos + pltpu.roll(x, LANES - half, 1) * s1 + pltpu.roll(x, half, 1) * s2


def _mixer_in_kernel(h_ref, gmix_ref, win_ref, gq_ref, wuq_ref, gkv_ref, wk_ref, wv_ref, gqkq_ref, gqkk_ref,
                     mc_ref, ms1_ref, ms2_ref, rc_ref, rs1_ref, rs2_ref,
                     q_ref, k_ref, v_ref, rq_ref, rk_ref, rv_ref, sg_ref, ga_ref, gr_ref):
    x = h_ref[...]
    hn = (x * lax.rsqrt(jnp.mean(x * x, axis=-1, keepdims=True) + EPS) * gmix_ref[...]).astype(jnp.bfloat16)

    def seg(i):
        return jnp.dot(hn, win_ref[:, _SEG_STARTS[i]:_SEG_STARTS[i + 1]], preferred_element_type=jnp.float32)

    def lora_norm(c, g_ref):
        return (c * lax.rsqrt(jnp.mean(c * c, axis=-1, keepdims=True) + EPS) * g_ref[...]).astype(jnp.bfloat16)

    cq = lora_norm(seg(0), gq_ref)
    ckv = lora_norm(seg(1), gkv_ref)
    kpe = seg(2)
    q = jnp.dot(cq, wuq_ref[...], preferred_element_type=jnp.float32)
    kn = jnp.dot(ckv, wk_ref[...], preferred_element_type=jnp.float32)
    v_ref[...] = jnp.dot(ckv, wv_ref[...], preferred_element_type=jnp.float32).astype(jnp.bfloat16)
    mc, ms1, ms2 = mc_ref[...], ms1_ref[...], ms2_ref[...]
    scale = math.log2(math.e) / math.sqrt(MLA_QK)

    def qk_norm_rope(t, g_ref):
        t = t * lax.rsqrt(jnp.sum(t * t, axis=-1, keepdims=True) * (1.0 / MLA_QK) + EPS) * g_ref[...]
        return _apply_rope(t, mc, ms1, ms2, MLA_ROPE // 2)

    for hd in range(MLA_HEADS):
        blk = slice(hd * HEAD_LANES, (hd + 1) * HEAD_LANES)
        q_ref[:, blk] = (qk_norm_rope(q[:, blk], gqkq_ref) * scale).astype(jnp.bfloat16)
        k_ref[:, blk] = qk_norm_rope(kn[:, blk] + kpe, gqkk_ref).astype(jnp.bfloat16)

    rc, rs1, rs2 = rc_ref[...], rs1_ref[...], rs2_ref[...]
    rq = seg(3)
    rk = seg(4)
    for j in range(RET_QK_WIDE // LANES):
        blk = slice(j * LANES, (j + 1) * LANES)
        rq_ref[:, blk] = _apply_rope(rq[:, blk], rc, rs1, rs2, RET_DK // 2).astype(jnp.bfloat16)
        rk_ref[:, blk] = (_apply_rope(rk[:, blk], rc, rs1, rs2, RET_DK // 2) * (RET_DK ** -0.5)).astype(jnp.bfloat16)
    rv_ref[...] = seg(5).astype(jnp.bfloat16)
    gate = seg(6)
    sg_ref[...] = (gate * jax.nn.sigmoid(gate)).astype(jnp.bfloat16)
    ga_ref[...] = jax.nn.sigmoid(seg(7)).astype(jnp.bfloat16)
    gr_ref[...] = jax.nn.sigmoid(seg(8)).astype(jnp.bfloat16)


def _attention_kernel(q_ref, k_ref, v_ref, o_ref, m_ref, l_ref, acc_ref):
    qi = pl.program_id(2)
    t = ATTN_TILE
    n_heads = q_ref.shape[1] // HEAD_LANES
    n_lane_tiles = t // LANES
    m_ref[...] = jnp.full(m_ref.shape, MASK_VALUE, jnp.float32)
    l_ref[...] = jnp.zeros(l_ref.shape, jnp.float32)
    acc_ref[...] = jnp.zeros(acc_ref.shape, jnp.float32)

    def lane_tiles(x):
        return [x[:, i * LANES:(i + 1) * LANES] for i in range(n_lane_tiles)]

    def scores(j, hd, masked):
        blk = slice(hd * HEAD_LANES, (hd + 1) * HEAD_LANES)
        start = pl.multiple_of(j * t, t)
        s = lax.dot_general(q_ref[:, blk], k_ref[pl.ds(start, t), blk], NT_DIMS,
                            preferred_element_type=jnp.float32)
        if masked:
            r = lax.broadcasted_iota(jnp.int32, (t, t), 0) + qi * t
            c = lax.broadcasted_iota(jnp.int32, (t, t), 1) + j * t
            s = jnp.where((c <= r) & ((c >= PAD_ROWS) | (r < PAD_ROWS)), s, MASK_VALUE)
        return s

    def max_step(j, masked):
        for hd in range(n_heads):
            m = m_ref[hd]
            for part in lane_tiles(scores(j, hd, masked)):
                m = jnp.maximum(m, part)
            m_ref[hd] = m

    def sum_step(j, masked):
        start = pl.multiple_of(j * t, t)
        for hd in range(n_heads):
            blk = slice(hd * HEAD_LANES, (hd + 1) * HEAD_LANES)
            p = jnp.exp2(scores(j, hd, masked) - jnp.tile(m_ref[hd], (1, n_lane_tiles)))
            l = l_ref[hd]
            for part in lane_tiles(p):
                l = l + part
            l_ref[hd] = l
            acc_ref[hd] += jnp.dot(p.astype(jnp.bfloat16), v_ref[pl.ds(start, t), blk],
                                   preferred_element_type=jnp.float32)

    def sweep(step):
        def body(j, carry):
            edge = (j == 0) | (j == qi)

            @pl.when(edge)
            def _():
                step(j, True)

            @pl.when(jnp.logical_not(edge))
            def _():
                step(j, False)

            return carry

        lax.fori_loop(0, qi + 1, body, 0)

    sweep(max_step)
    for hd in range(n_heads):
        m_ref[hd] = jnp.broadcast_to(jnp.max(m_ref[hd], axis=-1, keepdims=True), (t, LANES))
    sweep(sum_step)
    for hd in range(n_heads):
        blk = slice(hd * HEAD_LANES, (hd + 1) * HEAD_LANES)
        o_ref[:, blk] = (acc_ref[hd] / jnp.sum(l_ref[hd], axis=-1, keepdims=True)).astype(jnp.bfloat16)


def _retention_kernel(q_ref, k_ref, v_ref, sg_ref, gn_ref, decay_ref, zeta_ref, xi_ref, gamma_ref, o_ref):
    n_chunks = q_ref.shape[0] // CHUNK
    lane = lax.broadcasted_iota(jnp.int32, (CHUNK, LANES), 1)
    row = lax.broadcasted_iota(jnp.int32, (LANES, RET_DV), 0)
    is_a = lane < RET_DK
    gn = gn_ref[...]
    contract_rows = (((0,), (0,)), ((), ()))

    def group_norm(y):
        mu = jnp.mean(y, axis=-1, keepdims=True)
        d = y - mu
        return d * lax.rsqrt(jnp.mean(d * d, axis=-1, keepdims=True) + EPS)

    def chunk(n, state):
        rows = pl.ds(pl.multiple_of(n * CHUNK, CHUNK), CHUNK)
        qb = q_ref[rows, :]
        kb = k_ref[rows, :]
        state_bf = state.astype(jnp.bfloat16)
        new_parts = []
        for hd in range(2):
            qh = jnp.where(is_a if hd == 0 else ~is_a, qb, jnp.zeros_like(qb))
            vh = v_ref[rows, hd * RET_DV:(hd + 1) * RET_DV]
            scores = lax.dot_general(qh, kb, NT_DIMS, preferred_element_type=jnp.float32) * decay_ref[hd]
            inner = jnp.dot(scores.astype(jnp.bfloat16), vh, preferred_element_type=jnp.float32)
            cross = jnp.dot(qh, state_bf, preferred_element_type=jnp.float32) * xi_ref[hd]
            yn = group_norm(inner + cross) * gn[:, hd * RET_DV:(hd + 1) * RET_DV]
            sg = sg_ref[rows, hd * RET_DV:(hd + 1) * RET_DV].astype(jnp.float32)
            o_ref[rows, hd * RET_DV:(hd + 1) * RET_DV] = (sg * yn).astype(jnp.bfloat16)
            kz = (kb.astype(jnp.float32) * zeta_ref[hd]).astype(jnp.bfloat16)
            new_parts.append(lax.dot_general(kz, vh, contract_rows, preferred_element_type=jnp.float32))
        return gamma_ref[...] * state + jnp.where(row < RET_DK, new_parts[0], new_parts[1])

    lax.fori_loop(0, n_chunks, chunk, jnp.zeros((LANES, RET_DV), jnp.float32))


def _mixer_out_kernel(h_ref, o_ref, yr_ref, ga_ref, gr_ref, wa_ref, wr_ref, wm_ref, out_ref):
    y_a = jnp.dot(o_ref[...], wa_ref[...], preferred_element_type=jnp.float32)
    y_r = jnp.dot(yr_ref[...], wr_ref[...], preferred_element_type=jnp.float32)
    merged = ga_ref[...].astype(jnp.float32) * y_a + gr_ref[...].astype(jnp.float32) * y_r
    out_ref[...] = h_ref[...] + jnp.dot(merged.astype(jnp.bfloat16), wm_ref[...],
                                        preferred_element_type=jnp.float32)


def _pad_heads(w, heads, width):
    lead = w.shape[:-1]
    w = w.reshape(lead + (heads, width))
    w = jnp.pad(w, [(0, 0)] * len(lead) + [(0, 0), (0, HEAD_LANES - width)])
    return w.reshape(lead + (heads * HEAD_LANES,))


def _mixer(x, meta_tokens, g_mix, w_in, g_q_lora, w_uq, g_kv_lora, w_ukv, g_qk_q, g_qk_k,
           w_mla_out, g_ret_gn, w_ret_out, w_mix_out):
    B, S, D = x.shape
    bf = jnp.bfloat16
    rows = PAD_ROWS + N_META + S
    nt = rows // CHUNK
    hp = jnp.concatenate([jnp.zeros((B, PAD_ROWS, D), x.dtype),
                          jnp.broadcast_to(meta_tokens.astype(x.dtype)[None], (B, N_META, D)), x], axis=1)
    hp = hp.reshape(B * rows, D)

    segs = jnp.split(w_in, np.cumsum(IN_SIZES)[:-1].tolist(), axis=1)
    kpe_cols = jnp.pad(segs[2], ((0, 0), (MLA_NOPE, HEAD_LANES - MLA_NOPE - MLA_ROPE)))
    win = jnp.concatenate(segs[:2] + [kpe_cols] + segs[3:], axis=1).astype(bf)
    wuq = _pad_heads(w_uq, MLA_HEADS, MLA_QK).astype(bf)
    wkv = w_ukv.reshape(MLA_KV_LORA, MLA_HEADS, MLA_NOPE + MLA_V)
    wk = _pad_heads(wkv[:, :, :MLA_NOPE].reshape(MLA_KV_LORA, -1), MLA_HEADS, MLA_NOPE).astype(bf)
    wv = _pad_heads(wkv[:, :, MLA_NOPE:].reshape(MLA_KV_LORA, -1), MLA_HEADS, MLA_V).astype(bf)
    gqkq = jnp.pad(g_qk_q, (0, HEAD_LANES - MLA_QK)).reshape(1, HEAD_LANES)
    gqkk = jnp.pad(g_qk_k, (0, HEAD_LANES - MLA_QK)).reshape(1, HEAD_LANES)
    wa = _pad_heads(w_mla_out.T, MLA_HEADS, MLA_V).T.astype(bf)
    mla_tabs = _rope_tables(rows, MLA_ROPE, MLA_NOPE, HEAD_LANES)
    ret_tabs = _rope_tables(rows, RET_DK, 0, RET_DK)

    row_tile = lambda width: pl.BlockSpec((CHUNK, width), lambda b, i: (b * nt + i, 0))
    const = lambda shape: pl.BlockSpec(shape, lambda b, i: (0,) * len(shape))
    tab = pl.BlockSpec((CHUNK, LANES), lambda b, i: (i, 0))
    n_rows = B * rows
    widths = (MLA_WIDE, MLA_WIDE, MLA_WIDE, RET_QK_WIDE, RET_QK_WIDE, RET_V_WIDE, RET_V_WIDE, D, D)
    q, k, v, rq, rk, rv, sg, ga, gr = pl.pallas_call(
        _mixer_in_kernel,
        out_shape=tuple(jax.ShapeDtypeStruct((n_rows, w), bf) for w in widths),
        grid=(B, nt),
        in_specs=[row_tile(D), const((1, D)), const(win.shape), const((1, MLA_Q_LORA)), const(wuq.shape),
                  const((1, MLA_KV_LORA)), const(wk.shape), const(wv.shape), const((1, HEAD_LANES)),
                  const((1, HEAD_LANES))] + [tab] * 6,
        out_specs=tuple(row_tile(w) for w in widths),
        compiler_params=pltpu.CompilerParams(dimension_semantics=("arbitrary", "arbitrary"),
                                             vmem_limit_bytes=MIXER_VMEM_LIMIT),
        name="mixer_in",
    )(hp, g_mix.reshape(1, D), win, g_q_lora.reshape(1, -1), wuq, g_kv_lora.reshape(1, -1), wk, wv,
      gqkq, gqkk, *mla_tabs, *ret_tabs)

    group = ATTN_HEAD_GROUP * HEAD_LANES
    seq_spec = pl.BlockSpec((rows, group), lambda b, g, i: (b, g))
    n_attn = rows // ATTN_TILE
    tile_spec = pl.BlockSpec((ATTN_TILE, group), lambda b, g, i: (b * n_attn + i, g))
    o = pl.pallas_call(
        _attention_kernel,
        out_shape=jax.ShapeDtypeStruct((n_rows, MLA_WIDE), bf),
        grid=(B, MLA_HEADS // ATTN_HEAD_GROUP, n_attn),
        in_specs=[tile_spec, seq_spec, seq_spec],
        out_specs=tile_spec,
        scratch_shapes=[pltpu.VMEM((ATTN_HEAD_GROUP, ATTN_TILE, HEAD_LANES), jnp.float32)] * 3,
        compiler_params=pltpu.CompilerParams(dimension_semantics=("arbitrary",) * 3,
                                             vmem_limit_bytes=MIXER_VMEM_LIMIT),
        name="mla_attention",
    )(q, k, v)

    log_gamma = jnp.log(1.0 - 2.0 ** (-5.0 - jnp.arange(RET_HEADS, dtype=jnp.float32)))
    idx = jnp.arange(CHUNK, dtype=jnp.float32)
    diff = idx[:, None] - idx[None, :]
    decay = jnp.where(diff[None] >= 0, jnp.exp(jnp.maximum(diff, 0.0)[None] * log_gamma[:, None, None]), 0.0)
    zeta = jnp.exp((CHUNK - 1 - idx)[None, :] * log_gamma[:, None])
    xi = jnp.exp((idx + 1)[None, :] * log_gamma[:, None])
    gamma_c = jnp.exp(CHUNK * log_gamma)
    zeta_b = jnp.broadcast_to(zeta[:, :, None], (RET_HEADS, CHUNK, LANES))
    xi_b = jnp.broadcast_to(xi[:, :, None], (RET_HEADS, CHUNK, RET_DV))
    gamma_rows = jnp.broadcast_to(jnp.repeat(gamma_c, RET_DK).reshape(RET_HEADS // 2, LANES, 1),
                                  (RET_HEADS // 2, LANES, RET_DV))
    pair_seq = lambda width: pl.BlockSpec((rows, width), lambda b, p: (b, p))
    pair_tab = lambda d1, d2: pl.BlockSpec((2, d1, d2), lambda b, p: (p, 0, 0))
    yr = pl.pallas_call(
        _retention_kernel,
        out_shape=jax.ShapeDtypeStruct((n_rows, RET_V_WIDE), bf),
        grid=(B, RET_HEADS // 2),
        in_specs=[pair_seq(LANES), pair_seq(LANES), pair_seq(2 * RET_DV), pair_seq(2 * RET_DV),
                  pl.BlockSpec((1, 2 * RET_DV), lambda b, p: (0, p)),
                  pair_tab(CHUNK, CHUNK), pair_tab(CHUNK, LANES), pair_tab(CHUNK, RET_DV),
                  pl.BlockSpec((None, LANES, RET_DV), lambda b, p: (p, 0, 0))],
        out_specs=pair_seq(2 * RET_DV),
        compiler_params=pltpu.CompilerParams(dimension_semantics=("arbitrary", "arbitrary"),
                                             vmem_limit_bytes=MIXER_VMEM_LIMIT),
        name="retention",
    )(rq, rk, rv, sg, g_ret_gn.reshape(1, -1), decay, zeta_b, xi_b, gamma_rows)

    seq_tiles = S // CHUNK
    first = (PAD_ROWS + N_META) // CHUNK
    in_tile = lambda width: pl.BlockSpec((CHUNK, width), lambda b, i: (b * nt + first + i, 0))
    return pl.pallas_call(
        _mixer_out_kernel,
        out_shape=jax.ShapeDtypeStruct((B * S, D), jnp.float32),
        grid=(B, seq_tiles),
        in_specs=[in_tile(D), in_tile(MLA_WIDE), in_tile(RET_V_WIDE), in_tile(D), in_tile(D),
                  const(wa.shape), const(w_ret_out.shape), const(w_mix_out.shape)],
        out_specs=pl.BlockSpec((CHUNK, D), lambda b, i: (b * seq_tiles + i, 0)),
        compiler_params=pltpu.CompilerParams(dimension_semantics=("arbitrary", "arbitrary"),
                                             vmem_limit_bytes=MIXER_VMEM_LIMIT),
        name="mixer_out",
    )(hp, o, yr, ga, gr, wa, w_ret_out.astype(bf), w_mix_out.astype(bf))


def _top_rows(s, count):
    n = s.shape[0]
    iota = lax.broadcasted_iota(jnp.int32, s.shape, 0)
    vals, idxs = [], []
    for _ in range(count):
        m = jnp.max(s, axis=0, keepdims=True)
        idx = jnp.min(jnp.where(s == m, iota, n), axis=0, keepdims=True)
        vals.append(m)
        idxs.append(idx)
        s = jnp.where(iota == idx, -jnp.inf, s)
    return vals, idxs


def _route_head(q, k1, k2):
    s1 = lax.dot_general(k1, q, NT_DIMS, preferred_element_type=jnp.float32)
    s2 = lax.dot_general(k2, q, NT_DIMS, preferred_element_type=jnp.float32)
    v1, i1 = _top_rows(s1, PEER_TOPK)
    v2, i2 = _top_rows(s2, PEER_TOPK)
    v2_all = jnp.concatenate(v2, axis=0)
    i2_all = jnp.concatenate(i2, axis=0)
    n_mid = SUBLANES
    cand = [v1[0] + v2_all] + [v1[a] + v2_all[:n_mid] for a in range(1, n_mid)]
    cand.append(jnp.concatenate(v1[n_mid:], axis=0) + v2[0])
    cand_rows = [i1[0] * PEER_NKEYS + i2_all] + [i1[a] * PEER_NKEYS + i2_all[:n_mid] for a in range(1, n_mid)]
    cand_rows.append(jnp.concatenate(i1[n_mid:], axis=0) * PEER_NKEYS + i2[0])
    cand = jnp.concatenate(cand, axis=0)
    cand_rows = jnp.concatenate(cand_rows, axis=0) * WORD_ROWS
    sc, ci = _top_rows(cand, PEER_TOPK)
    iota = lax.broadcasted_iota(jnp.int32, cand.shape, 0)
    rows = [jnp.max(jnp.where(iota == c, cand_rows, -1), axis=0, keepdims=True) for c in ci]
    ex = [jnp.exp(v - sc[0]) for v in sc]
    denom = ex[0]
    for e in ex[1:]:
        denom = denom + e
    return jnp.concatenate(rows, axis=0), jnp.concatenate(ex, axis=0) / denom


def _peer_route_kernel(h_ref, g_ref, wq_ref, k1_ref, k2_ref, xn_ref, rows_ref, gates_ref,
                       xb_ref, rows_scr, gates_scr):
    grp = pl.program_id(1)

    @pl.when(grp == 0)
    def _():
        x = h_ref[...]
        xn = x * lax.rsqrt(jnp.mean(x * x, axis=-1, keepdims=True) + EPS) * g_ref[...]
        xn_ref[...] = xn
        xb_ref[...] = xn.astype(jnp.bfloat16)

    q = jnp.dot(xb_ref[...], wq_ref[...], preferred_element_type=jnp.float32).astype(jnp.bfloat16)
    k1, k2 = k1_ref[...], k2_ref[...]
    for j in range(ROUTE_HEAD_GROUP):
        rows, gates = _route_head(q[:, j * PEER_DQ:(j + 1) * PEER_DQ], k1, k2)
        base = pl.multiple_of((grp * ROUTE_HEAD_GROUP + j) * PEER_TOPK, PEER_TOPK)
        rows_scr[pl.ds(base, PEER_TOPK), :] = rows
        gates_scr[pl.ds(base, PEER_TOPK), :] = gates

    @pl.when(grp == PEER_HEADS // ROUTE_HEAD_GROUP - 1)
    def _():
        for j in range(rows_ref.shape[0]):
            rows_ref[j] = rows_scr[:, j * PEER_TOKEN_BLOCK:(j + 1) * PEER_TOKEN_BLOCK]
        gates_ref[...] = gates_scr[...].T


def _peer_route(h, g_ffn, w_peer_q, keys_1, keys_2):
    T = h.shape[0]
    tb = ROUTE_TOKEN_BLOCK
    per_step = tb // PEER_TOKEN_BLOCK
    half = PEER_DQ // 2
    zeros = jnp.zeros((PEER_NKEYS, half), jnp.bfloat16)
    k1 = jnp.concatenate([keys_1.astype(jnp.bfloat16), zeros], axis=1)
    k2 = jnp.concatenate([zeros, keys_2.astype(jnp.bfloat16)], axis=1)
    keys_spec = pl.BlockSpec((PEER_NKEYS, PEER_DQ), lambda i, g: (0, 0))
    pair_spec = pl.BlockSpec((tb, PEER_PAIRS), lambda i, g: (i, 0))
    return pl.pallas_call(
        _peer_route_kernel,
        out_shape=(jax.ShapeDtypeStruct((T, D_MODEL), jnp.float32),
                   jax.ShapeDtypeStruct((T // PEER_TOKEN_BLOCK, PEER_PAIRS, PEER_TOKEN_BLOCK), jnp.int32),
                   jax.ShapeDtypeStruct((T, PEER_PAIRS), jnp.float32)),
        grid=(T // tb, PEER_HEADS // ROUTE_HEAD_GROUP),
        in_specs=[pl.BlockSpec((tb, D_MODEL), lambda i, g: (i, 0)),
                  pl.BlockSpec((1, D_MODEL), lambda i, g: (0, 0)),
                  pl.BlockSpec((D_MODEL, ROUTE_HEAD_GROUP * PEER_DQ), lambda i, g: (0, g)),
                  keys_spec, keys_spec],
        out_specs=(pl.BlockSpec((tb, D_MODEL), lambda i, g: (i, 0)),
                   pl.BlockSpec((per_step, PEER_PAIRS, PEER_TOKEN_BLOCK), lambda i, g: (i, 0, 0)),
                   pair_spec),
        scratch_shapes=[pltpu.VMEM((tb, D_MODEL), jnp.bfloat16),
                        pltpu.VMEM((PEER_PAIRS, tb), jnp.int32),
                        pltpu.VMEM((PEER_PAIRS, tb), jnp.float32)],
        compiler_params=pltpu.CompilerParams(dimension_semantics=("arbitrary", "arbitrary")),
        name="peer_route",
    )(h, g_ffn.reshape(1, D_MODEL), w_peer_q.astype(jnp.bfloat16), k1, k2)


def _pack_expert_table(tab):
    bits = lax.bitcast_convert_type(tab.astype(jnp.bfloat16), jnp.uint16).astype(jnp.uint32)
    bits = bits.reshape(tab.shape[0], WORD_ROWS, 2, LANES)
    words = bits[:, :, 0, :] | (bits[:, :, 1, :] << 16)
    return lax.bitcast_convert_type(words, jnp.int32).reshape(tab.shape[0] * WORD_ROWS, LANES)


def _gather_rows(idx_ref, tab_ref, dst_refs, t0):
    for k in range(PEER_PAIRS):
        idx_k = idx_ref.at[k]
        for s, dst_ref in enumerate(dst_refs):
            row = pl.multiple_of(idx_k[t0 + s], WORD_ROWS)
            dst_ref[k * WORD_ROWS:(k + 1) * WORD_ROWS, :] = tab_ref[pl.ds(row, WORD_ROWS), :]


def _expert_token_loop(idx_hbm, idx_bufs, sem, tab_ref, rows_refs, n_tokens, consume):
    i = pl.program_id(0)
    n = pl.num_programs(0)
    n_slots = len(rows_refs)
    assert n_tokens % n_slots == 0

    def copy(block, slot):
        return pltpu.make_async_copy(idx_hbm.at[block], idx_bufs[slot], sem.at[slot])

    def consume_group(t0):
        for s in range(n_slots):
            consume(t0 + s, s)

    @pl.when(i == 0)
    def _():
        copy(0, 0).start()
        copy(0, 0).wait()
        _gather_rows(idx_bufs[0], tab_ref, rows_refs, 0)

    def run(slot):
        has_next = i + 1 < n

        @pl.when(has_next)
        def _():
            copy(i + 1, 1 - slot).start()

        def step(j, carry):
            consume_group(n_slots * j)
            _gather_rows(idx_bufs[slot], tab_ref, rows_refs, n_slots * (j + 1))
            return carry

        lax.fori_loop(0, n_tokens // n_slots - 1, step, 0)

        @pl.when(has_next)
        def _():
            copy(i + 1, 1 - slot).wait()
            consume_group(n_tokens - n_slots)
            _gather_rows(idx_bufs[1 - slot], tab_ref, rows_refs, 0)

        @pl.when(jnp.logical_not(has_next))
        def _():
            consume_group(n_tokens - n_slots)

    for slot in range(2):
        pl.when(lax.rem(i, 2) == slot)(lambda slot=slot: run(slot))


def _diag_mask():
    lane = lax.broadcasted_iota(jnp.int32, (SUBLANES, GATHER_ROWS), 1)
    sub = lax.broadcasted_iota(jnp.int32, (SUBLANES, GATHER_ROWS), 0)
    return (lane & (SUBLANES - 1)) == sub


def _split_bf16(x):
    hi = x.astype(jnp.bfloat16)
    lo = (x - hi.astype(jnp.float32)).astype(jnp.bfloat16)
    return hi, lo


def _peer_hidden_kernel(idx_hbm, x_ref, gate_ref, tab_ref, sel_ref, w_ref, diag_ref, idx_a, idx_b, sem,
                        *rows_refs):
    mask = _diag_mask()

    def consume(t, slot):
        rows = pltpu.bitcast(rows_refs[slot][...], jnp.bfloat16)
        xb = x_ref[t].astype(jnp.bfloat16)
        d = lax.dot_general(xb, rows, NT_DIMS, preferred_element_type=jnp.float32)
        diag_ref[pl.ds(t, 1), :] = jnp.sum(jnp.where(mask, d, 0.0), axis=0, keepdims=True)

    _expert_token_loop(idx_hbm, (idx_a, idx_b), sem, tab_ref, rows_refs, x_ref.shape[0], consume)
    hi, lo = _split_bf16(diag_ref[...])
    sel = sel_ref[...]
    s = (jnp.dot(hi, sel, preferred_element_type=jnp.float32)
         + jnp.dot(lo, sel, preferred_element_type=jnp.float32))
    act = 0.5 * s * (1.0 + lax.erf(s * (1.0 / math.sqrt(2.0))))
    w_ref[...] = gate_ref[...] * act


def _peer_output_kernel(idx_hbm, w_ref, h_ref, tab_ref, expand_ref, out_ref, whi_ref, wlo_ref, idx_a, idx_b, sem,
                        *rows_refs):
    mask = _diag_mask()
    hi, lo = _split_bf16(w_ref[...])
    expand = expand_ref[...]
    whi_ref[...] = jnp.dot(hi, expand, preferred_element_type=jnp.float32)
    wlo_ref[...] = jnp.dot(lo, expand, preferred_element_type=jnp.float32)

    def consume(t, slot):
        rows = pltpu.bitcast(rows_refs[slot][...], jnp.bfloat16)
        parts = []
        for w_part in (whi_ref, wlo_ref):
            w_row = jnp.broadcast_to(w_part[pl.ds(t, 1), :], (SUBLANES, GATHER_ROWS))
            parts.append(jnp.where(mask, w_row, 0.0))
        lhs = jnp.concatenate(parts, axis=0).astype(jnp.bfloat16)
        y = jnp.dot(lhs, rows, preferred_element_type=jnp.float32)
        y = y[:SUBLANES] + y[SUBLANES:]
        y_row = jnp.concatenate([y[r:r + 1, :] for r in range(SUBLANES)], axis=1)
        out_ref[pl.ds(t, 1), :] = h_ref[pl.ds(t, 1), :] + y_row

    _expert_token_loop(idx_hbm, (idx_a, idx_b), sem, tab_ref, rows_refs, h_ref.shape[0], consume)


def _peer_experts(h, xn, rows, gates, peer_u, peer_v):
    T = h.shape[0]
    tb = PEER_TOKEN_BLOCK
    u_tab = _pack_expert_table(peer_u)
    v_tab = _pack_expert_table(peer_v)
    expand = (jnp.arange(PEER_PAIRS)[:, None] == jnp.arange(GATHER_ROWS)[None, :] // SUBLANES
              ).astype(jnp.bfloat16)
    params = pltpu.CompilerParams(dimension_semantics=("arbitrary",), vmem_limit_bytes=PEER_VMEM_LIMIT)
    hbm_ref = pl.BlockSpec(memory_space=pl.ANY)
    index_scratch = [pltpu.SMEM((PEER_PAIRS, tb), jnp.int32)] * 2 + [pltpu.SemaphoreType.DMA((2,))]
    pair_block = pl.BlockSpec((tb, PEER_PAIRS), lambda i: (i, 0))
    vmem_rows = pl.BlockSpec((tb, D_MODEL), lambda i: (i, 0))
    whole_vmem = pl.BlockSpec(memory_space=pltpu.VMEM)
    rows_scratch = pltpu.VMEM((GATHER_WORD_ROWS, LANES), jnp.int32)
    wide_scratch = pltpu.VMEM((tb, GATHER_ROWS), jnp.float32)

    w = pl.pallas_call(
        _peer_hidden_kernel,
        out_shape=jax.ShapeDtypeStruct((T, PEER_PAIRS), jnp.float32),
        grid=(T // tb,),
        in_specs=[hbm_ref, pl.BlockSpec((tb, SUBLANES, LANES), lambda i: (i, 0, 0)), pair_block,
                  whole_vmem, whole_vmem],
        out_specs=pair_block,
        scratch_shapes=[wide_scratch] + index_scratch + [rows_scratch] * HIDDEN_SLOTS,
        compiler_params=params,
        name="peer_hidden",
    )(rows, xn.reshape(T, SUBLANES, LANES), gates, u_tab, expand.T)

    return pl.pallas_call(
        _peer_output_kernel,
        out_shape=jax.ShapeDtypeStruct((T, D_MODEL), jnp.float32),
        grid=(T // tb,),
        in_specs=[hbm_ref, pair_block, vmem_rows, whole_vmem, whole_vmem],
        out_specs=vmem_rows,
        scratch_shapes=[wide_scratch, wide_scratch] + index_scratch + [rows_scratch] * OUTPUT_SLOTS,
        compiler_params=params,
        name="peer_output",
    )(rows, w, h, v_tab, expand)


def kernel(x, meta_tokens, g_mix, w_in, g_q_lora, w_uq, g_kv_lora, w_ukv, g_qk_q, g_qk_k, w_mla_out, g_ret_gn, w_ret_out, w_mix_out, g_ffn, w_peer_q, peer_keys_1, peer_keys_2, peer_u, peer_v):
    B, S, D = x.shape
    h = _mixer(x, meta_tokens, g_mix[0], w_in[0], g_q_lora[0], w_uq[0], g_kv_lora[0], w_ukv[0],
               g_qk_q[0], g_qk_k[0], w_mla_out[0], g_ret_gn[0], w_ret_out[0], w_mix_out[0])
    xn, rows, gates = _peer_route(h, g_ffn[0], w_peer_q[0], peer_keys_1[0], peer_keys_2[0])
    out = _peer_experts(h, xn, rows, gates, peer_u[0], peer_v[0])
    return out.reshape(B, S, D)
```

```python
import math

import jax
import jax.numpy as jnp
import numpy as np
from jax import lax
from jax.experimental import pallas as pl
from jax.experimental.pallas import tpu as pltpu

D_MODEL = 1024
N_META = 16
CHUNK = 128
ROPE_BASE = 10000.0
EPS = 1e-6
MLA_HEADS = 8
MLA_Q_LORA = 384
MLA_KV_LORA = 256
MLA_NOPE = 64
MLA_ROPE = 32
MLA_V = 64
MLA_QK = MLA_NOPE + MLA_ROPE
RET_HEADS = 8
RET_DK = 64
RET_DV = 128
PEER_HEADS = 8
PEER_NKEYS = 128
PEER_EXPERTS = PEER_NKEYS * PEER_NKEYS
PEER_DQ = 128
PEER_TOPK = 16
PEER_PAIRS = PEER_HEADS * PEER_TOPK
IN_SIZES = (MLA_Q_LORA, MLA_KV_LORA, MLA_ROPE, RET_HEADS * RET_DK, RET_HEADS * RET_DK,
            RET_HEADS * RET_DV, RET_HEADS * RET_DV, D_MODEL, D_MODEL)

LANES = 128
SUBLANES = 8
WORD_ROWS = D_MODEL // (2 * LANES)
GATHER_WORD_ROWS = PEER_PAIRS * WORD_ROWS
GATHER_ROWS = 2 * GATHER_WORD_ROWS
PEER_TOKEN_BLOCK = LANES
ROUTE_TOKEN_BLOCK = 256
ROUTE_HEAD_GROUP = 4
HIDDEN_SLOTS = 16
OUTPUT_SLOTS = 16
PEER_VMEM_LIMIT = 56 * 1024 * 1024
NT_DIMS = (((1,), (1,)), ((), ()))


PAD_ROWS = (-N_META) % CHUNK
HEAD_LANES = 128
MLA_WIDE = MLA_HEADS * HEAD_LANES
RET_QK_WIDE = RET_HEADS * RET_DK
RET_V_WIDE = RET_HEADS * RET_DV
MASK_VALUE = -1e30
MIXER_VMEM_LIMIT = 48 * 1024 * 1024
ATTN_HEAD_GROUP = 8
ATTN_TILE = 3 * CHUNK
RET_PAIR_GROUP = 2
_SEG_SIZES = (MLA_Q_LORA, MLA_KV_LORA, HEAD_LANES, RET_QK_WIDE, RET_QK_WIDE, RET_V_WIDE, RET_V_WIDE,
              D_MODEL, D_MODEL)
_SEG_STARTS = tuple(int(v) for v in np.cumsum((0,) + _SEG_SIZES))


def _rope_tables(n_rows, dim, block_offset, repeat):
    half = dim // 2
    pos = jnp.arange(n_rows, dtype=jnp.float32) - PAD_ROWS
    inv = ROPE_BASE ** (-jnp.arange(half, dtype=jnp.float32) / half)
    lane = np.arange(LANES)
    rel = (lane - block_offset) % repeat
    in_rope = (lane >= block_offset) & (rel < dim)
    ang = pos[:, None] * inv[rel % half][None, :]
    cos = jnp.where(in_rope[None, :], jnp.cos(ang), 1.0)
    sin = jnp.sin(ang)
    s1 = jnp.where((in_rope & (rel < half))[None, :], -sin, 0.0)
    s2 = jnp.where((in_rope & (rel >= half))[None, :], sin, 0.0)
    return cos, s1, s2


def _apply_rope(x, cos, s1, s2, half):
    return x * cos + pltpu.roll(x, LANES - half, 1) * s1 + pltpu.roll(x, half, 1) * s2


def _mixer_in_kernel(h_ref, gmix_ref, win_ref, gq_ref, wuq_ref, gkv_ref, wk_ref, wv_ref, gqkq_ref, gqkk_ref,
                     mc_ref, ms1_ref, ms2_ref, rc_ref, rs1_ref, rs2_ref,
                     q_ref, k_ref, v_ref, rq_ref, rk_ref, rv_ref, sg_ref, ga_ref, gr_ref):
    x = h_ref[...]
    hn = (x * lax.rsqrt(jnp.mean(x * x, axis=-1, keepdims=True) + EPS) * gmix_ref[...]).astype(jnp.bfloat16)

    def seg(i):
        return jnp.dot(hn, win_ref[:, _SEG_STARTS[i]:_SEG_STARTS[i + 1]], preferred_element_type=jnp.float32)

    def lora_norm(c, g_ref):
        return (c * lax.rsqrt(jnp.mean(c * c, axis=-1, keepdims=True) + EPS) * g_ref[...]).astype(jnp.bfloat16)

    cq = lora_norm(seg(0), gq_ref)
    ckv = lora_norm(seg(1), gkv_ref)
    kpe = seg(2)
    q = jnp.dot(cq, wuq_ref[...], preferred_element_type=jnp.float32)
    kn = jnp.dot(ckv, wk_ref[...], preferred_element_type=jnp.float32)
    v_ref[...] = jnp.dot(ckv, wv_ref[...], preferred_element_type=jnp.float32).astype(jnp.bfloat16)
    mc, ms1, ms2 = mc_ref[...], ms1_ref[...], ms2_ref[...]
    scale = math.log2(math.e) / math.sqrt(MLA_QK)

    def qk_norm_rope(t, g_ref):
        t = t * lax.rsqrt(jnp.sum(t * t, axis=-1, keepdims=True) * (1.0 / MLA_QK) + EPS) * g_ref[...]
        return _apply_rope(t, mc, ms1, ms2, MLA_ROPE // 2)

    for hd in range(MLA_HEADS):
        blk = slice(hd * HEAD_LANES, (hd + 1) * HEAD_LANES)
        q_ref[:, blk] = (qk_norm_rope(q[:, blk], gqkq_ref) * scale).astype(jnp.bfloat16)
        k_ref[:, blk] = qk_norm_rope(kn[:, blk] + kpe, gqkk_ref).astype(jnp.bfloat16)

    rc, rs1, rs2 = rc_ref[...], rs1_ref[...], rs2_ref[...]
    rq = seg(3)
    rk = seg(4)
    for j in range(RET_QK_WIDE // LANES):
        blk = slice(j * LANES, (j + 1) * LANES)
        rq_ref[:, blk] = _apply_rope(rq[:, blk], rc, rs1, rs2, RET_DK // 2).astype(jnp.bfloat16)
        rk_ref[:, blk] = (_apply_rope(rk[:, blk], rc, rs1, rs2, RET_DK // 2) * (RET_DK ** -0.5)).astype(jnp.bfloat16)
    rv_ref[...] = seg(5).astype(jnp.bfloat16)
    gate = seg(6)
    sg_ref[...] = (gate * jax.nn.sigmoid(gate)).astype(jnp.bfloat16)
    ga_ref[...] = jax.nn.sigmoid(seg(7)).astype(jnp.bfloat16)
    gr_ref[...] = jax.nn.sigmoid(seg(8)).astype(jnp.bfloat16)


def _attention_kernel(q_ref, k_ref, v_ref, o_ref, m_ref, l_ref, acc_ref):
    qi = pl.program_id(2)
    t = ATTN_TILE
    n_heads = q_ref.shape[1] // HEAD_LANES
    n_lane_tiles = t // LANES
    m_ref[...] = jnp.full(m_ref.shape, MASK_VALUE, jnp.float32)
    l_ref[...] = jnp.zeros(l_ref.shape, jnp.float32)
    acc_ref[...] = jnp.zeros(acc_ref.shape, jnp.float32)

    def lane_tiles(x):
        return [x[:, i * LANES:(i + 1) * LANES] for i in range(n_lane_tiles)]

    def scores(j, hd, masked):
        blk = slice(hd * HEAD_LANES, (hd + 1) * HEAD_LANES)
        start = pl.multiple_of(j * t, t)
        s = lax.dot_general(q_ref[:, blk], k_ref[pl.ds(start, t), blk], NT_DIMS,
                            preferred_element_type=jnp.float32)
        if masked:
            r = lax.broadcasted_iota(jnp.int32, (t, t), 0) + qi * t
            c = lax.broadcasted_iota(jnp.int32, (t, t), 1) + j * t
            s = jnp.where((c <= r) & ((c >= PAD_ROWS) | (r < PAD_ROWS)), s, MASK_VALUE)
        return s

    def max_step(j, masked):
        for hd in range(n_heads):
            m = m_ref[hd]
            for part in lane_tiles(scores(j, hd, masked)):
                m = jnp.maximum(m, part)
            m_ref[hd] = m

    def sum_step(j, masked):
        start = pl.multiple_of(j * t, t)
        for hd in range(n_heads):
            blk = slice(hd * HEAD_LANES, (hd + 1) * HEAD_LANES)
            p = jnp.exp2(scores(j, hd, masked) - jnp.tile(m_ref[hd], (1, n_lane_tiles)))
            l = l_ref[hd]
            for part in lane_tiles(p):
                l = l + part
            l_ref[hd] = l
            acc_ref[hd] += jnp.dot(p.astype(jnp.bfloat16), v_ref[pl.ds(start, t), blk],
                                   preferred_element_type=jnp.float32)

    def sweep(step):
        def body(j, carry):
            edge = (j == 0) | (j == qi)

            @pl.when(edge)
            def _():
                step(j, True)

            @pl.when(jnp.logical_not(edge))
            def _():
                step(j, False)

            return carry

        lax.fori_loop(0, qi + 1, body, 0)

    sweep(max_step)
    for hd in range(n_heads):
        m_ref[hd] = jnp.broadcast_to(jnp.max(m_ref[hd], axis=-1, keepdims=True), (t, LANES))
    sweep(sum_step)
    for hd in range(n_heads):
        blk = slice(hd * HEAD_LANES, (hd + 1) * HEAD_LANES)
        o_ref[:, blk] = (acc_ref[hd] / jnp.sum(l_ref[hd], axis=-1, keepdims=True)).astype(jnp.bfloat16)


def _retention_kernel(q_ref, k_ref, v_ref, sg_ref, gn_ref, decay_ref, zeta_ref, xi_ref, gamma_ref, o_ref):
    n_chunks = q_ref.shape[0] // CHUNK
    n_pairs = q_ref.shape[1] // LANES
    lane = lax.broadcasted_iota(jnp.int32, (CHUNK, LANES), 1)
    row = lax.broadcasted_iota(jnp.int32, (LANES, RET_DV), 0)
    is_a = lane < RET_DK
    gn = gn_ref[...]
    contract_rows = (((0,), (0,)), ((), ()))

    def group_norm(y):
        mu = jnp.mean(y, axis=-1, keepdims=True)
        d = y - mu
        return d * lax.rsqrt(jnp.mean(d * d, axis=-1, keepdims=True) + EPS)

    def pair_chunk(rows, p, state):
        qb = q_ref[rows, p * LANES:(p + 1) * LANES]
        kb = k_ref[rows, p * LANES:(p + 1) * LANES]
        state_bf = state.astype(jnp.bfloat16)
        new_parts = []
        for hd in range(2):
            head = 2 * p + hd
            cols = slice(head * RET_DV, (head + 1) * RET_DV)
            qh = jnp.where(is_a if hd == 0 else ~is_a, qb, jnp.zeros_like(qb))
            vh = v_ref[rows, cols]
            scores = lax.dot_general(qh, kb, NT_DIMS, preferred_element_type=jnp.float32) * decay_ref[head]
            inner = jnp.dot(scores.astype(jnp.bfloat16), vh, preferred_element_type=jnp.float32)
            cross = jnp.dot(qh, state_bf, preferred_element_type=jnp.float32) * xi_ref[head]
            yn = group_norm(inner + cross) * gn[:, cols]
            o_ref[rows, cols] = (sg_ref[rows, cols].astype(jnp.float32) * yn).astype(jnp.bfloat16)
            kz = (kb.astype(jnp.float32) * zeta_ref[head]).astype(jnp.bfloat16)
            new_parts.append(lax.dot_general(kz, vh, contract_rows, preferred_element_type=jnp.float32))
        return gamma_ref[p] * state + jnp.where(row < RET_DK, new_parts[0], new_parts[1])

    def chunk(n, states):
        rows = pl.ds(pl.multiple_of(n * CHUNK, CHUNK), CHUNK)
        return tuple(pair_chunk(rows, p, state) for p, state in enumerate(states))

    lax.fori_loop(0, n_chunks, chunk, tuple(jnp.zeros((LANES, RET_DV), jnp.float32) for _ in range(n_pairs)))


def _mixer_out_kernel(h_ref, o_ref, yr_ref, ga_ref, gr_ref, wa_ref, wr_ref, wm_ref, out_ref):
    y_a = jnp.dot(o_ref[...], wa_ref[...], preferred_element_type=jnp.float32)
    y_r = jnp.dot(yr_ref[...], wr_ref[...], preferred_element_type=jnp.float32)
    merged = ga_ref[...].astype(jnp.float32) * y_a + gr_ref[...].astype(jnp.float32) * y_r
    out_ref[...] = h_ref[...] + jnp.dot(merged.astype(jnp.bfloat16), wm_ref[...],
                                        preferred_element_type=jnp.float32)


def _pad_heads(w, heads, width):
    lead = w.shape[:-1]
    w = w.reshape(lead + (heads, width))
    w = jnp.pad(w, [(0, 0)] * len(lead) + [(0, 0), (0, HEAD_LANES - width)])
    return w.reshape(lead + (heads * HEAD_LANES,))


def _mixer(x, meta_tokens, g_mix, w_in, g_q_lora, w_uq, g_kv_lora, w_ukv, g_qk_q, g_qk_k,
           w_mla_out, g_ret_gn, w_ret_out, w_mix_out):
    B, S, D = x.shape
    bf = jnp.bfloat16
    rows = PAD_ROWS + N_META + S
    nt = rows // CHUNK
    hp = jnp.concatenate([jnp.zeros((B, PAD_ROWS, D), x.dtype),
                          jnp.broadcast_to(meta_tokens.astype(x.dtype)[None], (B, N_META, D)), x], axis=1)
    hp = hp.reshape(B * rows, D)

    segs = jnp.split(w_in, np.cumsum(IN_SIZES)[:-1].tolist(), axis=1)
    kpe_cols = jnp.pad(segs[2], ((0, 0), (MLA_NOPE, HEAD_LANES - MLA_NOPE - MLA_ROPE)))
    win = jnp.concatenate(segs[:2] + [kpe_cols] + segs[3:], axis=1).astype(bf)
    wuq = _pad_heads(w_uq, MLA_HEADS, MLA_QK).astype(bf)
    wkv = w_ukv.reshape(MLA_KV_LORA, MLA_HEADS, MLA_NOPE + MLA_V)
    wk = _pad_heads(wkv[:, :, :MLA_NOPE].reshape(MLA_KV_LORA, -1), MLA_HEADS, MLA_NOPE).astype(bf)
    wv = _pad_heads(wkv[:, :, MLA_NOPE:].reshape(MLA_KV_LORA, -1), MLA_HEADS, MLA_V).astype(bf)
    gqkq = jnp.pad(g_qk_q, (0, HEAD_LANES - MLA_QK)).reshape(1, HEAD_LANES)
    gqkk = jnp.pad(g_qk_k, (0, HEAD_LANES - MLA_QK)).reshape(1, HEAD_LANES)
    wa = _pad_heads(w_mla_out.T, MLA_HEADS, MLA_V).T.astype(bf)
    mla_tabs = _rope_tables(rows, MLA_ROPE, MLA_NOPE, HEAD_LANES)
    ret_tabs = _rope_tables(rows, RET_DK, 0, RET_DK)

    row_tile = lambda width: pl.BlockSpec((CHUNK, width), lambda b, i: (b * nt + i, 0))
    const = lambda shape: pl.BlockSpec(shape, lambda b, i: (0,) * len(shape))
    tab = pl.BlockSpec((CHUNK, LANES), lambda b, i: (i, 0))
    n_rows = B * rows
    widths = (MLA_WIDE, MLA_WIDE, MLA_WIDE, RET_QK_WIDE, RET_QK_WIDE, RET_V_WIDE, RET_V_WIDE, D, D)
    q, k, v, rq, rk, rv, sg, ga, gr = pl.pallas_call(
        _mixer_in_kernel,
        out_shape=tuple(jax.ShapeDtypeStruct((n_rows, w), bf) for w in widths),
        grid=(B, nt),
        in_specs=[row_tile(D), const((1, D)), const(win.shape), const((1, MLA_Q_LORA)), const(wuq.shape),
                  const((1, MLA_KV_LORA)), const(wk.shape), const(wv.shape), const((1, HEAD_LANES)),
                  const((1, HEAD_LANES))] + [tab] * 6,
        out_specs=tuple(row_tile(w) for w in widths),
        compiler_params=pltpu.CompilerParams(dimension_semantics=("arbitrary", "arbitrary"),
                                             vmem_limit_bytes=MIXER_VMEM_LIMIT),
        name="mixer_in",
    )(hp, g_mix.reshape(1, D), win, g_q_lora.reshape(1, -1), wuq, g_kv_lora.reshape(1, -1), wk, wv,
      gqkq, gqkk, *mla_tabs, *ret_tabs)

    group = ATTN_HEAD_GROUP * HEAD_LANES
    seq_spec = pl.BlockSpec((rows, group), lambda b, g, i: (b, g))
    n_attn = rows // ATTN_TILE
    tile_spec = pl.BlockSpec((ATTN_TILE, group), lambda b, g, i: (b * n_attn + i, g))
    o = pl.pallas_call(
        _attention_kernel,
        out_shape=jax.ShapeDtypeStruct((n_rows, MLA_WIDE), bf),
        grid=(B, MLA_HEADS // ATTN_HEAD_GROUP, n_attn),
        in_specs=[tile_spec, seq_spec, seq_spec],
        out_specs=tile_spec,
        scratch_shapes=[pltpu.VMEM((ATTN_HEAD_GROUP, ATTN_TILE, HEAD_LANES), jnp.float32)] * 3,
        compiler_params=pltpu.CompilerParams(dimension_semantics=("arbitrary",) * 3,
                                             vmem_limit_bytes=MIXER_VMEM_LIMIT),
        name="mla_attention",
    )(q, k, v)

    log_gamma = jnp.log(1.0 - 2.0 ** (-5.0 - jnp.arange(RET_HEADS, dtype=jnp.float32)))
    idx = jnp.arange(CHUNK, dtype=jnp.float32)
    diff = idx[:, None] - idx[None, :]
    decay = jnp.where(diff[None] >= 0, jnp.exp(jnp.maximum(diff, 0.0)[None] * log_gamma[:, None, None]), 0.0)
    zeta = jnp.exp((CHUNK - 1 - idx)[None, :] * log_gamma[:, None])
    xi = jnp.exp((idx + 1)[None, :] * log_gamma[:, None])
    gamma_c = jnp.exp(CHUNK * log_gamma)
    zeta_b = jnp.broadcast_to(zeta[:, :, None], (RET_HEADS, CHUNK, LANES))
    xi_b = jnp.broadcast_to(xi[:, :, None], (RET_HEADS, CHUNK, RET_DV))
    gamma_rows = jnp.broadcast_to(jnp.repeat(gamma_c, RET_DK).reshape(RET_HEADS // 2, LANES, 1),
                                  (RET_HEADS // 2, LANES, RET_DV))
    g = RET_PAIR_GROUP
    pair_seq = lambda width: pl.BlockSpec((rows, g * width), lambda b, p: (b, p))
    pair_tab = lambda d1, d2: pl.BlockSpec((2 * g, d1, d2), lambda b, p: (p, 0, 0))
    yr = pl.pallas_call(
        _retention_kernel,
        out_shape=jax.ShapeDtypeStruct((n_rows, RET_V_WIDE), bf),
        grid=(B, RET_HEADS // (2 * g)),
        in_specs=[pair_seq(LANES), pair_seq(LANES), pair_seq(2 * RET_DV), pair_seq(2 * RET_DV),
                  pl.BlockSpec((1, g * 2 * RET_DV), lambda b, p: (0, p)),
                  pair_tab(CHUNK, CHUNK), pair_tab(CHUNK, LANES), pair_tab(CHUNK, RET_DV),
                  pl.BlockSpec((g, LANES, RET_DV), lambda b, p: (p, 0, 0))],
        out_specs=pair_seq(2 * RET_DV),
        compiler_params=pltpu.CompilerParams(dimension_semantics=("arbitrary", "arbitrary"),
                                             vmem_limit_bytes=MIXER_VMEM_LIMIT),
        name="retention",
    )(rq, rk, rv, sg, g_ret_gn.reshape(1, -1), decay, zeta_b, xi_b, gamma_rows)

    seq_tiles = S // CHUNK
    first = (PAD_ROWS + N_META) // CHUNK
    in_tile = lambda width: pl.BlockSpec((CHUNK, width), lambda b, i: (b * nt + first + i, 0))
    return pl.pallas_call(
        _mixer_out_kernel,
        out_shape=jax.ShapeDtypeStruct((B * S, D), jnp.float32),
        grid=(B, seq_tiles),
        in_specs=[in_tile(D), in_tile(MLA_WIDE), in_tile(RET_V_WIDE), in_tile(D), in_tile(D),
                  const(wa.shape), const(w_ret_out.shape), const(w_mix_out.shape)],
        out_specs=pl.BlockSpec((CHUNK, D), lambda b, i: (b * seq_tiles + i, 0)),
        compiler_params=pltpu.CompilerParams(dimension_semantics=("arbitrary", "arbitrary"),
                                             vmem_limit_bytes=MIXER_VMEM_LIMIT),
        name="mixer_out",
    )(hp, o, yr, ga, gr, wa, w_ret_out.astype(bf), w_mix_out.astype(bf))


def _top_rows(s, count):
    n = s.shape[0]
    iota = lax.broadcasted_iota(jnp.int32, s.shape, 0)
    vals, idxs = [], []
    for _ in range(count):
        m = jnp.max(s, axis=0, keepdims=True)
        idx = jnp.min(jnp.where(s == m, iota, n), axis=0, keepdims=True)
        vals.append(m)
        idxs.append(idx)
        s = jnp.where(iota == idx, -jnp.inf, s)
    return vals, idxs


def _route_head(q, k1, k2):
    s1 = lax.dot_general(k1, q, NT_DIMS, preferred_element_type=jnp.float32)
    s2 = lax.dot_general(k2, q, NT_DIMS, preferred_element_type=jnp.float32)
    v1, i1 = _top_rows(s1, PEER_TOPK)
    v2, i2 = _top_rows(s2, PEER_TOPK)
    v2_all = jnp.concatenate(v2, axis=0)
    i2_all = jnp.concatenate(i2, axis=0)
    n_mid = SUBLANES
    cand = [v1[0] + v2_all] + [v1[a] + v2_all[:n_mid] for a in range(1, n_mid)]
    cand.append(jnp.concatenate(v1[n_mid:], axis=0) + v2[0])
    cand_rows = [i1[0] * PEER_NKEYS + i2_all] + [i1[a] * PEER_NKEYS + i2_all[:n_mid] for a in range(1, n_mid)]
    cand_rows.append(jnp.concatenate(i1[n_mid:], axis=0) * PEER_NKEYS + i2[0])
    cand = jnp.concatenate(cand, axis=0)
    cand_rows = jnp.concatenate(cand_rows, axis=0) * WORD_ROWS
    sc, ci = _top_rows(cand, PEER_TOPK)
    iota = lax.broadcasted_iota(jnp.int32, cand.shape, 0)
    rows = [jnp.max(jnp.where(iota == c, cand_rows, -1), axis=0, keepdims=True) for c in ci]
    ex = [jnp.exp(v - sc[0]) for v in sc]
    denom = ex[0]
    for e in ex[1:]:
        denom = denom + e
    return jnp.concatenate(rows, axis=0), jnp.concatenate(ex, axis=0) / denom


def _peer_route_kernel(h_ref, g_ref, wq_ref, k1_ref, k2_ref, xn_ref, rows_ref, gates_ref,
                       xb_ref, rows_scr, gates_scr):
    grp = pl.program_id(1)

    @pl.when(grp == 0)
    def _():
        x = h_ref[...]
        xn = x * lax.rsqrt(jnp.mean(x * x, axis=-1, keepdims=True) + EPS) * g_ref[...]
        xn_ref[...] = xn
        xb_ref[...] = xn.astype(jnp.bfloat16)

    q = jnp.dot(xb_ref[...], wq_ref[...], preferred_element_type=jnp.float32).astype(jnp.bfloat16)
    k1, k2 = k1_ref[...], k2_ref[...]
    for j in range(ROUTE_HEAD_GROUP):
        rows, gates = _route_head(q[:, j * PEER_DQ:(j + 1) * PEER_DQ], k1, k2)
        base = pl.multiple_of((grp * ROUTE_HEAD_GROUP + j) * PEER_TOPK, PEER_TOPK)
        rows_scr[pl.ds(base, PEER_TOPK), :] = rows
        gates_scr[pl.ds(base, PEER_TOPK), :] = gates

    @pl.when(grp == PEER_HEADS // ROUTE_HEAD_GROUP - 1)
    def _():
        for j in range(rows_ref.shape[0]):
            rows_ref[j] = rows_scr[:, j * PEER_TOKEN_BLOCK:(j + 1) * PEER_TOKEN_BLOCK]
        gates_ref[...] = gates_scr[...].T


def _peer_route(h, g_ffn, w_peer_q, keys_1, keys_2):
    T = h.shape[0]
    tb = ROUTE_TOKEN_BLOCK
    per_step = tb // PEER_TOKEN_BLOCK
    half = PEER_DQ // 2
    zeros = jnp.zeros((PEER_NKEYS, half), jnp.bfloat16)
    k1 = jnp.concatenate([keys_1.astype(jnp.bfloat16), zeros], axis=1)
    k2 = jnp.concatenate([zeros, keys_2.astype(jnp.bfloat16)], axis=1)
    keys_spec = pl.BlockSpec((PEER_NKEYS, PEER_DQ), lambda i, g: (0, 0))
    pair_spec = pl.BlockSpec((tb, PEER_PAIRS), lambda i, g: (i, 0))
    return pl.pallas_call(
        _peer_route_kernel,
        out_shape=(jax.ShapeDtypeStruct((T, D_MODEL), jnp.float32),
                   jax.ShapeDtypeStruct((T // PEER_TOKEN_BLOCK, PEER_PAIRS, PEER_TOKEN_BLOCK), jnp.int32),
                   jax.ShapeDtypeStruct((T, PEER_PAIRS), jnp.float32)),
        grid=(T // tb, PEER_HEADS // ROUTE_HEAD_GROUP),
        in_specs=[pl.BlockSpec((tb, D_MODEL), lambda i, g: (i, 0)),
                  pl.BlockSpec((1, D_MODEL), lambda i, g: (0, 0)),
                  pl.BlockSpec((D_MODEL, ROUTE_HEAD_GROUP * PEER_DQ), lambda i, g: (0, g)),
                  keys_spec, keys_spec],
        out_specs=(pl.BlockSpec((tb, D_MODEL), lambda i, g: (i, 0)),
                   pl.BlockSpec((per_step, PEER_PAIRS, PEER_TOKEN_BLOCK), lambda i, g: (i, 0, 0)),
                   pair_spec),
        scratch_shapes=[pltpu.VMEM((tb, D_MODEL), jnp.bfloat16),
                        pltpu.VMEM((PEER_PAIRS, tb), jnp.int32),
                        pltpu.VMEM((PEER_PAIRS, tb), jnp.float32)],
        compiler_params=pltpu.CompilerParams(dimension_semantics=("arbitrary", "arbitrary")),
        name="peer_route",
    )(h, g_ffn.reshape(1, D_MODEL), w_peer_q.astype(jnp.bfloat16), k1, k2)


def _pack_expert_table(tab):
    bits = lax.bitcast_convert_type(tab.astype(jnp.bfloat16), jnp.uint16).astype(jnp.uint32)
    bits = bits.reshape(tab.shape[0], WORD_ROWS, 2, LANES)
    words = bits[:, :, 0, :] | (bits[:, :, 1, :] << 16)
    return lax.bitcast_convert_type(words, jnp.int32).reshape(tab.shape[0] * WORD_ROWS, LANES)


def _gather_rows(idx_ref, tab_ref, dst_refs, t0):
    for k in range(PEER_PAIRS):
        idx_k = idx_ref.at[k]
        for s, dst_ref in enumerate(dst_refs):
            row = pl.multiple_of(idx_k[t0 + s], WORD_ROWS)
            dst_ref[k * WORD_ROWS:(k + 1) * WORD_ROWS, :] = tab_ref[pl.ds(row, WORD_ROWS), :]


def _expert_token_loop(idx_hbm, idx_bufs, sem, tab_ref, rows_refs, n_tokens, consume):
    i = pl.program_id(0)
    n = pl.num_programs(0)
    n_slots = len(rows_refs)
    assert n_tokens % n_slots == 0

    def copy(block, slot):
        return pltpu.make_async_copy(idx_hbm.at[block], idx_bufs[slot], sem.at[slot])

    def consume_group(t0):
        for s in range(n_slots):
            consume(t0 + s, s)

    @pl.when(i == 0)
    def _():
        copy(0, 0).start()
        copy(0, 0).wait()
        _gather_rows(idx_bufs[0], tab_ref, rows_refs, 0)

    def run(slot):
        has_next = i + 1 < n

        @pl.when(has_next)
        def _():
            copy(i + 1, 1 - slot).start()

        def step(j, carry):
            consume_group(n_slots * j)
            _gather_rows(idx_bufs[slot], tab_ref, rows_refs, n_slots * (j + 1))
            return carry

        lax.fori_loop(0, n_tokens // n_slots - 1, step, 0)

        @pl.when(has_next)
        def _():
            copy(i + 1, 1 - slot).wait()
            consume_group(n_tokens - n_slots)
            _gather_rows(idx_bufs[1 - slot], tab_ref, rows_refs, 0)

        @pl.when(jnp.logical_not(has_next))
        def _():
            consume_group(n_tokens - n_slots)

    for slot in range(2):
        pl.when(lax.rem(i, 2) == slot)(lambda slot=slot: run(slot))


def _diag_mask():
    lane = lax.broadcasted_iota(jnp.int32, (SUBLANES, GATHER_ROWS), 1)
    sub = lax.broadcasted_iota(jnp.int32, (SUBLANES, GATHER_ROWS), 0)
    return (lane & (SUBLANES - 1)) == sub


def _split_bf16(x):
    hi = x.astype(jnp.bfloat16)
    lo = (x - hi.astype(jnp.float32)).astype(jnp.bfloat16)
    return hi, lo


def _peer_hidden_kernel(idx_hbm, x_ref, gate_ref, tab_ref, sel_ref, w_ref, diag_ref, idx_a, idx_b, sem,
                        *rows_refs):
    mask = _diag_mask()

    def consume(t, slot):
        rows = pltpu.bitcast(rows_refs[slot][...], jnp.bfloat16)
        xb = x_ref[t].astype(jnp.bfloat16)
        d = lax.dot_general(xb, rows, NT_DIMS, preferred_element_type=jnp.float32)
        diag_ref[pl.ds(t, 1), :] = jnp.sum(jnp.where(mask, d, 0.0), axis=0, keepdims=True)

    _expert_token_loop(idx_hbm, (idx_a, idx_b), sem, tab_ref, rows_refs, x_ref.shape[0], consume)
    hi, lo = _split_bf16(diag_ref[...])
    sel = sel_ref[...]
    s = (jnp.dot(hi, sel, preferred_element_type=jnp.float32)
         + jnp.dot(lo, sel, preferred_element_type=jnp.float32))
    act = 0.5 * s * (1.0 + lax.erf(s * (1.0 / math.sqrt(2.0))))
    w_ref[...] = gate_ref[...] * act


def _peer_output_kernel(idx_hbm, w_ref, h_ref, tab_ref, expand_ref, out_ref, whi_ref, wlo_ref, idx_a, idx_b, sem,
                        *rows_refs):
    mask = _diag_mask()
    hi, lo = _split_bf16(w_ref[...])
    expand = expand_ref[...]
    whi_ref[...] = jnp.dot(hi, expand, preferred_element_type=jnp.float32)
    wlo_ref[...] = jnp.dot(lo, expand, preferred_element_type=jnp.float32)

    def consume(t, slot):
        rows = pltpu.bitcast(rows_refs[slot][...], jnp.bfloat16)
        parts = []
        for w_part in (whi_ref, wlo_ref):
            w_row = jnp.broadcast_to(w_part[pl.ds(t, 1), :], (SUBLANES, GATHER_ROWS))
            parts.append(jnp.where(mask, w_row, 0.0))
        lhs = jnp.concatenate(parts, axis=0).astype(jnp.bfloat16)
        y = jnp.dot(lhs, rows, preferred_element_type=jnp.float32)
        y = y[:SUBLANES] + y[SUBLANES:]
        y_row = jnp.concatenate([y[r:r + 1, :] for r in range(SUBLANES)], axis=1)
        out_ref[pl.ds(t, 1), :] = h_ref[pl.ds(t, 1), :] + y_row

    _expert_token_loop(idx_hbm, (idx_a, idx_b), sem, tab_ref, rows_refs, h_ref.shape[0], consume)


def _peer_experts(h, xn, rows, gates, peer_u, peer_v):
    T = h.shape[0]
    tb = PEER_TOKEN_BLOCK
    u_tab = _pack_expert_table(peer_u)
    v_tab = _pack_expert_table(peer_v)
    expand = (jnp.arange(PEER_PAIRS)[:, None] == jnp.arange(GATHER_ROWS)[None, :] // SUBLANES
              ).astype(jnp.bfloat16)
    params = pltpu.CompilerParams(dimension_semantics=("arbitrary",), vmem_limit_bytes=PEER_VMEM_LIMIT)
    hbm_ref = pl.BlockSpec(memory_space=pl.ANY)
    index_scratch = [pltpu.SMEM((PEER_PAIRS, tb), jnp.int32)] * 2 + [pltpu.SemaphoreType.DMA((2,))]
    pair_block = pl.BlockSpec((tb, PEER_PAIRS), lambda i: (i, 0))
    vmem_rows = pl.BlockSpec((tb, D_MODEL), lambda i: (i, 0))
    whole_vmem = pl.BlockSpec(memory_space=pltpu.VMEM)
    rows_scratch = pltpu.VMEM((GATHER_WORD_ROWS, LANES), jnp.int32)
    wide_scratch = pltpu.VMEM((tb, GATHER_ROWS), jnp.float32)

    w = pl.pallas_call(
        _peer_hidden_kernel,
        out_shape=jax.ShapeDtypeStruct((T, PEER_PAIRS), jnp.float32),
        grid=(T // tb,),
        in_specs=[hbm_ref, pl.BlockSpec((tb, SUBLANES, LANES), lambda i: (i, 0, 0)), pair_block,
                  whole_vmem, whole_vmem],
        out_specs=pair_block,
        scratch_shapes=[wide_scratch] + index_scratch + [rows_scratch] * HIDDEN_SLOTS,
        compiler_params=params,
        name="peer_hidden",
    )(rows, xn.reshape(T, SUBLANES, LANES), gates, u_tab, expand.T)

    return pl.pallas_call(
        _peer_output_kernel,
        out_shape=jax.ShapeDtypeStruct((T, D_MODEL), jnp.float32),
        grid=(T // tb,),
        in_specs=[hbm_ref, pair_block, vmem_rows, whole_vmem, whole_vmem],
        out_specs=vmem_rows,
        scratch_shapes=[wide_scratch, wide_scratch] + index_scratch + [rows_scratch] * OUTPUT_SLOTS,
        compiler_params=params,
        name="peer_output",
    )(rows, w, h, v_tab, expand)


def kernel(x, meta_tokens, g_mix, w_in, g_q_lora, w_uq, g_kv_lora, w_ukv, g_qk_q, g_qk_k, w_mla_out, g_ret_gn, w_ret_out, w_mix_out, g_ffn, w_peer_q, peer_keys_1, peer_keys_2, peer_u, peer_v):
    B, S, D = x.shape
    h = _mixer(x, meta_tokens, g_mix[0], w_in[0], g_q_lora[0], w_uq[0], g_kv_lora[0], w_ukv[0],
               g_qk_q[0], g_qk_k[0], w_mla_out[0], g_ret_gn[0], w_ret_out[0], w_mix_out[0])
    xn, rows, gates = _peer_route(h, g_ffn[0], w_peer_q[0], peer_keys_1[0], peer_keys_2[0])
    out = _peer_experts(h, xn, rows, gates, peer_u[0], peer_v[0])
    return out.reshape(B, S, D)
```

```python
import math

import jax
import jax.numpy as jnp
import numpy as np
from jax import lax
from jax.experimental import pallas as pl
from jax.experimental.pallas import tpu as pltpu

D_MODEL = 1024
N_META = 16
CHUNK = 128
ROPE_BASE = 10000.0
EPS = 1e-6
MLA_HEADS = 8
MLA_Q_LORA = 384
MLA_KV_LORA = 256
MLA_NOPE = 64
MLA_ROPE = 32
MLA_V = 64
MLA_QK = MLA_NOPE + MLA_ROPE
RET_HEADS = 8
RET_DK = 64
RET_DV = 128
PEER_HEADS = 8
PEER_NKEYS = 128
PEER_EXPERTS = PEER_NKEYS * PEER_NKEYS
PEER_DQ = 128
PEER_TOPK = 16
PEER_PAIRS = PEER_HEADS * PEER_TOPK
IN_SIZES = (MLA_Q_LORA, MLA_KV_LORA, MLA_ROPE, RET_HEADS * RET_DK, RET_HEADS * RET_DK,
            RET_HEADS * RET_DV, RET_HEADS * RET_DV, D_MODEL, D_MODEL)

LANES = 128
SUBLANES = 8
WORD_ROWS = D_MODEL // (2 * LANES)
GATHER_WORD_ROWS = PEER_PAIRS * WORD_ROWS
GATHER_ROWS = 2 * GATHER_WORD_ROWS
PEER_TOKEN_BLOCK = LANES
ROUTE_TOKEN_BLOCK = 256
ROUTE_HEAD_GROUP = 4
HIDDEN_SLOTS = 16
OUTPUT_SLOTS = 16
PEER_VMEM_LIMIT = 56 * 1024 * 1024
NT_DIMS = (((1,), (1,)), ((), ()))
HIGH_HALF_MASK = -65536
PACK_EXPERT_BLOCK = 256


PAD_ROWS = (-N_META) % CHUNK
HEAD_LANES = 128
MLA_WIDE = MLA_HEADS * HEAD_LANES
RET_QK_WIDE = RET_HEADS * RET_DK
RET_V_WIDE = RET_HEADS * RET_DV
MASK_VALUE = -1e30
MIXER_VMEM_LIMIT = 48 * 1024 * 1024
ATTN_HEAD_GROUP = 8
ATTN_TILE = 3 * CHUNK
RET_PAIR_GROUP = 2
_SEG_SIZES = (MLA_Q_LORA, MLA_KV_LORA, HEAD_LANES, RET_QK_WIDE, RET_QK_WIDE, RET_V_WIDE, RET_V_WIDE,
              D_MODEL, D_MODEL)
_SEG_STARTS = tuple(int(v) for v in np.cumsum((0,) + _SEG_SIZES))


def _rope_tables(n_rows, dim, block_offset, repeat):
    half = dim // 2
    pos = jnp.arange(n_rows, dtype=jnp.float32) - PAD_ROWS
    inv = ROPE_BASE ** (-jnp.arange(half, dtype=jnp.float32) / half)
    lane = np.arange(LANES)
    rel = (lane - block_offset) % repeat
    in_rope = (lane >= block_offset) & (rel < dim)
    ang = pos[:, None] * inv[rel % half][None, :]
    cos = jnp.where(in_rope[None, :], jnp.cos(ang), 1.0)
    sin = jnp.sin(ang)
    s1 = jnp.where((in_rope & (rel < half))[None, :], -sin, 0.0)
    s2 = jnp.where((in_rope & (rel >= half))[None, :], sin, 0.0)
    return cos, s1, s2


def _apply_rope(x, cos, s1, s2, half):
    return x * cos + pltpu.roll(x, LANES - half, 1) * s1 + pltpu.roll(x, half, 1) * s2


def _mixer_in_kernel(h_ref, gmix_ref, win_ref, gq_ref, wuq_ref, gkv_ref, wk_ref, wv_ref, gqkq_ref, gqkk_ref,
                     mc_ref, ms1_ref, ms2_ref, rc_ref, rs1_ref, rs2_ref,
                     q_ref, k_ref, v_ref, rq_ref, rk_ref, rv_ref, sg_ref, ga_ref, gr_ref):
    x = h_ref[...]
    hn = (x * lax.rsqrt(jnp.mean(x * x, axis=-1, keepdims=True) + EPS) * gmix_ref[...]).astype(jnp.bfloat16)

    def seg(i):
        return jnp.dot(hn, win_ref[:, _SEG_STARTS[i]:_SEG_STARTS[i + 1]], preferred_element_type=jnp.float32)

    def lora_norm(c, g_ref):
        return (c * lax.rsqrt(jnp.mean(c * c, axis=-1, keepdims=True) + EPS) * g_ref[...]).astype(jnp.bfloat16)

    cq = lora_norm(seg(0), gq_ref)
    ckv = lora_norm(seg(1), gkv_ref)
    kpe = seg(2)
    q = jnp.dot(cq, wuq_ref[...], preferred_element_type=jnp.float32)
    kn = jnp.dot(ckv, wk_ref[...], preferred_element_type=jnp.float32)
    v_ref[...] = jnp.dot(ckv, wv_ref[...], preferred_element_type=jnp.float32).astype(jnp.bfloat16)
    mc, ms1, ms2 = mc_ref[...], ms1_ref[...], ms2_ref[...]
    scale = math.log2(math.e) / math.sqrt(MLA_QK)

    def qk_norm_rope(t, g_ref):
        t = t * lax.rsqrt(jnp.sum(t * t, axis=-1, keepdims=True) * (1.0 / MLA_QK) + EPS) * g_ref[...]
        return _apply_rope(t, mc, ms1, ms2, MLA_ROPE // 2)

    for hd in range(MLA_HEADS):
        blk = slice(hd * HEAD_LANES, (hd + 1) * HEAD_LANES)
        q_ref[:, blk] = (qk_norm_rope(q[:, blk], gqkq_ref) * scale).astype(jnp.bfloat16)
        k_ref[:, blk] = qk_norm_rope(kn[:, blk] + kpe, gqkk_ref).astype(jnp.bfloat16)

    rc, rs1, rs2 = rc_ref[...], rs1_ref[...], rs2_ref[...]
    rq = seg(3)
    rk = seg(4)
    for j in range(RET_QK_WIDE // LANES):
        blk = slice(j * LANES, (j + 1) * LANES)
        rq_ref[:, blk] = _apply_rope(rq[:, blk], rc, rs1, rs2, RET_DK // 2).astype(jnp.bfloat16)
        rk_ref[:, blk] = (_apply_rope(rk[:, blk], rc, rs1, rs2, RET_DK // 2) * (RET_DK ** -0.5)).astype(jnp.bfloat16)
    rv_ref[...] = seg(5).astype(jnp.bfloat16)
    gate = seg(6)
    sg_ref[...] = (gate * jax.nn.sigmoid(gate)).astype(jnp.bfloat16)
    ga_ref[...] = jax.nn.sigmoid(seg(7)).astype(jnp.bfloat16)
    gr_ref[...] = jax.nn.sigmoid(seg(8)).astype(jnp.bfloat16)


def _attention_kernel(q_ref, k_ref, v_ref, o_ref, m_ref, l_ref, acc_ref):
    qi = pl.program_id(2)
    t = ATTN_TILE
    n_heads = q_ref.shape[1] // HEAD_LANES
    n_lane_tiles = t // LANES
    m_ref[...] = jnp.full(m_ref.shape, MASK_VALUE, jnp.float32)
    l_ref[...] = jnp.zeros(l_ref.shape, jnp.float32)
    acc_ref[...] = jnp.zeros(acc_ref.shape, jnp.float32)

    def lane_tiles(x):
        return [x[:, i * LANES:(i + 1) * LANES] for i in range(n_lane_tiles)]

    def scores(j, hd, masked):
        blk = slice(hd * HEAD_LANES, (hd + 1) * HEAD_LANES)
        start = pl.multiple_of(j * t, t)
        s = lax.dot_general(q_ref[:, blk], k_ref[pl.ds(start, t), blk], NT_DIMS,
                            preferred_element_type=jnp.float32)
        if masked:
            r = lax.broadcasted_iota(jnp.int32, (t, t), 0) + qi * t
            c = lax.broadcasted_iota(jnp.int32, (t, t), 1) + j * t
            s = jnp.where((c <= r) & ((c >= PAD_ROWS) | (r < PAD_ROWS)), s, MASK_VALUE)
        return s

    def max_step(j, masked):
        for hd in range(n_heads):
            m = m_ref[hd]
            for part in lane_tiles(scores(j, hd, masked)):
                m = jnp.maximum(m, part)
            m_ref[hd] = m

    def sum_step(j, masked):
        start = pl.multiple_of(j * t, t)
        for hd in range(n_heads):
            blk = slice(hd * HEAD_LANES, (hd + 1) * HEAD_LANES)
            p = jnp.exp2(scores(j, hd, masked) - jnp.tile(m_ref[hd], (1, n_lane_tiles)))
            l = l_ref[hd]
            for part in lane_tiles(p):
                l = l + part
            l_ref[hd] = l
            acc_ref[hd] += jnp.dot(p.astype(jnp.bfloat16), v_ref[pl.ds(start, t), blk],
                                   preferred_element_type=jnp.float32)

    def sweep(step):
        def body(j, carry):
            edge = (j == 0) | (j == qi)

            @pl.when(edge)
            def _():
                step(j, True)

            @pl.when(jnp.logical_not(edge))
            def _():
                step(j, False)

            return carry

        lax.fori_loop(0, qi + 1, body, 0)

    sweep(max_step)
    for hd in range(n_heads):
        m_ref[hd] = jnp.broadcast_to(jnp.max(m_ref[hd], axis=-1, keepdims=True), (t, LANES))
    sweep(sum_step)
    for hd in range(n_heads):
        blk = slice(hd * HEAD_LANES, (hd + 1) * HEAD_LANES)
        o_ref[:, blk] = (acc_ref[hd] / jnp.sum(l_ref[hd], axis=-1, keepdims=True)).astype(jnp.bfloat16)


def _retention_kernel(q_ref, k_ref, v_ref, sg_ref, gn_ref, decay_ref, zeta_ref, xi_ref, gamma_ref, o_ref):
    n_chunks = q_ref.shape[0] // CHUNK
    n_pairs = q_ref.shape[1] // LANES
    lane = lax.broadcasted_iota(jnp.int32, (CHUNK, LANES), 1)
    row = lax.broadcasted_iota(jnp.int32, (LANES, RET_DV), 0)
    is_a = lane < RET_DK
    gn = gn_ref[...]
    contract_rows = (((0,), (0,)), ((), ()))

    def group_norm(y):
        mu = jnp.mean(y, axis=-1, keepdims=True)
        d = y - mu
        return d * lax.rsqrt(jnp.mean(d * d, axis=-1, keepdims=True) + EPS)

    def pair_chunk(rows, p, state):
        qb = q_ref[rows, p * LANES:(p + 1) * LANES]
        kb = k_ref[rows, p * LANES:(p + 1) * LANES]
        state_bf = state.astype(jnp.bfloat16)
        new_parts = []
        for hd in range(2):
            head = 2 * p + hd
            cols = slice(head * RET_DV, (head + 1) * RET_DV)
            qh = jnp.where(is_a if hd == 0 else ~is_a, qb, jnp.zeros_like(qb))
            vh = v_ref[rows, cols]
            scores = lax.dot_general(qh, kb, NT_DIMS, preferred_element_type=jnp.float32) * decay_ref[head]
            inner = jnp.dot(scores.astype(jnp.bfloat16), vh, preferred_element_type=jnp.float32)
            cross = jnp.dot(qh, state_bf, preferred_element_type=jnp.float32) * xi_ref[head]
            yn = group_norm(inner + cross) * gn[:, cols]
            o_ref[rows, cols] = (sg_ref[rows, cols].astype(jnp.float32) * yn).astype(jnp.bfloat16)
            kz = (kb.astype(jnp.float32) * zeta_ref[head]).astype(jnp.bfloat16)
            new_parts.append(lax.dot_general(kz, vh, contract_rows, preferred_element_type=jnp.float32))
        return gamma_ref[p] * state + jnp.where(row < RET_DK, new_parts[0], new_parts[1])

    def chunk(n, states):
        rows = pl.ds(pl.multiple_of(n * CHUNK, CHUNK), CHUNK)
        return tuple(pair_chunk(rows, p, state) for p, state in enumerate(states))

    lax.fori_loop(0, n_chunks, chunk, tuple(jnp.zeros((LANES, RET_DV), jnp.float32) for _ in range(n_pairs)))


def _mixer_out_kernel(h_ref, o_ref, yr_ref, ga_ref, gr_ref, wa_ref, wr_ref, wm_ref, out_ref):
    y_a = jnp.dot(o_ref[...], wa_ref[...], preferred_element_type=jnp.float32)
    y_r = jnp.dot(yr_ref[...], wr_ref[...], preferred_element_type=jnp.float32)
    merged = ga_ref[...].astype(jnp.float32) * y_a + gr_ref[...].astype(jnp.float32) * y_r
    out_ref[...] = h_ref[...] + jnp.dot(merged.astype(jnp.bfloat16), wm_ref[...],
                                        preferred_element_type=jnp.float32)


def _pad_heads(w, heads, width):
    lead = w.shape[:-1]
    w = w.reshape(lead + (heads, width))
    w = jnp.pad(w, [(0, 0)] * len(lead) + [(0, 0), (0, HEAD_LANES - width)])
    return w.reshape(lead + (heads * HEAD_LANES,))


def _mixer(x, meta_tokens, g_mix, w_in, g_q_lora, w_uq, g_kv_lora, w_ukv, g_qk_q, g_qk_k,
           w_mla_out, g_ret_gn, w_ret_out, w_mix_out):
    B, S, D = x.shape
    bf = jnp.bfloat16
    rows = PAD_ROWS + N_META + S
    nt = rows // CHUNK
    hp = jnp.concatenate([jnp.zeros((B, PAD_ROWS, D), x.dtype),
                          jnp.broadcast_to(meta_tokens.astype(x.dtype)[None], (B, N_META, D)), x], axis=1)
    hp = hp.reshape(B * rows, D)

    segs = jnp.split(w_in, np.cumsum(IN_SIZES)[:-1].tolist(), axis=1)
    kpe_cols = jnp.pad(segs[2], ((0, 0), (MLA_NOPE, HEAD_LANES - MLA_NOPE - MLA_ROPE)))
    win = jnp.concatenate(segs[:2] + [kpe_cols] + segs[3:], axis=1).astype(bf)
    wuq = _pad_heads(w_uq, MLA_HEADS, MLA_QK).astype(bf)
    wkv = w_ukv.reshape(MLA_KV_LORA, MLA_HEADS, MLA_NOPE + MLA_V)
    wk = _pad_heads(wkv[:, :, :MLA_NOPE].reshape(MLA_KV_LORA, -1), MLA_HEADS, MLA_NOPE).astype(bf)
    wv = _pad_heads(wkv[:, :, MLA_NOPE:].reshape(MLA_KV_LORA, -1), MLA_HEADS, MLA_V).astype(bf)
    gqkq = jnp.pad(g_qk_q, (0, HEAD_LANES - MLA_QK)).reshape(1, HEAD_LANES)
    gqkk = jnp.pad(g_qk_k, (0, HEAD_LANES - MLA_QK)).reshape(1, HEAD_LANES)
    wa = _pad_heads(w_mla_out.T, MLA_HEADS, MLA_V).T.astype(bf)
    mla_tabs = _rope_tables(rows, MLA_ROPE, MLA_NOPE, HEAD_LANES)
    ret_tabs = _rope_tables(rows, RET_DK, 0, RET_DK)

    row_tile = lambda width: pl.BlockSpec((CHUNK, width), lambda b, i: (b * nt + i, 0))
    const = lambda shape: pl.BlockSpec(shape, lambda b, i: (0,) * len(shape))
    tab = pl.BlockSpec((CHUNK, LANES), lambda b, i: (i, 0))
    n_rows = B * rows
    widths = (MLA_WIDE, MLA_WIDE, MLA_WIDE, RET_QK_WIDE, RET_QK_WIDE, RET_V_WIDE, RET_V_WIDE, D, D)
    q, k, v, rq, rk, rv, sg, ga, gr = pl.pallas_call(
        _mixer_in_kernel,
        out_shape=tuple(jax.ShapeDtypeStruct((n_rows, w), bf) for w in widths),
        grid=(B, nt),
        in_specs=[row_tile(D), const((1, D)), const(win.shape), const((1, MLA_Q_LORA)), const(wuq.shape),
                  const((1, MLA_KV_LORA)), const(wk.shape), const(wv.shape), const((1, HEAD_LANES)),
                  const((1, HEAD_LANES))] + [tab] * 6,
        out_specs=tuple(row_tile(w) for w in widths),
        compiler_params=pltpu.CompilerParams(dimension_semantics=("arbitrary", "arbitrary"),
                                             vmem_limit_bytes=MIXER_VMEM_LIMIT),
        name="mixer_in",
    )(hp, g_mix.reshape(1, D), win, g_q_lora.reshape(1, -1), wuq, g_kv_lora.reshape(1, -1), wk, wv,
      gqkq, gqkk, *mla_tabs, *ret_tabs)

    group = ATTN_HEAD_GROUP * HEAD_LANES
    seq_spec = pl.BlockSpec((rows, group), lambda b, g, i: (b, g))
    n_attn = rows // ATTN_TILE
    tile_spec = pl.BlockSpec((ATTN_TILE, group), lambda b, g, i: (b * n_attn + i, g))
    o = pl.pallas_call(
        _attention_kernel,
        out_shape=jax.ShapeDtypeStruct((n_rows, MLA_WIDE), bf),
        grid=(B, MLA_HEADS // ATTN_HEAD_GROUP, n_attn),
        in_specs=[tile_spec, seq_spec, seq_spec],
        out_specs=tile_spec,
        scratch_shapes=[pltpu.VMEM((ATTN_HEAD_GROUP, ATTN_TILE, HEAD_LANES), jnp.float32)] * 3,
        compiler_params=pltpu.CompilerParams(dimension_semantics=("arbitrary",) * 3,
                                             vmem_limit_bytes=MIXER_VMEM_LIMIT),
        name="mla_attention",
    )(q, k, v)

    log_gamma = jnp.log(1.0 - 2.0 ** (-5.0 - jnp.arange(RET_HEADS, dtype=jnp.float32)))
    idx = jnp.arange(CHUNK, dtype=jnp.float32)
    diff = idx[:, None] - idx[None, :]
    decay = jnp.where(diff[None] >= 0, jnp.exp(jnp.maximum(diff, 0.0)[None] * log_gamma[:, None, None]), 0.0)
    zeta = jnp.exp((CHUNK - 1 - idx)[None, :] * log_gamma[:, None])
    xi = jnp.exp((idx + 1)[None, :] * log_gamma[:, None])
    gamma_c = jnp.exp(CHUNK * log_gamma)
    zeta_b = jnp.broadcast_to(zeta[:, :, None], (RET_HEADS, CHUNK, LANES))
    xi_b = jnp.broadcast_to(xi[:, :, None], (RET_HEADS, CHUNK, RET_DV))
    gamma_rows = jnp.broadcast_to(jnp.repeat(gamma_c, RET_DK).reshape(RET_HEADS // 2, LANES, 1),
                                  (RET_HEADS // 2, LANES, RET_DV))
    g = RET_PAIR_GROUP
    pair_seq = lambda width: pl.BlockSpec((rows, g * width), lambda b, p: (b, p))
    pair_tab = lambda d1, d2: pl.BlockSpec((2 * g, d1, d2), lambda b, p: (p, 0, 0))
    yr = pl.pallas_call(
        _retention_kernel,
        out_shape=jax.ShapeDtypeStruct((n_rows, RET_V_WIDE), bf),
        grid=(B, RET_HEADS // (2 * g)),
        in_specs=[pair_seq(LANES), pair_seq(LANES), pair_seq(2 * RET_DV), pair_seq(2 * RET_DV),
                  pl.BlockSpec((1, g * 2 * RET_DV), lambda b, p: (0, p)),
                  pair_tab(CHUNK, CHUNK), pair_tab(CHUNK, LANES), pair_tab(CHUNK, RET_DV),
                  pl.BlockSpec((g, LANES, RET_DV), lambda b, p: (p, 0, 0))],
        out_specs=pair_seq(2 * RET_DV),
        compiler_params=pltpu.CompilerParams(dimension_semantics=("arbitrary", "arbitrary"),
                                             vmem_limit_bytes=MIXER_VMEM_LIMIT),
        name="retention",
    )(rq, rk, rv, sg, g_ret_gn.reshape(1, -1), decay, zeta_b, xi_b, gamma_rows)

    seq_tiles = S // CHUNK
    first = (PAD_ROWS + N_META) // CHUNK
    in_tile = lambda width: pl.BlockSpec((CHUNK, width), lambda b, i: (b * nt + first + i, 0))
    return pl.pallas_call(
        _mixer_out_kernel,
        out_shape=jax.ShapeDtypeStruct((B * S, D), jnp.float32),
        grid=(B, seq_tiles),
        in_specs=[in_tile(D), in_tile(MLA_WIDE), in_tile(RET_V_WIDE), in_tile(D), in_tile(D),
                  const(wa.shape), const(w_ret_out.shape), const(w_mix_out.shape)],
        out_specs=pl.BlockSpec((CHUNK, D), lambda b, i: (b * seq_tiles + i, 0)),
        compiler_params=pltpu.CompilerParams(dimension_semantics=("arbitrary", "arbitrary"),
                                             vmem_limit_bytes=MIXER_VMEM_LIMIT),
        name="mixer_out",
    )(hp, o, yr, ga, gr, wa, w_ret_out.astype(bf), w_mix_out.astype(bf))


def _top_rows(s, count):
    n = s.shape[0]
    iota = lax.broadcasted_iota(jnp.int32, s.shape, 0)
    vals, idxs = [], []
    for _ in range(count):
        m = jnp.max(s, axis=0, keepdims=True)
        idx = jnp.min(jnp.where(s == m, iota, n), axis=0, keepdims=True)
        vals.append(m)
        idxs.append(idx)
        s = jnp.where(iota == idx, -jnp.inf, s)
    return vals, idxs


def _route_head(q, k1, k2):
    s1 = lax.dot_general(k1, q, NT_DIMS, preferred_element_type=jnp.float32)
    s2 = lax.dot_general(k2, q, NT_DIMS, preferred_element_type=jnp.float32)
    v1, i1 = _top_rows(s1, PEER_TOPK)
    v2, i2 = _top_rows(s2, PEER_TOPK)
    v2_all = jnp.concatenate(v2, axis=0)
    i2_all = jnp.concatenate(i2, axis=0)
    n_mid = SUBLANES
    cand = [v1[0] + v2_all] + [v1[a] + v2_all[:n_mid] for a in range(1, n_mid)]
    cand.append(jnp.concatenate(v1[n_mid:], axis=0) + v2[0])
    cand_rows = [i1[0] * PEER_NKEYS + i2_all] + [i1[a] * PEER_NKEYS + i2_all[:n_mid] for a in range(1, n_mid)]
    cand_rows.append(jnp.concatenate(i1[n_mid:], axis=0) * PEER_NKEYS + i2[0])
    cand = jnp.concatenate(cand, axis=0)
    cand_rows = jnp.concatenate(cand_rows, axis=0) * WORD_ROWS
    sc, ci = _top_rows(cand, PEER_TOPK)
    iota = lax.broadcasted_iota(jnp.int32, cand.shape, 0)
    rows = [jnp.max(jnp.where(iota == c, cand_rows, -1), axis=0, keepdims=True) for c in ci]
    ex = [jnp.exp(v - sc[0]) for v in sc]
    denom = ex[0]
    for e in ex[1:]:
        denom = denom + e
    return jnp.concatenate(rows, axis=0), jnp.concatenate(ex, axis=0) / denom


def _peer_route_kernel(h_ref, g_ref, wq_ref, k1_ref, k2_ref, xn_ref, rows_ref, gates_ref,
                       xb_ref, rows_scr, gates_scr):
    grp = pl.program_id(1)

    @pl.when(grp == 0)
    def _():
        x = h_ref[...]
        xn = x * lax.rsqrt(jnp.mean(x * x, axis=-1, keepdims=True) + EPS) * g_ref[...]
        for r in range(SUBLANES):
            xn_ref[pl.ds(r, x.shape[0], stride=SUBLANES), :] = xn[:, r * LANES:(r + 1) * LANES]
        xb_ref[...] = xn.astype(jnp.bfloat16)

    q = jnp.dot(xb_ref[...], wq_ref[...], preferred_element_type=jnp.float32).astype(jnp.bfloat16)
    k1, k2 = k1_ref[...], k2_ref[...]
    for j in range(ROUTE_HEAD_GROUP):
        rows, gates = _route_head(q[:, j * PEER_DQ:(j + 1) * PEER_DQ], k1, k2)
        base = pl.multiple_of((grp * ROUTE_HEAD_GROUP + j) * PEER_TOPK, PEER_TOPK)
        rows_scr[pl.ds(base, PEER_TOPK), :] = rows
        gates_scr[pl.ds(base, PEER_TOPK), :] = gates

    @pl.when(grp == PEER_HEADS // ROUTE_HEAD_GROUP - 1)
    def _():
        for j in range(rows_ref.shape[0]):
            rows_ref[j] = rows_scr[:, j * PEER_TOKEN_BLOCK:(j + 1) * PEER_TOKEN_BLOCK]
        gates_ref[...] = gates_scr[...].T


def _peer_route(h, g_ffn, w_peer_q, keys_1, keys_2):
    T = h.shape[0]
    tb = ROUTE_TOKEN_BLOCK
    per_step = tb // PEER_TOKEN_BLOCK
    half = PEER_DQ // 2
    zeros = jnp.zeros((PEER_NKEYS, half), jnp.bfloat16)
    k1 = jnp.concatenate([keys_1.astype(jnp.bfloat16), zeros], axis=1)
    k2 = jnp.concatenate([zeros, keys_2.astype(jnp.bfloat16)], axis=1)
    keys_spec = pl.BlockSpec((PEER_NKEYS, PEER_DQ), lambda i, g: (0, 0))
    pair_spec = pl.BlockSpec((tb, PEER_PAIRS), lambda i, g: (i, 0))
    return pl.pallas_call(
        _peer_route_kernel,
        out_shape=(jax.ShapeDtypeStruct((T * SUBLANES, LANES), jnp.float32),
                   jax.ShapeDtypeStruct((T // PEER_TOKEN_BLOCK, PEER_PAIRS, PEER_TOKEN_BLOCK), jnp.int32),
                   jax.ShapeDtypeStruct((T, PEER_PAIRS), jnp.float32)),
        grid=(T // tb, PEER_HEADS // ROUTE_HEAD_GROUP),
        in_specs=[pl.BlockSpec((tb, D_MODEL), lambda i, g: (i, 0)),
                  pl.BlockSpec((1, D_MODEL), lambda i, g: (0, 0)),
                  pl.BlockSpec((D_MODEL, ROUTE_HEAD_GROUP * PEER_DQ), lambda i, g: (0, g)),
                  keys_spec, keys_spec],
        out_specs=(pl.BlockSpec((tb * SUBLANES, LANES), lambda i, g: (i, 0)),
                   pl.BlockSpec((per_step, PEER_PAIRS, PEER_TOKEN_BLOCK), lambda i, g: (i, 0, 0)),
                   pair_spec),
        scratch_shapes=[pltpu.VMEM((tb, D_MODEL), jnp.bfloat16),
                        pltpu.VMEM((PEER_PAIRS, tb), jnp.int32),
                        pltpu.VMEM((PEER_PAIRS, tb), jnp.float32)],
        compiler_params=pltpu.CompilerParams(dimension_semantics=("arbitrary", "arbitrary")),
        name="peer_route",
    )(h, g_ffn.reshape(1, D_MODEL), w_peer_q.astype(jnp.bfloat16), k1, k2)


def _pack_table_kernel(tab_ref, out_ref):
    n = tab_ref.shape[0]
    bits = lax.bitcast_convert_type(tab_ref[...].astype(jnp.bfloat16).astype(jnp.float32), jnp.int32)
    for j in range(WORD_ROWS):
        lo = lax.shift_right_logical(bits[:, 2 * j * LANES:(2 * j + 1) * LANES], 16)
        hi = bits[:, (2 * j + 1) * LANES:(2 * j + 2) * LANES] & HIGH_HALF_MASK
        out_ref[pl.ds(j, n, stride=WORD_ROWS), :] = hi | lo


def _pack_expert_table(tab):
    n = PACK_EXPERT_BLOCK
    return pl.pallas_call(
        _pack_table_kernel,
        out_shape=jax.ShapeDtypeStruct((tab.shape[0] * WORD_ROWS, LANES), jnp.int32),
        grid=(tab.shape[0] // n,),
        in_specs=[pl.BlockSpec((n, D_MODEL), lambda i: (i, 0))],
        out_specs=pl.BlockSpec((n * WORD_ROWS, LANES), lambda i: (i, 0)),
        compiler_params=pltpu.CompilerParams(dimension_semantics=("arbitrary",)),
        name="pack_table",
    )(tab)


def _gather_rows(idx_ref, tab_ref, dst_refs, t0):
    for k in range(PEER_PAIRS):
        idx_k = idx_ref.at[k]
        for s, dst_ref in enumerate(dst_refs):
            row = pl.multiple_of(idx_k[t0 + s], WORD_ROWS)
            dst_ref[k * WORD_ROWS:(k + 1) * WORD_ROWS, :] = tab_ref[pl.ds(row, WORD_ROWS), :]


def _expert_token_loop(idx_hbm, idx_bufs, sem, tab_ref, rows_refs, n_tokens, consume):
    i = pl.program_id(0)
    n = pl.num_programs(0)
    n_slots = len(rows_refs)
    assert n_tokens % n_slots == 0

    def copy(block, slot):
        return pltpu.make_async_copy(idx_hbm.at[block], idx_bufs[slot], sem.at[slot])

    def consume_group(t0):
        for s in range(n_slots):
            consume(t0 + s, s)

    @pl.when(i == 0)
    def _():
        copy(0, 0).start()
        copy(0, 0).wait()
        _gather_rows(idx_bufs[0], tab_ref, rows_refs, 0)

    def run(slot):
        has_next = i + 1 < n

        @pl.when(has_next)
        def _():
            copy(i + 1, 1 - slot).start()

        def step(j, carry):
            consume_group(n_slots * j)
            _gather_rows(idx_bufs[slot], tab_ref, rows_refs, n_slots * (j + 1))
            return carry

        lax.fori_loop(0, n_tokens // n_slots - 1, step, 0)

        @pl.when(has_next)
        def _():
            copy(i + 1, 1 - slot).wait()
            consume_group(n_tokens - n_slots)
            _gather_rows(idx_bufs[1 - slot], tab_ref, rows_refs, 0)

        @pl.when(jnp.logical_not(has_next))
        def _():
            consume_group(n_tokens - n_slots)

    for slot in range(2):
        pl.when(lax.rem(i, 2) == slot)(lambda slot=slot: run(slot))


def _diag_mask():
    lane = lax.broadcasted_iota(jnp.int32, (SUBLANES, GATHER_ROWS), 1)
    sub = lax.broadcasted_iota(jnp.int32, (SUBLANES, GATHER_ROWS), 0)
    return (lane & (SUBLANES - 1)) == sub


def _split_bf16(x):
    hi = x.astype(jnp.bfloat16)
    lo = (x - hi.astype(jnp.float32)).astype(jnp.bfloat16)
    return hi, lo


def _peer_hidden_kernel(idx_hbm, x_ref, gate_ref, tab_ref, sel_ref, w_ref, diag_ref, idx_a, idx_b, sem,
                        *rows_refs):
    mask = _diag_mask()

    def consume(t, slot):
        rows = pltpu.bitcast(rows_refs[slot][...], jnp.bfloat16)
        xb = x_ref[t].astype(jnp.bfloat16)
        d = lax.dot_general(xb, rows, NT_DIMS, preferred_element_type=jnp.float32)
        diag_ref[pl.ds(t, 1), :] = jnp.sum(jnp.where(mask, d, 0.0), axis=0, keepdims=True)

    _expert_token_loop(idx_hbm, (idx_a, idx_b), sem, tab_ref, rows_refs, x_ref.shape[0], consume)
    hi, lo = _split_bf16(diag_ref[...])
    sel = sel_ref[...]
    s = (jnp.dot(hi, sel, preferred_element_type=jnp.float32)
         + jnp.dot(lo, sel, preferred_element_type=jnp.float32))
    act = 0.5 * s * (1.0 + lax.erf(s * (1.0 / math.sqrt(2.0))))
    w_ref[...] = gate_ref[...] * act


def _peer_output_kernel(idx_hbm, w_ref, h_ref, tab_ref, expand_ref, out_ref, whi_ref, wlo_ref, idx_a, idx_b, sem,
                        *rows_refs):
    mask = _diag_mask()
    hi, lo = _split_bf16(w_ref[...])
    expand = expand_ref[...]
    whi_ref[...] = jnp.dot(hi, expand, preferred_element_type=jnp.float32)
    wlo_ref[...] = jnp.dot(lo, expand, preferred_element_type=jnp.float32)

    def consume(t, slot):
        rows = pltpu.bitcast(rows_refs[slot][...], jnp.bfloat16)
        parts = []
        for w_part in (whi_ref, wlo_ref):
            w_row = jnp.broadcast_to(w_part[pl.ds(t, 1), :], (SUBLANES, GATHER_ROWS))
            parts.append(jnp.where(mask, w_row, 0.0))
        lhs = jnp.concatenate(parts, axis=0).astype(jnp.bfloat16)
        y = jnp.dot(lhs, rows, preferred_element_type=jnp.float32)
        y = y[:SUBLANES] + y[SUBLANES:]
        y_row = jnp.concatenate([y[r:r + 1, :] for r in range(SUBLANES)], axis=1)
        out_ref[pl.ds(t, 1), :] = h_ref[pl.ds(t, 1), :] + y_row

    _expert_token_loop(idx_hbm, (idx_a, idx_b), sem, tab_ref, rows_refs, h_ref.shape[0], consume)


def _peer_experts(h, xn, rows, gates, peer_u, peer_v):
    T = h.shape[0]
    tb = PEER_TOKEN_BLOCK
    u_tab = _pack_expert_table(peer_u)
    v_tab = _pack_expert_table(peer_v)
    expand = (jnp.arange(PEER_PAIRS)[:, None] == jnp.arange(GATHER_ROWS)[None, :] // SUBLANES
              ).astype(jnp.bfloat16)
    params = pltpu.CompilerParams(dimension_semantics=("arbitrary",), vmem_limit_bytes=PEER_VMEM_LIMIT)
    hbm_ref = pl.BlockSpec(memory_space=pl.ANY)
    index_scratch = [pltpu.SMEM((PEER_PAIRS, tb), jnp.int32)] * 2 + [pltpu.SemaphoreType.DMA((2,))]
    pair_block = pl.BlockSpec((tb, PEER_PAIRS), lambda i: (i, 0))
    vmem_rows = pl.BlockSpec((tb, D_MODEL), lambda i: (i, 0))
    whole_vmem = pl.BlockSpec(memory_space=pltpu.VMEM)
    rows_scratch = pltpu.VMEM((GATHER_WORD_ROWS, LANES), jnp.int32)
    wide_scratch = pltpu.VMEM((tb, GATHER_ROWS), jnp.float32)

    w = pl.pallas_call(
        _peer_hidden_kernel,
        out_shape=jax.ShapeDtypeStruct((T, PEER_PAIRS), jnp.float32),
        grid=(T // tb,),
        in_specs=[hbm_ref, pl.BlockSpec((tb, SUBLANES, LANES), lambda i: (i, 0, 0)), pair_block,
                  whole_vmem, whole_vmem],
        out_specs=pair_block,
        scratch_shapes=[wide_scratch] + index_scratch + [rows_scratch] * HIDDEN_SLOTS,
        compiler_params=params,
        name="peer_hidden",
    )(rows, xn.reshape(T, SUBLANES, LANES), gates, u_tab, expand.T)

    return pl.pallas_call(
        _peer_output_kernel,
        out_shape=jax.ShapeDtypeStruct((T, D_MODEL), jnp.float32),
        grid=(T // tb,),
        in_specs=[hbm_ref, pair_block, vmem_rows, whole_vmem, whole_vmem],
        out_specs=vmem_rows,
        scratch_shapes=[wide_scratch, wide_scratch] + index_scratch + [rows_scratch] * OUTPUT_SLOTS,
        compiler_params=params,
        name="peer_output",
    )(rows, w, h, v_tab, expand)


def kernel(x, meta_tokens, g_mix, w_in, g_q_lora, w_uq, g_kv_lora, w_ukv, g_qk_q, g_qk_k, w_mla_out, g_ret_gn, w_ret_out, w_mix_out, g_ffn, w_peer_q, peer_keys_1, peer_keys_2, peer_u, peer_v):
    B, S, D = x.shape
    h = _mixer(x, meta_tokens, g_mix[0], w_in[0], g_q_lora[0], w_uq[0], g_kv_lora[0], w_ukv[0],
               g_qk_q[0], g_qk_k[0], w_mla_out[0], g_ret_gn[0], w_ret_out[0], w_mix_out[0])
    xn, rows, gates = _peer_route(h, g_ffn[0], w_peer_q[0], peer_keys_1[0], peer_keys_2[0])
    out = _peer_experts(h, xn, rows, gates, peer_u[0], peer_v[0])
    return out.reshape(B, S, D)
```

```python
import math

import jax
import jax.numpy as jnp
import numpy as np
from jax import lax
from jax.experimental import pallas as pl
from jax.experimental.pallas import tpu as pltpu

D_MODEL = 1024
N_META = 16
CHUNK = 128
ROPE_BASE = 10000.0
EPS = 1e-6
MLA_HEADS = 8
MLA_Q_LORA = 384
MLA_KV_LORA = 256
MLA_NOPE = 64
MLA_ROPE = 32
MLA_V = 64
MLA_QK = MLA_NOPE + MLA_ROPE
RET_HEADS = 8
RET_DK = 64
RET_DV = 128
PEER_HEADS = 8
PEER_NKEYS = 128
PEER_EXPERTS = PEER_NKEYS * PEER_NKEYS
PEER_DQ = 128
PEER_TOPK = 16
PEER_PAIRS = PEER_HEADS * PEER_TOPK
IN_SIZES = (MLA_Q_LORA, MLA_KV_LORA, MLA_ROPE, RET_HEADS * RET_DK, RET_HEADS * RET_DK,
            RET_HEADS * RET_DV, RET_HEADS * RET_DV, D_MODEL, D_MODEL)

LANES = 128
SUBLANES = 8
WORD_ROWS = D_MODEL // (2 * LANES)
GATHER_WORD_ROWS = PEER_PAIRS * WORD_ROWS
GATHER_ROWS = 2 * GATHER_WORD_ROWS
PEER_TOKEN_BLOCK = LANES
ROUTE_TOKEN_BLOCK = 128
ROUTE_HEAD_GROUP = 8
HIDDEN_SLOTS = 16
OUTPUT_SLOTS = 16
PEER_VMEM_LIMIT = 56 * 1024 * 1024
NT_DIMS = (((1,), (1,)), ((), ()))
HIGH_HALF_MASK = -65536
PACK_EXPERT_BLOCK = 256


PAD_ROWS = (-N_META) % CHUNK
HEAD_LANES = 128
MLA_WIDE = MLA_HEADS * HEAD_LANES
RET_QK_WIDE = RET_HEADS * RET_DK
RET_V_WIDE = RET_HEADS * RET_DV
MASK_VALUE = -1e30
MIXER_VMEM_LIMIT = 48 * 1024 * 1024
ATTN_HEAD_GROUP = 8
ATTN_TILE = 3 * CHUNK
RET_PAIR_GROUP = 2
_SEG_SIZES = (MLA_Q_LORA, MLA_KV_LORA, HEAD_LANES, RET_QK_WIDE, RET_QK_WIDE, RET_V_WIDE, RET_V_WIDE,
              D_MODEL, D_MODEL)
_SEG_STARTS = tuple(int(v) for v in np.cumsum((0,) + _SEG_SIZES))


def _rope_tables(n_rows, dim, block_offset, repeat):
    half = dim // 2
    pos = jnp.arange(n_rows, dtype=jnp.float32) - PAD_ROWS
    inv = ROPE_BASE ** (-jnp.arange(half, dtype=jnp.float32) / half)
    lane = np.arange(LANES)
    rel = (lane - block_offset) % repeat
    in_rope = (lane >= block_offset) & (rel < dim)
    ang = pos[:, None] * inv[rel % half][None, :]
    cos = jnp.where(in_rope[None, :], jnp.cos(ang), 1.0)
    sin = jnp.sin(ang)
    s1 = jnp.where((in_rope & (rel < half))[None, :], -sin, 0.0)
    s2 = jnp.where((in_rope & (rel >= half))[None, :], sin, 0.0)
    return cos, s1, s2


def _apply_rope(x, cos, s1, s2, half):
    return x * cos + pltpu.roll(x, LANES - half, 1) * s1 + pltpu.roll(x, half, 1) * s2


def _mixer_in_kernel(h_ref, head_ref, gmix_ref, win_ref, gq_ref, wuq_ref, gkv_ref, wk_ref, wv_ref, gqkq_ref, gqkk_ref,
                     mc_ref, ms1_ref, ms2_ref, rc_ref, rs1_ref, rs2_ref,
                     q_ref, k_ref, v_ref, rq_ref, rk_ref, rv_ref, sg_ref, ga_ref, gr_ref):
    x = jnp.where(pl.program_id(1) == 0, head_ref[...], h_ref[...])
    hn = (x * lax.rsqrt(jnp.mean(x * x, axis=-1, keepdims=True) + EPS) * gmix_ref[...]).astype(jnp.bfloat16)

    def seg(i):
        return jnp.dot(hn, win_ref[:, _SEG_STARTS[i]:_SEG_STARTS[i + 1]], preferred_element_type=jnp.float32)

    def lora_norm(c, g_ref):
        return (c * lax.rsqrt(jnp.mean(c * c, axis=-1, keepdims=True) + EPS) * g_ref[...]).astype(jnp.bfloat16)

    cq = lora_norm(seg(0), gq_ref)
    ckv = lora_norm(seg(1), gkv_ref)
    kpe = seg(2)
    q = jnp.dot(cq, wuq_ref[...], preferred_element_type=jnp.float32)
    kn = jnp.dot(ckv, wk_ref[...], preferred_element_type=jnp.float32)
    v_ref[...] = jnp.dot(ckv, wv_ref[...], preferred_element_type=jnp.float32).astype(jnp.bfloat16)
    mc, ms1, ms2 = mc_ref[...], ms1_ref[...], ms2_ref[...]
    scale = math.log2(math.e) / math.sqrt(MLA_QK)

    def qk_norm_rope(t, g_ref):
        t = t * lax.rsqrt(jnp.sum(t * t, axis=-1, keepdims=True) * (1.0 / MLA_QK) + EPS) * g_ref[...]
        return _apply_rope(t, mc, ms1, ms2, MLA_ROPE // 2)

    for hd in range(MLA_HEADS):
        blk = slice(hd * HEAD_LANES, (hd + 1) * HEAD_LANES)
        q_ref[:, blk] = (qk_norm_rope(q[:, blk], gqkq_ref) * scale).astype(jnp.bfloat16)
        k_ref[:, blk] = qk_norm_rope(kn[:, blk] + kpe, gqkk_ref).astype(jnp.bfloat16)

    rc, rs1, rs2 = rc_ref[...], rs1_ref[...], rs2_ref[...]
    rq = seg(3)
    rk = seg(4)
    for j in range(RET_QK_WIDE // LANES):
        blk = slice(j * LANES, (j + 1) * LANES)
        rq_ref[:, blk] = _apply_rope(rq[:, blk], rc, rs1, rs2, RET_DK // 2).astype(jnp.bfloat16)
        rk_ref[:, blk] = (_apply_rope(rk[:, blk], rc, rs1, rs2, RET_DK // 2) * (RET_DK ** -0.5)).astype(jnp.bfloat16)
    rv_ref[...] = seg(5).astype(jnp.bfloat16)
    gate = seg(6)
    sg_ref[...] = (gate * jax.nn.sigmoid(gate)).astype(jnp.bfloat16)
    ga_ref[...] = jax.nn.sigmoid(seg(7)).astype(jnp.bfloat16)
    gr_ref[...] = jax.nn.sigmoid(seg(8)).astype(jnp.bfloat16)


def _attention_kernel(q_ref, k_ref, v_ref, o_ref, m_ref, l_ref, acc_ref):
    qi = pl.program_id(2)
    t = ATTN_TILE
    n_heads = q_ref.shape[1] // HEAD_LANES
    n_lane_tiles = t // LANES
    m_ref[...] = jnp.full(m_ref.shape, MASK_VALUE, jnp.float32)
    l_ref[...] = jnp.zeros(l_ref.shape, jnp.float32)
    acc_ref[...] = jnp.zeros(acc_ref.shape, jnp.float32)

    def lane_tiles(x):
        return [x[:, i * LANES:(i + 1) * LANES] for i in range(n_lane_tiles)]

    def scores(j, hd, masked):
        blk = slice(hd * HEAD_LANES, (hd + 1) * HEAD_LANES)
        start = pl.multiple_of(j * t, t)
        s = lax.dot_general(q_ref[:, blk], k_ref[pl.ds(start, t), blk], NT_DIMS,
                            preferred_element_type=jnp.float32)
        if masked:
            r = lax.broadcasted_iota(jnp.int32, (t, t), 0) + qi * t
            c = lax.broadcasted_iota(jnp.int32, (t, t), 1) + j * t
            s = jnp.where((c <= r) & ((c >= PAD_ROWS) | (r < PAD_ROWS)), s, MASK_VALUE)
        return s

    def max_step(j, masked):
        for hd in range(n_heads):
            m = m_ref[hd]
            for part in lane_tiles(scores(j, hd, masked)):
                m = jnp.maximum(m, part)
            m_ref[hd] = m

    def sum_step(j, masked):
        start = pl.multiple_of(j * t, t)
        for hd in range(n_heads):
            blk = slice(hd * HEAD_LANES, (hd + 1) * HEAD_LANES)
            p = jnp.exp2(scores(j, hd, masked) - jnp.tile(m_ref[hd], (1, n_lane_tiles)))
            l = l_ref[hd]
            for part in lane_tiles(p):
                l = l + part
            l_ref[hd] = l
            acc_ref[hd] += jnp.dot(p.astype(jnp.bfloat16), v_ref[pl.ds(start, t), blk],
                                   preferred_element_type=jnp.float32)

    def sweep(step):
        def body(j, carry):
            edge = (j == 0) | (j == qi)

            @pl.when(edge)
            def _():
                step(j, True)

            @pl.when(jnp.logical_not(edge))
            def _():
                step(j, False)

            return carry

        lax.fori_loop(0, qi + 1, body, 0)

    sweep(max_step)
    for hd in range(n_heads):
        m_ref[hd] = jnp.broadcast_to(jnp.max(m_ref[hd], axis=-1, keepdims=True), (t, LANES))
    sweep(sum_step)
    for hd in range(n_heads):
        blk = slice(hd * HEAD_LANES, (hd + 1) * HEAD_LANES)
        o_ref[:, blk] = (acc_ref[hd] / jnp.sum(l_ref[hd], axis=-1, keepdims=True)).astype(jnp.bfloat16)


def _retention_kernel(q_ref, k_ref, v_ref, sg_ref, gn_ref, decay_ref, zeta_ref, xi_ref, gamma_ref, o_ref):
    n_chunks = q_ref.shape[0] // CHUNK
    n_pairs = q_ref.shape[1] // LANES
    lane = lax.broadcasted_iota(jnp.int32, (CHUNK, LANES), 1)
    row = lax.broadcasted_iota(jnp.int32, (LANES, RET_DV), 0)
    is_a = lane < RET_DK
    gn = gn_ref[...]
    contract_rows = (((0,), (0,)), ((), ()))

    def group_norm(y):
        mu = jnp.mean(y, axis=-1, keepdims=True)
        d = y - mu
        return d * lax.rsqrt(jnp.mean(d * d, axis=-1, keepdims=True) + EPS)

    def pair_chunk(rows, p, state):
        qb = q_ref[rows, p * LANES:(p + 1) * LANES]
        kb = k_ref[rows, p * LANES:(p + 1) * LANES]
        state_bf = state.astype(jnp.bfloat16)
        new_parts = []
        for hd in range(2):
            head = 2 * p + hd
            cols = slice(head * RET_DV, (head + 1) * RET_DV)
            qh = jnp.where(is_a if hd == 0 else ~is_a, qb, jnp.zeros_like(qb))
            vh = v_ref[rows, cols]
            scores = lax.dot_general(qh, kb, NT_DIMS, preferred_element_type=jnp.float32) * decay_ref[head]
            inner = jnp.dot(scores.astype(jnp.bfloat16), vh, preferred_element_type=jnp.float32)
            cross = jnp.dot(qh, state_bf, preferred_element_type=jnp.float32) * xi_ref[head]
            yn = group_norm(inner + cross) * gn[:, cols]
            o_ref[rows, cols] = (sg_ref[rows, cols].astype(jnp.float32) * yn).astype(jnp.bfloat16)
            kz = (kb.astype(jnp.float32) * zeta_ref[head]).astype(jnp.bfloat16)
            new_parts.append(lax.dot_general(kz, vh, contract_rows, preferred_element_type=jnp.float32))
        return gamma_ref[p] * state + jnp.where(row < RET_DK, new_parts[0], new_parts[1])

    def chunk(n, states):
        rows = pl.ds(pl.multiple_of(n * CHUNK, CHUNK), CHUNK)
        return tuple(pair_chunk(rows, p, state) for p, state in enumerate(states))

    lax.fori_loop(0, n_chunks, chunk, tuple(jnp.zeros((LANES, RET_DV), jnp.float32) for _ in range(n_pairs)))


def _mixer_out_kernel(h_ref, o_ref, yr_ref, ga_ref, gr_ref, wa_ref, wr_ref, wm_ref, out_ref):
    y_a = jnp.dot(o_ref[...], wa_ref[...], preferred_element_type=jnp.float32)
    y_r = jnp.dot(yr_ref[...], wr_ref[...], preferred_element_type=jnp.float32)
    merged = ga_ref[...].astype(jnp.float32) * y_a + gr_ref[...].astype(jnp.float32) * y_r
    out_ref[...] = h_ref[...] + jnp.dot(merged.astype(jnp.bfloat16), wm_ref[...],
                                        preferred_element_type=jnp.float32)


def _pad_heads(w, heads, width):
    lead = w.shape[:-1]
    w = w.reshape(lead + (heads, width))
    w = jnp.pad(w, [(0, 0)] * len(lead) + [(0, 0), (0, HEAD_LANES - width)])
    return w.reshape(lead + (heads * HEAD_LANES,))


def _mixer(x, meta_tokens, g_mix, w_in, g_q_lora, w_uq, g_kv_lora, w_ukv, g_qk_q, g_qk_k,
           w_mla_out, g_ret_gn, w_ret_out, w_mix_out):
    B, S, D = x.shape
    bf = jnp.bfloat16
    rows = PAD_ROWS + N_META + S
    nt = rows // CHUNK
    seq_tiles = S // CHUNK
    assert PAD_ROWS + N_META == CHUNK and S % CHUNK == 0
    x2 = x.reshape(B * S, D)
    head_tile = jnp.concatenate([jnp.zeros((PAD_ROWS, D), x.dtype), meta_tokens.astype(x.dtype)], axis=0)

    segs = jnp.split(w_in, np.cumsum(IN_SIZES)[:-1].tolist(), axis=1)
    kpe_cols = jnp.pad(segs[2], ((0, 0), (MLA_NOPE, HEAD_LANES - MLA_NOPE - MLA_ROPE)))
    win = jnp.concatenate(segs[:2] + [kpe_cols] + segs[3:], axis=1).astype(bf)
    wuq = _pad_heads(w_uq, MLA_HEADS, MLA_QK).astype(bf)
    wkv = w_ukv.reshape(MLA_KV_LORA, MLA_HEADS, MLA_NOPE + MLA_V)
    wk = _pad_heads(wkv[:, :, :MLA_NOPE].reshape(MLA_KV_LORA, -1), MLA_HEADS, MLA_NOPE).astype(bf)
    wv = _pad_heads(wkv[:, :, MLA_NOPE:].reshape(MLA_KV_LORA, -1), MLA_HEADS, MLA_V).astype(bf)
    gqkq = jnp.pad(g_qk_q, (0, HEAD_LANES - MLA_QK)).reshape(1, HEAD_LANES)
    gqkk = jnp.pad(g_qk_k, (0, HEAD_LANES - MLA_QK)).reshape(1, HEAD_LANES)
    wa = _pad_heads(w_mla_out.T, MLA_HEADS, MLA_V).T.astype(bf)
    mla_tabs = _rope_tables(rows, MLA_ROPE, MLA_NOPE, HEAD_LANES)
    ret_tabs = _rope_tables(rows, RET_DK, 0, RET_DK)

    row_tile = lambda width: pl.BlockSpec((CHUNK, width), lambda b, i: (b * nt + i, 0))
    const = lambda shape: pl.BlockSpec(shape, lambda b, i: (0,) * len(shape))
    tab = pl.BlockSpec((CHUNK, LANES), lambda b, i: (i, 0))
    n_rows = B * rows
    widths = (MLA_WIDE, MLA_WIDE, MLA_WIDE, RET_QK_WIDE, RET_QK_WIDE, RET_V_WIDE, RET_V_WIDE, D, D)
    q, k, v, rq, rk, rv, sg, ga, gr = pl.pallas_call(
        _mixer_in_kernel,
        out_shape=tuple(jax.ShapeDtypeStruct((n_rows, w), bf) for w in widths),
        grid=(B, nt),
        in_specs=[pl.BlockSpec((CHUNK, D), lambda b, i: (b * seq_tiles + jnp.maximum(i - 1, 0), 0)),
                  const((CHUNK, D)), const((1, D)), const(win.shape), const((1, MLA_Q_LORA)), const(wuq.shape),
                  const((1, MLA_KV_LORA)), const(wk.shape), const(wv.shape), const((1, HEAD_LANES)),
                  const((1, HEAD_LANES))] + [tab] * 6,
        out_specs=tuple(row_tile(w) for w in widths),
        compiler_params=pltpu.CompilerParams(dimension_semantics=("arbitrary", "arbitrary"),
                                             vmem_limit_bytes=MIXER_VMEM_LIMIT),
        name="mixer_in",
    )(x2, head_tile, g_mix.reshape(1, D), win, g_q_lora.reshape(1, -1), wuq, g_kv_lora.reshape(1, -1), wk, wv,
      gqkq, gqkk, *mla_tabs, *ret_tabs)

    group = ATTN_HEAD_GROUP * HEAD_LANES
    seq_spec = pl.BlockSpec((rows, group), lambda b, g, i: (b, g))
    n_attn = rows // ATTN_TILE
    tile_spec = pl.BlockSpec((ATTN_TILE, group), lambda b, g, i: (b * n_attn + i, g))
    o = pl.pallas_call(
        _attention_kernel,
        out_shape=jax.ShapeDtypeStruct((n_rows, MLA_WIDE), bf),
        grid=(B, MLA_HEADS // ATTN_HEAD_GROUP, n_attn),
        in_specs=[tile_spec, seq_spec, seq_spec],
        out_specs=tile_spec,
        scratch_shapes=[pltpu.VMEM((ATTN_HEAD_GROUP, ATTN_TILE, HEAD_LANES), jnp.float32)] * 3,
        compiler_params=pltpu.CompilerParams(dimension_semantics=("arbitrary",) * 3,
                                             vmem_limit_bytes=MIXER_VMEM_LIMIT),
        name="mla_attention",
    )(q, k, v)

    log_gamma = jnp.log(1.0 - 2.0 ** (-5.0 - jnp.arange(RET_HEADS, dtype=jnp.float32)))
    idx = jnp.arange(CHUNK, dtype=jnp.float32)
    diff = idx[:, None] - idx[None, :]
    decay = jnp.where(diff[None] >= 0, jnp.exp(jnp.maximum(diff, 0.0)[None] * log_gamma[:, None, None]), 0.0)
    zeta = jnp.exp((CHUNK - 1 - idx)[None, :] * log_gamma[:, None])
    xi = jnp.exp((idx + 1)[None, :] * log_gamma[:, None])
    gamma_c = jnp.exp(CHUNK * log_gamma)
    zeta_b = jnp.broadcast_to(zeta[:, :, None], (RET_HEADS, CHUNK, LANES))
    xi_b = jnp.broadcast_to(xi[:, :, None], (RET_HEADS, CHUNK, RET_DV))
    gamma_rows = jnp.broadcast_to(jnp.repeat(gamma_c, RET_DK).reshape(RET_HEADS // 2, LANES, 1),
                                  (RET_HEADS // 2, LANES, RET_DV))
    g = RET_PAIR_GROUP
    pair_seq = lambda width: pl.BlockSpec((rows, g * width), lambda b, p: (b, p))
    pair_tab = lambda d1, d2: pl.BlockSpec((2 * g, d1, d2), lambda b, p: (p, 0, 0))
    yr = pl.pallas_call(
        _retention_kernel,
        out_shape=jax.ShapeDtypeStruct((n_rows, RET_V_WIDE), bf),
        grid=(B, RET_HEADS // (2 * g)),
        in_specs=[pair_seq(LANES), pair_seq(LANES), pair_seq(2 * RET_DV), pair_seq(2 * RET_DV),
                  pl.BlockSpec((1, g * 2 * RET_DV), lambda b, p: (0, p)),
                  pair_tab(CHUNK, CHUNK), pair_tab(CHUNK, LANES), pair_tab(CHUNK, RET_DV),
                  pl.BlockSpec((g, LANES, RET_DV), lambda b, p: (p, 0, 0))],
        out_specs=pair_seq(2 * RET_DV),
        compiler_params=pltpu.CompilerParams(dimension_semantics=("arbitrary", "arbitrary"),
                                             vmem_limit_bytes=MIXER_VMEM_LIMIT),
        name="retention",
    )(rq, rk, rv, sg, g_ret_gn.reshape(1, -1), decay, zeta_b, xi_b, gamma_rows)

    in_tile = lambda width: pl.BlockSpec((CHUNK, width), lambda b, i: (b * nt + 1 + i, 0))
    seq_tile = pl.BlockSpec((CHUNK, D), lambda b, i: (b * seq_tiles + i, 0))
    return pl.pallas_call(
        _mixer_out_kernel,
        out_shape=jax.ShapeDtypeStruct((B * S, D), jnp.float32),
        grid=(B, seq_tiles),
        in_specs=[seq_tile, in_tile(MLA_WIDE), in_tile(RET_V_WIDE), in_tile(D), in_tile(D),
                  const(wa.shape), const(w_ret_out.shape), const(w_mix_out.shape)],
        out_specs=seq_tile,
        compiler_params=pltpu.CompilerParams(dimension_semantics=("arbitrary", "arbitrary"),
                                             vmem_limit_bytes=MIXER_VMEM_LIMIT),
        name="mixer_out",
    )(x2, o, yr, ga, gr, wa, w_ret_out.astype(bf), w_mix_out.astype(bf))


def _top_rows(s, count):
    n = s.shape[0]
    iota = lax.broadcasted_iota(jnp.int32, s.shape, 0).astype(jnp.float32)
    vals, idxs = [], []
    for _ in range(count):
        m = jnp.max(s, axis=0, keepdims=True)
        idx = jnp.min(jnp.where(s == m, iota, float(n)), axis=0, keepdims=True)
        vals.append(m)
        idxs.append(idx)
        s = jnp.where(iota == idx, -jnp.inf, s)
    return vals, idxs


def _route_head(q, k1, k2):
    s1 = lax.dot_general(k1, q, NT_DIMS, preferred_element_type=jnp.float32)
    s2 = lax.dot_general(k2, q, NT_DIMS, preferred_element_type=jnp.float32)
    v1, i1 = _top_rows(s1, PEER_TOPK)
    v2, i2 = _top_rows(s2, PEER_TOPK)
    v2_all = jnp.concatenate(v2, axis=0)
    i2_all = jnp.concatenate(i2, axis=0)
    n_mid = SUBLANES
    cand = [v1[0] + v2_all] + [v1[a] + v2_all[:n_mid] for a in range(1, n_mid)]
    cand.append(jnp.concatenate(v1[n_mid:], axis=0) + v2[0])
    cand_rows = [i1[0] * PEER_NKEYS + i2_all] + [i1[a] * PEER_NKEYS + i2_all[:n_mid] for a in range(1, n_mid)]
    cand_rows.append(jnp.concatenate(i1[n_mid:], axis=0) * PEER_NKEYS + i2[0])
    cand = jnp.concatenate(cand, axis=0)
    cand_rows = jnp.concatenate(cand_rows, axis=0) * float(WORD_ROWS)
    sc, ci = _top_rows(cand, PEER_TOPK)
    iota = lax.broadcasted_iota(jnp.int32, cand.shape, 0).astype(jnp.float32)
    rows = [jnp.max(jnp.where(iota == c, cand_rows, -1.0), axis=0, keepdims=True) for c in ci]
    ex = [jnp.exp(v - sc[0]) for v in sc]
    denom = ex[0]
    for e in ex[1:]:
        denom = denom + e
    return jnp.concatenate(rows, axis=0).astype(jnp.int32), jnp.concatenate(ex, axis=0) / denom


def _peer_route_kernel(h_ref, g_ref, wq_ref, k1_ref, k2_ref, xn_ref, rows_ref, gates_ref,
                       xb_ref, rows_scr, gates_scr):
    grp = pl.program_id(1)

    @pl.when(grp == 0)
    def _():
        x = h_ref[...]
        xn = x * lax.rsqrt(jnp.mean(x * x, axis=-1, keepdims=True) + EPS) * g_ref[...]
        for r in range(SUBLANES):
            xn_ref[pl.ds(r, x.shape[0], stride=SUBLANES), :] = xn[:, r * LANES:(r + 1) * LANES]
        xb_ref[...] = xn.astype(jnp.bfloat16)

    q = jnp.dot(xb_ref[...], wq_ref[...], preferred_element_type=jnp.float32).astype(jnp.bfloat16)
    k1, k2 = k1_ref[...], k2_ref[...]
    for j in range(ROUTE_HEAD_GROUP):
        rows, gates = _route_head(q[:, j * PEER_DQ:(j + 1) * PEER_DQ], k1, k2)
        base = pl.multiple_of((grp * ROUTE_HEAD_GROUP + j) * PEER_TOPK, PEER_TOPK)
        rows_scr[pl.ds(base, PEER_TOPK), :] = rows
        gates_scr[pl.ds(base, PEER_TOPK), :] = gates

    @pl.when(grp == PEER_HEADS // ROUTE_HEAD_GROUP - 1)
    def _():
        for j in range(rows_ref.shape[0]):
            rows_ref[j] = rows_scr[:, j * PEER_TOKEN_BLOCK:(j + 1) * PEER_TOKEN_BLOCK]
        gates_ref[...] = gates_scr[...].T


def _peer_route(h, g_ffn, w_peer_q, keys_1, keys_2):
    T = h.shape[0]
    tb = ROUTE_TOKEN_BLOCK
    per_step = tb // PEER_TOKEN_BLOCK
    half = PEER_DQ // 2
    zeros = jnp.zeros((PEER_NKEYS, half), jnp.bfloat16)
    k1 = jnp.concatenate([keys_1.astype(jnp.bfloat16), zeros], axis=1)
    k2 = jnp.concatenate([zeros, keys_2.astype(jnp.bfloat16)], axis=1)
    keys_spec = pl.BlockSpec((PEER_NKEYS, PEER_DQ), lambda i, g: (0, 0))
    pair_spec = pl.BlockSpec((tb, PEER_PAIRS), lambda i, g: (i, 0))
    return pl.pallas_call(
        _peer_route_kernel,
        out_shape=(jax.ShapeDtypeStruct((T * SUBLANES, LANES), jnp.float32),
                   jax.ShapeDtypeStruct((T // PEER_TOKEN_BLOCK, PEER_PAIRS, PEER_TOKEN_BLOCK), jnp.int32),
                   jax.ShapeDtypeStruct((T, PEER_PAIRS), jnp.float32)),
        grid=(T // tb, PEER_HEADS // ROUTE_HEAD_GROUP),
        in_specs=[pl.BlockSpec((tb, D_MODEL), lambda i, g: (i, 0)),
                  pl.BlockSpec((1, D_MODEL), lambda i, g: (0, 0)),
                  pl.BlockSpec((D_MODEL, ROUTE_HEAD_GROUP * PEER_DQ), lambda i, g: (0, g)),
                  keys_spec, keys_spec],
        out_specs=(pl.BlockSpec((tb * SUBLANES, LANES), lambda i, g: (i, 0)),
                   pl.BlockSpec((per_step, PEER_PAIRS, PEER_TOKEN_BLOCK), lambda i, g: (i, 0, 0)),
                   pair_spec),
        scratch_shapes=[pltpu.VMEM((tb, D_MODEL), jnp.bfloat16),
                        pltpu.VMEM((PEER_PAIRS, tb), jnp.int32),
                        pltpu.VMEM((PEER_PAIRS, tb), jnp.float32)],
        compiler_params=pltpu.CompilerParams(dimension_semantics=("arbitrary", "arbitrary")),
        name="peer_route",
    )(h, g_ffn.reshape(1, D_MODEL), w_peer_q.astype(jnp.bfloat16), k1, k2)


def _pack_table_kernel(tab_ref, out_ref):
    n = tab_ref.shape[0]
    bits = lax.bitcast_convert_type(tab_ref[...].astype(jnp.bfloat16).astype(jnp.float32), jnp.int32)
    for j in range(WORD_ROWS):
        lo = lax.shift_right_logical(bits[:, 2 * j * LANES:(2 * j + 1) * LANES], 16)
        hi = bits[:, (2 * j + 1) * LANES:(2 * j + 2) * LANES] & HIGH_HALF_MASK
        out_ref[pl.ds(j, n, stride=WORD_ROWS), :] = hi | lo


def _pack_expert_table(tab):
    n = PACK_EXPERT_BLOCK
    return pl.pallas_call(
        _pack_table_kernel,
        out_shape=jax.ShapeDtypeStruct((tab.shape[0] * WORD_ROWS, LANES), jnp.int32),
        grid=(tab.shape[0] // n,),
        in_specs=[pl.BlockSpec((n, D_MODEL), lambda i: (i, 0))],
        out_specs=pl.BlockSpec((n * WORD_ROWS, LANES), lambda i: (i, 0)),
        compiler_params=pltpu.CompilerParams(dimension_semantics=("arbitrary",)),
        name="pack_table",
    )(tab)


def _gather_rows(idx_ref, tab_ref, dst_refs, t0):
    for k in range(PEER_PAIRS):
        idx_k = idx_ref.at[k]
        for s, dst_ref in enumerate(dst_refs):
            row = pl.multiple_of(idx_k[t0 + s], WORD_ROWS)
            dst_ref[k * WORD_ROWS:(k + 1) * WORD_ROWS, :] = tab_ref[pl.ds(row, WORD_ROWS), :]


def _expert_token_loop(idx_hbm, idx_bufs, sem, tab_ref, rows_refs, n_tokens, consume):
    i = pl.program_id(0)
    n = pl.num_programs(0)
    n_slots = len(rows_refs)
    assert n_tokens % n_slots == 0

    def copy(block, slot):
        return pltpu.make_async_copy(idx_hbm.at[block], idx_bufs[slot], sem.at[slot])

    def consume_group(t0):
        for s in range(n_slots):
            consume(t0 + s, s)

    @pl.when(i == 0)
    def _():
        copy(0, 0).start()
        copy(0, 0).wait()
        _gather_rows(idx_bufs[0], tab_ref, rows_refs, 0)

    def run(slot):
        has_next = i + 1 < n

        @pl.when(has_next)
        def _():
            copy(i + 1, 1 - slot).start()

        def step(j, carry):
            consume_group(n_slots * j)
            _gather_rows(idx_bufs[slot], tab_ref, rows_refs, n_slots * (j + 1))
            return carry

        lax.fori_loop(0, n_tokens // n_slots - 1, step, 0)

        @pl.when(has_next)
        def _():
            copy(i + 1, 1 - slot).wait()
            consume_group(n_tokens - n_slots)
            _gather_rows(idx_bufs[1 - slot], tab_ref, rows_refs, 0)

        @pl.when(jnp.logical_not(has_next))
        def _():
            consume_group(n_tokens - n_slots)

    for slot in range(2):
        pl.when(lax.rem(i, 2) == slot)(lambda slot=slot: run(slot))


def _diag_mask():
    lane = lax.broadcasted_iota(jnp.int32, (SUBLANES, GATHER_ROWS), 1)
    sub = lax.broadcasted_iota(jnp.int32, (SUBLANES, GATHER_ROWS), 0)
    return (lane & (SUBLANES - 1)) == sub


def _split_bf16(x):
    hi = x.astype(jnp.bfloat16)
    lo = (x - hi.astype(jnp.float32)).astype(jnp.bfloat16)
    return hi, lo


def _peer_hidden_kernel(idx_hbm, x_ref, gate_ref, tab_ref, sel_ref, w_ref, diag_ref, idx_a, idx_b, sem,
                        *rows_refs):
    mask = _diag_mask()

    def consume(t, slot):
        rows = pltpu.bitcast(rows_refs[slot][...], jnp.bfloat16)
        xb = x_ref[t].astype(jnp.bfloat16)
        d = lax.dot_general(xb, rows, NT_DIMS, preferred_element_type=jnp.float32)
        diag_ref[pl.ds(t, 1), :] = jnp.sum(jnp.where(mask, d, 0.0), axis=0, keepdims=True)

    _expert_token_loop(idx_hbm, (idx_a, idx_b), sem, tab_ref, rows_refs, x_ref.shape[0], consume)
    hi, lo = _split_bf16(diag_ref[...])
    sel = sel_ref[...]
    s = (jnp.dot(hi, sel, preferred_element_type=jnp.float32)
         + jnp.dot(lo, sel, preferred_element_type=jnp.float32))
    act = 0.5 * s * (1.0 + lax.erf(s * (1.0 / math.sqrt(2.0))))
    w_ref[...] = gate_ref[...] * act


def _peer_output_kernel(idx_hbm, w_ref, h_ref, tab_ref, expand_ref, out_ref, whi_ref, wlo_ref, idx_a, idx_b, sem,
                        *rows_refs):
    mask = _diag_mask()
    hi, lo = _split_bf16(w_ref[...])
    expand = expand_ref[...]
    whi_ref[...] = jnp.dot(hi, expand, preferred_element_type=jnp.float32)
    wlo_ref[...] = jnp.dot(lo, expand, preferred_element_type=jnp.float32)

    def consume(t, slot):
        rows = pltpu.bitcast(rows_refs[slot][...], jnp.bfloat16)
        parts = []
        for w_part in (whi_ref, wlo_ref):
            w_row = jnp.broadcast_to(w_part[pl.ds(t, 1), :], (SUBLANES, GATHER_ROWS))
            parts.append(jnp.where(mask, w_row, 0.0))
        lhs = jnp.concatenate(parts, axis=0).astype(jnp.bfloat16)
        y = jnp.dot(lhs, rows, preferred_element_type=jnp.float32)
        y = y[:SUBLANES] + y[SUBLANES:]
        y_row = jnp.concatenate([y[r:r + 1, :] for r in range(SUBLANES)], axis=1)
        out_ref[pl.ds(t, 1), :] = h_ref[pl.ds(t, 1), :] + y_row

    _expert_token_loop(idx_hbm, (idx_a, idx_b), sem, tab_ref, rows_refs, h_ref.shape[0], consume)


def _peer_experts(h, xn, rows, gates, peer_u, peer_v):
    T = h.shape[0]
    tb = PEER_TOKEN_BLOCK
    u_tab = _pack_expert_table(peer_u)
    v_tab = _pack_expert_table(peer_v)
    expand = (jnp.arange(PEER_PAIRS)[:, None] == jnp.arange(GATHER_ROWS)[None, :] // SUBLANES
              ).astype(jnp.bfloat16)
    params = pltpu.CompilerParams(dimension_semantics=("arbitrary",), vmem_limit_bytes=PEER_VMEM_LIMIT)
    hbm_ref = pl.BlockSpec(memory_space=pl.ANY)
    index_scratch = [pltpu.SMEM((PEER_PAIRS, tb), jnp.int32)] * 2 + [pltpu.SemaphoreType.DMA((2,))]
    pair_block = pl.BlockSpec((tb, PEER_PAIRS), lambda i: (i, 0))
    vmem_rows = pl.BlockSpec((tb, D_MODEL), lambda i: (i, 0))
    whole_vmem = pl.BlockSpec(memory_space=pltpu.VMEM)
    rows_scratch = pltpu.VMEM((GATHER_WORD_ROWS, LANES), jnp.int32)
    wide_scratch = pltpu.VMEM((tb, GATHER_ROWS), jnp.float32)

    w = pl.pallas_call(
        _peer_hidden_kernel,
        out_shape=jax.ShapeDtypeStruct((T, PEER_PAIRS), jnp.float32),
        grid=(T // tb,),
        in_specs=[hbm_ref, pl.BlockSpec((tb, SUBLANES, LANES), lambda i: (i, 0, 0)), pair_block,
                  whole_vmem, whole_vmem],
        out_specs=pair_block,
        scratch_shapes=[wide_scratch] + index_scratch + [rows_scratch] * HIDDEN_SLOTS,
        compiler_params=params,
        name="peer_hidden",
    )(rows, xn.reshape(T, SUBLANES, LANES), gates, u_tab, expand.T)

    return pl.pallas_call(
        _peer_output_kernel,
        out_shape=jax.ShapeDtypeStruct((T, D_MODEL), jnp.float32),
        grid=(T // tb,),
        in_specs=[hbm_ref, pair_block, vmem_rows, whole_vmem, whole_vmem],
        out_specs=vmem_rows,
        scratch_shapes=[wide_scratch, wide_scratch] + index_scratch + [rows_scratch] * OUTPUT_SLOTS,
        compiler_params=params,
        name="peer_output",
    )(rows, w, h, v_tab, expand)


def kernel(x, meta_tokens, g_mix, w_in, g_q_lora, w_uq, g_kv_lora, w_ukv, g_qk_q, g_qk_k, w_mla_out, g_ret_gn, w_ret_out, w_mix_out, g_ffn, w_peer_q, peer_keys_1, peer_keys_2, peer_u, peer_v):
    B, S, D = x.shape
    h = _mixer(x, meta_tokens, g_mix[0], w_in[0], g_q_lora[0], w_uq[0], g_kv_lora[0], w_ukv[0],
               g_qk_q[0], g_qk_k[0], w_mla_out[0], g_ret_gn[0], w_ret_out[0], w_mix_out[0])
    xn, rows, gates = _peer_route(h, g_ffn[0], w_peer_q[0], peer_keys_1[0], peer_keys_2[0])
    out = _peer_experts(h, xn, rows, gates, peer_u[0], peer_v[0])
    return out.reshape(B, S, D)
```

```python
import math

import jax
import jax.numpy as jnp
import numpy as np
from jax import lax
from jax.experimental import pallas as pl
from jax.experimental.pallas import tpu as pltpu

D_MODEL = 1024
N_META = 16
CHUNK = 128
ROPE_BASE = 10000.0
EPS = 1e-6
MLA_HEADS = 8
MLA_Q_LORA = 384
MLA_KV_LORA = 256
MLA_NOPE = 64
MLA_ROPE = 32
MLA_V = 64
MLA_QK = MLA_NOPE + MLA_ROPE
RET_HEADS = 8
RET_DK = 64
RET_DV = 128
PEER_HEADS = 8
PEER_NKEYS = 128
PEER_EXPERTS = PEER_NKEYS * PEER_NKEYS
PEER_DQ = 128
PEER_TOPK = 16
PEER_PAIRS = PEER_HEADS * PEER_TOPK
IN_SIZES = (MLA_Q_LORA, MLA_KV_LORA, MLA_ROPE, RET_HEADS * RET_DK, RET_HEADS * RET_DK,
            RET_HEADS * RET_DV, RET_HEADS * RET_DV, D_MODEL, D_MODEL)

LANES = 128
SUBLANES = 8
WORD_ROWS = D_MODEL // (2 * LANES)
GATHER_WORD_ROWS = PEER_PAIRS * WORD_ROWS
GATHER_ROWS = 2 * GATHER_WORD_ROWS
PEER_TOKEN_BLOCK = LANES
ROUTE_TOKEN_BLOCK = 128
ROUTE_HEAD_GROUP = 8
HIDDEN_SLOTS = 32
OUTPUT_SLOTS = 32
PEER_VMEM_LIMIT = 56 * 1024 * 1024
NT_DIMS = (((1,), (1,)), ((), ()))
HIGH_HALF_MASK = -65536
PACK_EXPERT_BLOCK = 256


PAD_ROWS = (-N_META) % CHUNK
HEAD_LANES = 128
MLA_WIDE = MLA_HEADS * HEAD_LANES
RET_QK_WIDE = RET_HEADS * RET_DK
RET_V_WIDE = RET_HEADS * RET_DV
MASK_VALUE = -1e30
MIXER_VMEM_LIMIT = 48 * 1024 * 1024
ATTN_HEAD_GROUP = 8
ATTN_TILE = 3 * CHUNK
RET_PAIR_GROUP = 2
_SEG_SIZES = (MLA_Q_LORA, MLA_KV_LORA, HEAD_LANES, RET_QK_WIDE, RET_QK_WIDE, RET_V_WIDE, RET_V_WIDE,
              D_MODEL, D_MODEL)
_SEG_STARTS = tuple(int(v) for v in np.cumsum((0,) + _SEG_SIZES))


def _rope_tables(n_rows, dim, block_offset, repeat):
    half = dim // 2
    pos = jnp.arange(n_rows, dtype=jnp.float32) - PAD_ROWS
    inv = ROPE_BASE ** (-jnp.arange(half, dtype=jnp.float32) / half)
    lane = np.arange(LANES)
    rel = (lane - block_offset) % repeat
    in_rope = (lane >= block_offset) & (rel < dim)
    ang = pos[:, None] * inv[rel % half][None, :]
    cos = jnp.where(in_rope[None, :], jnp.cos(ang), 1.0)
    sin = jnp.sin(ang)
    s1 = jnp.where((in_rope & (rel < half))[None, :], -sin, 0.0)
    s2 = jnp.where((in_rope & (rel >= half))[None, :], sin, 0.0)
    return cos, s1, s2


def _apply_rope(x, cos, s1, s2, half):
    return x * cos + pltpu.roll(x, LANES - half, 1) * s1 + pltpu.roll(x, half, 1) * s2


def _mixer_in_kernel(h_ref, head_ref, gmix_ref, win_ref, gq_ref, wuq_ref, gkv_ref, wk_ref, wv_ref, gqkq_ref, gqkk_ref,
                     mc_ref, ms1_ref, ms2_ref, rc_ref, rs1_ref, rs2_ref,
                     q_ref, k_ref, v_ref, rq_ref, rk_ref, rv_ref, sg_ref, ga_ref, gr_ref):
    x = jnp.where(pl.program_id(1) == 0, head_ref[...], h_ref[...])
    hn = (x * lax.rsqrt(jnp.mean(x * x, axis=-1, keepdims=True) + EPS) * gmix_ref[...]).astype(jnp.bfloat16)

    def seg(i):
        return jnp.dot(hn, win_ref[:, _SEG_STARTS[i]:_SEG_STARTS[i + 1]], preferred_element_type=jnp.float32)

    def lora_norm(c, g_ref):
        return (c * lax.rsqrt(jnp.mean(c * c, axis=-1, keepdims=True) + EPS) * g_ref[...]).astype(jnp.bfloat16)

    cq = lora_norm(seg(0), gq_ref)
    ckv = lora_norm(seg(1), gkv_ref)
    kpe = seg(2)
    q = jnp.dot(cq, wuq_ref[...], preferred_element_type=jnp.float32)
    kn = jnp.dot(ckv, wk_ref[...], preferred_element_type=jnp.float32)
    v_ref[...] = jnp.dot(ckv, wv_ref[...], preferred_element_type=jnp.float32).astype(jnp.bfloat16)
    mc, ms1, ms2 = mc_ref[...], ms1_ref[...], ms2_ref[...]
    scale = math.log2(math.e) / math.sqrt(MLA_QK)

    def qk_norm_rope(t, g_ref):
        t = t * lax.rsqrt(jnp.sum(t * t, axis=-1, keepdims=True) * (1.0 / MLA_QK) + EPS) * g_ref[...]
        return _apply_rope(t, mc, ms1, ms2, MLA_ROPE // 2)

    for hd in range(MLA_HEADS):
        blk = slice(hd * HEAD_LANES, (hd + 1) * HEAD_LANES)
        q_ref[:, blk] = (qk_norm_rope(q[:, blk], gqkq_ref) * scale).astype(jnp.bfloat16)
        k_ref[:, blk] = qk_norm_rope(kn[:, blk] + kpe, gqkk_ref).astype(jnp.bfloat16)

    rc, rs1, rs2 = rc_ref[...], rs1_ref[...], rs2_ref[...]
    rq = seg(3)
    rk = seg(4)
    for j in range(RET_QK_WIDE // LANES):
        blk = slice(j * LANES, (j + 1) * LANES)
        rq_ref[:, blk] = _apply_rope(rq[:, blk], rc, rs1, rs2, RET_DK // 2).astype(jnp.bfloat16)
        rk_ref[:, blk] = (_apply_rope(rk[:, blk], rc, rs1, rs2, RET_DK // 2) * (RET_DK ** -0.5)).astype(jnp.bfloat16)
    rv_ref[...] = seg(5).astype(jnp.bfloat16)
    gate = seg(6)
    sg_ref[...] = (gate * jax.nn.sigmoid(gate)).astype(jnp.bfloat16)
    ga_ref[...] = jax.nn.sigmoid(seg(7)).astype(jnp.bfloat16)
    gr_ref[...] = jax.nn.sigmoid(seg(8)).astype(jnp.bfloat16)


def _attention_kernel(q_ref, k_ref, v_ref, o_ref, m_ref, l_ref, acc_ref):
    qi = pl.program_id(2)
    t = ATTN_TILE
    n_heads = q_ref.shape[1] // HEAD_LANES
    n_lane_tiles = t // LANES
    m_ref[...] = jnp.full(m_ref.shape, MASK_VALUE, jnp.float32)
    l_ref[...] = jnp.zeros(l_ref.shape, jnp.float32)
    acc_ref[...] = jnp.zeros(acc_ref.shape, jnp.float32)

    def lane_tiles(x):
        return [x[:, i * LANES:(i + 1) * LANES] for i in range(n_lane_tiles)]

    def scores(j, hd, masked):
        blk = slice(hd * HEAD_LANES, (hd + 1) * HEAD_LANES)
        start = pl.multiple_of(j * t, t)
        s = lax.dot_general(q_ref[:, blk], k_ref[pl.ds(start, t), blk], NT_DIMS,
                            preferred_element_type=jnp.float32)
        if masked:
            r = lax.broadcasted_iota(jnp.int32, (t, t), 0) + qi * t
            c = lax.broadcasted_iota(jnp.int32, (t, t), 1) + j * t
            s = jnp.where((c <= r) & ((c >= PAD_ROWS) | (r < PAD_ROWS)), s, MASK_VALUE)
        return s

    def max_step(j, masked):
        for hd in range(n_heads):
            m = m_ref[hd]
            for part in lane_tiles(scores(j, hd, masked)):
                m = jnp.maximum(m, part)
            m_ref[hd] = m

    def sum_step(j, masked):
        start = pl.multiple_of(j * t, t)
        for hd in range(n_heads):
            blk = slice(hd * HEAD_LANES, (hd + 1) * HEAD_LANES)
            p = jnp.exp2(scores(j, hd, masked) - jnp.tile(m_ref[hd], (1, n_lane_tiles)))
            l = l_ref[hd]
            for part in lane_tiles(p):
                l = l + part
            l_ref[hd] = l
            acc_ref[hd] += jnp.dot(p.astype(jnp.bfloat16), v_ref[pl.ds(start, t), blk],
                                   preferred_element_type=jnp.float32)

    def sweep(step):
        def body(j, carry):
            edge = (j == 0) | (j == qi)

            @pl.when(edge)
            def _():
                step(j, True)

            @pl.when(jnp.logical_not(edge))
            def _():
                step(j, False)

            return carry

        lax.fori_loop(0, qi + 1, body, 0)

    sweep(max_step)
    for hd in range(n_heads):
        m_ref[hd] = jnp.broadcast_to(jnp.max(m_ref[hd], axis=-1, keepdims=True), (t, LANES))
    sweep(sum_step)
    for hd in range(n_heads):
        blk = slice(hd * HEAD_LANES, (hd + 1) * HEAD_LANES)
        o_ref[:, blk] = (acc_ref[hd] / jnp.sum(l_ref[hd], axis=-1, keepdims=True)).astype(jnp.bfloat16)


def _retention_kernel(q_ref, k_ref, v_ref, sg_ref, gn_ref, decay_ref, zeta_ref, xi_ref, gamma_ref, o_ref):
    n_chunks = q_ref.shape[0] // CHUNK
    n_pairs = q_ref.shape[1] // LANES
    lane = lax.broadcasted_iota(jnp.int32, (CHUNK, LANES), 1)
    row = lax.broadcasted_iota(jnp.int32, (LANES, RET_DV), 0)
    is_a = lane < RET_DK
    gn = gn_ref[...]
    contract_rows = (((0,), (0,)), ((), ()))

    def group_norm(y):
        mu = jnp.mean(y, axis=-1, keepdims=True)
        d = y - mu
        return d * lax.rsqrt(jnp.mean(d * d, axis=-1, keepdims=True) + EPS)

    def pair_chunk(rows, p, state):
        qb = q_ref[rows, p * LANES:(p + 1) * LANES]
        kb = k_ref[rows, p * LANES:(p + 1) * LANES]
        state_bf = state.astype(jnp.bfloat16)
        new_parts = []
        for hd in range(2):
            head = 2 * p + hd
            cols = slice(head * RET_DV, (head + 1) * RET_DV)
            qh = jnp.where(is_a if hd == 0 else ~is_a, qb, jnp.zeros_like(qb))
            vh = v_ref[rows, cols]
            scores = lax.dot_general(qh, kb, NT_DIMS, preferred_element_type=jnp.float32) * decay_ref[head]
            inner = jnp.dot(scores.astype(jnp.bfloat16), vh, preferred_element_type=jnp.float32)
            cross = jnp.dot(qh, state_bf, preferred_element_type=jnp.float32) * xi_ref[head]
            yn = group_norm(inner + cross) * gn[:, cols]
            o_ref[rows, cols] = (sg_ref[rows, cols].astype(jnp.float32) * yn).astype(jnp.bfloat16)
            kz = (kb.astype(jnp.float32) * zeta_ref[head]).astype(jnp.bfloat16)
            new_parts.append(lax.dot_general(kz, vh, contract_rows, preferred_element_type=jnp.float32))
        return gamma_ref[p] * state + jnp.where(row < RET_DK, new_parts[0], new_parts[1])

    def chunk(n, states):
        rows = pl.ds(pl.multiple_of(n * CHUNK, CHUNK), CHUNK)
        return tuple(pair_chunk(rows, p, state) for p, state in enumerate(states))

    lax.fori_loop(0, n_chunks, chunk, tuple(jnp.zeros((LANES, RET_DV), jnp.float32) for _ in range(n_pairs)))


def _mixer_out_kernel(h_ref, o_ref, yr_ref, ga_ref, gr_ref, wa_ref, wr_ref, wm_ref, out_ref):
    y_a = jnp.dot(o_ref[...], wa_ref[...], preferred_element_type=jnp.float32)
    y_r = jnp.dot(yr_ref[...], wr_ref[...], preferred_element_type=jnp.float32)
    merged = ga_ref[...].astype(jnp.float32) * y_a + gr_ref[...].astype(jnp.float32) * y_r
    out_ref[...] = h_ref[...] + jnp.dot(merged.astype(jnp.bfloat16), wm_ref[...],
                                        preferred_element_type=jnp.float32)


def _pad_heads(w, heads, width):
    lead = w.shape[:-1]
    w = w.reshape(lead + (heads, width))
    w = jnp.pad(w, [(0, 0)] * len(lead) + [(0, 0), (0, HEAD_LANES - width)])
    return w.reshape(lead + (heads * HEAD_LANES,))


def _mixer(x, meta_tokens, g_mix, w_in, g_q_lora, w_uq, g_kv_lora, w_ukv, g_qk_q, g_qk_k,
           w_mla_out, g_ret_gn, w_ret_out, w_mix_out):
    B, S, D = x.shape
    bf = jnp.bfloat16
    rows = PAD_ROWS + N_META + S
    nt = rows // CHUNK
    seq_tiles = S // CHUNK
    assert PAD_ROWS + N_META == CHUNK and S % CHUNK == 0
    x2 = x.reshape(B * S, D)
    head_tile = jnp.concatenate([jnp.zeros((PAD_ROWS, D), x.dtype), meta_tokens.astype(x.dtype)], axis=0)

    segs = jnp.split(w_in, np.cumsum(IN_SIZES)[:-1].tolist(), axis=1)
    kpe_cols = jnp.pad(segs[2], ((0, 0), (MLA_NOPE, HEAD_LANES - MLA_NOPE - MLA_ROPE)))
    win = jnp.concatenate(segs[:2] + [kpe_cols] + segs[3:], axis=1).astype(bf)
    wuq = _pad_heads(w_uq, MLA_HEADS, MLA_QK).astype(bf)
    wkv = w_ukv.reshape(MLA_KV_LORA, MLA_HEADS, MLA_NOPE + MLA_V)
    wk = _pad_heads(wkv[:, :, :MLA_NOPE].reshape(MLA_KV_LORA, -1), MLA_HEADS, MLA_NOPE).astype(bf)
    wv = _pad_heads(wkv[:, :, MLA_NOPE:].reshape(MLA_KV_LORA, -1), MLA_HEADS, MLA_V).astype(bf)
    gqkq = jnp.pad(g_qk_q, (0, HEAD_LANES - MLA_QK)).reshape(1, HEAD_LANES)
    gqkk = jnp.pad(g_qk_k, (0, HEAD_LANES - MLA_QK)).reshape(1, HEAD_LANES)
    wa = _pad_heads(w_mla_out.T, MLA_HEADS, MLA_V).T.astype(bf)
    mla_tabs = _rope_tables(rows, MLA_ROPE, MLA_NOPE, HEAD_LANES)
    ret_tabs = _rope_tables(rows, RET_DK, 0, RET_DK)

    row_tile = lambda width: pl.BlockSpec((CHUNK, width), lambda b, i: (b * nt + i, 0))
    const = lambda shape: pl.BlockSpec(shape, lambda b, i: (0,) * len(shape))
    tab = pl.BlockSpec((CHUNK, LANES), lambda b, i: (i, 0))
    n_rows = B * rows
    widths = (MLA_WIDE, MLA_WIDE, MLA_WIDE, RET_QK_WIDE, RET_QK_WIDE, RET_V_WIDE, RET_V_WIDE, D, D)
    q, k, v, rq, rk, rv, sg, ga, gr = pl.pallas_call(
        _mixer_in_kernel,
        out_shape=tuple(jax.ShapeDtypeStruct((n_rows, w), bf) for w in widths),
        grid=(B, nt),
        in_specs=[pl.BlockSpec((CHUNK, D), lambda b, i: (b * seq_tiles + jnp.maximum(i - 1, 0), 0)),
                  const((CHUNK, D)), const((1, D)), const(win.shape), const((1, MLA_Q_LORA)), const(wuq.shape),
                  const((1, MLA_KV_LORA)), const(wk.shape), const(wv.shape), const((1, HEAD_LANES)),
                  const((1, HEAD_LANES))] + [tab] * 6,
        out_specs=tuple(row_tile(w) for w in widths),
        compiler_params=pltpu.CompilerParams(dimension_semantics=("arbitrary", "arbitrary"),
                                             vmem_limit_bytes=MIXER_VMEM_LIMIT),
        name="mixer_in",
    )(x2, head_tile, g_mix.reshape(1, D), win, g_q_lora.reshape(1, -1), wuq, g_kv_lora.reshape(1, -1), wk, wv,
      gqkq, gqkk, *mla_tabs, *ret_tabs)

    group = ATTN_HEAD_GROUP * HEAD_LANES
    seq_spec = pl.BlockSpec((rows, group), lambda b, g, i: (b, g))
    n_attn = rows // ATTN_TILE
    tile_spec = pl.BlockSpec((ATTN_TILE, group), lambda b, g, i: (b * n_attn + i, g))
    o = pl.pallas_call(
        _attention_kernel,
        out_shape=jax.ShapeDtypeStruct((n_rows, MLA_WIDE), bf),
        grid=(B, MLA_HEADS // ATTN_HEAD_GROUP, n_attn),
        in_specs=[tile_spec, seq_spec, seq_spec],
        out_specs=tile_spec,
        scratch_shapes=[pltpu.VMEM((ATTN_HEAD_GROUP, ATTN_TILE, HEAD_LANES), jnp.float32)] * 3,
        compiler_params=pltpu.CompilerParams(dimension_semantics=("arbitrary",) * 3,
                                             vmem_limit_bytes=MIXER_VMEM_LIMIT),
        name="mla_attention",
    )(q, k, v)

    log_gamma = jnp.log(1.0 - 2.0 ** (-5.0 - jnp.arange(RET_HEADS, dtype=jnp.float32)))
    idx = jnp.arange(CHUNK, dtype=jnp.float32)
    diff = idx[:, None] - idx[None, :]
    decay = jnp.where(diff[None] >= 0, jnp.exp(jnp.maximum(diff, 0.0)[None] * log_gamma[:, None, None]), 0.0)
    zeta = jnp.exp((CHUNK - 1 - idx)[None, :] * log_gamma[:, None])
    xi = jnp.exp((idx + 1)[None, :] * log_gamma[:, None])
    gamma_c = jnp.exp(CHUNK * log_gamma)
    zeta_b = jnp.broadcast_to(zeta[:, :, None], (RET_HEADS, CHUNK, LANES))
    xi_b = jnp.broadcast_to(xi[:, :, None], (RET_HEADS, CHUNK, RET_DV))
    gamma_rows = jnp.broadcast_to(jnp.repeat(gamma_c, RET_DK).reshape(RET_HEADS // 2, LANES, 1),
                                  (RET_HEADS // 2, LANES, RET_DV))
    g = RET_PAIR_GROUP
    pair_seq = lambda width: pl.BlockSpec((rows, g * width), lambda b, p: (b, p))
    pair_tab = lambda d1, d2: pl.BlockSpec((2 * g, d1, d2), lambda b, p: (p, 0, 0))
    yr = pl.pallas_call(
        _retention_kernel,
        out_shape=jax.ShapeDtypeStruct((n_rows, RET_V_WIDE), bf),
        grid=(B, RET_HEADS // (2 * g)),
        in_specs=[pair_seq(LANES), pair_seq(LANES), pair_seq(2 * RET_DV), pair_seq(2 * RET_DV),
                  pl.BlockSpec((1, g * 2 * RET_DV), lambda b, p: (0, p)),
                  pair_tab(CHUNK, CHUNK), pair_tab(CHUNK, LANES), pair_tab(CHUNK, RET_DV),
                  pl.BlockSpec((g, LANES, RET_DV), lambda b, p: (p, 0, 0))],
        out_specs=pair_seq(2 * RET_DV),
        compiler_params=pltpu.CompilerParams(dimension_semantics=("arbitrary", "arbitrary"),
                                             vmem_limit_bytes=MIXER_VMEM_LIMIT),
        name="retention",
    )(rq, rk, rv, sg, g_ret_gn.reshape(1, -1), decay, zeta_b, xi_b, gamma_rows)

    in_tile = lambda width: pl.BlockSpec((CHUNK, width), lambda b, i: (b * nt + 1 + i, 0))
    seq_tile = pl.BlockSpec((CHUNK, D), lambda b, i: (b * seq_tiles + i, 0))
    return pl.pallas_call(
        _mixer_out_kernel,
        out_shape=jax.ShapeDtypeStruct((B * S, D), jnp.float32),
        grid=(B, seq_tiles),
        in_specs=[seq_tile, in_tile(MLA_WIDE), in_tile(RET_V_WIDE), in_tile(D), in_tile(D),
                  const(wa.shape), const(w_ret_out.shape), const(w_mix_out.shape)],
        out_specs=seq_tile,
        compiler_params=pltpu.CompilerParams(dimension_semantics=("arbitrary", "arbitrary"),
                                             vmem_limit_bytes=MIXER_VMEM_LIMIT),
        name="mixer_out",
    )(x2, o, yr, ga, gr, wa, w_ret_out.astype(bf), w_mix_out.astype(bf))


def _top_rows(s, count):
    n = s.shape[0]
    iota = lax.broadcasted_iota(jnp.int32, s.shape, 0).astype(jnp.float32)
    vals, idxs = [], []
    for _ in range(count):
        m = jnp.max(s, axis=0, keepdims=True)
        idx = jnp.min(jnp.where(s == m, iota, float(n)), axis=0, keepdims=True)
        vals.append(m)
        idxs.append(idx)
        s = jnp.where(iota == idx, -jnp.inf, s)
    return vals, idxs


def _route_head(q, k1, k2):
    s1 = lax.dot_general(k1, q, NT_DIMS, preferred_element_type=jnp.float32)
    s2 = lax.dot_general(k2, q, NT_DIMS, preferred_element_type=jnp.float32)
    v1, i1 = _top_rows(s1, PEER_TOPK)
    v2, i2 = _top_rows(s2, PEER_TOPK)
    v2_all = jnp.concatenate(v2, axis=0)
    i2_all = jnp.concatenate(i2, axis=0)
    n_mid = SUBLANES
    cand = [v1[0] + v2_all] + [v1[a] + v2_all[:n_mid] for a in range(1, n_mid)]
    cand.append(jnp.concatenate(v1[n_mid:], axis=0) + v2[0])
    cand_rows = [i1[0] * PEER_NKEYS + i2_all] + [i1[a] * PEER_NKEYS + i2_all[:n_mid] for a in range(1, n_mid)]
    cand_rows.append(jnp.concatenate(i1[n_mid:], axis=0) * PEER_NKEYS + i2[0])
    cand = jnp.concatenate(cand, axis=0)
    cand_rows = jnp.concatenate(cand_rows, axis=0) * float(WORD_ROWS)
    sc, ci = _top_rows(cand, PEER_TOPK)
    iota = lax.broadcasted_iota(jnp.int32, cand.shape, 0).astype(jnp.float32)
    rows = [jnp.max(jnp.where(iota == c, cand_rows, -1.0), axis=0, keepdims=True) for c in ci]
    ex = [jnp.exp(v - sc[0]) for v in sc]
    denom = ex[0]
    for e in ex[1:]:
        denom = denom + e
    return jnp.concatenate(rows, axis=0).astype(jnp.int32), jnp.concatenate(ex, axis=0) / denom


def _peer_route_kernel(h_ref, g_ref, wq_ref, k1_ref, k2_ref, xn_ref, rows_ref, gates_ref,
                       xb_ref, rows_scr, gates_scr):
    grp = pl.program_id(1)

    @pl.when(grp == 0)
    def _():
        x = h_ref[...]
        xn = x * lax.rsqrt(jnp.mean(x * x, axis=-1, keepdims=True) + EPS) * g_ref[...]
        for r in range(SUBLANES):
            xn_ref[pl.ds(r, x.shape[0], stride=SUBLANES), :] = xn[:, r * LANES:(r + 1) * LANES]
        xb_ref[...] = xn.astype(jnp.bfloat16)

    q = jnp.dot(xb_ref[...], wq_ref[...], preferred_element_type=jnp.float32).astype(jnp.bfloat16)
    k1, k2 = k1_ref[...], k2_ref[...]
    for j in range(ROUTE_HEAD_GROUP):
        rows, gates = _route_head(q[:, j * PEER_DQ:(j + 1) * PEER_DQ], k1, k2)
        base = pl.multiple_of((grp * ROUTE_HEAD_GROUP + j) * PEER_TOPK, PEER_TOPK)
        rows_scr[pl.ds(base, PEER_TOPK), :] = rows
        gates_scr[pl.ds(base, PEER_TOPK), :] = gates

    @pl.when(grp == PEER_HEADS // ROUTE_HEAD_GROUP - 1)
    def _():
        for j in range(rows_ref.shape[0]):
            rows_ref[j] = rows_scr[:, j * PEER_TOKEN_BLOCK:(j + 1) * PEER_TOKEN_BLOCK]
        gates_ref[...] = gates_scr[...].T


def _peer_route(h, g_ffn, w_peer_q, keys_1, keys_2):
    T = h.shape[0]
    tb = ROUTE_TOKEN_BLOCK
    per_step = tb // PEER_TOKEN_BLOCK
    half = PEER_DQ // 2
    zeros = jnp.zeros((PEER_NKEYS, half), jnp.bfloat16)
    k1 = jnp.concatenate([keys_1.astype(jnp.bfloat16), zeros], axis=1)
    k2 = jnp.concatenate([zeros, keys_2.astype(jnp.bfloat16)], axis=1)
    keys_spec = pl.BlockSpec((PEER_NKEYS, PEER_DQ), lambda i, g: (0, 0))
    pair_spec = pl.BlockSpec((tb, PEER_PAIRS), lambda i, g: (i, 0))
    return pl.pallas_call(
        _peer_route_kernel,
        out_shape=(jax.ShapeDtypeStruct((T * SUBLANES, LANES), jnp.float32),
                   jax.ShapeDtypeStruct((T // PEER_TOKEN_BLOCK, PEER_PAIRS, PEER_TOKEN_BLOCK), jnp.int32),
                   jax.ShapeDtypeStruct((T, PEER_PAIRS), jnp.float32)),
        grid=(T // tb, PEER_HEADS // ROUTE_HEAD_GROUP),
        in_specs=[pl.BlockSpec((tb, D_MODEL), lambda i, g: (i, 0)),
                  pl.BlockSpec((1, D_MODEL), lambda i, g: (0, 0)),
                  pl.BlockSpec((D_MODEL, ROUTE_HEAD_GROUP * PEER_DQ), lambda i, g: (0, g)),
                  keys_spec, keys_spec],
        out_specs=(pl.BlockSpec((tb * SUBLANES, LANES), lambda i, g: (i, 0)),
                   pl.BlockSpec((per_step, PEER_PAIRS, PEER_TOKEN_BLOCK), lambda i, g: (i, 0, 0)),
                   pair_spec),
        scratch_shapes=[pltpu.VMEM((tb, D_MODEL), jnp.bfloat16),
                        pltpu.VMEM((PEER_PAIRS, tb), jnp.int32),
                        pltpu.VMEM((PEER_PAIRS, tb), jnp.float32)],
        compiler_params=pltpu.CompilerParams(dimension_semantics=("arbitrary", "arbitrary")),
        name="peer_route",
    )(h, g_ffn.reshape(1, D_MODEL), w_peer_q.astype(jnp.bfloat16), k1, k2)


def _pack_table_kernel(tab_ref, out_ref):
    n = tab_ref.shape[0]
    bits = lax.bitcast_convert_type(tab_ref[...].astype(jnp.bfloat16).astype(jnp.float32), jnp.int32)
    for j in range(WORD_ROWS):
        lo = lax.shift_right_logical(bits[:, 2 * j * LANES:(2 * j + 1) * LANES], 16)
        hi = bits[:, (2 * j + 1) * LANES:(2 * j + 2) * LANES] & HIGH_HALF_MASK
        out_ref[pl.ds(j, n, stride=WORD_ROWS), :] = hi | lo


def _pack_expert_table(tab):
    n = PACK_EXPERT_BLOCK
    return pl.pallas_call(
        _pack_table_kernel,
        out_shape=jax.ShapeDtypeStruct((tab.shape[0] * WORD_ROWS, LANES), jnp.int32),
        grid=(tab.shape[0] // n,),
        in_specs=[pl.BlockSpec((n, D_MODEL), lambda i: (i, 0))],
        out_specs=pl.BlockSpec((n * WORD_ROWS, LANES), lambda i: (i, 0)),
        compiler_params=pltpu.CompilerParams(dimension_semantics=("arbitrary",)),
        name="pack_table",
    )(tab)


def _gather_rows(idx_ref, tab_ref, dst_refs, t0):
    for k in range(PEER_PAIRS):
        idx_k = idx_ref.at[k]
        for s, dst_ref in enumerate(dst_refs):
            row = pl.multiple_of(idx_k[t0 + s], WORD_ROWS)
            dst_ref[k * WORD_ROWS:(k + 1) * WORD_ROWS, :] = tab_ref[pl.ds(row, WORD_ROWS), :]


def _expert_token_loop(idx_hbm, idx_bufs, sem, tab_ref, rows_refs, n_tokens, consume):
    i = pl.program_id(0)
    n = pl.num_programs(0)
    n_slots = len(rows_refs)
    assert n_tokens % n_slots == 0

    def copy(block, slot):
        return pltpu.make_async_copy(idx_hbm.at[block], idx_bufs[slot], sem.at[slot])

    def consume_group(t0):
        for s in range(n_slots):
            consume(t0 + s, s)

    @pl.when(i == 0)
    def _():
        copy(0, 0).start()
        copy(0, 0).wait()
        _gather_rows(idx_bufs[0], tab_ref, rows_refs, 0)

    def run(slot):
        has_next = i + 1 < n

        @pl.when(has_next)
        def _():
            copy(i + 1, 1 - slot).start()

        def step(j, carry):
            consume_group(n_slots * j)
            _gather_rows(idx_bufs[slot], tab_ref, rows_refs, n_slots * (j + 1))
            return carry

        lax.fori_loop(0, n_tokens // n_slots - 1, step, 0)

        @pl.when(has_next)
        def _():
            copy(i + 1, 1 - slot).wait()
            consume_group(n_tokens - n_slots)
            _gather_rows(idx_bufs[1 - slot], tab_ref, rows_refs, 0)

        @pl.when(jnp.logical_not(has_next))
        def _():
            consume_group(n_tokens - n_slots)

    for slot in range(2):
        pl.when(lax.rem(i, 2) == slot)(lambda slot=slot: run(slot))


def _diag_mask():
    lane = lax.broadcasted_iota(jnp.int32, (SUBLANES, GATHER_ROWS), 1)
    sub = lax.broadcasted_iota(jnp.int32, (SUBLANES, GATHER_ROWS), 0)
    return (lane & (SUBLANES - 1)) == sub


def _split_bf16(x):
    hi = x.astype(jnp.bfloat16)
    lo = (x - hi.astype(jnp.float32)).astype(jnp.bfloat16)
    return hi, lo


def _peer_hidden_kernel(idx_hbm, x_ref, gate_ref, tab_ref, sel_ref, w_ref, diag_ref, idx_a, idx_b, sem,
                        *rows_refs):
    mask = _diag_mask()

    def consume(t, slot):
        rows = pltpu.bitcast(rows_refs[slot][...], jnp.bfloat16)
        xb = x_ref[t].astype(jnp.bfloat16)
        d = lax.dot_general(xb, rows, NT_DIMS, preferred_element_type=jnp.float32)
        diag_ref[pl.ds(t, 1), :] = jnp.sum(jnp.where(mask, d, 0.0), axis=0, keepdims=True)

    _expert_token_loop(idx_hbm, (idx_a, idx_b), sem, tab_ref, rows_refs, x_ref.shape[0], consume)
    hi, lo = _split_bf16(diag_ref[...])
    sel = sel_ref[...]
    s = (jnp.dot(hi, sel, preferred_element_type=jnp.float32)
         + jnp.dot(lo, sel, preferred_element_type=jnp.float32))
    act = 0.5 * s * (1.0 + lax.erf(s * (1.0 / math.sqrt(2.0))))
    w_ref[...] = gate_ref[...] * act


def _peer_output_kernel(idx_hbm, w_ref, h_ref, tab_ref, expand_ref, out_ref, whi_ref, wlo_ref, idx_a, idx_b, sem,
                        *rows_refs):
    mask = _diag_mask()
    hi, lo = _split_bf16(w_ref[...])
    expand = expand_ref[...]
    whi_ref[...] = jnp.dot(hi, expand, preferred_element_type=jnp.float32)
    wlo_ref[...] = jnp.dot(lo, expand, preferred_element_type=jnp.float32)

    def consume(t, slot):
        rows = pltpu.bitcast(rows_refs[slot][...], jnp.bfloat16)
        parts = []
        for w_part in (whi_ref, wlo_ref):
            w_row = jnp.broadcast_to(w_part[pl.ds(t, 1), :], (SUBLANES, GATHER_ROWS))
            parts.append(jnp.where(mask, w_row, 0.0))
        lhs = jnp.concatenate(parts, axis=0).astype(jnp.bfloat16)
        y = jnp.dot(lhs, rows, preferred_element_type=jnp.float32)
        y = y[:SUBLANES] + y[SUBLANES:]
        y_row = jnp.concatenate([y[r:r + 1, :] for r in range(SUBLANES)], axis=1)
        out_ref[pl.ds(t, 1), :] = h_ref[pl.ds(t, 1), :] + y_row

    _expert_token_loop(idx_hbm, (idx_a, idx_b), sem, tab_ref, rows_refs, h_ref.shape[0], consume)


def _peer_experts(h, xn, rows, gates, peer_u, peer_v):
    T = h.shape[0]
    tb = PEER_TOKEN_BLOCK
    u_tab = _pack_expert_table(peer_u)
    v_tab = _pack_expert_table(peer_v)
    expand = (jnp.arange(PEER_PAIRS)[:, None] == jnp.arange(GATHER_ROWS)[None, :] // SUBLANES
              ).astype(jnp.bfloat16)
    params = pltpu.CompilerParams(dimension_semantics=("arbitrary",), vmem_limit_bytes=PEER_VMEM_LIMIT)
    hbm_ref = pl.BlockSpec(memory_space=pl.ANY)
    index_scratch = [pltpu.SMEM((PEER_PAIRS, tb), jnp.int32)] * 2 + [pltpu.SemaphoreType.DMA((2,))]
    pair_block = pl.BlockSpec((tb, PEER_PAIRS), lambda i: (i, 0))
    vmem_rows = pl.BlockSpec((tb, D_MODEL), lambda i: (i, 0))
    whole_vmem = pl.BlockSpec(memory_space=pltpu.VMEM)
    rows_scratch = pltpu.VMEM((GATHER_WORD_ROWS, LANES), jnp.int32)
    wide_scratch = pltpu.VMEM((tb, GATHER_ROWS), jnp.float32)

    w = pl.pallas_call(
        _peer_hidden_kernel,
        out_shape=jax.ShapeDtypeStruct((T, PEER_PAIRS), jnp.float32),
        grid=(T // tb,),
        in_specs=[hbm_ref, pl.BlockSpec((tb, SUBLANES, LANES), lambda i: (i, 0, 0)), pair_block,
                  whole_vmem, whole_vmem],
        out_specs=pair_block,
        scratch_shapes=[wide_scratch] + index_scratch + [rows_scratch] * HIDDEN_SLOTS,
        compiler_params=params,
        name="peer_hidden",
    )(rows, xn.reshape(T, SUBLANES, LANES), gates, u_tab, expand.T)

    return pl.pallas_call(
        _peer_output_kernel,
        out_shape=jax.ShapeDtypeStruct((T, D_MODEL), jnp.float32),
        grid=(T // tb,),
        in_specs=[hbm_ref, pair_block, vmem_rows, whole_vmem, whole_vmem],
        out_specs=vmem_rows,
        scratch_shapes=[wide_scratch, wide_scratch] + index_scratch + [rows_scratch] * OUTPUT_SLOTS,
        compiler_params=params,
        name="peer_output",
    )(rows, w, h, v_tab, expand)


def kernel(x, meta_tokens, g_mix, w_in, g_q_lora, w_uq, g_kv_lora, w_ukv, g_qk_q, g_qk_k, w_mla_out, g_ret_gn, w_ret_out, w_mix_out, g_ffn, w_peer_q, peer_keys_1, peer_keys_2, peer_u, peer_v):
    B, S, D = x.shape
    h = _mixer(x, meta_tokens, g_mix[0], w_in[0], g_q_lora[0], w_uq[0], g_kv_lora[0], w_ukv[0],
               g_qk_q[0], g_qk_k[0], w_mla_out[0], g_ret_gn[0], w_ret_out[0], w_mix_out[0])
    xn, rows, gates = _peer_route(h, g_ffn[0], w_peer_q[0], peer_keys_1[0], peer_keys_2[0])
    out = _peer_experts(h, xn, rows, gates, peer_u[0], peer_v[0])
    return out.reshape(B, S, D)
```

```python
import math

import jax
import jax.numpy as jnp
import numpy as np
from jax import lax
from jax.experimental import pallas as pl
from jax.experimental.pallas import tpu as pltpu

D_MODEL = 1024
N_META = 16
CHUNK = 128
ROPE_BASE = 10000.0
EPS = 1e-6
MLA_HEADS = 8
MLA_Q_LORA = 384
MLA_KV_LORA = 256
MLA_NOPE = 64
MLA_ROPE = 32
MLA_V = 64
MLA_QK = MLA_NOPE + MLA_ROPE
RET_HEADS = 8
RET_DK = 64
RET_DV = 128
PEER_HEADS = 8
PEER_NKEYS = 128
PEER_EXPERTS = PEER_NKEYS * PEER_NKEYS
PEER_DQ = 128
PEER_TOPK = 16
PEER_PAIRS = PEER_HEADS * PEER_TOPK
IN_SIZES = (MLA_Q_LORA, MLA_KV_LORA, MLA_ROPE, RET_HEADS * RET_DK, RET_HEADS * RET_DK,
            RET_HEADS * RET_DV, RET_HEADS * RET_DV, D_MODEL, D_MODEL)

LANES = 128
SUBLANES = 8
WORD_ROWS = D_MODEL // (2 * LANES)
GATHER_WORD_ROWS = PEER_PAIRS * WORD_ROWS
GATHER_ROWS = 2 * GATHER_WORD_ROWS
PEER_TOKEN_BLOCK = LANES
ROUTE_TOKEN_BLOCK = 128
ROUTE_HEAD_GROUP = 8
HIDDEN_SLOTS = 32
OUTPUT_SLOTS = 32
PEER_VMEM_LIMIT = 56 * 1024 * 1024
NT_DIMS = (((1,), (1,)), ((), ()))
HIGH_HALF_MASK = -65536
PACK_EXPERT_BLOCK = 256


PAD_ROWS = (-N_META) % CHUNK
HEAD_LANES = 128
MLA_WIDE = MLA_HEADS * HEAD_LANES
RET_QK_WIDE = RET_HEADS * RET_DK
RET_V_WIDE = RET_HEADS * RET_DV
MASK_VALUE = -1e30
MIXER_VMEM_LIMIT = 48 * 1024 * 1024
ATTN_HEAD_GROUP = 8
ATTN_TILE = 3 * CHUNK
MIXER_IN_TILES = 3
RET_PAIR_GROUP = 2
_SEG_SIZES = (MLA_Q_LORA, MLA_KV_LORA, HEAD_LANES, RET_QK_WIDE, RET_QK_WIDE, RET_V_WIDE, RET_V_WIDE,
              D_MODEL, D_MODEL)
_SEG_STARTS = tuple(int(v) for v in np.cumsum((0,) + _SEG_SIZES))


def _rope_tables(n_rows, dim, block_offset, repeat):
    half = dim // 2
    pos = jnp.arange(n_rows, dtype=jnp.float32) - PAD_ROWS
    inv = ROPE_BASE ** (-jnp.arange(half, dtype=jnp.float32) / half)
    lane = np.arange(LANES)
    rel = (lane - block_offset) % repeat
    in_rope = (lane >= block_offset) & (rel < dim)
    ang = pos[:, None] * inv[rel % half][None, :]
    cos = jnp.where(in_rope[None, :], jnp.cos(ang), 1.0)
    sin = jnp.sin(ang)
    s1 = jnp.where((in_rope & (rel < half))[None, :], -sin, 0.0)
    s2 = jnp.where((in_rope & (rel >= half))[None, :], sin, 0.0)
    return cos, s1, s2


def _apply_rope(x, cos, s1, s2, half):
    return x * cos + pltpu.roll(x, LANES - half, 1) * s1 + pltpu.roll(x, half, 1) * s2


def _mixer_in_kernel(h0_ref, h1_ref, h2_ref, head_ref, gmix_ref, win_ref, gq_ref, wuq_ref, gkv_ref, wk_ref, wv_ref, gqkq_ref, gqkk_ref,
                     mc_ref, ms1_ref, ms2_ref, rc_ref, rs1_ref, rs2_ref,
                     q_ref, k_ref, v_ref, rq_ref, rk_ref, rv_ref, sg_ref, ga_ref, gr_ref):
    first = jnp.where(pl.program_id(1) == 0, head_ref[...], h0_ref[...])
    x = jnp.concatenate([first, h1_ref[...], h2_ref[...]], axis=0)
    hn = (x * lax.rsqrt(jnp.mean(x * x, axis=-1, keepdims=True) + EPS) * gmix_ref[...]).astype(jnp.bfloat16)

    def seg(i):
        return jnp.dot(hn, win_ref[:, _SEG_STARTS[i]:_SEG_STARTS[i + 1]], preferred_element_type=jnp.float32)

    def lora_norm(c, g_ref):
        return (c * lax.rsqrt(jnp.mean(c * c, axis=-1, keepdims=True) + EPS) * g_ref[...]).astype(jnp.bfloat16)

    cq = lora_norm(seg(0), gq_ref)
    ckv = lora_norm(seg(1), gkv_ref)
    kpe = seg(2)
    q = jnp.dot(cq, wuq_ref[...], preferred_element_type=jnp.float32)
    kn = jnp.dot(ckv, wk_ref[...], preferred_element_type=jnp.float32)
    v_ref[...] = jnp.dot(ckv, wv_ref[...], preferred_element_type=jnp.float32).astype(jnp.bfloat16)
    mc, ms1, ms2 = mc_ref[...], ms1_ref[...], ms2_ref[...]
    scale = math.log2(math.e) / math.sqrt(MLA_QK)

    def qk_norm_rope(t, g_ref):
        t = t * lax.rsqrt(jnp.sum(t * t, axis=-1, keepdims=True) * (1.0 / MLA_QK) + EPS) * g_ref[...]
        return _apply_rope(t, mc, ms1, ms2, MLA_ROPE // 2)

    for hd in range(MLA_HEADS):
        blk = slice(hd * HEAD_LANES, (hd + 1) * HEAD_LANES)
        q_ref[:, blk] = (qk_norm_rope(q[:, blk], gqkq_ref) * scale).astype(jnp.bfloat16)
        k_ref[:, blk] = qk_norm_rope(kn[:, blk] + kpe, gqkk_ref).astype(jnp.bfloat16)

    rc, rs1, rs2 = rc_ref[...], rs1_ref[...], rs2_ref[...]
    rq = seg(3)
    rk = seg(4)
    for j in range(RET_QK_WIDE // LANES):
        blk = slice(j * LANES, (j + 1) * LANES)
        rq_ref[:, blk] = _apply_rope(rq[:, blk], rc, rs1, rs2, RET_DK // 2).astype(jnp.bfloat16)
        rk_ref[:, blk] = (_apply_rope(rk[:, blk], rc, rs1, rs2, RET_DK // 2) * (RET_DK ** -0.5)).astype(jnp.bfloat16)
    rv_ref[...] = seg(5).astype(jnp.bfloat16)
    gate = seg(6)
    sg_ref[...] = (gate * jax.nn.sigmoid(gate)).astype(jnp.bfloat16)
    ga_ref[...] = jax.nn.sigmoid(seg(7)).astype(jnp.bfloat16)
    gr_ref[...] = jax.nn.sigmoid(seg(8)).astype(jnp.bfloat16)


def _attention_kernel(q_ref, k_ref, v_ref, o_ref, m_ref, l_ref, acc_ref):
    qi = pl.program_id(2)
    t = ATTN_TILE
    n_heads = q_ref.shape[1] // HEAD_LANES
    n_lane_tiles = t // LANES
    m_ref[...] = jnp.full(m_ref.shape, MASK_VALUE, jnp.float32)
    l_ref[...] = jnp.zeros(l_ref.shape, jnp.float32)
    acc_ref[...] = jnp.zeros(acc_ref.shape, jnp.float32)

    def lane_tiles(x):
        return [x[:, i * LANES:(i + 1) * LANES] for i in range(n_lane_tiles)]

    def scores(j, hd, masked):
        blk = slice(hd * HEAD_LANES, (hd + 1) * HEAD_LANES)
        start = pl.multiple_of(j * t, t)
        s = lax.dot_general(q_ref[:, blk], k_ref[pl.ds(start, t), blk], NT_DIMS,
                            preferred_element_type=jnp.float32)
        if masked:
            r = lax.broadcasted_iota(jnp.int32, (t, t), 0) + qi * t
            c = lax.broadcasted_iota(jnp.int32, (t, t), 1) + j * t
            s = jnp.where((c <= r) & ((c >= PAD_ROWS) | (r < PAD_ROWS)), s, MASK_VALUE)
        return s

    def max_step(j, masked):
        for hd in range(n_heads):
            m = m_ref[hd]
            for part in lane_tiles(scores(j, hd, masked)):
                m = jnp.maximum(m, part)
            m_ref[hd] = m

    def sum_step(j, masked):
        start = pl.multiple_of(j * t, t)
        for hd in range(n_heads):
            blk = slice(hd * HEAD_LANES, (hd + 1) * HEAD_LANES)
            p = jnp.exp2(scores(j, hd, masked) - jnp.tile(m_ref[hd], (1, n_lane_tiles)))
            l = l_ref[hd]
            for part in lane_tiles(p):
                l = l + part
            l_ref[hd] = l
            acc_ref[hd] += jnp.dot(p.astype(jnp.bfloat16), v_ref[pl.ds(start, t), blk],
                                   preferred_element_type=jnp.float32)

    def sweep(step):
        def body(j, carry):
            edge = (j == 0) | (j == qi)

            @pl.when(edge)
            def _():
                step(j, True)

            @pl.when(jnp.logical_not(edge))
            def _():
                step(j, False)

            return carry

        lax.fori_loop(0, qi + 1, body, 0)

    sweep(max_step)
    for hd in range(n_heads):
        m_ref[hd] = jnp.broadcast_to(jnp.max(m_ref[hd], axis=-1, keepdims=True), (t, LANES))
    sweep(sum_step)
    for hd in range(n_heads):
        blk = slice(hd * HEAD_LANES, (hd + 1) * HEAD_LANES)
        o_ref[:, blk] = (acc_ref[hd] / jnp.sum(l_ref[hd], axis=-1, keepdims=True)).astype(jnp.bfloat16)


def _retention_kernel(q_ref, k_ref, v_ref, sg_ref, gn_ref, decay_ref, zeta_ref, xi_ref, gamma_ref, o_ref):
    n_chunks = q_ref.shape[0] // CHUNK
    n_pairs = q_ref.shape[1] // LANES
    lane = lax.broadcasted_iota(jnp.int32, (CHUNK, LANES), 1)
    row = lax.broadcasted_iota(jnp.int32, (LANES, RET_DV), 0)
    is_a = lane < RET_DK
    gn = gn_ref[...]
    contract_rows = (((0,), (0,)), ((), ()))

    def group_norm(y):
        mu = jnp.mean(y, axis=-1, keepdims=True)
        d = y - mu
        return d * lax.rsqrt(jnp.mean(d * d, axis=-1, keepdims=True) + EPS)

    def pair_chunk(rows, p, state):
        qb = q_ref[rows, p * LANES:(p + 1) * LANES]
        kb = k_ref[rows, p * LANES:(p + 1) * LANES]
        state_bf = state.astype(jnp.bfloat16)
        new_parts = []
        for hd in range(2):
            head = 2 * p + hd
            cols = slice(head * RET_DV, (head + 1) * RET_DV)
            qh = jnp.where(is_a if hd == 0 else ~is_a, qb, jnp.zeros_like(qb))
            vh = v_ref[rows, cols]
            scores = lax.dot_general(qh, kb, NT_DIMS, preferred_element_type=jnp.float32) * decay_ref[head]
            inner = jnp.dot(scores.astype(jnp.bfloat16), vh, preferred_element_type=jnp.float32)
            cross = jnp.dot(qh, state_bf, preferred_element_type=jnp.float32) * xi_ref[head]
            yn = group_norm(inner + cross) * gn[:, cols]
            o_ref[rows, cols] = (sg_ref[rows, cols].astype(jnp.float32) * yn).astype(jnp.bfloat16)
            kz = (kb.astype(jnp.float32) * zeta_ref[head]).astype(jnp.bfloat16)
            new_parts.append(lax.dot_general(kz, vh, contract_rows, preferred_element_type=jnp.float32))
        return gamma_ref[p] * state + jnp.where(row < RET_DK, new_parts[0], new_parts[1])

    def chunk(n, states):
        rows = pl.ds(pl.multiple_of(n * CHUNK, CHUNK), CHUNK)
        return tuple(pair_chunk(rows, p, state) for p, state in enumerate(states))

    lax.fori_loop(0, n_chunks, chunk, tuple(jnp.zeros((LANES, RET_DV), jnp.float32) for _ in range(n_pairs)))


def _mixer_out_kernel(h_ref, o_ref, yr_ref, ga_ref, gr_ref, wa_ref, wr_ref, wm_ref, out_ref):
    y_a = jnp.dot(o_ref[...], wa_ref[...], preferred_element_type=jnp.float32)
    y_r = jnp.dot(yr_ref[...], wr_ref[...], preferred_element_type=jnp.float32)
    merged = ga_ref[...].astype(jnp.float32) * y_a + gr_ref[...].astype(jnp.float32) * y_r
    out_ref[...] = h_ref[...] + jnp.dot(merged.astype(jnp.bfloat16), wm_ref[...],
                                        preferred_element_type=jnp.float32)


def _pad_heads(w, heads, width):
    lead = w.shape[:-1]
    w = w.reshape(lead + (heads, width))
    w = jnp.pad(w, [(0, 0)] * len(lead) + [(0, 0), (0, HEAD_LANES - width)])
    return w.reshape(lead + (heads * HEAD_LANES,))


def _mixer(x, meta_tokens, g_mix, w_in, g_q_lora, w_uq, g_kv_lora, w_ukv, g_qk_q, g_qk_k,
           w_mla_out, g_ret_gn, w_ret_out, w_mix_out):
    B, S, D = x.shape
    bf = jnp.bfloat16
    rows = PAD_ROWS + N_META + S
    nt = rows // CHUNK
    seq_tiles = S // CHUNK
    assert PAD_ROWS + N_META == CHUNK and S % CHUNK == 0
    x2 = x.reshape(B * S, D)
    head_tile = jnp.concatenate([jnp.zeros((PAD_ROWS, D), x.dtype), meta_tokens.astype(x.dtype)], axis=0)

    segs = jnp.split(w_in, np.cumsum(IN_SIZES)[:-1].tolist(), axis=1)
    kpe_cols = jnp.pad(segs[2], ((0, 0), (MLA_NOPE, HEAD_LANES - MLA_NOPE - MLA_ROPE)))
    win = jnp.concatenate(segs[:2] + [kpe_cols] + segs[3:], axis=1).astype(bf)
    wuq = _pad_heads(w_uq, MLA_HEADS, MLA_QK).astype(bf)
    wkv = w_ukv.reshape(MLA_KV_LORA, MLA_HEADS, MLA_NOPE + MLA_V)
    wk = _pad_heads(wkv[:, :, :MLA_NOPE].reshape(MLA_KV_LORA, -1), MLA_HEADS, MLA_NOPE).astype(bf)
    wv = _pad_heads(wkv[:, :, MLA_NOPE:].reshape(MLA_KV_LORA, -1), MLA_HEADS, MLA_V).astype(bf)
    gqkq = jnp.pad(g_qk_q, (0, HEAD_LANES - MLA_QK)).reshape(1, HEAD_LANES)
    gqkk = jnp.pad(g_qk_k, (0, HEAD_LANES - MLA_QK)).reshape(1, HEAD_LANES)
    wa = _pad_heads(w_mla_out.T, MLA_HEADS, MLA_V).T.astype(bf)
    mla_tabs = _rope_tables(rows, MLA_ROPE, MLA_NOPE, HEAD_LANES)
    ret_tabs = _rope_tables(rows, RET_DK, 0, RET_DK)

    m = MIXER_IN_TILES
    assert nt % m == 0 and m == 3
    row_tile = lambda width: pl.BlockSpec((m * CHUNK, width), lambda b, i: (b * (nt // m) + i, 0))
    seq_in = lambda j: pl.BlockSpec((CHUNK, D), lambda b, i: (b * seq_tiles + jnp.maximum(m * i + j - 1, 0), 0))
    const = lambda shape: pl.BlockSpec(shape, lambda b, i: (0,) * len(shape))
    tab = pl.BlockSpec((m * CHUNK, LANES), lambda b, i: (i, 0))
    n_rows = B * rows
    widths = (MLA_WIDE, MLA_WIDE, MLA_WIDE, RET_QK_WIDE, RET_QK_WIDE, RET_V_WIDE, RET_V_WIDE, D, D)
    q, k, v, rq, rk, rv, sg, ga, gr = pl.pallas_call(
        _mixer_in_kernel,
        out_shape=tuple(jax.ShapeDtypeStruct((n_rows, w), bf) for w in widths),
        grid=(B, nt // m),
        in_specs=[seq_in(0), seq_in(1), seq_in(2), const((CHUNK, D)), const((1, D)), const(win.shape), const((1, MLA_Q_LORA)), const(wuq.shape),
                  const((1, MLA_KV_LORA)), const(wk.shape), const(wv.shape), const((1, HEAD_LANES)),
                  const((1, HEAD_LANES))] + [tab] * 6,
        out_specs=tuple(row_tile(w) for w in widths),
        compiler_params=pltpu.CompilerParams(dimension_semantics=("arbitrary", "arbitrary"),
                                             vmem_limit_bytes=MIXER_VMEM_LIMIT),
        name="mixer_in",
    )(x2, x2, x2, head_tile, g_mix.reshape(1, D), win, g_q_lora.reshape(1, -1), wuq, g_kv_lora.reshape(1, -1), wk, wv,
      gqkq, gqkk, *mla_tabs, *ret_tabs)

    group = ATTN_HEAD_GROUP * HEAD_LANES
    seq_spec = pl.BlockSpec((rows, group), lambda b, g, i: (b, g))
    n_attn = rows // ATTN_TILE
    tile_spec = pl.BlockSpec((ATTN_TILE, group), lambda b, g, i: (b * n_attn + i, g))
    o = pl.pallas_call(
        _attention_kernel,
        out_shape=jax.ShapeDtypeStruct((n_rows, MLA_WIDE), bf),
        grid=(B, MLA_HEADS // ATTN_HEAD_GROUP, n_attn),
        in_specs=[tile_spec, seq_spec, seq_spec],
        out_specs=tile_spec,
        scratch_shapes=[pltpu.VMEM((ATTN_HEAD_GROUP, ATTN_TILE, HEAD_LANES), jnp.float32)] * 3,
        compiler_params=pltpu.CompilerParams(dimension_semantics=("arbitrary",) * 3,
                                             vmem_limit_bytes=MIXER_VMEM_LIMIT),
        name="mla_attention",
    )(q, k, v)

    log_gamma = jnp.log(1.0 - 2.0 ** (-5.0 - jnp.arange(RET_HEADS, dtype=jnp.float32)))
    idx = jnp.arange(CHUNK, dtype=jnp.float32)
    diff = idx[:, None] - idx[None, :]
    decay = jnp.where(diff[None] >= 0, jnp.exp(jnp.maximum(diff, 0.0)[None] * log_gamma[:, None, None]), 0.0)
    zeta = jnp.exp((CHUNK - 1 - idx)[None, :] * log_gamma[:, None])
    xi = jnp.exp((idx + 1)[None, :] * log_gamma[:, None])
    gamma_c = jnp.exp(CHUNK * log_gamma)
    zeta_b = jnp.broadcast_to(zeta[:, :, None], (RET_HEADS, CHUNK, LANES))
    xi_b = jnp.broadcast_to(xi[:, :, None], (RET_HEADS, CHUNK, RET_DV))
    gamma_rows = jnp.broadcast_to(jnp.repeat(gamma_c, RET_DK).reshape(RET_HEADS // 2, LANES, 1),
                                  (RET_HEADS // 2, LANES, RET_DV))
    g = RET_PAIR_GROUP
    pair_seq = lambda width: pl.BlockSpec((rows, g * width), lambda b, p: (b, p))
    pair_tab = lambda d1, d2: pl.BlockSpec((2 * g, d1, d2), lambda b, p: (p, 0, 0))
    yr = pl.pallas_call(
        _retention_kernel,
        out_shape=jax.ShapeDtypeStruct((n_rows, RET_V_WIDE), bf),
        grid=(B, RET_HEADS // (2 * g)),
        in_specs=[pair_seq(LANES), pair_seq(LANES), pair_seq(2 * RET_DV), pair_seq(2 * RET_DV),
                  pl.BlockSpec((1, g * 2 * RET_DV), lambda b, p: (0, p)),
                  pair_tab(CHUNK, CHUNK), pair_tab(CHUNK, LANES), pair_tab(CHUNK, RET_DV),
                  pl.BlockSpec((g, LANES, RET_DV), lambda b, p: (p, 0, 0))],
        out_specs=pair_seq(2 * RET_DV),
        compiler_params=pltpu.CompilerParams(dimension_semantics=("arbitrary", "arbitrary"),
                                             vmem_limit_bytes=MIXER_VMEM_LIMIT),
        name="retention",
    )(rq, rk, rv, sg, g_ret_gn.reshape(1, -1), decay, zeta_b, xi_b, gamma_rows)

    in_tile = lambda width: pl.BlockSpec((CHUNK, width), lambda b, i: (b * nt + 1 + i, 0))
    seq_tile = pl.BlockSpec((CHUNK, D), lambda b, i: (b * seq_tiles + i, 0))
    return pl.pallas_call(
        _mixer_out_kernel,
        out_shape=jax.ShapeDtypeStruct((B * S, D), jnp.float32),
        grid=(B, seq_tiles),
        in_specs=[seq_tile, in_tile(MLA_WIDE), in_tile(RET_V_WIDE), in_tile(D), in_tile(D),
                  const(wa.shape), const(w_ret_out.shape), const(w_mix_out.shape)],
        out_specs=seq_tile,
        compiler_params=pltpu.CompilerParams(dimension_semantics=("arbitrary", "arbitrary"),
                                             vmem_limit_bytes=MIXER_VMEM_LIMIT),
        name="mixer_out",
    )(x2, o, yr, ga, gr, wa, w_ret_out.astype(bf), w_mix_out.astype(bf))


def _top_rows(s, count):
    n = s.shape[0]
    iota = lax.broadcasted_iota(jnp.int32, s.shape, 0).astype(jnp.float32)
    vals, idxs = [], []
    for _ in range(count):
        m = jnp.max(s, axis=0, keepdims=True)
        idx = jnp.min(jnp.where(s == m, iota, float(n)), axis=0, keepdims=True)
        vals.append(m)
        idxs.append(idx)
        s = jnp.where(iota == idx, -jnp.inf, s)
    return vals, idxs


def _route_head(q, k1, k2):
    s1 = lax.dot_general(k1, q, NT_DIMS, preferred_element_type=jnp.float32)
    s2 = lax.dot_general(k2, q, NT_DIMS, preferred_element_type=jnp.float32)
    v1, i1 = _top_rows(s1, PEER_TOPK)
    v2, i2 = _top_rows(s2, PEER_TOPK)
    v2_all = jnp.concatenate(v2, axis=0)
    i2_all = jnp.concatenate(i2, axis=0)
    n_mid = SUBLANES
    cand = [v1[0] + v2_all] + [v1[a] + v2_all[:n_mid] for a in range(1, n_mid)]
    cand.append(jnp.concatenate(v1[n_mid:], axis=0) + v2[0])
    cand_rows = [i1[0] * PEER_NKEYS + i2_all] + [i1[a] * PEER_NKEYS + i2_all[:n_mid] for a in range(1, n_mid)]
    cand_rows.append(jnp.concatenate(i1[n_mid:], axis=0) * PEER_NKEYS + i2[0])
    cand = jnp.concatenate(cand, axis=0)
    cand_rows = jnp.concatenate(cand_rows, axis=0) * float(WORD_ROWS)
    sc, ci = _top_rows(cand, PEER_TOPK)
    iota = lax.broadcasted_iota(jnp.int32, cand.shape, 0).astype(jnp.float32)
    rows = [jnp.max(jnp.where(iota == c, cand_rows, -1.0), axis=0, keepdims=True) for c in ci]
    ex = [jnp.exp(v - sc[0]) for v in sc]
    denom = ex[0]
    for e in ex[1:]:
        denom = denom + e
    return jnp.concatenate(rows, axis=0).astype(jnp.int32), jnp.concatenate(ex, axis=0) / denom


def _peer_route_kernel(h_ref, g_ref, wq_ref, k1_ref, k2_ref, xn_ref, rows_ref, gates_ref,
                       xb_ref, rows_scr, gates_scr):
    grp = pl.program_id(1)

    @pl.when(grp == 0)
    def _():
        x = h_ref[...]
        xn = x * lax.rsqrt(jnp.mean(x * x, axis=-1, keepdims=True) + EPS) * g_ref[...]
        for r in range(SUBLANES):
            xn_ref[pl.ds(r, x.shape[0], stride=SUBLANES), :] = xn[:, r * LANES:(r + 1) * LANES]
        xb_ref[...] = xn.astype(jnp.bfloat16)

    q = jnp.dot(xb_ref[...], wq_ref[...], preferred_element_type=jnp.float32).astype(jnp.bfloat16)
    k1, k2 = k1_ref[...], k2_ref[...]
    for j in range(ROUTE_HEAD_GROUP):
        rows, gates = _route_head(q[:, j * PEER_DQ:(j + 1) * PEER_DQ], k1, k2)
        base = pl.multiple_of((grp * ROUTE_HEAD_GROUP + j) * PEER_TOPK, PEER_TOPK)
        rows_scr[pl.ds(base, PEER_TOPK), :] = rows
        gates_scr[pl.ds(base, PEER_TOPK), :] = gates

    @pl.when(grp == PEER_HEADS // ROUTE_HEAD_GROUP - 1)
    def _():
        for j in range(rows_ref.shape[0]):
            rows_ref[j] = rows_scr[:, j * PEER_TOKEN_BLOCK:(j + 1) * PEER_TOKEN_BLOCK]
        gates_ref[...] = gates_scr[...].T


def _peer_route(h, g_ffn, w_peer_q, keys_1, keys_2):
    T = h.shape[0]
    tb = ROUTE_TOKEN_BLOCK
    per_step = tb // PEER_TOKEN_BLOCK
    half = PEER_DQ // 2
    zeros = jnp.zeros((PEER_NKEYS, half), jnp.bfloat16)
    k1 = jnp.concatenate([keys_1.astype(jnp.bfloat16), zeros], axis=1)
    k2 = jnp.concatenate([zeros, keys_2.astype(jnp.bfloat16)], axis=1)
    keys_spec = pl.BlockSpec((PEER_NKEYS, PEER_DQ), lambda i, g: (0, 0))
    pair_spec = pl.BlockSpec((tb, PEER_PAIRS), lambda i, g: (i, 0))
    return pl.pallas_call(
        _peer_route_kernel,
        out_shape=(jax.ShapeDtypeStruct((T * SUBLANES, LANES), jnp.float32),
                   jax.ShapeDtypeStruct((T // PEER_TOKEN_BLOCK, PEER_PAIRS, PEER_TOKEN_BLOCK), jnp.int32),
                   jax.ShapeDtypeStruct((T, PEER_PAIRS), jnp.float32)),
        grid=(T // tb, PEER_HEADS // ROUTE_HEAD_GROUP),
        in_specs=[pl.BlockSpec((tb, D_MODEL), lambda i, g: (i, 0)),
                  pl.BlockSpec((1, D_MODEL), lambda i, g: (0, 0)),
                  pl.BlockSpec((D_MODEL, ROUTE_HEAD_GROUP * PEER_DQ), lambda i, g: (0, g)),
                  keys_spec, keys_spec],
        out_specs=(pl.BlockSpec((tb * SUBLANES, LANES), lambda i, g: (i, 0)),
                   pl.BlockSpec((per_step, PEER_PAIRS, PEER_TOKEN_BLOCK), lambda i, g: (i, 0, 0)),
                   pair_spec),
        scratch_shapes=[pltpu.VMEM((tb, D_MODEL), jnp.bfloat16),
                        pltpu.VMEM((PEER_PAIRS, tb), jnp.int32),
                        pltpu.VMEM((PEER_PAIRS, tb), jnp.float32)],
        compiler_params=pltpu.CompilerParams(dimension_semantics=("arbitrary", "arbitrary")),
        name="peer_route",
    )(h, g_ffn.reshape(1, D_MODEL), w_peer_q.astype(jnp.bfloat16), k1, k2)


def _pack_table_kernel(tab_ref, out_ref):
    n = tab_ref.shape[0]
    bits = lax.bitcast_convert_type(tab_ref[...].astype(jnp.bfloat16).astype(jnp.float32), jnp.int32)
    for j in range(WORD_ROWS):
        lo = lax.shift_right_logical(bits[:, 2 * j * LANES:(2 * j + 1) * LANES], 16)
        hi = bits[:, (2 * j + 1) * LANES:(2 * j + 2) * LANES] & HIGH_HALF_MASK
        out_ref[pl.ds(j, n, stride=WORD_ROWS), :] = hi | lo


def _pack_expert_table(tab):
    n = PACK_EXPERT_BLOCK
    return pl.pallas_call(
        _pack_table_kernel,
        out_shape=jax.ShapeDtypeStruct((tab.shape[0] * WORD_ROWS, LANES), jnp.int32),
        grid=(tab.shape[0] // n,),
        in_specs=[pl.BlockSpec((n, D_MODEL), lambda i: (i, 0))],
        out_specs=pl.BlockSpec((n * WORD_ROWS, LANES), lambda i: (i, 0)),
        compiler_params=pltpu.CompilerParams(dimension_semantics=("arbitrary",)),
        name="pack_table",
    )(tab)


def _gather_rows(idx_ref, tab_ref, dst_refs, t0):
    for k in range(PEER_PAIRS):
        idx_k = idx_ref.at[k]
        for s, dst_ref in enumerate(dst_refs):
            row = pl.multiple_of(idx_k[t0 + s], WORD_ROWS)
            dst_ref[k * WORD_ROWS:(k + 1) * WORD_ROWS, :] = tab_ref[pl.ds(row, WORD_ROWS), :]


def _expert_token_loop(idx_hbm, idx_bufs, sem, tab_ref, rows_refs, n_tokens, consume):
    i = pl.program_id(0)
    n = pl.num_programs(0)
    n_slots = len(rows_refs)
    assert n_tokens % n_slots == 0

    def copy(block, slot):
        return pltpu.make_async_copy(idx_hbm.at[block], idx_bufs[slot], sem.at[slot])

    def consume_group(t0):
        for s in range(n_slots):
            consume(t0 + s, s)

    @pl.when(i == 0)
    def _():
        copy(0, 0).start()
        copy(0, 0).wait()
        _gather_rows(idx_bufs[0], tab_ref, rows_refs, 0)

    def run(slot):
        has_next = i + 1 < n

        @pl.when(has_next)
        def _():
            copy(i + 1, 1 - slot).start()

        def step(j, carry):
            consume_group(n_slots * j)
            _gather_rows(idx_bufs[slot], tab_ref, rows_refs, n_slots * (j + 1))
            return carry

        lax.fori_loop(0, n_tokens // n_slots - 1, step, 0)

        @pl.when(has_next)
        def _():
            copy(i + 1, 1 - slot).wait()
            consume_group(n_tokens - n_slots)
            _gather_rows(idx_bufs[1 - slot], tab_ref, rows_refs, 0)

        @pl.when(jnp.logical_not(has_next))
        def _():
            consume_group(n_tokens - n_slots)

    for slot in range(2):
        pl.when(lax.rem(i, 2) == slot)(lambda slot=slot: run(slot))


def _diag_mask():
    lane = lax.broadcasted_iota(jnp.int32, (SUBLANES, GATHER_ROWS), 1)
    sub = lax.broadcasted_iota(jnp.int32, (SUBLANES, GATHER_ROWS), 0)
    return (lane & (SUBLANES - 1)) == sub


def _split_bf16(x):
    hi = x.astype(jnp.bfloat16)
    lo = (x - hi.astype(jnp.float32)).astype(jnp.bfloat16)
    return hi, lo


def _peer_hidden_kernel(idx_hbm, x_ref, gate_ref, tab_ref, sel_ref, w_ref, diag_ref, idx_a, idx_b, sem,
                        *rows_refs):
    mask = _diag_mask()

    def consume(t, slot):
        rows = pltpu.bitcast(rows_refs[slot][...], jnp.bfloat16)
        xb = x_ref[t].astype(jnp.bfloat16)
        d = lax.dot_general(xb, rows, NT_DIMS, preferred_element_type=jnp.float32)
        diag_ref[pl.ds(t, 1), :] = jnp.sum(jnp.where(mask, d, 0.0), axis=0, keepdims=True)

    _expert_token_loop(idx_hbm, (idx_a, idx_b), sem, tab_ref, rows_refs, x_ref.shape[0], consume)
    hi, lo = _split_bf16(diag_ref[...])
    sel = sel_ref[...]
    s = (jnp.dot(hi, sel, preferred_element_type=jnp.float32)
         + jnp.dot(lo, sel, preferred_element_type=jnp.float32))
    act = 0.5 * s * (1.0 + lax.erf(s * (1.0 / math.sqrt(2.0))))
    w_ref[...] = gate_ref[...] * act


def _peer_output_kernel(idx_hbm, w_ref, h_ref, tab_ref, expand_ref, out_ref, whi_ref, wlo_ref, idx_a, idx_b, sem,
                        *rows_refs):
    mask = _diag_mask()
    hi, lo = _split_bf16(w_ref[...])
    expand = expand_ref[...]
    whi_ref[...] = jnp.dot(hi, expand, preferred_element_type=jnp.float32)
    wlo_ref[...] = jnp.dot(lo, expand, preferred_element_type=jnp.float32)

    def consume(t, slot):
        rows = pltpu.bitcast(rows_refs[slot][...], jnp.bfloat16)
        parts = []
        for w_part in (whi_ref, wlo_ref):
            w_row = jnp.broadcast_to(w_part[pl.ds(t, 1), :], (SUBLANES, GATHER_ROWS))
            parts.append(jnp.where(mask, w_row, 0.0))
        lhs = jnp.concatenate(parts, axis=0).astype(jnp.bfloat16)
        y = jnp.dot(lhs, rows, preferred_element_type=jnp.float32)
        y = y[:SUBLANES] + y[SUBLANES:]
        y_row = jnp.concatenate([y[r:r + 1, :] for r in range(SUBLANES)], axis=1)
        out_ref[pl.ds(t, 1), :] = h_ref[pl.ds(t, 1), :] + y_row

    _expert_token_loop(idx_hbm, (idx_a, idx_b), sem, tab_ref, rows_refs, h_ref.shape[0], consume)


def _peer_experts(h, xn, rows, gates, peer_u, peer_v):
    T = h.shape[0]
    tb = PEER_TOKEN_BLOCK
    u_tab = _pack_expert_table(peer_u)
    v_tab = _pack_expert_table(peer_v)
    expand = (jnp.arange(PEER_PAIRS)[:, None] == jnp.arange(GATHER_ROWS)[None, :] // SUBLANES
              ).astype(jnp.bfloat16)
    params = pltpu.CompilerParams(dimension_semantics=("arbitrary",), vmem_limit_bytes=PEER_VMEM_LIMIT)
    hbm_ref = pl.BlockSpec(memory_space=pl.ANY)
    index_scratch = [pltpu.SMEM((PEER_PAIRS, tb), jnp.int32)] * 2 + [pltpu.SemaphoreType.DMA((2,))]
    pair_block = pl.BlockSpec((tb, PEER_PAIRS), lambda i: (i, 0))
    vmem_rows = pl.BlockSpec((tb, D_MODEL), lambda i: (i, 0))
    whole_vmem = pl.BlockSpec(memory_space=pltpu.VMEM)
    rows_scratch = pltpu.VMEM((GATHER_WORD_ROWS, LANES), jnp.int32)
    wide_scratch = pltpu.VMEM((tb, GATHER_ROWS), jnp.float32)

    w = pl.pallas_call(
        _peer_hidden_kernel,
        out_shape=jax.ShapeDtypeStruct((T, PEER_PAIRS), jnp.float32),
        grid=(T // tb,),
        in_specs=[hbm_ref, pl.BlockSpec((tb, SUBLANES, LANES), lambda i: (i, 0, 0)), pair_block,
                  whole_vmem, whole_vmem],
        out_specs=pair_block,
        scratch_shapes=[wide_scratch] + index_scratch + [rows_scratch] * HIDDEN_SLOTS,
        compiler_params=params,
        name="peer_hidden",
    )(rows, xn.reshape(T, SUBLANES, LANES), gates, u_tab, expand.T)

    return pl.pallas_call(
        _peer_output_kernel,
        out_shape=jax.ShapeDtypeStruct((T, D_MODEL), jnp.float32),
        grid=(T // tb,),
        in_specs=[hbm_ref, pair_block, vmem_rows, whole_vmem, whole_vmem],
        out_specs=vmem_rows,
        scratch_shapes=[wide_scratch, wide_scratch] + index_scratch + [rows_scratch] * OUTPUT_SLOTS,
        compiler_params=params,
        name="peer_output",
    )(rows, w, h, v_tab, expand)


def kernel(x, meta_tokens, g_mix, w_in, g_q_lora, w_uq, g_kv_lora, w_ukv, g_qk_q, g_qk_k, w_mla_out, g_ret_gn, w_ret_out, w_mix_out, g_ffn, w_peer_q, peer_keys_1, peer_keys_2, peer_u, peer_v):
    B, S, D = x.shape
    h = _mixer(x, meta_tokens, g_mix[0], w_in[0], g_q_lora[0], w_uq[0], g_kv_lora[0], w_ukv[0],
               g_qk_q[0], g_qk_k[0], w_mla_out[0], g_ret_gn[0], w_ret_out[0], w_mix_out[0])
    xn, rows, gates = _peer_route(h, g_ffn[0], w_peer_q[0], peer_keys_1[0], peer_keys_2[0])
    out = _peer_experts(h, xn, rows, gates, peer_u[0], peer_v[0])
    return out.reshape(B, S, D)
```

```python
import math

import jax
import jax.numpy as jnp
import numpy as np
from jax import lax
from jax.experimental import pallas as pl
from jax.experimental.pallas import tpu as pltpu

D_MODEL = 1024
N_META = 16
CHUNK = 128
ROPE_BASE = 10000.0
EPS = 1e-6
MLA_HEADS = 8
MLA_Q_LORA = 384
MLA_KV_LORA = 256
MLA_NOPE = 64
MLA_ROPE = 32
MLA_V = 64
MLA_QK = MLA_NOPE + MLA_ROPE
RET_HEADS = 8
RET_DK = 64
RET_DV = 128
PEER_HEADS = 8
PEER_NKEYS = 128
PEER_DQ = 128
PEER_TOPK = 16
PEER_PAIRS = PEER_HEADS * PEER_TOPK
IN_SIZES = (MLA_Q_LORA, MLA_KV_LORA, MLA_ROPE, RET_HEADS * RET_DK, RET_HEADS * RET_DK,
            RET_HEADS * RET_DV, RET_HEADS * RET_DV, D_MODEL, D_MODEL)

LANES = 128
SUBLANES = 8
WORD_ROWS = D_MODEL // (2 * LANES)
GATHER_WORD_ROWS = PEER_PAIRS * WORD_ROWS
GATHER_ROWS = 2 * GATHER_WORD_ROWS
PEER_TOKEN_BLOCK = LANES
ROUTE_TOKEN_BLOCK = 128
ROUTE_HEAD_GROUP = 8
HIDDEN_SLOTS = 32
OUTPUT_SLOTS = 32
PEER_VMEM_LIMIT = 56 * 1024 * 1024
NT_DIMS = (((1,), (1,)), ((), ()))
HIGH_HALF_MASK = -65536
PACK_EXPERT_BLOCK = 256


PAD_ROWS = (-N_META) % CHUNK
HEAD_LANES = 128
MLA_WIDE = MLA_HEADS * HEAD_LANES
RET_QK_WIDE = RET_HEADS * RET_DK
RET_V_WIDE = RET_HEADS * RET_DV
MASK_VALUE = -1e30
MIXER_VMEM_LIMIT = 48 * 1024 * 1024
ATTN_HEAD_GROUP = 8
ATTN_TILE = 3 * CHUNK
MIXER_IN_TILES = 3
RET_PAIR_GROUP = 2
_SEG_SIZES = (MLA_Q_LORA, MLA_KV_LORA, HEAD_LANES, RET_QK_WIDE, RET_QK_WIDE, RET_V_WIDE, RET_V_WIDE,
              D_MODEL, D_MODEL)
_SEG_STARTS = tuple(int(v) for v in np.cumsum((0,) + _SEG_SIZES))


def _rope_tables(n_rows, dim, block_offset, repeat):
    half = dim // 2
    pos = jnp.arange(n_rows, dtype=jnp.float32) - PAD_ROWS
    inv = ROPE_BASE ** (-jnp.arange(half, dtype=jnp.float32) / half)
    lane = np.arange(LANES)
    rel = (lane - block_offset) % repeat
    in_rope = (lane >= block_offset) & (rel < dim)
    ang = pos[:, None] * inv[rel % half][None, :]
    cos = jnp.where(in_rope[None, :], jnp.cos(ang), 1.0)
    sin = jnp.sin(ang)
    s1 = jnp.where((in_rope & (rel < half))[None, :], -sin, 0.0)
    s2 = jnp.where((in_rope & (rel >= half))[None, :], sin, 0.0)
    return cos, s1, s2


def _apply_rope(x, cos, s1, s2, half):
    return x * cos + pltpu.roll(x, LANES - half, 1) * s1 + pltpu.roll(x, half, 1) * s2


def _mixer_in_kernel(h0_ref, h1_ref, h2_ref, head_ref, gmix_ref, win_ref, gq_ref, wuq_ref, gkv_ref, wk_ref, wv_ref,
                     gqkq_ref, gqkk_ref, mc_ref, ms1_ref, ms2_ref, rc_ref, rs1_ref, rs2_ref,
                     q_ref, k_ref, v_ref, rq_ref, rk_ref, rv_ref, sg_ref, ga_ref, gr_ref):
    first = jnp.where(pl.program_id(1) == 0, head_ref[...], h0_ref[...])
    x = jnp.concatenate([first, h1_ref[...], h2_ref[...]], axis=0)
    hn = (x * lax.rsqrt(jnp.mean(x * x, axis=-1, keepdims=True) + EPS) * gmix_ref[...]).astype(jnp.bfloat16)

    def seg(i):
        return jnp.dot(hn, win_ref[:, _SEG_STARTS[i]:_SEG_STARTS[i + 1]], preferred_element_type=jnp.float32)

    def lora_norm(c, g_ref):
        return (c * lax.rsqrt(jnp.mean(c * c, axis=-1, keepdims=True) + EPS) * g_ref[...]).astype(jnp.bfloat16)

    cq = lora_norm(seg(0), gq_ref)
    ckv = lora_norm(seg(1), gkv_ref)
    kpe = seg(2)
    q = jnp.dot(cq, wuq_ref[...], preferred_element_type=jnp.float32)
    kn = jnp.dot(ckv, wk_ref[...], preferred_element_type=jnp.float32)
    v_ref[...] = jnp.dot(ckv, wv_ref[...], preferred_element_type=jnp.float32).astype(jnp.bfloat16)
    mc, ms1, ms2 = mc_ref[...], ms1_ref[...], ms2_ref[...]
    scale = math.log2(math.e) / math.sqrt(MLA_QK)

    def qk_norm_rope(t, g_ref):
        t = t * lax.rsqrt(jnp.sum(t * t, axis=-1, keepdims=True) * (1.0 / MLA_QK) + EPS) * g_ref[...]
        return _apply_rope(t, mc, ms1, ms2, MLA_ROPE // 2)

    for hd in range(MLA_HEADS):
        blk = slice(hd * HEAD_LANES, (hd + 1) * HEAD_LANES)
        q_ref[:, blk] = (qk_norm_rope(q[:, blk], gqkq_ref) * scale).astype(jnp.bfloat16)
        k_ref[:, blk] = qk_norm_rope(kn[:, blk] + kpe, gqkk_ref).astype(jnp.bfloat16)

    rc, rs1, rs2 = rc_ref[...], rs1_ref[...], rs2_ref[...]
    rq = seg(3)
    rk = seg(4)
    for j in range(RET_QK_WIDE // LANES):
        blk = slice(j * LANES, (j + 1) * LANES)
        rq_ref[:, blk] = _apply_rope(rq[:, blk], rc, rs1, rs2, RET_DK // 2).astype(jnp.bfloat16)
        rk_ref[:, blk] = (_apply_rope(rk[:, blk], rc, rs1, rs2, RET_DK // 2) * (RET_DK ** -0.5)).astype(jnp.bfloat16)
    rv_ref[...] = seg(5).astype(jnp.bfloat16)
    gate = seg(6)
    sg_ref[...] = (gate * jax.nn.sigmoid(gate)).astype(jnp.bfloat16)
    ga_ref[...] = jax.nn.sigmoid(seg(7)).astype(jnp.bfloat16)
    gr_ref[...] = jax.nn.sigmoid(seg(8)).astype(jnp.bfloat16)


def _attention_kernel(q_ref, k_ref, v_ref, o_ref, m_ref, l_ref, acc_ref):
    qi = pl.program_id(2)
    t = ATTN_TILE
    n_heads = q_ref.shape[1] // HEAD_LANES
    n_lane_tiles = t // LANES
    m_ref[...] = jnp.full(m_ref.shape, MASK_VALUE, jnp.float32)
    l_ref[...] = jnp.zeros(l_ref.shape, jnp.float32)
    acc_ref[...] = jnp.zeros(acc_ref.shape, jnp.float32)

    def lane_tiles(x):
        return [x[:, i * LANES:(i + 1) * LANES] for i in range(n_lane_tiles)]

    def scores(j, hd, masked):
        blk = slice(hd * HEAD_LANES, (hd + 1) * HEAD_LANES)
        start = pl.multiple_of(j * t, t)
        s = lax.dot_general(q_ref[:, blk], k_ref[pl.ds(start, t), blk], NT_DIMS,
                            preferred_element_type=jnp.float32)
        if masked:
            r = lax.broadcasted_iota(jnp.int32, (t, t), 0) + qi * t
            c = lax.broadcasted_iota(jnp.int32, (t, t), 1) + j * t
            s = jnp.where((c <= r) & ((c >= PAD_ROWS) | (r < PAD_ROWS)), s, MASK_VALUE)
        return s

    def max_step(j, masked):
        for hd in range(n_heads):
            m = m_ref[hd]
            for part in lane_tiles(scores(j, hd, masked)):
                m = jnp.maximum(m, part)
            m_ref[hd] = m

    def sum_step(j, masked):
        start = pl.multiple_of(j * t, t)
        for hd in range(n_heads):
            blk = slice(hd * HEAD_LANES, (hd + 1) * HEAD_LANES)
            p = jnp.exp2(scores(j, hd, masked) - jnp.tile(m_ref[hd], (1, n_lane_tiles)))
            l = l_ref[hd]
            for part in lane_tiles(p):
                l = l + part
            l_ref[hd] = l
            acc_ref[hd] += jnp.dot(p.astype(jnp.bfloat16), v_ref[pl.ds(start, t), blk],
                                   preferred_element_type=jnp.float32)

    def sweep(step):
        def body(j, carry):
            edge = (j == 0) | (j == qi)

            @pl.when(edge)
            def _():
                step(j, True)

            @pl.when(jnp.logical_not(edge))
            def _():
                step(j, False)

            return carry

        lax.fori_loop(0, qi + 1, body, 0)

    sweep(max_step)
    for hd in range(n_heads):
        m_ref[hd] = jnp.broadcast_to(jnp.max(m_ref[hd], axis=-1, keepdims=True), (t, LANES))
    sweep(sum_step)
    for hd in range(n_heads):
        blk = slice(hd * HEAD_LANES, (hd + 1) * HEAD_LANES)
        o_ref[:, blk] = (acc_ref[hd] / jnp.sum(l_ref[hd], axis=-1, keepdims=True)).astype(jnp.bfloat16)


def _retention_kernel(q_ref, k_ref, v_ref, sg_ref, gn_ref, decay_ref, zeta_ref, xi_ref, gamma_ref, o_ref):
    n_chunks = q_ref.shape[0] // CHUNK
    n_pairs = q_ref.shape[1] // LANES
    lane = lax.broadcasted_iota(jnp.int32, (CHUNK, LANES), 1)
    row = lax.broadcasted_iota(jnp.int32, (LANES, RET_DV), 0)
    is_a = lane < RET_DK
    gn = gn_ref[...]
    contract_rows = (((0,), (0,)), ((), ()))

    def group_norm(y):
        mu = jnp.mean(y, axis=-1, keepdims=True)
        d = y - mu
        return d * lax.rsqrt(jnp.mean(d * d, axis=-1, keepdims=True) + EPS)

    def pair_chunk(rows, p, state):
        qb = q_ref[rows, p * LANES:(p + 1) * LANES]
        kb = k_ref[rows, p * LANES:(p + 1) * LANES]
        state_bf = state.astype(jnp.bfloat16)
        new_parts = []
        for hd in range(2):
            head = 2 * p + hd
            cols = slice(head * RET_DV, (head + 1) * RET_DV)
            qh = jnp.where(is_a if hd == 0 else ~is_a, qb, jnp.zeros_like(qb))
            vh = v_ref[rows, cols]
            scores = lax.dot_general(qh, kb, NT_DIMS, preferred_element_type=jnp.float32) * decay_ref[head]
            inner = jnp.dot(scores.astype(jnp.bfloat16), vh, preferred_element_type=jnp.float32)
            cross = jnp.dot(qh, state_bf, preferred_element_type=jnp.float32) * xi_ref[head]
            yn = group_norm(inner + cross) * gn[:, cols]
            o_ref[rows, cols] = (sg_ref[rows, cols].astype(jnp.float32) * yn).astype(jnp.bfloat16)
            kz = (kb.astype(jnp.float32) * zeta_ref[head]).astype(jnp.bfloat16)
            new_parts.append(lax.dot_general(kz, vh, contract_rows, preferred_element_type=jnp.float32))
        return gamma_ref[p] * state + jnp.where(row < RET_DK, new_parts[0], new_parts[1])

    def chunk(n, states):
        rows = pl.ds(pl.multiple_of(n * CHUNK, CHUNK), CHUNK)
        return tuple(pair_chunk(rows, p, state) for p, state in enumerate(states))

    lax.fori_loop(0, n_chunks, chunk, tuple(jnp.zeros((LANES, RET_DV), jnp.float32) for _ in range(n_pairs)))


def _mixer_out_kernel(h_ref, o_ref, yr_ref, ga_ref, gr_ref, wa_ref, wr_ref, wm_ref, out_ref):
    y_a = jnp.dot(o_ref[...], wa_ref[...], preferred_element_type=jnp.float32)
    y_r = jnp.dot(yr_ref[...], wr_ref[...], preferred_element_type=jnp.float32)
    merged = ga_ref[...].astype(jnp.float32) * y_a + gr_ref[...].astype(jnp.float32) * y_r
    out_ref[...] = h_ref[...] + jnp.dot(merged.astype(jnp.bfloat16), wm_ref[...],
                                        preferred_element_type=jnp.float32)


def _pad_heads(w, heads, width):
    lead = w.shape[:-1]
    w = w.reshape(lead + (heads, width))
    w = jnp.pad(w, [(0, 0)] * len(lead) + [(0, 0), (0, HEAD_LANES - width)])
    return w.reshape(lead + (heads * HEAD_LANES,))


def _mixer(x, meta_tokens, g_mix, w_in, g_q_lora, w_uq, g_kv_lora, w_ukv, g_qk_q, g_qk_k,
           w_mla_out, g_ret_gn, w_ret_out, w_mix_out):
    B, S, D = x.shape
    bf = jnp.bfloat16
    rows = PAD_ROWS + N_META + S
    nt = rows // CHUNK
    seq_tiles = S // CHUNK
    assert PAD_ROWS + N_META == CHUNK and S % CHUNK == 0
    x2 = x.reshape(B * S, D)
    head_tile = jnp.concatenate([jnp.zeros((PAD_ROWS, D), x.dtype), meta_tokens.astype(x.dtype)], axis=0)

    segs = jnp.split(w_in, np.cumsum(IN_SIZES)[:-1].tolist(), axis=1)
    kpe_cols = jnp.pad(segs[2], ((0, 0), (MLA_NOPE, HEAD_LANES - MLA_NOPE - MLA_ROPE)))
    win = jnp.concatenate(segs[:2] + [kpe_cols] + segs[3:], axis=1).astype(bf)
    wuq = _pad_heads(w_uq, MLA_HEADS, MLA_QK).astype(bf)
    wkv = w_ukv.reshape(MLA_KV_LORA, MLA_HEADS, MLA_NOPE + MLA_V)
    wk = _pad_heads(wkv[:, :, :MLA_NOPE].reshape(MLA_KV_LORA, -1), MLA_HEADS, MLA_NOPE).astype(bf)
    wv = _pad_heads(wkv[:, :, MLA_NOPE:].reshape(MLA_KV_LORA, -1), MLA_HEADS, MLA_V).astype(bf)
    gqkq = jnp.pad(g_qk_q, (0, HEAD_LANES - MLA_QK)).reshape(1, HEAD_LANES)
    gqkk = jnp.pad(g_qk_k, (0, HEAD_LANES - MLA_QK)).reshape(1, HEAD_LANES)
    wa = _pad_heads(w_mla_out.T, MLA_HEADS, MLA_V).T.astype(bf)
    mla_tabs = _rope_tables(rows, MLA_ROPE, MLA_NOPE, HEAD_LANES)
    ret_tabs = _rope_tables(rows, RET_DK, 0, RET_DK)

    m = MIXER_IN_TILES
    assert nt % m == 0 and m == 3
    row_tile = lambda width: pl.BlockSpec((m * CHUNK, width), lambda b, i: (b * (nt // m) + i, 0))
    seq_in = lambda j: pl.BlockSpec((CHUNK, D), lambda b, i: (b * seq_tiles + jnp.maximum(m * i + j - 1, 0), 0))
    const = lambda shape: pl.BlockSpec(shape, lambda b, i: (0,) * len(shape))
    tab = pl.BlockSpec((m * CHUNK, LANES), lambda b, i: (i, 0))
    n_rows = B * rows
    widths = (MLA_WIDE, MLA_WIDE, MLA_WIDE, RET_QK_WIDE, RET_QK_WIDE, RET_V_WIDE, RET_V_WIDE, D, D)
    q, k, v, rq, rk, rv, sg, ga, gr = pl.pallas_call(
        _mixer_in_kernel,
        out_shape=tuple(jax.ShapeDtypeStruct((n_rows, w), bf) for w in widths),
        grid=(B, nt // m),
        in_specs=[seq_in(0), seq_in(1), seq_in(2), const((CHUNK, D)), const((1, D)), const(win.shape),
                  const((1, MLA_Q_LORA)), const(wuq.shape), const((1, MLA_KV_LORA)), const(wk.shape), const(wv.shape), const((1, HEAD_LANES)),
                  const((1, HEAD_LANES))] + [tab] * 6,
        out_specs=tuple(row_tile(w) for w in widths),
        compiler_params=pltpu.CompilerParams(dimension_semantics=("arbitrary", "arbitrary"),
                                             vmem_limit_bytes=MIXER_VMEM_LIMIT),
        name="mixer_in",
    )(x2, x2, x2, head_tile, g_mix.reshape(1, D), win, g_q_lora.reshape(1, -1), wuq, g_kv_lora.reshape(1, -1), wk, wv,
      gqkq, gqkk, *mla_tabs, *ret_tabs)

    group = ATTN_HEAD_GROUP * HEAD_LANES
    seq_spec = pl.BlockSpec((rows, group), lambda b, g, i: (b, g))
    n_attn = rows // ATTN_TILE
    tile_spec = pl.BlockSpec((ATTN_TILE, group), lambda b, g, i: (b * n_attn + i, g))
    o = pl.pallas_call(
        _attention_kernel,
        out_shape=jax.ShapeDtypeStruct((n_rows, MLA_WIDE), bf),
        grid=(B, MLA_HEADS // ATTN_HEAD_GROUP, n_attn),
        in_specs=[tile_spec, seq_spec, seq_spec],
        out_specs=tile_spec,
        scratch_shapes=[pltpu.VMEM((ATTN_HEAD_GROUP, ATTN_TILE, HEAD_LANES), jnp.float32)] * 3,
        compiler_params=pltpu.CompilerParams(dimension_semantics=("arbitrary",) * 3,
                                             vmem_limit_bytes=MIXER_VMEM_LIMIT),
        name="mla_attention",
    )(q, k, v)

    log_gamma = jnp.log(1.0 - 2.0 ** (-5.0 - jnp.arange(RET_HEADS, dtype=jnp.float32)))
    idx = jnp.arange(CHUNK, dtype=jnp.float32)
    diff = idx[:, None] - idx[None, :]
    decay = jnp.where(diff[None] >= 0, jnp.exp(jnp.maximum(diff, 0.0)[None] * log_gamma[:, None, None]), 0.0)
    zeta = jnp.exp((CHUNK - 1 - idx)[None, :] * log_gamma[:, None])
    xi = jnp.exp((idx + 1)[None, :] * log_gamma[:, None])
    gamma_c = jnp.exp(CHUNK * log_gamma)
    zeta_b = jnp.broadcast_to(zeta[:, :, None], (RET_HEADS, CHUNK, LANES))
    xi_b = jnp.broadcast_to(xi[:, :, None], (RET_HEADS, CHUNK, RET_DV))
    gamma_rows = jnp.broadcast_to(jnp.repeat(gamma_c, RET_DK).reshape(RET_HEADS // 2, LANES, 1),
                                  (RET_HEADS // 2, LANES, RET_DV))
    g = RET_PAIR_GROUP
    pair_seq = lambda width: pl.BlockSpec((rows, g * width), lambda b, p: (b, p))
    pair_tab = lambda d1, d2: pl.BlockSpec((2 * g, d1, d2), lambda b, p: (p, 0, 0))
    yr = pl.pallas_call(
        _retention_kernel,
        out_shape=jax.ShapeDtypeStruct((n_rows, RET_V_WIDE), bf),
        grid=(B, RET_HEADS // (2 * g)),
        in_specs=[pair_seq(LANES), pair_seq(LANES), pair_seq(2 * RET_DV), pair_seq(2 * RET_DV),
                  pl.BlockSpec((1, g * 2 * RET_DV), lambda b, p: (0, p)),
                  pair_tab(CHUNK, CHUNK), pair_tab(CHUNK, LANES), pair_tab(CHUNK, RET_DV),
                  pl.BlockSpec((g, LANES, RET_DV), lambda b, p: (p, 0, 0))],
        out_specs=pair_seq(2 * RET_DV),
        compiler_params=pltpu.CompilerParams(dimension_semantics=("arbitrary", "arbitrary"),
                                             vmem_limit_bytes=MIXER_VMEM_LIMIT),
        name="retention",
    )(rq, rk, rv, sg, g_ret_gn.reshape(1, -1), decay, zeta_b, xi_b, gamma_rows)

    in_tile = lambda width: pl.BlockSpec((CHUNK, width), lambda b, i: (b * nt + 1 + i, 0))
    seq_tile = pl.BlockSpec((CHUNK, D), lambda b, i: (b * seq_tiles + i, 0))
    return pl.pallas_call(
        _mixer_out_kernel,
        out_shape=jax.ShapeDtypeStruct((B * S, D), jnp.float32),
        grid=(B, seq_tiles),
        in_specs=[seq_tile, in_tile(MLA_WIDE), in_tile(RET_V_WIDE), in_tile(D), in_tile(D),
                  const(wa.shape), const(w_ret_out.shape), const(w_mix_out.shape)],
        out_specs=seq_tile,
        compiler_params=pltpu.CompilerParams(dimension_semantics=("arbitrary", "arbitrary"),
                                             vmem_limit_bytes=MIXER_VMEM_LIMIT),
        name="mixer_out",
    )(x2, o, yr, ga, gr, wa, w_ret_out.astype(bf), w_mix_out.astype(bf))


def _top_rows(s, count):
    n = s.shape[0]
    iota = lax.broadcasted_iota(jnp.int32, s.shape, 0).astype(jnp.float32)
    vals, idxs = [], []
    for _ in range(count):
        m = jnp.max(s, axis=0, keepdims=True)
        idx = jnp.min(jnp.where(s == m, iota, float(n)), axis=0, keepdims=True)
        vals.append(m)
        idxs.append(idx)
        s = jnp.where(iota == idx, -jnp.inf, s)
    return vals, idxs


def _route_head(q, k1, k2):
    s1 = lax.dot_general(k1, q, NT_DIMS, preferred_element_type=jnp.float32)
    s2 = lax.dot_general(k2, q, NT_DIMS, preferred_element_type=jnp.float32)
    v1, i1 = _top_rows(s1, PEER_TOPK)
    v2, i2 = _top_rows(s2, PEER_TOPK)
    v2_all = jnp.concatenate(v2, axis=0)
    i2_all = jnp.concatenate(i2, axis=0)
    n_mid = SUBLANES
    kept = [1 << (PEER_TOPK // (a + 1) - 1).bit_length() for a in range(n_mid)]
    n_fill = -sum(kept) % SUBLANES
    cand = [v1[a] + v2_all[:kept[a]] for a in range(n_mid)]
    cand_rows = [i1[a] * PEER_NKEYS + i2_all[:kept[a]] for a in range(n_mid)]
    if n_fill:
        cand.append(jnp.full((n_fill, q.shape[0]), -jnp.inf, jnp.float32))
        cand_rows.append(jnp.full((n_fill, q.shape[0]), -1.0, jnp.float32))
    cand.append(jnp.concatenate(v1[n_mid:], axis=0) + v2[0])
    cand_rows.append(jnp.concatenate(i1[n_mid:], axis=0) * PEER_NKEYS + i2[0])
    cand = jnp.concatenate(cand, axis=0)
    cand_rows = jnp.concatenate(cand_rows, axis=0) * float(WORD_ROWS)
    sc, ci = _top_rows(cand, PEER_TOPK)
    iota = lax.broadcasted_iota(jnp.int32, cand.shape, 0).astype(jnp.float32)
    rows = [jnp.max(jnp.where(iota == c, cand_rows, -1.0), axis=0, keepdims=True) for c in ci]
    ex = [jnp.exp(v - sc[0]) for v in sc]
    denom = ex[0]
    for e in ex[1:]:
        denom = denom + e
    return jnp.concatenate(rows, axis=0).astype(jnp.int32), jnp.concatenate(ex, axis=0) / denom


def _peer_route_kernel(h_ref, g_ref, wq_ref, k1_ref, k2_ref, xn_ref, rows_ref, gates_ref,
                       xb_ref, rows_scr, gates_scr):
    grp = pl.program_id(1)

    @pl.when(grp == 0)
    def _():
        x = h_ref[...]
        xn = x * lax.rsqrt(jnp.mean(x * x, axis=-1, keepdims=True) + EPS) * g_ref[...]
        for r in range(SUBLANES):
            xn_ref[pl.ds(r, x.shape[0], stride=SUBLANES), :] = xn[:, r * LANES:(r + 1) * LANES]
        xb_ref[...] = xn.astype(jnp.bfloat16)

    q = jnp.dot(xb_ref[...], wq_ref[...], preferred_element_type=jnp.float32).astype(jnp.bfloat16)
    k1, k2 = k1_ref[...], k2_ref[...]
    for j in range(ROUTE_HEAD_GROUP):
        rows, gates = _route_head(q[:, j * PEER_DQ:(j + 1) * PEER_DQ], k1, k2)
        base = pl.multiple_of((grp * ROUTE_HEAD_GROUP + j) * PEER_TOPK, PEER_TOPK)
        rows_scr[pl.ds(base, PEER_TOPK), :] = rows
        gates_scr[pl.ds(base, PEER_TOPK), :] = gates

    @pl.when(grp == PEER_HEADS // ROUTE_HEAD_GROUP - 1)
    def _():
        for j in range(rows_ref.shape[0]):
            rows_ref[j] = rows_scr[:, j * PEER_TOKEN_BLOCK:(j + 1) * PEER_TOKEN_BLOCK]
        gates_ref[...] = gates_scr[...].T


def _peer_route(h, g_ffn, w_peer_q, keys_1, keys_2):
    T = h.shape[0]
    tb = ROUTE_TOKEN_BLOCK
    per_step = tb // PEER_TOKEN_BLOCK
    half = PEER_DQ // 2
    zeros = jnp.zeros((PEER_NKEYS, half), jnp.bfloat16)
    k1 = jnp.concatenate([keys_1.astype(jnp.bfloat16), zeros], axis=1)
    k2 = jnp.concatenate([zeros, keys_2.astype(jnp.bfloat16)], axis=1)
    keys_spec = pl.BlockSpec((PEER_NKEYS, PEER_DQ), lambda i, g: (0, 0))
    pair_spec = pl.BlockSpec((tb, PEER_PAIRS), lambda i, g: (i, 0))
    return pl.pallas_call(
        _peer_route_kernel,
        out_shape=(jax.ShapeDtypeStruct((T * SUBLANES, LANES), jnp.float32),
                   jax.ShapeDtypeStruct((T // PEER_TOKEN_BLOCK, PEER_PAIRS, PEER_TOKEN_BLOCK), jnp.int32),
                   jax.ShapeDtypeStruct((T, PEER_PAIRS), jnp.float32)),
        grid=(T // tb, PEER_HEADS // ROUTE_HEAD_GROUP),
        in_specs=[pl.BlockSpec((tb, D_MODEL), lambda i, g: (i, 0)),
                  pl.BlockSpec((1, D_MODEL), lambda i, g: (0, 0)),
                  pl.BlockSpec((D_MODEL, ROUTE_HEAD_GROUP * PEER_DQ), lambda i, g: (0, g)),
                  keys_spec, keys_spec],
        out_specs=(pl.BlockSpec((tb * SUBLANES, LANES), lambda i, g: (i, 0)),
                   pl.BlockSpec((per_step, PEER_PAIRS, PEER_TOKEN_BLOCK), lambda i, g: (i, 0, 0)),
                   pair_spec),
        scratch_shapes=[pltpu.VMEM((tb, D_MODEL), jnp.bfloat16),
                        pltpu.VMEM((PEER_PAIRS, tb), jnp.int32),
                        pltpu.VMEM((PEER_PAIRS, tb), jnp.float32)],
        compiler_params=pltpu.CompilerParams(dimension_semantics=("arbitrary", "arbitrary")),
        name="peer_route",
    )(h, g_ffn.reshape(1, D_MODEL), w_peer_q.astype(jnp.bfloat16), k1, k2)


def _pack_table_kernel(tab_ref, out_ref):
    n = tab_ref.shape[0]
    bits = lax.bitcast_convert_type(tab_ref[...].astype(jnp.bfloat16).astype(jnp.float32), jnp.int32)
    for j in range(WORD_ROWS):
        lo = lax.shift_right_logical(bits[:, 2 * j * LANES:(2 * j + 1) * LANES], 16)
        hi = bits[:, (2 * j + 1) * LANES:(2 * j + 2) * LANES] & HIGH_HALF_MASK
        out_ref[pl.ds(j, n, stride=WORD_ROWS), :] = hi | lo


def _pack_expert_table(tab):
    n = PACK_EXPERT_BLOCK
    return pl.pallas_call(
        _pack_table_kernel,
        out_shape=jax.ShapeDtypeStruct((tab.shape[0] * WORD_ROWS, LANES), jnp.int32),
        grid=(tab.shape[0] // n,),
        in_specs=[pl.BlockSpec((n, D_MODEL), lambda i: (i, 0))],
        out_specs=pl.BlockSpec((n * WORD_ROWS, LANES), lambda i: (i, 0)),
        compiler_params=pltpu.CompilerParams(dimension_semantics=("arbitrary",)),
        name="pack_table",
    )(tab)


def _gather_rows(idx_ref, tab_ref, dst_refs, t0):
    for k in range(PEER_PAIRS):
        idx_k = idx_ref.at[k]
        for s, dst_ref in enumerate(dst_refs):
            row = pl.multiple_of(idx_k[t0 + s], WORD_ROWS)
            dst_ref[k * WORD_ROWS:(k + 1) * WORD_ROWS, :] = tab_ref[pl.ds(row, WORD_ROWS), :]


def _expert_token_loop(idx_hbm, idx_bufs, sem, tab_ref, rows_refs, n_tokens, consume):
    i = pl.program_id(0)
    n = pl.num_programs(0)
    n_slots = len(rows_refs)
    assert n_tokens % n_slots == 0

    def copy(block, slot):
        return pltpu.make_async_copy(idx_hbm.at[block], idx_bufs[slot], sem.at[slot])

    def consume_group(t0):
        for s in range(n_slots):
            consume(t0 + s, s)

    @pl.when(i == 0)
    def _():
        copy(0, 0).start()
        copy(0, 0).wait()
        _gather_rows(idx_bufs[0], tab_ref, rows_refs, 0)

    def run(slot):
        has_next = i + 1 < n

        @pl.when(has_next)
        def _():
            copy(i + 1, 1 - slot).start()

        def step(j, carry):
            consume_group(n_slots * j)
            _gather_rows(idx_bufs[slot], tab_ref, rows_refs, n_slots * (j + 1))
            return carry

        lax.fori_loop(0, n_tokens // n_slots - 1, step, 0)

        @pl.when(has_next)
        def _():
            copy(i + 1, 1 - slot).wait()
            consume_group(n_tokens - n_slots)
            _gather_rows(idx_bufs[1 - slot], tab_ref, rows_refs, 0)

        @pl.when(jnp.logical_not(has_next))
        def _():
            consume_group(n_tokens - n_slots)

    for slot in range(2):
        pl.when(lax.rem(i, 2) == slot)(lambda slot=slot: run(slot))


def _diag_mask():
    lane = lax.broadcasted_iota(jnp.int32, (SUBLANES, GATHER_ROWS), 1)
    sub = lax.broadcasted_iota(jnp.int32, (SUBLANES, GATHER_ROWS), 0)
    return (lane & (SUBLANES - 1)) == sub


def _split_bf16(x):
    hi = x.astype(jnp.bfloat16)
    lo = (x - hi.astype(jnp.float32)).astype(jnp.bfloat16)
    return hi, lo


def _peer_hidden_kernel(idx_hbm, x_ref, gate_ref, tab_ref, sel_ref, w_ref, diag_ref, idx_a, idx_b, sem,
                        *rows_refs):
    mask = _diag_mask()

    def consume(t, slot):
        rows = pltpu.bitcast(rows_refs[slot][...], jnp.bfloat16)
        xb = x_ref[t].astype(jnp.bfloat16)
        d = lax.dot_general(xb, rows, NT_DIMS, preferred_element_type=jnp.float32)
        diag_ref[pl.ds(t, 1), :] = jnp.sum(jnp.where(mask, d, 0.0), axis=0, keepdims=True)

    _expert_token_loop(idx_hbm, (idx_a, idx_b), sem, tab_ref, rows_refs, x_ref.shape[0], consume)
    hi, lo = _split_bf16(diag_ref[...])
    sel = sel_ref[...]
    s = (jnp.dot(hi, sel, preferred_element_type=jnp.float32)
         + jnp.dot(lo, sel, preferred_element_type=jnp.float32))
    act = 0.5 * s * (1.0 + lax.erf(s * (1.0 / math.sqrt(2.0))))
    w_ref[...] = gate_ref[...] * act


def _peer_output_kernel(idx_hbm, w_ref, h_ref, tab_ref, expand_ref, out_ref, whi_ref, wlo_ref, idx_a, idx_b, sem,
                        *rows_refs):
    mask = _diag_mask()
    hi, lo = _split_bf16(w_ref[...])
    expand = expand_ref[...]
    whi_ref[...] = jnp.dot(hi, expand, preferred_element_type=jnp.float32)
    wlo_ref[...] = jnp.dot(lo, expand, preferred_element_type=jnp.float32)

    def consume(t, slot):
        rows = pltpu.bitcast(rows_refs[slot][...], jnp.bfloat16)
        parts = []
        for w_part in (whi_ref, wlo_ref):
            w_row = jnp.broadcast_to(w_part[pl.ds(t, 1), :], (SUBLANES, GATHER_ROWS))
            parts.append(jnp.where(mask, w_row, 0.0))
        lhs = jnp.concatenate(parts, axis=0).astype(jnp.bfloat16)
        y = jnp.dot(lhs, rows, preferred_element_type=jnp.float32)
        y = y[:SUBLANES] + y[SUBLANES:]
        y_row = jnp.concatenate([y[r:r + 1, :] for r in range(SUBLANES)], axis=1)
        out_ref[pl.ds(t, 1), :] = h_ref[pl.ds(t, 1), :] + y_row

    _expert_token_loop(idx_hbm, (idx_a, idx_b), sem, tab_ref, rows_refs, h_ref.shape[0], consume)


def _peer_experts(h, xn, rows, gates, peer_u, peer_v):
    T = h.shape[0]
    tb = PEER_TOKEN_BLOCK
    u_tab = _pack_expert_table(peer_u)
    v_tab = _pack_expert_table(peer_v)
    expand = (jnp.arange(PEER_PAIRS)[:, None] == jnp.arange(GATHER_ROWS)[None, :] // SUBLANES
              ).astype(jnp.bfloat16)
    params = pltpu.CompilerParams(dimension_semantics=("arbitrary",), vmem_limit_bytes=PEER_VMEM_LIMIT)
    hbm_ref = pl.BlockSpec(memory_space=pl.ANY)
    index_scratch = [pltpu.SMEM((PEER_PAIRS, tb), jnp.int32)] * 2 + [pltpu.SemaphoreType.DMA((2,))]
    pair_block = pl.BlockSpec((tb, PEER_PAIRS), lambda i: (i, 0))
    vmem_rows = pl.BlockSpec((tb, D_MODEL), lambda i: (i, 0))
    whole_vmem = pl.BlockSpec(memory_space=pltpu.VMEM)
    rows_scratch = pltpu.VMEM((GATHER_WORD_ROWS, LANES), jnp.int32)
    wide_scratch = pltpu.VMEM((tb, GATHER_ROWS), jnp.float32)

    w = pl.pallas_call(
        _peer_hidden_kernel,
        out_shape=jax.ShapeDtypeStruct((T, PEER_PAIRS), jnp.float32),
        grid=(T // tb,),
        in_specs=[hbm_ref, pl.BlockSpec((tb, SUBLANES, LANES), lambda i: (i, 0, 0)), pair_block,
                  whole_vmem, whole_vmem],
        out_specs=pair_block,
        scratch_shapes=[wide_scratch] + index_scratch + [rows_scratch] * HIDDEN_SLOTS,
        compiler_params=params,
        name="peer_hidden",
    )(rows, xn.reshape(T, SUBLANES, LANES), gates, u_tab, expand.T)

    return pl.pallas_call(
        _peer_output_kernel,
        out_shape=jax.ShapeDtypeStruct((T, D_MODEL), jnp.float32),
        grid=(T // tb,),
        in_specs=[hbm_ref, pair_block, vmem_rows, whole_vmem, whole_vmem],
        out_specs=vmem_rows,
        scratch_shapes=[wide_scratch, wide_scratch] + index_scratch + [rows_scratch] * OUTPUT_SLOTS,
        compiler_params=params,
        name="peer_output",
    )(rows, w, h, v_tab, expand)


def kernel(x, meta_tokens, g_mix, w_in, g_q_lora, w_uq, g_kv_lora, w_ukv, g_qk_q, g_qk_k, w_mla_out, g_ret_gn, w_ret_out, w_mix_out, g_ffn, w_peer_q, peer_keys_1, peer_keys_2, peer_u, peer_v):
    B, S, D = x.shape
    h = _mixer(x, meta_tokens, g_mix[0], w_in[0], g_q_lora[0], w_uq[0], g_kv_lora[0], w_ukv[0],
               g_qk_q[0], g_qk_k[0], w_mla_out[0], g_ret_gn[0], w_ret_out[0], w_mix_out[0])
    xn, rows, gates = _peer_route(h, g_ffn[0], w_peer_q[0], peer_keys_1[0], peer_keys_2[0])
    out = _peer_experts(h, xn, rows, gates, peer_u[0], peer_v[0])
    return out.reshape(B, S, D)
```

```python
import math

import jax
import jax.numpy as jnp
import numpy as np
from jax import lax
from jax.experimental import pallas as pl
from jax.experimental.pallas import tpu as pltpu

D_MODEL = 1024
N_META = 16
CHUNK = 128
ROPE_BASE = 10000.0
EPS = 1e-6
MLA_HEADS = 8
MLA_Q_LORA = 384
MLA_KV_LORA = 256
MLA_NOPE = 64
MLA_ROPE = 32
MLA_V = 64
MLA_QK = MLA_NOPE + MLA_ROPE
RET_HEADS = 8
RET_DK = 64
RET_DV = 128
PEER_HEADS = 8
PEER_NKEYS = 128
PEER_DQ = 128
PEER_TOPK = 16
PEER_PAIRS = PEER_HEADS * PEER_TOPK
IN_SIZES = (MLA_Q_LORA, MLA_KV_LORA, MLA_ROPE, RET_HEADS * RET_DK, RET_HEADS * RET_DK,
            RET_HEADS * RET_DV, RET_HEADS * RET_DV, D_MODEL, D_MODEL)

LANES = 128
SUBLANES = 8
WORD_ROWS = D_MODEL // (2 * LANES)
GATHER_WORD_ROWS = PEER_PAIRS * WORD_ROWS
GATHER_ROWS = 2 * GATHER_WORD_ROWS
PEER_TOKEN_BLOCK = LANES
ROUTE_TOKEN_BLOCK = 128
ROUTE_HEAD_GROUP = 8
HIDDEN_SLOTS = 32
OUTPUT_SLOTS = 32
PEER_VMEM_LIMIT = 56 * 1024 * 1024
NT_DIMS = (((1,), (1,)), ((), ()))
HIGH_HALF_MASK = -65536
PACK_EXPERT_BLOCK = 256


PAD_ROWS = (-N_META) % CHUNK
HEAD_LANES = 128
MLA_WIDE = MLA_HEADS * HEAD_LANES
RET_QK_WIDE = RET_HEADS * RET_DK
RET_V_WIDE = RET_HEADS * RET_DV
MASK_VALUE = -1e30
MIXER_VMEM_LIMIT = 48 * 1024 * 1024
ATTN_HEAD_GROUP = 8
ATTN_TILE = 3 * CHUNK
ATTN_KV_TILE = 2 * CHUNK
MIXER_IN_TILES = 3
RET_PAIR_GROUP = 2
_SEG_SIZES = (MLA_Q_LORA, MLA_KV_LORA, HEAD_LANES, RET_QK_WIDE, RET_QK_WIDE, RET_V_WIDE, RET_V_WIDE,
              D_MODEL, D_MODEL)
_SEG_STARTS = tuple(int(v) for v in np.cumsum((0,) + _SEG_SIZES))


def _rope_tables(n_rows, dim, block_offset, repeat):
    half = dim // 2
    pos = jnp.arange(n_rows, dtype=jnp.float32) - PAD_ROWS
    inv = ROPE_BASE ** (-jnp.arange(half, dtype=jnp.float32) / half)
    lane = np.arange(LANES)
    rel = (lane - block_offset) % repeat
    in_rope = (lane >= block_offset) & (rel < dim)
    ang = pos[:, None] * inv[rel % half][None, :]
    cos = jnp.where(in_rope[None, :], jnp.cos(ang), 1.0)
    sin = jnp.sin(ang)
    s1 = jnp.where((in_rope & (rel < half))[None, :], -sin, 0.0)
    s2 = jnp.where((in_rope & (rel >= half))[None, :], sin, 0.0)
    return cos, s1, s2


def _apply_rope(x, cos, s1, s2, half):
    return x * cos + pltpu.roll(x, LANES - half, 1) * s1 + pltpu.roll(x, half, 1) * s2


def _mixer_in_kernel(h0_ref, h1_ref, h2_ref, head_ref, gmix_ref, win_ref, gq_ref, wuq_ref, gkv_ref, wk_ref, wv_ref,
                     gqkq_ref, gqkk_ref, mc_ref, ms1_ref, ms2_ref, rc_ref, rs1_ref, rs2_ref,
                     q_ref, k_ref, v_ref, rq_ref, rk_ref, rv_ref, sg_ref, ga_ref, gr_ref):
    first = jnp.where(pl.program_id(1) == 0, head_ref[...], h0_ref[...])
    x = jnp.concatenate([first, h1_ref[...], h2_ref[...]], axis=0)
    hn = (x * lax.rsqrt(jnp.mean(x * x, axis=-1, keepdims=True) + EPS) * gmix_ref[...]).astype(jnp.bfloat16)

    def seg(i):
        return jnp.dot(hn, win_ref[:, _SEG_STARTS[i]:_SEG_STARTS[i + 1]], preferred_element_type=jnp.float32)

    def lora_norm(c, g_ref):
        return (c * lax.rsqrt(jnp.mean(c * c, axis=-1, keepdims=True) + EPS) * g_ref[...]).astype(jnp.bfloat16)

    cq = lora_norm(seg(0), gq_ref)
    ckv = lora_norm(seg(1), gkv_ref)
    kpe = seg(2)
    q = jnp.dot(cq, wuq_ref[...], preferred_element_type=jnp.float32)
    kn = jnp.dot(ckv, wk_ref[...], preferred_element_type=jnp.float32)
    v_ref[...] = jnp.dot(ckv, wv_ref[...], preferred_element_type=jnp.float32).astype(jnp.bfloat16)
    mc, ms1, ms2 = mc_ref[...], ms1_ref[...], ms2_ref[...]
    scale = math.log2(math.e) / math.sqrt(MLA_QK)

    def qk_norm_rope(t, g_ref):
        t = t * lax.rsqrt(jnp.sum(t * t, axis=-1, keepdims=True) * (1.0 / MLA_QK) + EPS) * g_ref[...]
        return _apply_rope(t, mc, ms1, ms2, MLA_ROPE // 2)

    for hd in range(MLA_HEADS):
        blk = slice(hd * HEAD_LANES, (hd + 1) * HEAD_LANES)
        q_ref[:, blk] = (qk_norm_rope(q[:, blk], gqkq_ref) * scale).astype(jnp.bfloat16)
        k_ref[:, blk] = qk_norm_rope(kn[:, blk] + kpe, gqkk_ref).astype(jnp.bfloat16)

    rc, rs1, rs2 = rc_ref[...], rs1_ref[...], rs2_ref[...]
    rq = seg(3)
    rk = seg(4)
    for j in range(RET_QK_WIDE // LANES):
        blk = slice(j * LANES, (j + 1) * LANES)
        rq_ref[:, blk] = _apply_rope(rq[:, blk], rc, rs1, rs2, RET_DK // 2).astype(jnp.bfloat16)
        rk_ref[:, blk] = (_apply_rope(rk[:, blk], rc, rs1, rs2, RET_DK // 2) * (RET_DK ** -0.5)).astype(jnp.bfloat16)
    rv_ref[...] = seg(5).astype(jnp.bfloat16)
    gate = seg(6)
    sg_ref[...] = (gate * jax.nn.sigmoid(gate)).astype(jnp.bfloat16)
    ga_ref[...] = jax.nn.sigmoid(seg(7)).astype(jnp.bfloat16)
    gr_ref[...] = jax.nn.sigmoid(seg(8)).astype(jnp.bfloat16)


def _attention_kernel(q_ref, k_ref, v_ref, o_ref, m_ref, l_ref, acc_ref):
    qi = pl.program_id(2)
    t = ATTN_TILE
    n_heads = q_ref.shape[1] // HEAD_LANES
    m_ref[...] = jnp.full(m_ref.shape, MASK_VALUE, jnp.float32)
    l_ref[...] = jnp.zeros(l_ref.shape, jnp.float32)
    acc_ref[...] = jnp.zeros(acc_ref.shape, jnp.float32)

    def lane_tiles(x):
        return [x[:, i * LANES:(i + 1) * LANES] for i in range(x.shape[1] // LANES)]

    def scores(start, width, hd, masked):
        blk = slice(hd * HEAD_LANES, (hd + 1) * HEAD_LANES)
        s = lax.dot_general(q_ref[:, blk], k_ref[pl.ds(start, width), blk], NT_DIMS,
                            preferred_element_type=jnp.float32)
        if masked:
            r = lax.broadcasted_iota(jnp.int32, (t, width), 0) + qi * t
            c = lax.broadcasted_iota(jnp.int32, (t, width), 1) + start
            s = jnp.where((c <= r) & ((c >= PAD_ROWS) | (r < PAD_ROWS)), s, MASK_VALUE)
        return s

    def max_step(start, width, masked):
        for hd in range(n_heads):
            m = m_ref[hd]
            for part in lane_tiles(scores(start, width, hd, masked)):
                m = jnp.maximum(m, part)
            m_ref[hd] = m

    def sum_step(start, width, masked):
        for hd in range(n_heads):
            blk = slice(hd * HEAD_LANES, (hd + 1) * HEAD_LANES)
            p = jnp.exp2(scores(start, width, hd, masked) - jnp.tile(m_ref[hd], (1, width // LANES)))
            l = l_ref[hd]
            for part in lane_tiles(p):
                l = l + part
            l_ref[hd] = l
            acc_ref[hd] += jnp.dot(p.astype(jnp.bfloat16), v_ref[pl.ds(start, width), blk],
                                   preferred_element_type=jnp.float32)

    n_cols = (qi + 1) * t
    n_full = n_cols // ATTN_KV_TILE

    def sweep(step):
        def body(j, carry):
            start = pl.multiple_of(j * ATTN_KV_TILE, ATTN_KV_TILE)
            edge = (j == 0) | ((j + 1) * ATTN_KV_TILE > qi * t)

            @pl.when(edge)
            def _():
                step(start, ATTN_KV_TILE, True)

            @pl.when(jnp.logical_not(edge))
            def _():
                step(start, ATTN_KV_TILE, False)

            return carry

        lax.fori_loop(0, n_full, body, 0)

        @pl.when(n_full * ATTN_KV_TILE < n_cols)
        def _():
            step(pl.multiple_of(n_full * ATTN_KV_TILE, CHUNK), CHUNK, True)

    sweep(max_step)
    for hd in range(n_heads):
        m_ref[hd] = jnp.broadcast_to(jnp.max(m_ref[hd], axis=-1, keepdims=True), (t, LANES))
    sweep(sum_step)
    for hd in range(n_heads):
        blk = slice(hd * HEAD_LANES, (hd + 1) * HEAD_LANES)
        o_ref[:, blk] = (acc_ref[hd] / jnp.sum(l_ref[hd], axis=-1, keepdims=True)).astype(jnp.bfloat16)


def _retention_kernel(q_ref, k_ref, v_ref, sg_ref, gn_ref, decay_ref, zeta_ref, xi_ref, gamma_ref, o_ref):
    n_chunks = q_ref.shape[0] // CHUNK
    n_pairs = q_ref.shape[1] // LANES
    lane = lax.broadcasted_iota(jnp.int32, (CHUNK, LANES), 1)
    row = lax.broadcasted_iota(jnp.int32, (LANES, RET_DV), 0)
    is_a = lane < RET_DK
    gn = gn_ref[...]
    contract_rows = (((0,), (0,)), ((), ()))

    def group_norm(y):
        mu = jnp.mean(y, axis=-1, keepdims=True)
        d = y - mu
        return d * lax.rsqrt(jnp.mean(d * d, axis=-1, keepdims=True) + EPS)

    def pair_chunk(rows, p, state):
        qb = q_ref[rows, p * LANES:(p + 1) * LANES]
        kb = k_ref[rows, p * LANES:(p + 1) * LANES]
        state_bf = state.astype(jnp.bfloat16)
        new_parts = []
        for hd in range(2):
            head = 2 * p + hd
            cols = slice(head * RET_DV, (head + 1) * RET_DV)
            qh = jnp.where(is_a if hd == 0 else ~is_a, qb, jnp.zeros_like(qb))
            vh = v_ref[rows, cols]
            scores = lax.dot_general(qh, kb, NT_DIMS, preferred_element_type=jnp.float32) * decay_ref[head]
            inner = jnp.dot(scores.astype(jnp.bfloat16), vh, preferred_element_type=jnp.float32)
            cross = jnp.dot(qh, state_bf, preferred_element_type=jnp.float32) * xi_ref[head]
            yn = group_norm(inner + cross) * gn[:, cols]
            o_ref[rows, cols] = (sg_ref[rows, cols].astype(jnp.float32) * yn).astype(jnp.bfloat16)
            kz = (kb.astype(jnp.float32) * zeta_ref[head]).astype(jnp.bfloat16)
            new_parts.append(lax.dot_general(kz, vh, contract_rows, preferred_element_type=jnp.float32))
        return gamma_ref[p] * state + jnp.where(row < RET_DK, new_parts[0], new_parts[1])

    def chunk(n, states):
        rows = pl.ds(pl.multiple_of(n * CHUNK, CHUNK), CHUNK)
        return tuple(pair_chunk(rows, p, state) for p, state in enumerate(states))

    lax.fori_loop(0, n_chunks, chunk, tuple(jnp.zeros((LANES, RET_DV), jnp.float32) for _ in range(n_pairs)))


def _mixer_out_kernel(h_ref, o_ref, yr_ref, ga_ref, gr_ref, wa_ref, wr_ref, wm_ref, out_ref):
    y_a = jnp.dot(o_ref[...], wa_ref[...], preferred_element_type=jnp.float32)
    y_r = jnp.dot(yr_ref[...], wr_ref[...], preferred_element_type=jnp.float32)
    merged = ga_ref[...].astype(jnp.float32) * y_a + gr_ref[...].astype(jnp.float32) * y_r
    out_ref[...] = h_ref[...] + jnp.dot(merged.astype(jnp.bfloat16), wm_ref[...],
                                        preferred_element_type=jnp.float32)


def _pad_heads(w, heads, width):
    lead = w.shape[:-1]
    w = w.reshape(lead + (heads, width))
    w = jnp.pad(w, [(0, 0)] * len(lead) + [(0, 0), (0, HEAD_LANES - width)])
    return w.reshape(lead + (heads * HEAD_LANES,))


def _mixer(x, meta_tokens, g_mix, w_in, g_q_lora, w_uq, g_kv_lora, w_ukv, g_qk_q, g_qk_k,
           w_mla_out, g_ret_gn, w_ret_out, w_mix_out):
    B, S, D = x.shape
    bf = jnp.bfloat16
    rows = PAD_ROWS + N_META + S
    nt = rows // CHUNK
    seq_tiles = S // CHUNK
    assert PAD_ROWS + N_META == CHUNK and S % CHUNK == 0
    x2 = x.reshape(B * S, D)
    head_tile = jnp.concatenate([jnp.zeros((PAD_ROWS, D), x.dtype), meta_tokens.astype(x.dtype)], axis=0)

    segs = jnp.split(w_in, np.cumsum(IN_SIZES)[:-1].tolist(), axis=1)
    kpe_cols = jnp.pad(segs[2], ((0, 0), (MLA_NOPE, HEAD_LANES - MLA_NOPE - MLA_ROPE)))
    win = jnp.concatenate(segs[:2] + [kpe_cols] + segs[3:], axis=1).astype(bf)
    wuq = _pad_heads(w_uq, MLA_HEADS, MLA_QK).astype(bf)
    wkv = w_ukv.reshape(MLA_KV_LORA, MLA_HEADS, MLA_NOPE + MLA_V)
    wk = _pad_heads(wkv[:, :, :MLA_NOPE].reshape(MLA_KV_LORA, -1), MLA_HEADS, MLA_NOPE).astype(bf)
    wv = _pad_heads(wkv[:, :, MLA_NOPE:].reshape(MLA_KV_LORA, -1), MLA_HEADS, MLA_V).astype(bf)
    gqkq = jnp.pad(g_qk_q, (0, HEAD_LANES - MLA_QK)).reshape(1, HEAD_LANES)
    gqkk = jnp.pad(g_qk_k, (0, HEAD_LANES - MLA_QK)).reshape(1, HEAD_LANES)
    wa = _pad_heads(w_mla_out.T, MLA_HEADS, MLA_V).T.astype(bf)
    mla_tabs = _rope_tables(rows, MLA_ROPE, MLA_NOPE, HEAD_LANES)
    ret_tabs = _rope_tables(rows, RET_DK, 0, RET_DK)

    m = MIXER_IN_TILES
    assert nt % m == 0 and m == 3
    row_tile = lambda width: pl.BlockSpec((m * CHUNK, width), lambda b, i: (b * (nt // m) + i, 0))
    seq_in = lambda j: pl.BlockSpec((CHUNK, D), lambda b, i: (b * seq_tiles + jnp.maximum(m * i + j - 1, 0), 0))
    const = lambda shape: pl.BlockSpec(shape, lambda b, i: (0,) * len(shape))
    tab = pl.BlockSpec((m * CHUNK, LANES), lambda b, i: (i, 0))
    n_rows = B * rows
    widths = (MLA_WIDE, MLA_WIDE, MLA_WIDE, RET_QK_WIDE, RET_QK_WIDE, RET_V_WIDE, RET_V_WIDE, D, D)
    q, k, v, rq, rk, rv, sg, ga, gr = pl.pallas_call(
        _mixer_in_kernel,
        out_shape=tuple(jax.ShapeDtypeStruct((n_rows, w), bf) for w in widths),
        grid=(B, nt // m),
        in_specs=[seq_in(0), seq_in(1), seq_in(2), const((CHUNK, D)), const((1, D)), const(win.shape),
                  const((1, MLA_Q_LORA)), const(wuq.shape), const((1, MLA_KV_LORA)), const(wk.shape), const(wv.shape), const((1, HEAD_LANES)),
                  const((1, HEAD_LANES))] + [tab] * 6,
        out_specs=tuple(row_tile(w) for w in widths),
        compiler_params=pltpu.CompilerParams(dimension_semantics=("arbitrary", "arbitrary"),
                                             vmem_limit_bytes=MIXER_VMEM_LIMIT),
        name="mixer_in",
    )(x2, x2, x2, head_tile, g_mix.reshape(1, D), win, g_q_lora.reshape(1, -1), wuq, g_kv_lora.reshape(1, -1), wk, wv,
      gqkq, gqkk, *mla_tabs, *ret_tabs)

    group = ATTN_HEAD_GROUP * HEAD_LANES
    seq_spec = pl.BlockSpec((rows, group), lambda b, g, i: (b, g))
    n_attn = rows // ATTN_TILE
    tile_spec = pl.BlockSpec((ATTN_TILE, group), lambda b, g, i: (b * n_attn + i, g))
    o = pl.pallas_call(
        _attention_kernel,
        out_shape=jax.ShapeDtypeStruct((n_rows, MLA_WIDE), bf),
        grid=(B, MLA_HEADS // ATTN_HEAD_GROUP, n_attn),
        in_specs=[tile_spec, seq_spec, seq_spec],
        out_specs=tile_spec,
        scratch_shapes=[pltpu.VMEM((ATTN_HEAD_GROUP, ATTN_TILE, HEAD_LANES), jnp.float32)] * 3,
        compiler_params=pltpu.CompilerParams(dimension_semantics=("arbitrary",) * 3,
                                             vmem_limit_bytes=MIXER_VMEM_LIMIT),
        name="mla_attention",
    )(q, k, v)

    log_gamma = jnp.log(1.0 - 2.0 ** (-5.0 - jnp.arange(RET_HEADS, dtype=jnp.float32)))
    idx = jnp.arange(CHUNK, dtype=jnp.float32)
    diff = idx[:, None] - idx[None, :]
    decay = jnp.where(diff[None] >= 0, jnp.exp(jnp.maximum(diff, 0.0)[None] * log_gamma[:, None, None]), 0.0)
    zeta = jnp.exp((CHUNK - 1 - idx)[None, :] * log_gamma[:, None])
    xi = jnp.exp((idx + 1)[None, :] * log_gamma[:, None])
    gamma_c = jnp.exp(CHUNK * log_gamma)
    zeta_b = jnp.broadcast_to(zeta[:, :, None], (RET_HEADS, CHUNK, LANES))
    xi_b = jnp.broadcast_to(xi[:, :, None], (RET_HEADS, CHUNK, RET_DV))
    gamma_rows = jnp.broadcast_to(jnp.repeat(gamma_c, RET_DK).reshape(RET_HEADS // 2, LANES, 1),
                                  (RET_HEADS // 2, LANES, RET_DV))
    g = RET_PAIR_GROUP
    pair_seq = lambda width: pl.BlockSpec((rows, g * width), lambda b, p: (b, p))
    pair_tab = lambda d1, d2: pl.BlockSpec((2 * g, d1, d2), lambda b, p: (p, 0, 0))
    yr = pl.pallas_call(
        _retention_kernel,
        out_shape=jax.ShapeDtypeStruct((n_rows, RET_V_WIDE), bf),
        grid=(B, RET_HEADS // (2 * g)),
        in_specs=[pair_seq(LANES), pair_seq(LANES), pair_seq(2 * RET_DV), pair_seq(2 * RET_DV),
                  pl.BlockSpec((1, g * 2 * RET_DV), lambda b, p: (0, p)),
                  pair_tab(CHUNK, CHUNK), pair_tab(CHUNK, LANES), pair_tab(CHUNK, RET_DV),
                  pl.BlockSpec((g, LANES, RET_DV), lambda b, p: (p, 0, 0))],
        out_specs=pair_seq(2 * RET_DV),
        compiler_params=pltpu.CompilerParams(dimension_semantics=("arbitrary", "arbitrary"),
                                             vmem_limit_bytes=MIXER_VMEM_LIMIT),
        name="retention",
    )(rq, rk, rv, sg, g_ret_gn.reshape(1, -1), decay, zeta_b, xi_b, gamma_rows)

    in_tile = lambda width: pl.BlockSpec((CHUNK, width), lambda b, i: (b * nt + 1 + i, 0))
    seq_tile = pl.BlockSpec((CHUNK, D), lambda b, i: (b * seq_tiles + i, 0))
    return pl.pallas_call(
        _mixer_out_kernel,
        out_shape=jax.ShapeDtypeStruct((B * S, D), jnp.float32),
        grid=(B, seq_tiles),
        in_specs=[seq_tile, in_tile(MLA_WIDE), in_tile(RET_V_WIDE), in_tile(D), in_tile(D),
                  const(wa.shape), const(w_ret_out.shape), const(w_mix_out.shape)],
        out_specs=seq_tile,
        compiler_params=pltpu.CompilerParams(dimension_semantics=("arbitrary", "arbitrary"),
                                             vmem_limit_bytes=MIXER_VMEM_LIMIT),
        name="mixer_out",
    )(x2, o, yr, ga, gr, wa, w_ret_out.astype(bf), w_mix_out.astype(bf))


def _top_rows(s, count):
    n = s.shape[0]
    iota = lax.broadcasted_iota(jnp.int32, s.shape, 0).astype(jnp.float32)
    vals, idxs = [], []
    for _ in range(count):
        m = jnp.max(s, axis=0, keepdims=True)
        idx = jnp.min(jnp.where(s == m, iota, float(n)), axis=0, keepdims=True)
        vals.append(m)
        idxs.append(idx)
        s = jnp.where(iota == idx, -jnp.inf, s)
    return vals, idxs


def _route_head(q, k1, k2):
    s1 = lax.dot_general(k1, q, NT_DIMS, preferred_element_type=jnp.float32)
    s2 = lax.dot_general(k2, q, NT_DIMS, preferred_element_type=jnp.float32)
    v1, i1 = _top_rows(s1, PEER_TOPK)
    v2, i2 = _top_rows(s2, PEER_TOPK)
    v2_all = jnp.concatenate(v2, axis=0)
    i2_all = jnp.concatenate(i2, axis=0)
    n_mid = SUBLANES
    kept = [1 << (PEER_TOPK // (a + 1) - 1).bit_length() for a in range(n_mid)]
    n_fill = -sum(kept) % SUBLANES
    cand = [v1[a] + v2_all[:kept[a]] for a in range(n_mid)]
    cand_rows = [i1[a] * PEER_NKEYS + i2_all[:kept[a]] for a in range(n_mid)]
    if n_fill:
        cand.append(jnp.full((n_fill, q.shape[0]), -jnp.inf, jnp.float32))
        cand_rows.append(jnp.full((n_fill, q.shape[0]), -1.0, jnp.float32))
    cand.append(jnp.concatenate(v1[n_mid:], axis=0) + v2[0])
    cand_rows.append(jnp.concatenate(i1[n_mid:], axis=0) * PEER_NKEYS + i2[0])
    cand = jnp.concatenate(cand, axis=0)
    cand_rows = jnp.concatenate(cand_rows, axis=0) * float(WORD_ROWS)
    sc, ci = _top_rows(cand, PEER_TOPK)
    iota = lax.broadcasted_iota(jnp.int32, cand.shape, 0).astype(jnp.float32)
    rows = [jnp.max(jnp.where(iota == c, cand_rows, -1.0), axis=0, keepdims=True) for c in ci]
    ex = [jnp.exp(v - sc[0]) for v in sc]
    denom = ex[0]
    for e in ex[1:]:
        denom = denom + e
    return jnp.concatenate(rows, axis=0).astype(jnp.int32), jnp.concatenate(ex, axis=0) / denom


def _peer_route_kernel(h_ref, g_ref, wq_ref, k1_ref, k2_ref, xn_ref, rows_ref, gates_ref,
                       xb_ref, rows_scr, gates_scr):
    grp = pl.program_id(1)

    @pl.when(grp == 0)
    def _():
        x = h_ref[...]
        xn = x * lax.rsqrt(jnp.mean(x * x, axis=-1, keepdims=True) + EPS) * g_ref[...]
        for r in range(SUBLANES):
            xn_ref[pl.ds(r, x.shape[0], stride=SUBLANES), :] = xn[:, r * LANES:(r + 1) * LANES]
        xb_ref[...] = xn.astype(jnp.bfloat16)

    q = jnp.dot(xb_ref[...], wq_ref[...], preferred_element_type=jnp.float32).astype(jnp.bfloat16)
    k1, k2 = k1_ref[...], k2_ref[...]
    for j in range(ROUTE_HEAD_GROUP):
        rows, gates = _route_head(q[:, j * PEER_DQ:(j + 1) * PEER_DQ], k1, k2)
        base = pl.multiple_of((grp * ROUTE_HEAD_GROUP + j) * PEER_TOPK, PEER_TOPK)
        rows_scr[pl.ds(base, PEER_TOPK), :] = rows
        gates_scr[pl.ds(base, PEER_TOPK), :] = gates

    @pl.when(grp == PEER_HEADS // ROUTE_HEAD_GROUP - 1)
    def _():
        for j in range(rows_ref.shape[0]):
            rows_ref[j] = rows_scr[:, j * PEER_TOKEN_BLOCK:(j + 1) * PEER_TOKEN_BLOCK]
        gates_ref[...] = gates_scr[...].T


def _peer_route(h, g_ffn, w_peer_q, keys_1, keys_2):
    T = h.shape[0]
    tb = ROUTE_TOKEN_BLOCK
    per_step = tb // PEER_TOKEN_BLOCK
    half = PEER_DQ // 2
    zeros = jnp.zeros((PEER_NKEYS, half), jnp.bfloat16)
    k1 = jnp.concatenate([keys_1.astype(jnp.bfloat16), zeros], axis=1)
    k2 = jnp.concatenate([zeros, keys_2.astype(jnp.bfloat16)], axis=1)
    keys_spec = pl.BlockSpec((PEER_NKEYS, PEER_DQ), lambda i, g: (0, 0))
    pair_spec = pl.BlockSpec((tb, PEER_PAIRS), lambda i, g: (i, 0))
    return pl.pallas_call(
        _peer_route_kernel,
        out_shape=(jax.ShapeDtypeStruct((T * SUBLANES, LANES), jnp.float32),
                   jax.ShapeDtypeStruct((T // PEER_TOKEN_BLOCK, PEER_PAIRS, PEER_TOKEN_BLOCK), jnp.int32),
                   jax.ShapeDtypeStruct((T, PEER_PAIRS), jnp.float32)),
        grid=(T // tb, PEER_HEADS // ROUTE_HEAD_GROUP),
        in_specs=[pl.BlockSpec((tb, D_MODEL), lambda i, g: (i, 0)),
                  pl.BlockSpec((1, D_MODEL), lambda i, g: (0, 0)),
                  pl.BlockSpec((D_MODEL, ROUTE_HEAD_GROUP * PEER_DQ), lambda i, g: (0, g)),
                  keys_spec, keys_spec],
        out_specs=(pl.BlockSpec((tb * SUBLANES, LANES), lambda i, g: (i, 0)),
                   pl.BlockSpec((per_step, PEER_PAIRS, PEER_TOKEN_BLOCK), lambda i, g: (i, 0, 0)),
                   pair_spec),
        scratch_shapes=[pltpu.VMEM((tb, D_MODEL), jnp.bfloat16),
                        pltpu.VMEM((PEER_PAIRS, tb), jnp.int32),
                        pltpu.VMEM((PEER_PAIRS, tb), jnp.float32)],
        compiler_params=pltpu.CompilerParams(dimension_semantics=("arbitrary", "arbitrary")),
        name="peer_route",
    )(h, g_ffn.reshape(1, D_MODEL), w_peer_q.astype(jnp.bfloat16), k1, k2)


def _pack_table_kernel(tab_ref, out_ref):
    n = tab_ref.shape[0]
    bits = lax.bitcast_convert_type(tab_ref[...].astype(jnp.bfloat16).astype(jnp.float32), jnp.int32)
    for j in range(WORD_ROWS):
        lo = lax.shift_right_logical(bits[:, 2 * j * LANES:(2 * j + 1) * LANES], 16)
        hi = bits[:, (2 * j + 1) * LANES:(2 * j + 2) * LANES] & HIGH_HALF_MASK
        out_ref[pl.ds(j, n, stride=WORD_ROWS), :] = hi | lo


def _pack_expert_table(tab):
    n = PACK_EXPERT_BLOCK
    return pl.pallas_call(
        _pack_table_kernel,
        out_shape=jax.ShapeDtypeStruct((tab.shape[0] * WORD_ROWS, LANES), jnp.int32),
        grid=(tab.shape[0] // n,),
        in_specs=[pl.BlockSpec((n, D_MODEL), lambda i: (i, 0))],
        out_specs=pl.BlockSpec((n * WORD_ROWS, LANES), lambda i: (i, 0)),
        compiler_params=pltpu.CompilerParams(dimension_semantics=("arbitrary",)),
        name="pack_table",
    )(tab)


def _gather_rows(idx_ref, tab_ref, dst_refs, t0):
    for k in range(PEER_PAIRS):
        idx_k = idx_ref.at[k]
        for s, dst_ref in enumerate(dst_refs):
            row = pl.multiple_of(idx_k[t0 + s], WORD_ROWS)
            dst_ref[k * WORD_ROWS:(k + 1) * WORD_ROWS, :] = tab_ref[pl.ds(row, WORD_ROWS), :]


def _expert_token_loop(idx_hbm, idx_bufs, sem, tab_ref, rows_refs, n_tokens, consume):
    i = pl.program_id(0)
    n = pl.num_programs(0)
    n_slots = len(rows_refs)
    assert n_tokens % n_slots == 0

    def copy(block, slot):
        return pltpu.make_async_copy(idx_hbm.at[block], idx_bufs[slot], sem.at[slot])

    def consume_group(t0):
        for s in range(n_slots):
            consume(t0 + s, s)

    @pl.when(i == 0)
    def _():
        copy(0, 0).start()
        copy(0, 0).wait()
        _gather_rows(idx_bufs[0], tab_ref, rows_refs, 0)

    def run(slot):
        has_next = i + 1 < n

        @pl.when(has_next)
        def _():
            copy(i + 1, 1 - slot).start()

        def step(j, carry):
            consume_group(n_slots * j)
            _gather_rows(idx_bufs[slot], tab_ref, rows_refs, n_slots * (j + 1))
            return carry

        lax.fori_loop(0, n_tokens // n_slots - 1, step, 0)

        @pl.when(has_next)
        def _():
            copy(i + 1, 1 - slot).wait()
            consume_group(n_tokens - n_slots)
            _gather_rows(idx_bufs[1 - slot], tab_ref, rows_refs, 0)

        @pl.when(jnp.logical_not(has_next))
        def _():
            consume_group(n_tokens - n_slots)

    for slot in range(2):
        pl.when(lax.rem(i, 2) == slot)(lambda slot=slot: run(slot))


def _diag_mask():
    lane = lax.broadcasted_iota(jnp.int32, (SUBLANES, GATHER_ROWS), 1)
    sub = lax.broadcasted_iota(jnp.int32, (SUBLANES, GATHER_ROWS), 0)
    return (lane & (SUBLANES - 1)) == sub


def _split_bf16(x):
    hi = x.astype(jnp.bfloat16)
    lo = (x - hi.astype(jnp.float32)).astype(jnp.bfloat16)
    return hi, lo


def _peer_hidden_kernel(idx_hbm, x_ref, gate_ref, tab_ref, sel_ref, w_ref, diag_ref, idx_a, idx_b, sem,
                        *rows_refs):
    mask = _diag_mask()

    def consume(t, slot):
        rows = pltpu.bitcast(rows_refs[slot][...], jnp.bfloat16)
        xb = x_ref[t].astype(jnp.bfloat16)
        d = lax.dot_general(xb, rows, NT_DIMS, preferred_element_type=jnp.float32)
        diag_ref[pl.ds(t, 1), :] = jnp.sum(jnp.where(mask, d, 0.0), axis=0, keepdims=True)

    _expert_token_loop(idx_hbm, (idx_a, idx_b), sem, tab_ref, rows_refs, x_ref.shape[0], consume)
    hi, lo = _split_bf16(diag_ref[...])
    sel = sel_ref[...]
    s = (jnp.dot(hi, sel, preferred_element_type=jnp.float32)
         + jnp.dot(lo, sel, preferred_element_type=jnp.float32))
    act = 0.5 * s * (1.0 + lax.erf(s * (1.0 / math.sqrt(2.0))))
    w_ref[...] = gate_ref[...] * act


def _peer_output_kernel(idx_hbm, w_ref, h_ref, tab_ref, expand_ref, out_ref, whi_ref, wlo_ref, idx_a, idx_b, sem,
                        *rows_refs):
    mask = _diag_mask()
    hi, lo = _split_bf16(w_ref[...])
    expand = expand_ref[...]
    whi_ref[...] = jnp.dot(hi, expand, preferred_element_type=jnp.float32)
    wlo_ref[...] = jnp.dot(lo, expand, preferred_element_type=jnp.float32)

    def consume(t, slot):
        rows = pltpu.bitcast(rows_refs[slot][...], jnp.bfloat16)
        parts = []
        for w_part in (whi_ref, wlo_ref):
            w_row = jnp.broadcast_to(w_part[pl.ds(t, 1), :], (SUBLANES, GATHER_ROWS))
            parts.append(jnp.where(mask, w_row, 0.0))
        lhs = jnp.concatenate(parts, axis=0).astype(jnp.bfloat16)
        y = jnp.dot(lhs, rows, preferred_element_type=jnp.float32)
        y = y[:SUBLANES] + y[SUBLANES:]
        y_row = jnp.concatenate([y[r:r + 1, :] for r in range(SUBLANES)], axis=1)
        out_ref[pl.ds(t, 1), :] = h_ref[pl.ds(t, 1), :] + y_row

    _expert_token_loop(idx_hbm, (idx_a, idx_b), sem, tab_ref, rows_refs, h_ref.shape[0], consume)


def _peer_experts(h, xn, rows, gates, peer_u, peer_v):
    T = h.shape[0]
    tb = PEER_TOKEN_BLOCK
    u_tab = _pack_expert_table(peer_u)
    v_tab = _pack_expert_table(peer_v)
    expand = (jnp.arange(PEER_PAIRS)[:, None] == jnp.arange(GATHER_ROWS)[None, :] // SUBLANES
              ).astype(jnp.bfloat16)
    params = pltpu.CompilerParams(dimension_semantics=("arbitrary",), vmem_limit_bytes=PEER_VMEM_LIMIT)
    hbm_ref = pl.BlockSpec(memory_space=pl.ANY)
    index_scratch = [pltpu.SMEM((PEER_PAIRS, tb), jnp.int32)] * 2 + [pltpu.SemaphoreType.DMA((2,))]
    pair_block = pl.BlockSpec((tb, PEER_PAIRS), lambda i: (i, 0))
    vmem_rows = pl.BlockSpec((tb, D_MODEL), lambda i: (i, 0))
    whole_vmem = pl.BlockSpec(memory_space=pltpu.VMEM)
    rows_scratch = pltpu.VMEM((GATHER_WORD_ROWS, LANES), jnp.int32)
    wide_scratch = pltpu.VMEM((tb, GATHER_ROWS), jnp.float32)

    w = pl.pallas_call(
        _peer_hidden_kernel,
        out_shape=jax.ShapeDtypeStruct((T, PEER_PAIRS), jnp.float32),
        grid=(T // tb,),
        in_specs=[hbm_ref, pl.BlockSpec((tb, SUBLANES, LANES), lambda i: (i, 0, 0)), pair_block,
                  whole_vmem, whole_vmem],
        out_specs=pair_block,
        scratch_shapes=[wide_scratch] + index_scratch + [rows_scratch] * HIDDEN_SLOTS,
        compiler_params=params,
        name="peer_hidden",
    )(rows, xn.reshape(T, SUBLANES, LANES), gates, u_tab, expand.T)

    return pl.pallas_call(
        _peer_output_kernel,
        out_shape=jax.ShapeDtypeStruct((T, D_MODEL), jnp.float32),
        grid=(T // tb,),
        in_specs=[hbm_ref, pair_block, vmem_rows, whole_vmem, whole_vmem],
        out_specs=vmem_rows,
        scratch_shapes=[wide_scratch, wide_scratch] + index_scratch + [rows_scratch] * OUTPUT_SLOTS,
        compiler_params=params,
        name="peer_output",
    )(rows, w, h, v_tab, expand)


def kernel(x, meta_tokens, g_mix, w_in, g_q_lora, w_uq, g_kv_lora, w_ukv, g_qk_q, g_qk_k, w_mla_out, g_ret_gn, w_ret_out, w_mix_out, g_ffn, w_peer_q, peer_keys_1, peer_keys_2, peer_u, peer_v):
    B, S, D = x.shape
    h = _mixer(x, meta_tokens, g_mix[0], w_in[0], g_q_lora[0], w_uq[0], g_kv_lora[0], w_ukv[0],
               g_qk_q[0], g_qk_k[0], w_mla_out[0], g_ret_gn[0], w_ret_out[0], w_mix_out[0])
    xn, rows, gates = _peer_route(h, g_ffn[0], w_peer_q[0], peer_keys_1[0], peer_keys_2[0])
    out = _peer_experts(h, xn, rows, gates, peer_u[0], peer_v[0])
    return out.reshape(B, S, D)
```

```python
import math

import jax
import jax.numpy as jnp
import numpy as np
from jax import lax
from jax.experimental import pallas as pl
from jax.experimental.pallas import tpu as pltpu

D_MODEL = 1024
N_META = 16
CHUNK = 128
ROPE_BASE = 10000.0
EPS = 1e-6
MLA_HEADS = 8
MLA_Q_LORA = 384
MLA_KV_LORA = 256
MLA_NOPE = 64
MLA_ROPE = 32
MLA_V = 64
MLA_QK = MLA_NOPE + MLA_ROPE
RET_HEADS = 8
RET_DK = 64
RET_DV = 128
PEER_HEADS = 8
PEER_NKEYS = 128
PEER_DQ = 128
PEER_TOPK = 16
PEER_PAIRS = PEER_HEADS * PEER_TOPK
IN_SIZES = (MLA_Q_LORA, MLA_KV_LORA, MLA_ROPE, RET_HEADS * RET_DK, RET_HEADS * RET_DK,
            RET_HEADS * RET_DV, RET_HEADS * RET_DV, D_MODEL, D_MODEL)

LANES = 128
SUBLANES = 8
WORD_ROWS = D_MODEL // (2 * LANES)
GATHER_WORD_ROWS = PEER_PAIRS * WORD_ROWS
GATHER_ROWS = 2 * GATHER_WORD_ROWS
PEER_TOKEN_BLOCK = LANES
ROUTE_TOKEN_BLOCK = 128
ROUTE_HEAD_GROUP = 8
HIDDEN_SLOTS = 32
OUTPUT_SLOTS = 32
PEER_VMEM_LIMIT = 56 * 1024 * 1024
NT_DIMS = (((1,), (1,)), ((), ()))
HIGH_HALF_MASK = -65536
PACK_EXPERT_BLOCK = 256


PAD_ROWS = (-N_META) % CHUNK
HEAD_LANES = 128
MLA_WIDE = MLA_HEADS * HEAD_LANES
RET_QK_WIDE = RET_HEADS * RET_DK
RET_V_WIDE = RET_HEADS * RET_DV
MASK_VALUE = -1e30
MIXER_VMEM_LIMIT = 48 * 1024 * 1024
ATTN_HEAD_GROUP = 8
ATTN_TILE = 3 * CHUNK
ATTN_KV_TILE = 2 * CHUNK
MIXER_IN_TILES = 3
MIXER_OUT_TILES = 4
RET_PAIR_GROUP = 2
_SEG_SIZES = (MLA_Q_LORA, MLA_KV_LORA, HEAD_LANES, RET_QK_WIDE, RET_QK_WIDE, RET_V_WIDE, RET_V_WIDE,
              D_MODEL, D_MODEL)
_SEG_STARTS = tuple(int(v) for v in np.cumsum((0,) + _SEG_SIZES))


def _rope_tables(n_rows, dim, block_offset, repeat):
    half = dim // 2
    pos = jnp.arange(n_rows, dtype=jnp.float32) - PAD_ROWS
    inv = ROPE_BASE ** (-jnp.arange(half, dtype=jnp.float32) / half)
    lane = np.arange(LANES)
    rel = (lane - block_offset) % repeat
    in_rope = (lane >= block_offset) & (rel < dim)
    ang = pos[:, None] * inv[rel % half][None, :]
    cos = jnp.where(in_rope[None, :], jnp.cos(ang), 1.0)
    sin = jnp.sin(ang)
    s1 = jnp.where((in_rope & (rel < half))[None, :], -sin, 0.0)
    s2 = jnp.where((in_rope & (rel >= half))[None, :], sin, 0.0)
    return cos, s1, s2


def _apply_rope(x, cos, s1, s2, half):
    return x * cos + pltpu.roll(x, LANES - half, 1) * s1 + pltpu.roll(x, half, 1) * s2


def _mixer_in_kernel(h0_ref, h1_ref, h2_ref, head_ref, gmix_ref, win_ref, gq_ref, wuq_ref, gkv_ref, wk_ref, wv_ref,
                     gqkq_ref, gqkk_ref, mc_ref, ms1_ref, ms2_ref, rc_ref, rs1_ref, rs2_ref,
                     q_ref, k_ref, v_ref, rq_ref, rk_ref, rv_ref, sg_ref, ga_ref, gr_ref):
    first = jnp.where(pl.program_id(1) == 0, head_ref[...], h0_ref[...])
    x = jnp.concatenate([first, h1_ref[...], h2_ref[...]], axis=0)
    hn = (x * lax.rsqrt(jnp.mean(x * x, axis=-1, keepdims=True) + EPS) * gmix_ref[...]).astype(jnp.bfloat16)

    def seg(i):
        return jnp.dot(hn, win_ref[:, _SEG_STARTS[i]:_SEG_STARTS[i + 1]], preferred_element_type=jnp.float32)

    def lora_norm(c, g_ref):
        return (c * lax.rsqrt(jnp.mean(c * c, axis=-1, keepdims=True) + EPS) * g_ref[...]).astype(jnp.bfloat16)

    cq = lora_norm(seg(0), gq_ref)
    ckv = lora_norm(seg(1), gkv_ref)
    kpe = seg(2)
    q = jnp.dot(cq, wuq_ref[...], preferred_element_type=jnp.float32)
    kn = jnp.dot(ckv, wk_ref[...], preferred_element_type=jnp.float32)
    v_ref[...] = jnp.dot(ckv, wv_ref[...], preferred_element_type=jnp.float32).astype(jnp.bfloat16)
    mc, ms1, ms2 = mc_ref[...], ms1_ref[...], ms2_ref[...]
    scale = math.log2(math.e) / math.sqrt(MLA_QK)

    def qk_norm_rope(t, g_ref):
        t = t * lax.rsqrt(jnp.sum(t * t, axis=-1, keepdims=True) * (1.0 / MLA_QK) + EPS) * g_ref[...]
        return _apply_rope(t, mc, ms1, ms2, MLA_ROPE // 2)

    for hd in range(MLA_HEADS):
        blk = slice(hd * HEAD_LANES, (hd + 1) * HEAD_LANES)
        q_ref[:, blk] = (qk_norm_rope(q[:, blk], gqkq_ref) * scale).astype(jnp.bfloat16)
        k_ref[:, blk] = qk_norm_rope(kn[:, blk] + kpe, gqkk_ref).astype(jnp.bfloat16)

    rc, rs1, rs2 = rc_ref[...], rs1_ref[...], rs2_ref[...]
    rq = seg(3)
    rk = seg(4)
    for j in range(RET_QK_WIDE // LANES):
        blk = slice(j * LANES, (j + 1) * LANES)
        rq_ref[:, blk] = _apply_rope(rq[:, blk], rc, rs1, rs2, RET_DK // 2).astype(jnp.bfloat16)
        rk_ref[:, blk] = (_apply_rope(rk[:, blk], rc, rs1, rs2, RET_DK // 2) * (RET_DK ** -0.5)).astype(jnp.bfloat16)
    rv_ref[...] = seg(5).astype(jnp.bfloat16)
    gate = seg(6)
    sg_ref[...] = (gate * jax.nn.sigmoid(gate)).astype(jnp.bfloat16)
    ga_ref[...] = jax.nn.sigmoid(seg(7)).astype(jnp.bfloat16)
    gr_ref[...] = jax.nn.sigmoid(seg(8)).astype(jnp.bfloat16)


def _attention_kernel(q_ref, k_ref, v_ref, o_ref, m_ref, l_ref, acc_ref):
    qi = pl.program_id(2)
    t = ATTN_TILE
    n_heads = q_ref.shape[1] // HEAD_LANES
    m_ref[...] = jnp.full(m_ref.shape, MASK_VALUE, jnp.float32)
    l_ref[...] = jnp.zeros(l_ref.shape, jnp.float32)
    acc_ref[...] = jnp.zeros(acc_ref.shape, jnp.float32)

    def lane_tiles(x):
        return [x[:, i * LANES:(i + 1) * LANES] for i in range(x.shape[1] // LANES)]

    def scores(start, width, hd, masked):
        blk = slice(hd * HEAD_LANES, (hd + 1) * HEAD_LANES)
        s = lax.dot_general(q_ref[:, blk], k_ref[pl.ds(start, width), blk], NT_DIMS,
                            preferred_element_type=jnp.float32)
        if masked:
            r = lax.broadcasted_iota(jnp.int32, (t, width), 0) + qi * t
            c = lax.broadcasted_iota(jnp.int32, (t, width), 1) + start
            s = jnp.where((c <= r) & ((c >= PAD_ROWS) | (r < PAD_ROWS)), s, MASK_VALUE)
        return s

    def max_step(start, width, masked):
        for hd in range(n_heads):
            m = m_ref[hd]
            for part in lane_tiles(scores(start, width, hd, masked)):
                m = jnp.maximum(m, part)
            m_ref[hd] = m

    def sum_step(start, width, masked):
        for hd in range(n_heads):
            blk = slice(hd * HEAD_LANES, (hd + 1) * HEAD_LANES)
            p = jnp.exp2(scores(start, width, hd, masked) - jnp.tile(m_ref[hd], (1, width // LANES)))
            l = l_ref[hd]
            for part in lane_tiles(p):
                l = l + part
            l_ref[hd] = l
            acc_ref[hd] += jnp.dot(p.astype(jnp.bfloat16), v_ref[pl.ds(start, width), blk],
                                   preferred_element_type=jnp.float32)

    n_cols = (qi + 1) * t
    n_full = n_cols // ATTN_KV_TILE

    def sweep(step):
        def body(j, carry):
            start = pl.multiple_of(j * ATTN_KV_TILE, ATTN_KV_TILE)
            edge = (j == 0) | ((j + 1) * ATTN_KV_TILE > qi * t)

            @pl.when(edge)
            def _():
                step(start, ATTN_KV_TILE, True)

            @pl.when(jnp.logical_not(edge))
            def _():
                step(start, ATTN_KV_TILE, False)

            return carry

        lax.fori_loop(0, n_full, body, 0)

        @pl.when(n_full * ATTN_KV_TILE < n_cols)
        def _():
            step(pl.multiple_of(n_full * ATTN_KV_TILE, CHUNK), CHUNK, True)

    sweep(max_step)
    for hd in range(n_heads):
        m_ref[hd] = jnp.broadcast_to(jnp.max(m_ref[hd], axis=-1, keepdims=True), (t, LANES))
    sweep(sum_step)
    for hd in range(n_heads):
        blk = slice(hd * HEAD_LANES, (hd + 1) * HEAD_LANES)
        o_ref[:, blk] = (acc_ref[hd] / jnp.sum(l_ref[hd], axis=-1, keepdims=True)).astype(jnp.bfloat16)


def _retention_kernel(q_ref, k_ref, v_ref, sg_ref, gn_ref, decay_ref, zeta_ref, xi_ref, gamma_ref, o_ref):
    n_chunks = q_ref.shape[0] // CHUNK
    n_pairs = q_ref.shape[1] // LANES
    lane = lax.broadcasted_iota(jnp.int32, (CHUNK, LANES), 1)
    row = lax.broadcasted_iota(jnp.int32, (LANES, RET_DV), 0)
    is_a = lane < RET_DK
    gn = gn_ref[...]
    contract_rows = (((0,), (0,)), ((), ()))

    def group_norm(y):
        mu = jnp.mean(y, axis=-1, keepdims=True)
        d = y - mu
        return d * lax.rsqrt(jnp.mean(d * d, axis=-1, keepdims=True) + EPS)

    def pair_chunk(rows, p, state):
        qb = q_ref[rows, p * LANES:(p + 1) * LANES]
        kb = k_ref[rows, p * LANES:(p + 1) * LANES]
        state_bf = state.astype(jnp.bfloat16)
        new_parts = []
        for hd in range(2):
            head = 2 * p + hd
            cols = slice(head * RET_DV, (head + 1) * RET_DV)
            qh = jnp.where(is_a if hd == 0 else ~is_a, qb, jnp.zeros_like(qb))
            vh = v_ref[rows, cols]
            scores = lax.dot_general(qh, kb, NT_DIMS, preferred_element_type=jnp.float32) * decay_ref[head]
            inner = jnp.dot(scores.astype(jnp.bfloat16), vh, preferred_element_type=jnp.float32)
            cross = jnp.dot(qh, state_bf, preferred_element_type=jnp.float32) * xi_ref[head]
            yn = group_norm(inner + cross) * gn[:, cols]
            o_ref[rows, cols] = (sg_ref[rows, cols].astype(jnp.float32) * yn).astype(jnp.bfloat16)
            kz = (kb.astype(jnp.float32) * zeta_ref[head]).astype(jnp.bfloat16)
            new_parts.append(lax.dot_general(kz, vh, contract_rows, preferred_element_type=jnp.float32))
        return gamma_ref[p] * state + jnp.where(row < RET_DK, new_parts[0], new_parts[1])

    def chunk(n, states):
        rows = pl.ds(pl.multiple_of(n * CHUNK, CHUNK), CHUNK)
        return tuple(pair_chunk(rows, p, state) for p, state in enumerate(states))

    lax.fori_loop(0, n_chunks, chunk, tuple(jnp.zeros((LANES, RET_DV), jnp.float32) for _ in range(n_pairs)))


def _mixer_out_kernel(h_ref, *refs):
    m = MIXER_OUT_TILES
    rows = [jnp.concatenate([r[...] for r in refs[j * m:(j + 1) * m]], axis=0) for j in range(4)]
    o, yr, ga, gr = rows
    wa_ref, wr_ref, wm_ref, out_ref = refs[4 * m:]
    y_a = jnp.dot(o, wa_ref[...], preferred_element_type=jnp.float32)
    y_r = jnp.dot(yr, wr_ref[...], preferred_element_type=jnp.float32)
    merged = ga.astype(jnp.float32) * y_a + gr.astype(jnp.float32) * y_r
    out_ref[...] = h_ref[...] + jnp.dot(merged.astype(jnp.bfloat16), wm_ref[...],
                                        preferred_element_type=jnp.float32)


def _pad_heads(w, heads, width):
    lead = w.shape[:-1]
    w = w.reshape(lead + (heads, width))
    w = jnp.pad(w, [(0, 0)] * len(lead) + [(0, 0), (0, HEAD_LANES - width)])
    return w.reshape(lead + (heads * HEAD_LANES,))


def _mixer(x, meta_tokens, g_mix, w_in, g_q_lora, w_uq, g_kv_lora, w_ukv, g_qk_q, g_qk_k,
           w_mla_out, g_ret_gn, w_ret_out, w_mix_out):
    B, S, D = x.shape
    bf = jnp.bfloat16
    rows = PAD_ROWS + N_META + S
    nt = rows // CHUNK
    seq_tiles = S // CHUNK
    assert PAD_ROWS + N_META == CHUNK and S % CHUNK == 0
    x2 = x.reshape(B * S, D)
    head_tile = jnp.concatenate([jnp.zeros((PAD_ROWS, D), x.dtype), meta_tokens.astype(x.dtype)], axis=0)

    segs = jnp.split(w_in, np.cumsum(IN_SIZES)[:-1].tolist(), axis=1)
    kpe_cols = jnp.pad(segs[2], ((0, 0), (MLA_NOPE, HEAD_LANES - MLA_NOPE - MLA_ROPE)))
    win = jnp.concatenate(segs[:2] + [kpe_cols] + segs[3:], axis=1).astype(bf)
    wuq = _pad_heads(w_uq, MLA_HEADS, MLA_QK).astype(bf)
    wkv = w_ukv.reshape(MLA_KV_LORA, MLA_HEADS, MLA_NOPE + MLA_V)
    wk = _pad_heads(wkv[:, :, :MLA_NOPE].reshape(MLA_KV_LORA, -1), MLA_HEADS, MLA_NOPE).astype(bf)
    wv = _pad_heads(wkv[:, :, MLA_NOPE:].reshape(MLA_KV_LORA, -1), MLA_HEADS, MLA_V).astype(bf)
    gqkq = jnp.pad(g_qk_q, (0, HEAD_LANES - MLA_QK)).reshape(1, HEAD_LANES)
    gqkk = jnp.pad(g_qk_k, (0, HEAD_LANES - MLA_QK)).reshape(1, HEAD_LANES)
    wa = _pad_heads(w_mla_out.T, MLA_HEADS, MLA_V).T.astype(bf)
    mla_tabs = _rope_tables(rows, MLA_ROPE, MLA_NOPE, HEAD_LANES)
    ret_tabs = _rope_tables(rows, RET_DK, 0, RET_DK)

    m = MIXER_IN_TILES
    assert nt % m == 0 and m == 3
    row_tile = lambda width: pl.BlockSpec((m * CHUNK, width), lambda b, i: (b * (nt // m) + i, 0))
    seq_in = lambda j: pl.BlockSpec((CHUNK, D), lambda b, i: (b * seq_tiles + jnp.maximum(m * i + j - 1, 0), 0))
    const = lambda shape: pl.BlockSpec(shape, lambda b, i: (0,) * len(shape))
    tab = pl.BlockSpec((m * CHUNK, LANES), lambda b, i: (i, 0))
    n_rows = B * rows
    widths = (MLA_WIDE, MLA_WIDE, MLA_WIDE, RET_QK_WIDE, RET_QK_WIDE, RET_V_WIDE, RET_V_WIDE, D, D)
    q, k, v, rq, rk, rv, sg, ga, gr = pl.pallas_call(
        _mixer_in_kernel,
        out_shape=tuple(jax.ShapeDtypeStruct((n_rows, w), bf) for w in widths),
        grid=(B, nt // m),
        in_specs=[seq_in(0), seq_in(1), seq_in(2), const((CHUNK, D)), const((1, D)), const(win.shape),
                  const((1, MLA_Q_LORA)), const(wuq.shape), const((1, MLA_KV_LORA)), const(wk.shape), const(wv.shape), const((1, HEAD_LANES)),
                  const((1, HEAD_LANES))] + [tab] * 6,
        out_specs=tuple(row_tile(w) for w in widths),
        compiler_params=pltpu.CompilerParams(dimension_semantics=("arbitrary", "arbitrary"),
                                             vmem_limit_bytes=MIXER_VMEM_LIMIT),
        name="mixer_in",
    )(x2, x2, x2, head_tile, g_mix.reshape(1, D), win, g_q_lora.reshape(1, -1), wuq, g_kv_lora.reshape(1, -1), wk, wv,
      gqkq, gqkk, *mla_tabs, *ret_tabs)

    group = ATTN_HEAD_GROUP * HEAD_LANES
    seq_spec = pl.BlockSpec((rows, group), lambda b, g, i: (b, g))
    n_attn = rows // ATTN_TILE
    tile_spec = pl.BlockSpec((ATTN_TILE, group), lambda b, g, i: (b * n_attn + i, g))
    o = pl.pallas_call(
        _attention_kernel,
        out_shape=jax.ShapeDtypeStruct((n_rows, MLA_WIDE), bf),
        grid=(B, MLA_HEADS // ATTN_HEAD_GROUP, n_attn),
        in_specs=[tile_spec, seq_spec, seq_spec],
        out_specs=tile_spec,
        scratch_shapes=[pltpu.VMEM((ATTN_HEAD_GROUP, ATTN_TILE, HEAD_LANES), jnp.float32)] * 3,
        compiler_params=pltpu.CompilerParams(dimension_semantics=("arbitrary",) * 3,
                                             vmem_limit_bytes=MIXER_VMEM_LIMIT),
        name="mla_attention",
    )(q, k, v)

    log_gamma = jnp.log(1.0 - 2.0 ** (-5.0 - jnp.arange(RET_HEADS, dtype=jnp.float32)))
    idx = jnp.arange(CHUNK, dtype=jnp.float32)
    diff = idx[:, None] - idx[None, :]
    decay = jnp.where(diff[None] >= 0, jnp.exp(jnp.maximum(diff, 0.0)[None] * log_gamma[:, None, None]), 0.0)
    zeta = jnp.exp((CHUNK - 1 - idx)[None, :] * log_gamma[:, None])
    xi = jnp.exp((idx + 1)[None, :] * log_gamma[:, None])
    gamma_c = jnp.exp(CHUNK * log_gamma)
    zeta_b = jnp.broadcast_to(zeta[:, :, None], (RET_HEADS, CHUNK, LANES))
    xi_b = jnp.broadcast_to(xi[:, :, None], (RET_HEADS, CHUNK, RET_DV))
    gamma_rows = jnp.broadcast_to(jnp.repeat(gamma_c, RET_DK).reshape(RET_HEADS // 2, LANES, 1),
                                  (RET_HEADS // 2, LANES, RET_DV))
    g = RET_PAIR_GROUP
    pair_seq = lambda width: pl.BlockSpec((rows, g * width), lambda b, p: (b, p))
    pair_tab = lambda d1, d2: pl.BlockSpec((2 * g, d1, d2), lambda b, p: (p, 0, 0))
    yr = pl.pallas_call(
        _retention_kernel,
        out_shape=jax.ShapeDtypeStruct((n_rows, RET_V_WIDE), bf),
        grid=(B, RET_HEADS // (2 * g)),
        in_specs=[pair_seq(LANES), pair_seq(LANES), pair_seq(2 * RET_DV), pair_seq(2 * RET_DV),
                  pl.BlockSpec((1, g * 2 * RET_DV), lambda b, p: (0, p)),
                  pair_tab(CHUNK, CHUNK), pair_tab(CHUNK, LANES), pair_tab(CHUNK, RET_DV),
                  pl.BlockSpec((g, LANES, RET_DV), lambda b, p: (p, 0, 0))],
        out_specs=pair_seq(2 * RET_DV),
        compiler_params=pltpu.CompilerParams(dimension_semantics=("arbitrary", "arbitrary"),
                                             vmem_limit_bytes=MIXER_VMEM_LIMIT),
        name="retention",
    )(rq, rk, rv, sg, g_ret_gn.reshape(1, -1), decay, zeta_b, xi_b, gamma_rows)

    mo = MIXER_OUT_TILES
    assert seq_tiles % mo == 0
    in_tiles = lambda width: [pl.BlockSpec((CHUNK, width), lambda b, i, j=j: (b * nt + 1 + mo * i + j, 0))
                              for j in range(mo)]
    seq_tile = pl.BlockSpec((mo * CHUNK, D), lambda b, i: (b * (seq_tiles // mo) + i, 0))
    return pl.pallas_call(
        _mixer_out_kernel,
        out_shape=jax.ShapeDtypeStruct((B * S, D), jnp.float32),
        grid=(B, seq_tiles // mo),
        in_specs=[seq_tile] + in_tiles(MLA_WIDE) + in_tiles(RET_V_WIDE) + in_tiles(D) + in_tiles(D)
        + [const(wa.shape), const(w_ret_out.shape), const(w_mix_out.shape)],
        out_specs=seq_tile,
        compiler_params=pltpu.CompilerParams(dimension_semantics=("arbitrary", "arbitrary"),
                                             vmem_limit_bytes=MIXER_VMEM_LIMIT),
        name="mixer_out",
    )(x2, *([o] * mo), *([yr] * mo), *([ga] * mo), *([gr] * mo), wa, w_ret_out.astype(bf), w_mix_out.astype(bf))


def _top_rows(s, count):
    n = s.shape[0]
    iota = lax.broadcasted_iota(jnp.int32, s.shape, 0).astype(jnp.float32)
    vals, idxs = [], []
    for _ in range(count):
        m = jnp.max(s, axis=0, keepdims=True)
        idx = jnp.min(jnp.where(s == m, iota, float(n)), axis=0, keepdims=True)
        vals.append(m)
        idxs.append(idx)
        s = jnp.where(iota == idx, -jnp.inf, s)
    return vals, idxs


def _route_head(q, k1, k2):
    s1 = lax.dot_general(k1, q, NT_DIMS, preferred_element_type=jnp.float32)
    s2 = lax.dot_general(k2, q, NT_DIMS, preferred_element_type=jnp.float32)
    v1, i1 = _top_rows(s1, PEER_TOPK)
    v2, i2 = _top_rows(s2, PEER_TOPK)
    v2_all = jnp.concatenate(v2, axis=0)
    i2_all = jnp.concatenate(i2, axis=0)
    n_mid = SUBLANES
    kept = [1 << (PEER_TOPK // (a + 1) - 1).bit_length() for a in range(n_mid)]
    n_fill = -sum(kept) % SUBLANES
    cand = [v1[a] + v2_all[:kept[a]] for a in range(n_mid)]
    cand_rows = [i1[a] * PEER_NKEYS + i2_all[:kept[a]] for a in range(n_mid)]
    if n_fill:
        cand.append(jnp.full((n_fill, q.shape[0]), -jnp.inf, jnp.float32))
        cand_rows.append(jnp.full((n_fill, q.shape[0]), -1.0, jnp.float32))
    cand.append(jnp.concatenate(v1[n_mid:], axis=0) + v2[0])
    cand_rows.append(jnp.concatenate(i1[n_mid:], axis=0) * PEER_NKEYS + i2[0])
    cand = jnp.concatenate(cand, axis=0)
    cand_rows = jnp.concatenate(cand_rows, axis=0) * float(WORD_ROWS)
    sc, ci = _top_rows(cand, PEER_TOPK)
    iota = lax.broadcasted_iota(jnp.int32, cand.shape, 0).astype(jnp.float32)
    rows = [jnp.max(jnp.where(iota == c, cand_rows, -1.0), axis=0, keepdims=True) for c in ci]
    ex = [jnp.exp(v - sc[0]) for v in sc]
    denom = ex[0]
    for e in ex[1:]:
        denom = denom + e
    return jnp.concatenate(rows, axis=0).astype(jnp.int32), jnp.concatenate(ex, axis=0) / denom


def _peer_route_kernel(h_ref, g_ref, wq_ref, k1_ref, k2_ref, xn_ref, rows_ref, gates_ref,
                       xb_ref, rows_scr, gates_scr):
    grp = pl.program_id(1)

    @pl.when(grp == 0)
    def _():
        x = h_ref[...]
        xn = x * lax.rsqrt(jnp.mean(x * x, axis=-1, keepdims=True) + EPS) * g_ref[...]
        for r in range(SUBLANES):
            xn_ref[pl.ds(r, x.shape[0], stride=SUBLANES), :] = xn[:, r * LANES:(r + 1) * LANES]
        xb_ref[...] = xn.astype(jnp.bfloat16)

    q = jnp.dot(xb_ref[...], wq_ref[...], preferred_element_type=jnp.float32).astype(jnp.bfloat16)
    k1, k2 = k1_ref[...], k2_ref[...]
    for j in range(ROUTE_HEAD_GROUP):
        rows, gates = _route_head(q[:, j * PEER_DQ:(j + 1) * PEER_DQ], k1, k2)
        base = pl.multiple_of((grp * ROUTE_HEAD_GROUP + j) * PEER_TOPK, PEER_TOPK)
        rows_scr[pl.ds(base, PEER_TOPK), :] = rows
        gates_scr[pl.ds(base, PEER_TOPK), :] = gates

    @pl.when(grp == PEER_HEADS // ROUTE_HEAD_GROUP - 1)
    def _():
        for j in range(rows_ref.shape[0]):
            rows_ref[j] = rows_scr[:, j * PEER_TOKEN_BLOCK:(j + 1) * PEER_TOKEN_BLOCK]
        gates_ref[...] = gates_scr[...].T


def _peer_route(h, g_ffn, w_peer_q, keys_1, keys_2):
    T = h.shape[0]
    tb = ROUTE_TOKEN_BLOCK
    per_step = tb // PEER_TOKEN_BLOCK
    half = PEER_DQ // 2
    zeros = jnp.zeros((PEER_NKEYS, half), jnp.bfloat16)
    k1 = jnp.concatenate([keys_1.astype(jnp.bfloat16), zeros], axis=1)
    k2 = jnp.concatenate([zeros, keys_2.astype(jnp.bfloat16)], axis=1)
    keys_spec = pl.BlockSpec((PEER_NKEYS, PEER_DQ), lambda i, g: (0, 0))
    pair_spec = pl.BlockSpec((tb, PEER_PAIRS), lambda i, g: (i, 0))
    return pl.pallas_call(
        _peer_route_kernel,
        out_shape=(jax.ShapeDtypeStruct((T * SUBLANES, LANES), jnp.float32),
                   jax.ShapeDtypeStruct((T // PEER_TOKEN_BLOCK, PEER_PAIRS, PEER_TOKEN_BLOCK), jnp.int32),
                   jax.ShapeDtypeStruct((T, PEER_PAIRS), jnp.float32)),
        grid=(T // tb, PEER_HEADS // ROUTE_HEAD_GROUP),
        in_specs=[pl.BlockSpec((tb, D_MODEL), lambda i, g: (i, 0)),
                  pl.BlockSpec((1, D_MODEL), lambda i, g: (0, 0)),
                  pl.BlockSpec((D_MODEL, ROUTE_HEAD_GROUP * PEER_DQ), lambda i, g: (0, g)),
                  keys_spec, keys_spec],
        out_specs=(pl.BlockSpec((tb * SUBLANES, LANES), lambda i, g: (i, 0)),
                   pl.BlockSpec((per_step, PEER_PAIRS, PEER_TOKEN_BLOCK), lambda i, g: (i, 0, 0)),
                   pair_spec),
        scratch_shapes=[pltpu.VMEM((tb, D_MODEL), jnp.bfloat16),
                        pltpu.VMEM((PEER_PAIRS, tb), jnp.int32),
                        pltpu.VMEM((PEER_PAIRS, tb), jnp.float32)],
        compiler_params=pltpu.CompilerParams(dimension_semantics=("arbitrary", "arbitrary")),
        name="peer_route",
    )(h, g_ffn.reshape(1, D_MODEL), w_peer_q.astype(jnp.bfloat16), k1, k2)


def _pack_table_kernel(tab_ref, out_ref):
    n = tab_ref.shape[0]
    bits = lax.bitcast_convert_type(tab_ref[...].astype(jnp.bfloat16).astype(jnp.float32), jnp.int32)
    for j in range(WORD_ROWS):
        lo = lax.shift_right_logical(bits[:, 2 * j * LANES:(2 * j + 1) * LANES], 16)
        hi = bits[:, (2 * j + 1) * LANES:(2 * j + 2) * LANES] & HIGH_HALF_MASK
        out_ref[pl.ds(j, n, stride=WORD_ROWS), :] = hi | lo


def _pack_expert_table(tab):
    n = PACK_EXPERT_BLOCK
    return pl.pallas_call(
        _pack_table_kernel,
        out_shape=jax.ShapeDtypeStruct((tab.shape[0] * WORD_ROWS, LANES), jnp.int32),
        grid=(tab.shape[0] // n,),
        in_specs=[pl.BlockSpec((n, D_MODEL), lambda i: (i, 0))],
        out_specs=pl.BlockSpec((n * WORD_ROWS, LANES), lambda i: (i, 0)),
        compiler_params=pltpu.CompilerParams(dimension_semantics=("arbitrary",)),
        name="pack_table",
    )(tab)


def _gather_rows(idx_ref, tab_ref, dst_refs, t0):
    for k in range(PEER_PAIRS):
        idx_k = idx_ref.at[k]
        for s, dst_ref in enumerate(dst_refs):
            row = pl.multiple_of(idx_k[t0 + s], WORD_ROWS)
            dst_ref[k * WORD_ROWS:(k + 1) * WORD_ROWS, :] = tab_ref[pl.ds(row, WORD_ROWS), :]


def _expert_token_loop(idx_hbm, idx_bufs, sem, tab_ref, rows_refs, n_tokens, consume):
    i = pl.program_id(0)
    n = pl.num_programs(0)
    n_slots = len(rows_refs)
    assert n_tokens % n_slots == 0

    def copy(block, slot):
        return pltpu.make_async_copy(idx_hbm.at[block], idx_bufs[slot], sem.at[slot])

    def consume_group(t0):
        for s in range(n_slots):
            consume(t0 + s, s)

    @pl.when(i == 0)
    def _():
        copy(0, 0).start()
        copy(0, 0).wait()
        _gather_rows(idx_bufs[0], tab_ref, rows_refs, 0)

    def run(slot):
        has_next = i + 1 < n

        @pl.when(has_next)
        def _():
            copy(i + 1, 1 - slot).start()

        def step(j, carry):
            consume_group(n_slots * j)
            _gather_rows(idx_bufs[slot], tab_ref, rows_refs, n_slots * (j + 1))
            return carry

        lax.fori_loop(0, n_tokens // n_slots - 1, step, 0)

        @pl.when(has_next)
        def _():
            copy(i + 1, 1 - slot).wait()
            consume_group(n_tokens - n_slots)
            _gather_rows(idx_bufs[1 - slot], tab_ref, rows_refs, 0)

        @pl.when(jnp.logical_not(has_next))
        def _():
            consume_group(n_tokens - n_slots)

    for slot in range(2):
        pl.when(lax.rem(i, 2) == slot)(lambda slot=slot: run(slot))


def _diag_mask():
    lane = lax.broadcasted_iota(jnp.int32, (SUBLANES, GATHER_ROWS), 1)
    sub = lax.broadcasted_iota(jnp.int32, (SUBLANES, GATHER_ROWS), 0)
    return (lane & (SUBLANES - 1)) == sub


def _split_bf16(x):
    hi = x.astype(jnp.bfloat16)
    lo = (x - hi.astype(jnp.float32)).astype(jnp.bfloat16)
    return hi, lo


def _peer_hidden_kernel(idx_hbm, x_ref, gate_ref, tab_ref, sel_ref, w_ref, diag_ref, idx_a, idx_b, sem,
                        *rows_refs):
    mask = _diag_mask()

    def consume(t, slot):
        rows = pltpu.bitcast(rows_refs[slot][...], jnp.bfloat16)
        xb = x_ref[t].astype(jnp.bfloat16)
        d = lax.dot_general(xb, rows, NT_DIMS, preferred_element_type=jnp.float32)
        diag_ref[pl.ds(t, 1), :] = jnp.sum(jnp.where(mask, d, 0.0), axis=0, keepdims=True)

    _expert_token_loop(idx_hbm, (idx_a, idx_b), sem, tab_ref, rows_refs, x_ref.shape[0], consume)
    hi, lo = _split_bf16(diag_ref[...])
    sel = sel_ref[...]
    s = (jnp.dot(hi, sel, preferred_element_type=jnp.float32)
         + jnp.dot(lo, sel, preferred_element_type=jnp.float32))
    act = 0.5 * s * (1.0 + lax.erf(s * (1.0 / math.sqrt(2.0))))
    w_ref[...] = gate_ref[...] * act


def _peer_output_kernel(idx_hbm, w_ref, h_ref, tab_ref, expand_ref, out_ref, whi_ref, wlo_ref, idx_a, idx_b, sem,
                        *rows_refs):
    mask = _diag_mask()
    hi, lo = _split_bf16(w_ref[...])
    expand = expand_ref[...]
    whi_ref[...] = jnp.dot(hi, expand, preferred_element_type=jnp.float32)
    wlo_ref[...] = jnp.dot(lo, expand, preferred_element_type=jnp.float32)

    def consume(t, slot):
        rows = pltpu.bitcast(rows_refs[slot][...], jnp.bfloat16)
        parts = []
        for w_part in (whi_ref, wlo_ref):
            w_row = jnp.broadcast_to(w_part[pl.ds(t, 1), :], (SUBLANES, GATHER_ROWS))
            parts.append(jnp.where(mask, w_row, 0.0))
        lhs = jnp.concatenate(parts, axis=0).astype(jnp.bfloat16)
        y = jnp.dot(lhs, rows, preferred_element_type=jnp.float32)
        y = y[:SUBLANES] + y[SUBLANES:]
        y_row = jnp.concatenate([y[r:r + 1, :] for r in range(SUBLANES)], axis=1)
        out_ref[pl.ds(t, 1), :] = h_ref[pl.ds(t, 1), :] + y_row

    _expert_token_loop(idx_hbm, (idx_a, idx_b), sem, tab_ref, rows_refs, h_ref.shape[0], consume)


def _peer_experts(h, xn, rows, gates, peer_u, peer_v):
    T = h.shape[0]
    tb = PEER_TOKEN_BLOCK
    u_tab = _pack_expert_table(peer_u)
    v_tab = _pack_expert_table(peer_v)
    expand = (jnp.arange(PEER_PAIRS)[:, None] == jnp.arange(GATHER_ROWS)[None, :] // SUBLANES
              ).astype(jnp.bfloat16)
    params = pltpu.CompilerParams(dimension_semantics=("arbitrary",), vmem_limit_bytes=PEER_VMEM_LIMIT)
    hbm_ref = pl.BlockSpec(memory_space=pl.ANY)
    index_scratch = [pltpu.SMEM((PEER_PAIRS, tb), jnp.int32)] * 2 + [pltpu.SemaphoreType.DMA((2,))]
    pair_block = pl.BlockSpec((tb, PEER_PAIRS), lambda i: (i, 0))
    vmem_rows = pl.BlockSpec((tb, D_MODEL), lambda i: (i, 0))
    whole_vmem = pl.BlockSpec(memory_space=pltpu.VMEM)
    rows_scratch = pltpu.VMEM((GATHER_WORD_ROWS, LANES), jnp.int32)
    wide_scratch = pltpu.VMEM((tb, GATHER_ROWS), jnp.float32)

    w = pl.pallas_call(
        _peer_hidden_kernel,
        out_shape=jax.ShapeDtypeStruct((T, PEER_PAIRS), jnp.float32),
        grid=(T // tb,),
        in_specs=[hbm_ref, pl.BlockSpec((tb, SUBLANES, LANES), lambda i: (i, 0, 0)), pair_block,
                  whole_vmem, whole_vmem],
        out_specs=pair_block,
        scratch_shapes=[wide_scratch] + index_scratch + [rows_scratch] * HIDDEN_SLOTS,
        compiler_params=params,
        name="peer_hidden",
    )(rows, xn.reshape(T, SUBLANES, LANES), gates, u_tab, expand.T)

    return pl.pallas_call(
        _peer_output_kernel,
        out_shape=jax.ShapeDtypeStruct((T, D_MODEL), jnp.float32),
        grid=(T // tb,),
        in_specs=[hbm_ref, pair_block, vmem_rows, whole_vmem, whole_vmem],
        out_specs=vmem_rows,
        scratch_shapes=[wide_scratch, wide_scratch] + index_scratch + [rows_scratch] * OUTPUT_SLOTS,
        compiler_params=params,
        name="peer_output",
    )(rows, w, h, v_tab, expand)


def kernel(x, meta_tokens, g_mix, w_in, g_q_lora, w_uq, g_kv_lora, w_ukv, g_qk_q, g_qk_k, w_mla_out, g_ret_gn, w_ret_out, w_mix_out, g_ffn, w_peer_q, peer_keys_1, peer_keys_2, peer_u, peer_v):
    B, S, D = x.shape
    h = _mixer(x, meta_tokens, g_mix[0], w_in[0], g_q_lora[0], w_uq[0], g_kv_lora[0], w_ukv[0],
               g_qk_q[0], g_qk_k[0], w_mla_out[0], g_ret_gn[0], w_ret_out[0], w_mix_out[0])
    xn, rows, gates = _peer_route(h, g_ffn[0], w_peer_q[0], peer_keys_1[0], peer_keys_2[0])
    out = _peer_experts(h, xn, rows, gates, peer_u[0], peer_v[0])
    return out.reshape(B, S, D)
```

```python
import math

import jax
import jax.numpy as jnp
import numpy as np
from jax import lax
from jax.experimental import pallas as pl
from jax.experimental.pallas import tpu as pltpu

D_MODEL = 1024
N_META = 16
CHUNK = 128
ROPE_BASE = 10000.0
EPS = 1e-6
MLA_HEADS = 8
MLA_Q_LORA = 384
MLA_KV_LORA = 256
MLA_NOPE = 64
MLA_ROPE = 32
MLA_V = 64
MLA_QK = MLA_NOPE + MLA_ROPE
RET_HEADS = 8
RET_DK = 64
RET_DV = 128
PEER_HEADS = 8
PEER_NKEYS = 128
PEER_DQ = 128
PEER_TOPK = 16
PEER_PAIRS = PEER_HEADS * PEER_TOPK
IN_SIZES = (MLA_Q_LORA, MLA_KV_LORA, MLA_ROPE, RET_HEADS * RET_DK, RET_HEADS * RET_DK,
            RET_HEADS * RET_DV, RET_HEADS * RET_DV, D_MODEL, D_MODEL)

LANES = 128
SUBLANES = 8
WORD_ROWS = D_MODEL // (2 * LANES)
GATHER_WORD_ROWS = PEER_PAIRS * WORD_ROWS
GATHER_ROWS = 2 * GATHER_WORD_ROWS
PEER_TOKEN_BLOCK = LANES
ROUTE_TOKEN_BLOCK = 256
ROUTE_HEAD_GROUP = 8
HIDDEN_SLOTS = 32
OUTPUT_SLOTS = 32
PEER_VMEM_LIMIT = 56 * 1024 * 1024
NT_DIMS = (((1,), (1,)), ((), ()))
HIGH_HALF_MASK = -65536
PACK_EXPERT_BLOCK = 256


PAD_ROWS = (-N_META) % CHUNK
HEAD_LANES = 128
MLA_WIDE = MLA_HEADS * HEAD_LANES
RET_QK_WIDE = RET_HEADS * RET_DK
RET_V_WIDE = RET_HEADS * RET_DV
MASK_VALUE = -1e30
MIXER_VMEM_LIMIT = 48 * 1024 * 1024
ATTN_HEAD_GROUP = 8
ATTN_TILE = 3 * CHUNK
ATTN_KV_TILE = 2 * CHUNK
MIXER_IN_TILES = 3
MIXER_OUT_TILES = 4
RET_PAIR_GROUP = 2
_SEG_SIZES = (MLA_Q_LORA, MLA_KV_LORA, HEAD_LANES, RET_QK_WIDE, RET_QK_WIDE, RET_V_WIDE, RET_V_WIDE,
              D_MODEL, D_MODEL)
_SEG_STARTS = tuple(int(v) for v in np.cumsum((0,) + _SEG_SIZES))


def _rope_tables(n_rows, dim, block_offset, repeat):
    half = dim // 2
    pos = jnp.arange(n_rows, dtype=jnp.float32) - PAD_ROWS
    inv = ROPE_BASE ** (-jnp.arange(half, dtype=jnp.float32) / half)
    lane = np.arange(LANES)
    rel = (lane - block_offset) % repeat
    in_rope = (lane >= block_offset) & (rel < dim)
    ang = pos[:, None] * inv[rel % half][None, :]
    cos = jnp.where(in_rope[None, :], jnp.cos(ang), 1.0)
    sin = jnp.sin(ang)
    s1 = jnp.where((in_rope & (rel < half))[None, :], -sin, 0.0)
    s2 = jnp.where((in_rope & (rel >= half))[None, :], sin, 0.0)
    return cos, s1, s2


def _apply_rope(x, cos, s1, s2, half):
    return x * cos + pltpu.roll(x, LANES - half, 1) * s1 + pltpu.roll(x, half, 1) * s2


def _mixer_in_kernel(h0_ref, h1_ref, h2_ref, head_ref, gmix_ref, win_ref, gq_ref, wuq_ref, gkv_ref, wk_ref, wv_ref,
                     gqkq_ref, gqkk_ref, mc_ref, ms1_ref, ms2_ref, rc_ref, rs1_ref, rs2_ref,
                     q_ref, k_ref, v_ref, rq_ref, rk_ref, rv_ref, sg_ref, ga_ref, gr_ref):
    first = jnp.where(pl.program_id(1) == 0, head_ref[...], h0_ref[...])
    x = jnp.concatenate([first, h1_ref[...], h2_ref[...]], axis=0)
    hn = (x * lax.rsqrt(jnp.mean(x * x, axis=-1, keepdims=True) + EPS) * gmix_ref[...]).astype(jnp.bfloat16)

    def seg(i):
        return jnp.dot(hn, win_ref[:, _SEG_STARTS[i]:_SEG_STARTS[i + 1]], preferred_element_type=jnp.float32)

    def lora_norm(c, g_ref):
        return (c * lax.rsqrt(jnp.mean(c * c, axis=-1, keepdims=True) + EPS) * g_ref[...]).astype(jnp.bfloat16)

    cq = lora_norm(seg(0), gq_ref)
    ckv = lora_norm(seg(1), gkv_ref)
    kpe = seg(2)
    q = jnp.dot(cq, wuq_ref[...], preferred_element_type=jnp.float32)
    kn = jnp.dot(ckv, wk_ref[...], preferred_element_type=jnp.float32)
    v_ref[...] = jnp.dot(ckv, wv_ref[...], preferred_element_type=jnp.float32).astype(jnp.bfloat16)
    mc, ms1, ms2 = mc_ref[...], ms1_ref[...], ms2_ref[...]
    scale = math.log2(math.e) / math.sqrt(MLA_QK)

    def qk_norm_rope(t, g_ref):
        t = t * lax.rsqrt(jnp.sum(t * t, axis=-1, keepdims=True) * (1.0 / MLA_QK) + EPS) * g_ref[...]
        return _apply_rope(t, mc, ms1, ms2, MLA_ROPE // 2)

    for hd in range(MLA_HEADS):
        blk = slice(hd * HEAD_LANES, (hd + 1) * HEAD_LANES)
        q_ref[:, blk] = (qk_norm_rope(q[:, blk], gqkq_ref) * scale).astype(jnp.bfloat16)
        k_ref[:, blk] = qk_norm_rope(kn[:, blk] + kpe, gqkk_ref).astype(jnp.bfloat16)

    rc, rs1, rs2 = rc_ref[...], rs1_ref[...], rs2_ref[...]
    rq = seg(3)
    rk = seg(4)
    for j in range(RET_QK_WIDE // LANES):
        blk = slice(j * LANES, (j + 1) * LANES)
        rq_ref[:, blk] = _apply_rope(rq[:, blk], rc, rs1, rs2, RET_DK // 2).astype(jnp.bfloat16)
        rk_ref[:, blk] = (_apply_rope(rk[:, blk], rc, rs1, rs2, RET_DK // 2) * (RET_DK ** -0.5)).astype(jnp.bfloat16)
    rv_ref[...] = seg(5).astype(jnp.bfloat16)
    gate = seg(6)
    sg_ref[...] = (gate * jax.nn.sigmoid(gate)).astype(jnp.bfloat16)
    ga_ref[...] = jax.nn.sigmoid(seg(7)).astype(jnp.bfloat16)
    gr_ref[...] = jax.nn.sigmoid(seg(8)).astype(jnp.bfloat16)


def _attention_kernel(q_ref, k_ref, v_ref, o_ref, m_ref, l_ref, acc_ref):
    qi = pl.program_id(2)
    t = ATTN_TILE
    n_heads = q_ref.shape[1] // HEAD_LANES
    m_ref[...] = jnp.full(m_ref.shape, MASK_VALUE, jnp.float32)
    l_ref[...] = jnp.zeros(l_ref.shape, jnp.float32)
    acc_ref[...] = jnp.zeros(acc_ref.shape, jnp.float32)

    def lane_tiles(x):
        return [x[:, i * LANES:(i + 1) * LANES] for i in range(x.shape[1] // LANES)]

    def scores(start, width, hd, masked):
        blk = slice(hd * HEAD_LANES, (hd + 1) * HEAD_LANES)
        s = lax.dot_general(q_ref[:, blk], k_ref[pl.ds(start, width), blk], NT_DIMS,
                            preferred_element_type=jnp.float32)
        if masked:
            r = lax.broadcasted_iota(jnp.int32, (t, width), 0) + qi * t
            c = lax.broadcasted_iota(jnp.int32, (t, width), 1) + start
            s = jnp.where((c <= r) & ((c >= PAD_ROWS) | (r < PAD_ROWS)), s, MASK_VALUE)
        return s

    def max_step(start, width, masked):
        for hd in range(n_heads):
            m = m_ref[hd]
            for part in lane_tiles(scores(start, width, hd, masked)):
                m = jnp.maximum(m, part)
            m_ref[hd] = m

    def sum_step(start, width, masked):
        for hd in range(n_heads):
            blk = slice(hd * HEAD_LANES, (hd + 1) * HEAD_LANES)
            p = jnp.exp2(scores(start, width, hd, masked) - jnp.tile(m_ref[hd], (1, width // LANES)))
            l = l_ref[hd]
            for part in lane_tiles(p):
                l = l + part
            l_ref[hd] = l
            acc_ref[hd] += jnp.dot(p.astype(jnp.bfloat16), v_ref[pl.ds(start, width), blk],
                                   preferred_element_type=jnp.float32)

    n_cols = (qi + 1) * t
    n_full = n_cols // ATTN_KV_TILE

    def sweep(step):
        def body(j, carry):
            start = pl.multiple_of(j * ATTN_KV_TILE, ATTN_KV_TILE)
            edge = (j == 0) | ((j + 1) * ATTN_KV_TILE > qi * t)

            @pl.when(edge)
            def _():
                step(start, ATTN_KV_TILE, True)

            @pl.when(jnp.logical_not(edge))
            def _():
                step(start, ATTN_KV_TILE, False)

            return carry

        lax.fori_loop(0, n_full, body, 0)

        @pl.when(n_full * ATTN_KV_TILE < n_cols)
        def _():
            step(pl.multiple_of(n_full * ATTN_KV_TILE, CHUNK), CHUNK, True)

    sweep(max_step)
    for hd in range(n_heads):
        m_ref[hd] = jnp.broadcast_to(jnp.max(m_ref[hd], axis=-1, keepdims=True), (t, LANES))
    sweep(sum_step)
    for hd in range(n_heads):
        blk = slice(hd * HEAD_LANES, (hd + 1) * HEAD_LANES)
        o_ref[:, blk] = (acc_ref[hd] / jnp.sum(l_ref[hd], axis=-1, keepdims=True)).astype(jnp.bfloat16)


def _retention_kernel(q_ref, k_ref, v_ref, sg_ref, gn_ref, decay_ref, zeta_ref, xi_ref, gamma_ref, o_ref):
    n_chunks = q_ref.shape[0] // CHUNK
    n_pairs = q_ref.shape[1] // LANES
    lane = lax.broadcasted_iota(jnp.int32, (CHUNK, LANES), 1)
    row = lax.broadcasted_iota(jnp.int32, (LANES, RET_DV), 0)
    is_a = lane < RET_DK
    gn = gn_ref[...]
    contract_rows = (((0,), (0,)), ((), ()))

    def group_norm(y):
        mu = jnp.mean(y, axis=-1, keepdims=True)
        d = y - mu
        return d * lax.rsqrt(jnp.mean(d * d, axis=-1, keepdims=True) + EPS)

    def pair_chunk(rows, p, state):
        qb = q_ref[rows, p * LANES:(p + 1) * LANES]
        kb = k_ref[rows, p * LANES:(p + 1) * LANES]
        state_bf = state.astype(jnp.bfloat16)
        new_parts = []
        for hd in range(2):
            head = 2 * p + hd
            cols = slice(head * RET_DV, (head + 1) * RET_DV)
            qh = jnp.where(is_a if hd == 0 else ~is_a, qb, jnp.zeros_like(qb))
            vh = v_ref[rows, cols]
            scores = lax.dot_general(qh, kb, NT_DIMS, preferred_element_type=jnp.float32) * decay_ref[head]
            inner = jnp.dot(scores.astype(jnp.bfloat16), vh, preferred_element_type=jnp.float32)
            cross = jnp.dot(qh, state_bf, preferred_element_type=jnp.float32) * xi_ref[head]
            yn = group_norm(inner + cross) * gn[:, cols]
            o_ref[rows, cols] = (sg_ref[rows, cols].astype(jnp.float32) * yn).astype(jnp.bfloat16)
            kz = (kb.astype(jnp.float32) * zeta_ref[head]).astype(jnp.bfloat16)
            new_parts.append(lax.dot_general(kz, vh, contract_rows, preferred_element_type=jnp.float32))
        return gamma_ref[p] * state + jnp.where(row < RET_DK, new_parts[0], new_parts[1])

    def chunk(n, states):
        rows = pl.ds(pl.multiple_of(n * CHUNK, CHUNK), CHUNK)
        return tuple(pair_chunk(rows, p, state) for p, state in enumerate(states))

    lax.fori_loop(0, n_chunks, chunk, tuple(jnp.zeros((LANES, RET_DV), jnp.float32) for _ in range(n_pairs)))


def _mixer_out_kernel(h_ref, *refs):
    m = MIXER_OUT_TILES
    rows = [jnp.concatenate([r[...] for r in refs[j * m:(j + 1) * m]], axis=0) for j in range(4)]
    o, yr, ga, gr = rows
    wa_ref, wr_ref, wm_ref, out_ref = refs[4 * m:]
    y_a = jnp.dot(o, wa_ref[...], preferred_element_type=jnp.float32)
    y_r = jnp.dot(yr, wr_ref[...], preferred_element_type=jnp.float32)
    merged = ga.astype(jnp.float32) * y_a + gr.astype(jnp.float32) * y_r
    out_ref[...] = h_ref[...] + jnp.dot(merged.astype(jnp.bfloat16), wm_ref[...],
                                        preferred_element_type=jnp.float32)


def _pad_heads(w, heads, width):
    lead = w.shape[:-1]
    w = w.reshape(lead + (heads, width))
    w = jnp.pad(w, [(0, 0)] * len(lead) + [(0, 0), (0, HEAD_LANES - width)])
    return w.reshape(lead + (heads * HEAD_LANES,))


def _mixer(x, meta_tokens, g_mix, w_in, g_q_lora, w_uq, g_kv_lora, w_ukv, g_qk_q, g_qk_k,
           w_mla_out, g_ret_gn, w_ret_out, w_mix_out):
    B, S, D = x.shape
    bf = jnp.bfloat16
    rows = PAD_ROWS + N_META + S
    nt = rows // CHUNK
    seq_tiles = S // CHUNK
    assert PAD_ROWS + N_META == CHUNK and S % CHUNK == 0
    x2 = x.reshape(B * S, D)
    head_tile = jnp.concatenate([jnp.zeros((PAD_ROWS, D), x.dtype), meta_tokens.astype(x.dtype)], axis=0)

    segs = jnp.split(w_in, np.cumsum(IN_SIZES)[:-1].tolist(), axis=1)
    kpe_cols = jnp.pad(segs[2], ((0, 0), (MLA_NOPE, HEAD_LANES - MLA_NOPE - MLA_ROPE)))
    win = jnp.concatenate(segs[:2] + [kpe_cols] + segs[3:], axis=1).astype(bf)
    wuq = _pad_heads(w_uq, MLA_HEADS, MLA_QK).astype(bf)
    wkv = w_ukv.reshape(MLA_KV_LORA, MLA_HEADS, MLA_NOPE + MLA_V)
    wk = _pad_heads(wkv[:, :, :MLA_NOPE].reshape(MLA_KV_LORA, -1), MLA_HEADS, MLA_NOPE).astype(bf)
    wv = _pad_heads(wkv[:, :, MLA_NOPE:].reshape(MLA_KV_LORA, -1), MLA_HEADS, MLA_V).astype(bf)
    gqkq = jnp.pad(g_qk_q, (0, HEAD_LANES - MLA_QK)).reshape(1, HEAD_LANES)
    gqkk = jnp.pad(g_qk_k, (0, HEAD_LANES - MLA_QK)).reshape(1, HEAD_LANES)
    wa = _pad_heads(w_mla_out.T, MLA_HEADS, MLA_V).T.astype(bf)
    mla_tabs = _rope_tables(rows, MLA_ROPE, MLA_NOPE, HEAD_LANES)
    ret_tabs = _rope_tables(rows, RET_DK, 0, RET_DK)

    m = MIXER_IN_TILES
    assert nt % m == 0 and m == 3
    row_tile = lambda width: pl.BlockSpec((m * CHUNK, width), lambda b, i: (b * (nt // m) + i, 0))
    seq_in = lambda j: pl.BlockSpec((CHUNK, D), lambda b, i: (b * seq_tiles + jnp.maximum(m * i + j - 1, 0), 0))
    const = lambda shape: pl.BlockSpec(shape, lambda b, i: (0,) * len(shape))
    tab = pl.BlockSpec((m * CHUNK, LANES), lambda b, i: (i, 0))
    n_rows = B * rows
    widths = (MLA_WIDE, MLA_WIDE, MLA_WIDE, RET_QK_WIDE, RET_QK_WIDE, RET_V_WIDE, RET_V_WIDE, D, D)
    q, k, v, rq, rk, rv, sg, ga, gr = pl.pallas_call(
        _mixer_in_kernel,
        out_shape=tuple(jax.ShapeDtypeStruct((n_rows, w), bf) for w in widths),
        grid=(B, nt // m),
        in_specs=[seq_in(0), seq_in(1), seq_in(2), const((CHUNK, D)), const((1, D)), const(win.shape),
                  const((1, MLA_Q_LORA)), const(wuq.shape), const((1, MLA_KV_LORA)), const(wk.shape), const(wv.shape), const((1, HEAD_LANES)),
                  const((1, HEAD_LANES))] + [tab] * 6,
        out_specs=tuple(row_tile(w) for w in widths),
        compiler_params=pltpu.CompilerParams(dimension_semantics=("arbitrary", "arbitrary"),
                                             vmem_limit_bytes=MIXER_VMEM_LIMIT),
        name="mixer_in",
    )(x2, x2, x2, head_tile, g_mix.reshape(1, D), win, g_q_lora.reshape(1, -1), wuq, g_kv_lora.reshape(1, -1), wk, wv,
      gqkq, gqkk, *mla_tabs, *ret_tabs)

    group = ATTN_HEAD_GROUP * HEAD_LANES
    seq_spec = pl.BlockSpec((rows, group), lambda b, g, i: (b, g))
    n_attn = rows // ATTN_TILE
    tile_spec = pl.BlockSpec((ATTN_TILE, group), lambda b, g, i: (b * n_attn + i, g))
    o = pl.pallas_call(
        _attention_kernel,
        out_shape=jax.ShapeDtypeStruct((n_rows, MLA_WIDE), bf),
        grid=(B, MLA_HEADS // ATTN_HEAD_GROUP, n_attn),
        in_specs=[tile_spec, seq_spec, seq_spec],
        out_specs=tile_spec,
        scratch_shapes=[pltpu.VMEM((ATTN_HEAD_GROUP, ATTN_TILE, HEAD_LANES), jnp.float32)] * 3,
        compiler_params=pltpu.CompilerParams(dimension_semantics=("arbitrary",) * 3,
                                             vmem_limit_bytes=MIXER_VMEM_LIMIT),
        name="mla_attention",
    )(q, k, v)

    log_gamma = jnp.log(1.0 - 2.0 ** (-5.0 - jnp.arange(RET_HEADS, dtype=jnp.float32)))
    idx = jnp.arange(CHUNK, dtype=jnp.float32)
    diff = idx[:, None] - idx[None, :]
    decay = jnp.where(diff[None] >= 0, jnp.exp(jnp.maximum(diff, 0.0)[None] * log_gamma[:, None, None]), 0.0)
    zeta = jnp.exp((CHUNK - 1 - idx)[None, :] * log_gamma[:, None])
    xi = jnp.exp((idx + 1)[None, :] * log_gamma[:, None])
    gamma_c = jnp.exp(CHUNK * log_gamma)
    zeta_b = jnp.broadcast_to(zeta[:, :, None], (RET_HEADS, CHUNK, LANES))
    xi_b = jnp.broadcast_to(xi[:, :, None], (RET_HEADS, CHUNK, RET_DV))
    gamma_rows = jnp.broadcast_to(jnp.repeat(gamma_c, RET_DK).reshape(RET_HEADS // 2, LANES, 1),
                                  (RET_HEADS // 2, LANES, RET_DV))
    g = RET_PAIR_GROUP
    pair_seq = lambda width: pl.BlockSpec((rows, g * width), lambda b, p: (b, p))
    pair_tab = lambda d1, d2: pl.BlockSpec((2 * g, d1, d2), lambda b, p: (p, 0, 0))
    yr = pl.pallas_call(
        _retention_kernel,
        out_shape=jax.ShapeDtypeStruct((n_rows, RET_V_WIDE), bf),
        grid=(B, RET_HEADS // (2 * g)),
        in_specs=[pair_seq(LANES), pair_seq(LANES), pair_seq(2 * RET_DV), pair_seq(2 * RET_DV),
                  pl.BlockSpec((1, g * 2 * RET_DV), lambda b, p: (0, p)),
                  pair_tab(CHUNK, CHUNK), pair_tab(CHUNK, LANES), pair_tab(CHUNK, RET_DV),
                  pl.BlockSpec((g, LANES, RET_DV), lambda b, p: (p, 0, 0))],
        out_specs=pair_seq(2 * RET_DV),
        compiler_params=pltpu.CompilerParams(dimension_semantics=("arbitrary", "arbitrary"),
                                             vmem_limit_bytes=MIXER_VMEM_LIMIT),
        name="retention",
    )(rq, rk, rv, sg, g_ret_gn.reshape(1, -1), decay, zeta_b, xi_b, gamma_rows)

    mo = MIXER_OUT_TILES
    assert seq_tiles % mo == 0
    in_tiles = lambda width: [pl.BlockSpec((CHUNK, width), lambda b, i, j=j: (b * nt + 1 + mo * i + j, 0))
                              for j in range(mo)]
    seq_tile = pl.BlockSpec((mo * CHUNK, D), lambda b, i: (b * (seq_tiles // mo) + i, 0))
    return pl.pallas_call(
        _mixer_out_kernel,
        out_shape=jax.ShapeDtypeStruct((B * S, D), jnp.float32),
        grid=(B, seq_tiles // mo),
        in_specs=[seq_tile] + in_tiles(MLA_WIDE) + in_tiles(RET_V_WIDE) + in_tiles(D) + in_tiles(D)
        + [const(wa.shape), const(w_ret_out.shape), const(w_mix_out.shape)],
        out_specs=seq_tile,
        compiler_params=pltpu.CompilerParams(dimension_semantics=("arbitrary", "arbitrary"),
                                             vmem_limit_bytes=MIXER_VMEM_LIMIT),
        name="mixer_out",
    )(x2, *([o] * mo), *([yr] * mo), *([ga] * mo), *([gr] * mo), wa, w_ret_out.astype(bf), w_mix_out.astype(bf))


def _top_rows(s, count):
    n = s.shape[0]
    iota = lax.broadcasted_iota(jnp.int32, s.shape, 0).astype(jnp.float32)
    vals, idxs = [], []
    for _ in range(count):
        m = jnp.max(s, axis=0, keepdims=True)
        idx = jnp.min(jnp.where(s == m, iota, float(n)), axis=0, keepdims=True)
        vals.append(m)
        idxs.append(idx)
        s = jnp.where(iota == idx, -jnp.inf, s)
    return vals, idxs


def _route_head(q, k1, k2):
    s1 = lax.dot_general(k1, q, NT_DIMS, preferred_element_type=jnp.float32)
    s2 = lax.dot_general(k2, q, NT_DIMS, preferred_element_type=jnp.float32)
    v1, i1 = _top_rows(s1, PEER_TOPK)
    v2, i2 = _top_rows(s2, PEER_TOPK)
    v2_all = jnp.concatenate(v2, axis=0)
    i2_all = jnp.concatenate(i2, axis=0)
    n_mid = SUBLANES
    kept = [1 << (PEER_TOPK // (a + 1) - 1).bit_length() for a in range(n_mid)]
    n_fill = -sum(kept) % SUBLANES
    cand = [v1[a] + v2_all[:kept[a]] for a in range(n_mid)]
    cand_rows = [i1[a] * PEER_NKEYS + i2_all[:kept[a]] for a in range(n_mid)]
    if n_fill:
        cand.append(jnp.full((n_fill, q.shape[0]), -jnp.inf, jnp.float32))
        cand_rows.append(jnp.full((n_fill, q.shape[0]), -1.0, jnp.float32))
    cand.append(jnp.concatenate(v1[n_mid:], axis=0) + v2[0])
    cand_rows.append(jnp.concatenate(i1[n_mid:], axis=0) * PEER_NKEYS + i2[0])
    cand = jnp.concatenate(cand, axis=0)
    cand_rows = jnp.concatenate(cand_rows, axis=0) * float(WORD_ROWS)
    sc, ci = _top_rows(cand, PEER_TOPK)
    iota = lax.broadcasted_iota(jnp.int32, cand.shape, 0).astype(jnp.float32)
    rows = [jnp.max(jnp.where(iota == c, cand_rows, -1.0), axis=0, keepdims=True) for c in ci]
    ex = [jnp.exp(v - sc[0]) for v in sc]
    denom = ex[0]
    for e in ex[1:]:
        denom = denom + e
    return jnp.concatenate(rows, axis=0).astype(jnp.int32), jnp.concatenate(ex, axis=0) / denom


def _peer_route_kernel(h_ref, g_ref, wq_ref, k1_ref, k2_ref, xn_ref, rows_ref, gates_ref,
                       xb_ref, rows_scr, gates_scr):
    grp = pl.program_id(1)

    @pl.when(grp == 0)
    def _():
        x = h_ref[...]
        xn = x * lax.rsqrt(jnp.mean(x * x, axis=-1, keepdims=True) + EPS) * g_ref[...]
        for r in range(SUBLANES):
            xn_ref[pl.ds(r, x.shape[0], stride=SUBLANES), :] = xn[:, r * LANES:(r + 1) * LANES]
        xb_ref[...] = xn.astype(jnp.bfloat16)

    q = jnp.dot(xb_ref[...], wq_ref[...], preferred_element_type=jnp.float32).astype(jnp.bfloat16)
    k1, k2 = k1_ref[...], k2_ref[...]
    for j in range(ROUTE_HEAD_GROUP):
        rows, gates = _route_head(q[:, j * PEER_DQ:(j + 1) * PEER_DQ], k1, k2)
        base = pl.multiple_of((grp * ROUTE_HEAD_GROUP + j) * PEER_TOPK, PEER_TOPK)
        rows_scr[pl.ds(base, PEER_TOPK), :] = rows
        gates_scr[pl.ds(base, PEER_TOPK), :] = gates

    @pl.when(grp == PEER_HEADS // ROUTE_HEAD_GROUP - 1)
    def _():
        for j in range(rows_ref.shape[0]):
            rows_ref[j] = rows_scr[:, j * PEER_TOKEN_BLOCK:(j + 1) * PEER_TOKEN_BLOCK]
        gates_ref[...] = gates_scr[...].T


def _peer_route(h, g_ffn, w_peer_q, keys_1, keys_2):
    T = h.shape[0]
    tb = ROUTE_TOKEN_BLOCK
    per_step = tb // PEER_TOKEN_BLOCK
    half = PEER_DQ // 2
    zeros = jnp.zeros((PEER_NKEYS, half), jnp.bfloat16)
    k1 = jnp.concatenate([keys_1.astype(jnp.bfloat16), zeros], axis=1)
    k2 = jnp.concatenate([zeros, keys_2.astype(jnp.bfloat16)], axis=1)
    keys_spec = pl.BlockSpec((PEER_NKEYS, PEER_DQ), lambda i, g: (0, 0))
    pair_spec = pl.BlockSpec((tb, PEER_PAIRS), lambda i, g: (i, 0))
    return pl.pallas_call(
        _peer_route_kernel,
        out_shape=(jax.ShapeDtypeStruct((T * SUBLANES, LANES), jnp.float32),
                   jax.ShapeDtypeStruct((T // PEER_TOKEN_BLOCK, PEER_PAIRS, PEER_TOKEN_BLOCK), jnp.int32),
                   jax.ShapeDtypeStruct((T, PEER_PAIRS), jnp.float32)),
        grid=(T // tb, PEER_HEADS // ROUTE_HEAD_GROUP),
        in_specs=[pl.BlockSpec((tb, D_MODEL), lambda i, g: (i, 0)),
                  pl.BlockSpec((1, D_MODEL), lambda i, g: (0, 0)),
                  pl.BlockSpec((D_MODEL, ROUTE_HEAD_GROUP * PEER_DQ), lambda i, g: (0, g)),
                  keys_spec, keys_spec],
        out_specs=(pl.BlockSpec((tb * SUBLANES, LANES), lambda i, g: (i, 0)),
                   pl.BlockSpec((per_step, PEER_PAIRS, PEER_TOKEN_BLOCK), lambda i, g: (i, 0, 0)),
                   pair_spec),
        scratch_shapes=[pltpu.VMEM((tb, D_MODEL), jnp.bfloat16),
                        pltpu.VMEM((PEER_PAIRS, tb), jnp.int32),
                        pltpu.VMEM((PEER_PAIRS, tb), jnp.float32)],
        compiler_params=pltpu.CompilerParams(dimension_semantics=("arbitrary", "arbitrary")),
        name="peer_route",
    )(h, g_ffn.reshape(1, D_MODEL), w_peer_q.astype(jnp.bfloat16), k1, k2)


def _pack_table_kernel(tab_ref, out_ref):
    n = tab_ref.shape[0]
    bits = lax.bitcast_convert_type(tab_ref[...].astype(jnp.bfloat16).astype(jnp.float32), jnp.int32)
    for j in range(WORD_ROWS):
        lo = lax.shift_right_logical(bits[:, 2 * j * LANES:(2 * j + 1) * LANES], 16)
        hi = bits[:, (2 * j + 1) * LANES:(2 * j + 2) * LANES] & HIGH_HALF_MASK
        out_ref[pl.ds(j, n, stride=WORD_ROWS), :] = hi | lo


def _pack_expert_table(tab):
    n = PACK_EXPERT_BLOCK
    return pl.pallas_call(
        _pack_table_kernel,
        out_shape=jax.ShapeDtypeStruct((tab.shape[0] * WORD_ROWS, LANES), jnp.int32),
        grid=(tab.shape[0] // n,),
        in_specs=[pl.BlockSpec((n, D_MODEL), lambda i: (i, 0))],
        out_specs=pl.BlockSpec((n * WORD_ROWS, LANES), lambda i: (i, 0)),
        compiler_params=pltpu.CompilerParams(dimension_semantics=("arbitrary",)),
        name="pack_table",
    )(tab)


def _gather_rows(idx_ref, tab_ref, dst_refs, t0):
    for k in range(PEER_PAIRS):
        idx_k = idx_ref.at[k]
        for s, dst_ref in enumerate(dst_refs):
            row = pl.multiple_of(idx_k[t0 + s], WORD_ROWS)
            dst_ref[k * WORD_ROWS:(k + 1) * WORD_ROWS, :] = tab_ref[pl.ds(row, WORD_ROWS), :]


def _expert_token_loop(idx_hbm, idx_bufs, sem, tab_ref, rows_refs, n_tokens, consume):
    i = pl.program_id(0)
    n = pl.num_programs(0)
    n_slots = len(rows_refs)
    assert n_tokens % n_slots == 0

    def copy(block, slot):
        return pltpu.make_async_copy(idx_hbm.at[block], idx_bufs[slot], sem.at[slot])

    def consume_group(t0):
        for s in range(n_slots):
            consume(t0 + s, s)

    @pl.when(i == 0)
    def _():
        copy(0, 0).start()
        copy(0, 0).wait()
        _gather_rows(idx_bufs[0], tab_ref, rows_refs, 0)

    def run(slot):
        has_next = i + 1 < n

        @pl.when(has_next)
        def _():
            copy(i + 1, 1 - slot).start()

        def step(j, carry):
            consume_group(n_slots * j)
            _gather_rows(idx_bufs[slot], tab_ref, rows_refs, n_slots * (j + 1))
            return carry

        lax.fori_loop(0, n_tokens // n_slots - 1, step, 0)

        @pl.when(has_next)
        def _():
            copy(i + 1, 1 - slot).wait()
            consume_group(n_tokens - n_slots)
            _gather_rows(idx_bufs[1 - slot], tab_ref, rows_refs, 0)

        @pl.when(jnp.logical_not(has_next))
        def _():
            consume_group(n_tokens - n_slots)

    for slot in range(2):
        pl.when(lax.rem(i, 2) == slot)(lambda slot=slot: run(slot))


def _diag_mask():
    lane = lax.broadcasted_iota(jnp.int32, (SUBLANES, GATHER_ROWS), 1)
    sub = lax.broadcasted_iota(jnp.int32, (SUBLANES, GATHER_ROWS), 0)
    return (lane & (SUBLANES - 1)) == sub


def _split_bf16(x):
    hi = x.astype(jnp.bfloat16)
    lo = (x - hi.astype(jnp.float32)).astype(jnp.bfloat16)
    return hi, lo


def _peer_hidden_kernel(idx_hbm, x_ref, gate_ref, tab_ref, sel_ref, w_ref, diag_ref, idx_a, idx_b, sem,
                        *rows_refs):
    mask = _diag_mask()

    def consume(t, slot):
        rows = pltpu.bitcast(rows_refs[slot][...], jnp.bfloat16)
        xb = x_ref[t].astype(jnp.bfloat16)
        d = lax.dot_general(xb, rows, NT_DIMS, preferred_element_type=jnp.float32)
        diag_ref[pl.ds(t, 1), :] = jnp.sum(jnp.where(mask, d, 0.0), axis=0, keepdims=True)

    _expert_token_loop(idx_hbm, (idx_a, idx_b), sem, tab_ref, rows_refs, x_ref.shape[0], consume)
    hi, lo = _split_bf16(diag_ref[...])
    sel = sel_ref[...]
    s = (jnp.dot(hi, sel, preferred_element_type=jnp.float32)
         + jnp.dot(lo, sel, preferred_element_type=jnp.float32))
    act = 0.5 * s * (1.0 + lax.erf(s * (1.0 / math.sqrt(2.0))))
    w_ref[...] = gate_ref[...] * act


def _peer_output_kernel(idx_hbm, w_ref, h_ref, tab_ref, expand_ref, out_ref, whi_ref, wlo_ref, idx_a, idx_b, sem,
                        *rows_refs):
    mask = _diag_mask()
    hi, lo = _split_bf16(w_ref[...])
    expand = expand_ref[...]
    whi_ref[...] = jnp.dot(hi, expand, preferred_element_type=jnp.float32)
    wlo_ref[...] = jnp.dot(lo, expand, preferred_element_type=jnp.float32)

    def consume(t, slot):
        rows = pltpu.bitcast(rows_refs[slot][...], jnp.bfloat16)
        parts = []
        for w_part in (whi_ref, wlo_ref):
            w_row = jnp.broadcast_to(w_part[pl.ds(t, 1), :], (SUBLANES, GATHER_ROWS))
            parts.append(jnp.where(mask, w_row, 0.0))
        lhs = jnp.concatenate(parts, axis=0).astype(jnp.bfloat16)
        y = jnp.dot(lhs, rows, preferred_element_type=jnp.float32)
        y = y[:SUBLANES] + y[SUBLANES:]
        y_row = jnp.concatenate([y[r:r + 1, :] for r in range(SUBLANES)], axis=1)
        out_ref[pl.ds(t, 1), :] = h_ref[pl.ds(t, 1), :] + y_row

    _expert_token_loop(idx_hbm, (idx_a, idx_b), sem, tab_ref, rows_refs, h_ref.shape[0], consume)


def _peer_experts(h, xn, rows, gates, peer_u, peer_v):
    T = h.shape[0]
    tb = PEER_TOKEN_BLOCK
    u_tab = _pack_expert_table(peer_u)
    v_tab = _pack_expert_table(peer_v)
    expand = (jnp.arange(PEER_PAIRS)[:, None] == jnp.arange(GATHER_ROWS)[None, :] // SUBLANES
              ).astype(jnp.bfloat16)
    params = pltpu.CompilerParams(dimension_semantics=("arbitrary",), vmem_limit_bytes=PEER_VMEM_LIMIT)
    hbm_ref = pl.BlockSpec(memory_space=pl.ANY)
    index_scratch = [pltpu.SMEM((PEER_PAIRS, tb), jnp.int32)] * 2 + [pltpu.SemaphoreType.DMA((2,))]
    pair_block = pl.BlockSpec((tb, PEER_PAIRS), lambda i: (i, 0))
    vmem_rows = pl.BlockSpec((tb, D_MODEL), lambda i: (i, 0))
    whole_vmem = pl.BlockSpec(memory_space=pltpu.VMEM)
    rows_scratch = pltpu.VMEM((GATHER_WORD_ROWS, LANES), jnp.int32)
    wide_scratch = pltpu.VMEM((tb, GATHER_ROWS), jnp.float32)

    w = pl.pallas_call(
        _peer_hidden_kernel,
        out_shape=jax.ShapeDtypeStruct((T, PEER_PAIRS), jnp.float32),
        grid=(T // tb,),
        in_specs=[hbm_ref, pl.BlockSpec((tb, SUBLANES, LANES), lambda i: (i, 0, 0)), pair_block,
                  whole_vmem, whole_vmem],
        out_specs=pair_block,
        scratch_shapes=[wide_scratch] + index_scratch + [rows_scratch] * HIDDEN_SLOTS,
        compiler_params=params,
        name="peer_hidden",
    )(rows, xn.reshape(T, SUBLANES, LANES), gates, u_tab, expand.T)

    return pl.pallas_call(
        _peer_output_kernel,
        out_shape=jax.ShapeDtypeStruct((T, D_MODEL), jnp.float32),
        grid=(T // tb,),
        in_specs=[hbm_ref, pair_block, vmem_rows, whole_vmem, whole_vmem],
        out_specs=vmem_rows,
        scratch_shapes=[wide_scratch, wide_scratch] + index_scratch + [rows_scratch] * OUTPUT_SLOTS,
        compiler_params=params,
        name="peer_output",
    )(rows, w, h, v_tab, expand)


def kernel(x, meta_tokens, g_mix, w_in, g_q_lora, w_uq, g_kv_lora, w_ukv, g_qk_q, g_qk_k, w_mla_out, g_ret_gn, w_ret_out, w_mix_out, g_ffn, w_peer_q, peer_keys_1, peer_keys_2, peer_u, peer_v):
    B, S, D = x.shape
    h = _mixer(x, meta_tokens, g_mix[0], w_in[0], g_q_lora[0], w_uq[0], g_kv_lora[0], w_ukv[0],
               g_qk_q[0], g_qk_k[0], w_mla_out[0], g_ret_gn[0], w_ret_out[0], w_mix_out[0])
    xn, rows, gates = _peer_route(h, g_ffn[0], w_peer_q[0], peer_keys_1[0], peer_keys_2[0])
    out = _peer_experts(h, xn, rows, gates, peer_u[0], peer_v[0])
    return out.reshape(B, S, D)
```
